```python
import jax
import jax.numpy as jnp
from jax import lax
import numpy as np

D_MODEL = 1024
BATCH = 2
SEQ = 16384
DEPTH = 4

GRID_W = 64
CTX_LEN = 256
EPS = 1e-6

GM_HEADS = 4
GM_WIDTH = D_MODEL // 4
GM_HEAD_DIM = GM_WIDTH // GM_HEADS
GM_CHUNK = 128

ML_HEADS = 4
ML_WIDTH = D_MODEL // 4
ML_HEAD_DIM = ML_WIDTH // ML_HEADS
ML_CHUNK = 128
ML_CONV = 3

NA_HEADS = 8
NA_WIDTH = D_MODEL // 2
NA_HEAD_DIM = NA_WIDTH // NA_HEADS
NA_ROWS = 8
NA_COLS = 16
NA_QCOLS = 16

ROPE_BASE = 10000.0

D_MIX = GM_WIDTH + ML_WIDTH + NA_WIDTH
OFF_ML = 2 * GM_WIDTH
OFF_NA = OFF_ML + 4 * ML_WIDTH + 4 * ML_HEADS
N_IN = OFF_NA + 3 * NA_WIDTH

N_EXPERTS = 16
N_GROUPS = 4
EXP_PER_GROUP = N_EXPERTS // N_GROUPS
TOP_K = 2
D_EXPERT = D_MODEL
MOE_BLOCK = 256

kernel_name = 'hybrid_gmlp_mlstm_natten_moe_dit'


def _rmsnorm(x, g):
    xf = x.astype(jnp.float32)
    y = xf * lax.rsqrt(jnp.mean(xf * xf, -1, keepdims=True) + EPS)
    return (y * g.astype(jnp.float32)).astype(x.dtype)


def _layernorm(x, g):
    xf = x.astype(jnp.float32)
    mu = jnp.mean(xf, -1, keepdims=True)
    var = jnp.mean(jnp.square(xf - mu), -1, keepdims=True)
    return ((xf - mu) * lax.rsqrt(var + EPS) * g.astype(jnp.float32)).astype(x.dtype)


def _modulate(x, g, shift, scale):
    return _rmsnorm(x, g) * (1 + scale) + shift


def _rope_half(x, pos):
    half = x.shape[-1] // 2
    inv = ROPE_BASE ** (-jnp.arange(half, dtype=jnp.float32) / half)
    ang = pos.astype(jnp.float32)[:, None] * inv
    cos = jnp.cos(ang)[:, None, :]
    sin = jnp.sin(ang)[:, None, :]
    xf = x.astype(jnp.float32)
    x1, x2 = xf[..., :half], xf[..., half:]
    return jnp.concatenate([x1 * cos - x2 * sin, x1 * sin + x2 * cos], -1).astype(x.dtype)


def _rope_2d(x):
    t = jnp.arange(x.shape[1])
    d = x.shape[-1] // 2
    return jnp.concatenate([_rope_half(x[..., :d], t // GRID_W), _rope_half(x[..., d:], t % GRID_W)], -1)


def _dwconv_centred(x, w, b):
    K = w.shape[0]
    pad = K // 2
    T = x.shape[1]
    xp = jnp.pad(x, ((0, 0), (pad, K - 1 - pad), (0, 0)))
    return sum(xp[:, j:j + T] * w[j] for j in range(K)) + b


def _chunk_gmlp(z, ws, bs, g):
    B, T, _ = z.shape
    z = jax.nn.gelu(z)
    u, v = z[..., :GM_WIDTH], z[..., GM_WIDTH:]
    v = _layernorm(v, g).reshape(B, T // GM_CHUNK, GM_CHUNK, GM_HEADS, GM_HEAD_DIM)
    sv = jnp.einsum('hts,bnshc->bnthc', ws, v) + jnp.transpose(bs)[None, None, :, :, None]
    return u * sv.reshape(B, T, GM_WIDTH)


def _mlstm_prep(p, conv_w, conv_b, gate_b, rotary):
    B, T, _ = p.shape
    W = ML_WIDTH
    qk = jax.nn.silu(_dwconv_centred(p[..., :2 * W], conv_w, conv_b))
    q = qk[..., :W].reshape(B, T, ML_HEADS, ML_HEAD_DIM)
    k = qk[..., W:].reshape(B, T, ML_HEADS, ML_HEAD_DIM)
    if rotary:
        q = _rope_2d(q)
        k = _rope_2d(k)
    v = p[..., 2 * W:3 * W].reshape(B, T, ML_HEADS, ML_HEAD_DIM)
    o = p[..., 3 * W:4 * W]
    gates = p[..., 4 * W:].astype(jnp.float32).reshape(B, T, 4, ML_HEADS) + gate_b.astype(jnp.float32)
    gates = jnp.transpose(gates, (2, 0, 3, 1))
    heads = lambda t: jnp.transpose(t, (0, 2, 1, 3)).astype(jnp.float32)
    return heads(q), heads(k), heads(v), o, gates


def _mlstm_chunkwise(q, k, v, log_i, log_f, state, need_out):
    B, H, T, dh = q.shape
    L = ML_CHUNK
    nc = T // L
    q = q.reshape(B, H, nc, L, dh) * (dh ** -0.5)
    k = k.reshape(B, H, nc, L, dh)
    v = v.reshape(B, H, nc, L, dh)
    li = log_i.reshape(B, H, nc, L)
    b = jnp.cumsum(log_f.reshape(B, H, nc, L), axis=-1)
    b_last = b[..., -1]
    w = b_last[..., None] - b + li
    w_max = jnp.max(w, -1)
    ew = jnp.exp(w - w_max[..., None])
    kv = jnp.einsum('bhnsk,bhnsv,bhns->bhnkv', k, v, ew)
    ksum = jnp.einsum('bhnsk,bhns->bhnk', k, ew)

    def step(carry, inp):
        C, n, m = carry
        kv_c, ks_c, wm_c, bl_c = inp
        m_new = jnp.maximum(bl_c + m, wm_c)
        a = jnp.exp(bl_c + m - m_new)
        g = jnp.exp(wm_c - m_new)
        C_new = a[..., None, None] * C + g[..., None, None] * kv_c
        n_new = a[..., None] * n + g[..., None] * ks_c
        return (C_new, n_new, m_new), (C, n, m)

    xs = (jnp.moveaxis(kv, 2, 0), jnp.moveaxis(ksum, 2, 0), jnp.moveaxis(w_max, 2, 0), jnp.moveaxis(b_last, 2, 0))
    final, starts = lax.scan(step, state, xs)
    if not need_out:
        return None, final
    C0 = jnp.moveaxis(starts[0], 0, 2)
    n0 = jnp.moveaxis(starts[1], 0, 2)
    m0 = jnp.moveaxis(starts[2], 0, 2)
    causal = jnp.tril(jnp.ones((L, L), dtype=bool))
    D = jnp.where(causal, b[..., :, None] - b[..., None, :] + li[..., None, :], -jnp.inf)
    g_inter = b + m0[..., None]
    m_t = jnp.maximum(g_inter, jnp.max(D, -1))
    S = jnp.einsum('bhntk,bhnsk->bhnts', q, k) * jnp.exp(D - m_t[..., None])
    a_t = jnp.exp(g_inter - m_t)
    num = jnp.einsum('bhnts,bhnsv->bhntv', S, v) + a_t[..., None] * jnp.einsum('bhntk,bhnkv->bhntv', q, C0)
    den = jnp.sum(S, -1) + a_t * jnp.einsum('bhntk,bhnk->bhnt', q, n0)
    h = num / jnp.maximum(jnp.abs(den), jnp.exp(-m_t))[..., None]
    return h.reshape(B, H, T, dh), final


def _mlstm_bidir(q, k, v, gates, st_f, st_b, need_out):
    li_f, lf_f = gates[0], jax.nn.log_sigmoid(gates[1])
    li_b, lf_b = gates[2], jax.nn.log_sigmoid(gates[3])
    flip = lambda t: jnp.flip(t, axis=2)
    h_f, fin_f = _mlstm_chunkwise(q, k, v, li_f, lf_f, st_f, need_out)
    h_b, fin_b = _mlstm_chunkwise(flip(q), flip(k), flip(v), flip(li_b), flip(lf_b), st_b, need_out)
    h = h_f + flip(h_b) if need_out else None
    return h, fin_f, fin_b


def _mlstm_out(h, o, g):
    B, H, T, dh = h.shape
    h = jnp.transpose(h, (0, 2, 1, 3))
    h = h * lax.rsqrt(jnp.mean(h * h, -1, keepdims=True) + EPS)
    y = h.reshape(B, T, H * dh) * g.astype(jnp.float32) * jax.nn.sigmoid(o.astype(jnp.float32))
    return y.astype(o.dtype)


def _na_latent(q, k, v, kc, vc, rpb):
    B, T, H, dh = q.shape
    rows = T // GRID_W
    kr = min(NA_ROWS, rows)
    ncb = GRID_W // NA_QCOLS
    kb = NA_QCOLS + NA_COLS
    qcol = np.arange(GRID_W).reshape(ncb, NA_QCOLS)
    cstart = np.clip(np.arange(ncb) * NA_QCOLS - NA_COLS // 2, 0, GRID_W - kb)
    kcol = cstart[:, None] + np.arange(kb)
    wstart = np.clip(qcol - NA_COLS // 2, 0, GRID_W - NA_COLS)
    col_ok = (kcol[:, None, :] >= wstart[:, :, None]) & (kcol[:, None, :] < wstart[:, :, None] + NA_COLS)
    dc = np.clip(kcol[:, None, :] - qcol[:, :, None], -(NA_COLS - 1), NA_COLS - 1) + NA_COLS - 1
    qg = (q * (dh ** -0.5)).reshape(B, rows, GRID_W, H, dh)
    kg = k.reshape(B, rows, GRID_W, H, dh)
    vg = v.reshape(B, rows, GRID_W, H, dh)
    n_loc = kr * kb

    def row_fn(r):
        r0 = jnp.clip(r - kr // 2, 0, rows - kr)
        kband = lax.dynamic_slice_in_dim(kg, r0, kr, axis=1)
        vband = lax.dynamic_slice_in_dim(vg, r0, kr, axis=1)
        qrow = lax.dynamic_index_in_dim(qg, r, axis=1, keepdims=False).reshape(B, ncb, NA_QCOLS, H, dh)
        kblk = jnp.stack([kband[:, :, int(s):int(s) + kb] for s in cstart], axis=1).reshape(B, ncb, n_loc, H, dh)
        vblk = jnp.stack([vband[:, :, int(s):int(s) + kb] for s in cstart], axis=1).reshape(B, ncb, n_loc, H, dh)
        s_loc = jnp.einsum('bjqhd,bjkhd->bhjqk', qrow, kblk).astype(jnp.float32)
        s_loc = s_loc.reshape(B, H, ncb, NA_QCOLS, kr, kb)
        dr = r0 + jnp.arange(kr) - r + (NA_ROWS - 1)
        bias = jnp.transpose(rpb[:, dr][:, :, dc], (0, 2, 3, 1, 4)).astype(jnp.float32)
        s_loc = jnp.where(col_ok[:, :, None, :], s_loc + bias, -jnp.inf).reshape(B, H, ncb, NA_QCOLS, n_loc)
        s_ctx = jnp.einsum('bjqhd,bkhd->bhjqk', qrow, kc).astype(jnp.float32)
        p = jax.nn.softmax(jnp.concatenate([s_loc, s_ctx], -1), axis=-1).astype(v.dtype)
        o = jnp.einsum('bhjqk,bjkhd->bjqhd', p[..., :n_loc], vblk) + jnp.einsum('bhjqk,bkhd->bjqhd', p[..., n_loc:], vc)
        return o.reshape(B, GRID_W, H, dh)

    out = lax.map(row_fn, jnp.arange(rows))
    return jnp.moveaxis(out, 0, 1).reshape(B, T, H * dh)


def _ctx_attention(q, k, v):
    B, L, H, dh = q.shape
    s = jnp.einsum('bqhd,bkhd->bhqk', q * (dh ** -0.5), k).astype(jnp.float32)
    p = jax.nn.softmax(s, axis=-1).astype(v.dtype)
    return jnp.einsum('bhqk,bkhd->bqhd', p, v).reshape(B, L, H * dh)


def _token_mixers(h, hc, w_in, gm_ws, gm_bs, gm_g, ml_cw, ml_cb, ml_gb, ml_g, na_rpb, need_ctx):
    B, T, _ = h.shape
    Lc = hc.shape[1]
    p = h @ w_in
    pc = hc @ w_in
    y_gm = _chunk_gmlp(p[..., :OFF_ML], gm_ws, gm_bs, gm_g)
    q_c, k_c, v_c, o_c, g_c = _mlstm_prep(pc[..., OFF_ML:OFF_NA], ml_cw, ml_cb, ml_gb, False)
    zero = (jnp.zeros((B, ML_HEADS, ML_HEAD_DIM, ML_HEAD_DIM), jnp.float32),
            jnp.zeros((B, ML_HEADS, ML_HEAD_DIM), jnp.float32),
            jnp.zeros((B, ML_HEADS), jnp.float32))
    h_c, st_f, st_b = _mlstm_bidir(q_c, k_c, v_c, g_c, zero, zero, need_ctx)
    q_l, k_l, v_l, o_l, g_l = _mlstm_prep(p[..., OFF_ML:OFF_NA], ml_cw, ml_cb, ml_gb, True)
    h_l, _, _ = _mlstm_bidir(q_l, k_l, v_l, g_l, st_f, st_b, True)
    y_ml = _mlstm_out(h_l, o_l, ml_g)
    na = p[..., OFF_NA:]
    nac = pc[..., OFF_NA:]
    heads = lambda t, L: t.reshape(B, L, NA_HEADS, NA_HEAD_DIM)
    nq, nk, nv = (heads(na[..., i * NA_WIDTH:(i + 1) * NA_WIDTH], T) for i in range(3))
    cq, ck, cv = (heads(nac[..., i * NA_WIDTH:(i + 1) * NA_WIDTH], Lc) for i in range(3))
    y_na = _na_latent(nq, nk, nv, ck, cv, na_rpb)
    y = jnp.concatenate([y_gm, y_ml, y_na], -1)
    if not need_ctx:
        return y, None
    yc = jnp.concatenate([_chunk_gmlp(pc[..., :OFF_ML], gm_ws, gm_bs, gm_g),
                          _mlstm_out(h_c, o_c, ml_g),
                          _ctx_attention(cq, ck, cv)], -1)
    return y, yc


def _moe_ffn(h, router_w, router_b, w_gate, w_up, w_down):
    n_tok, d = h.shape
    s = jax.nn.sigmoid(jnp.dot(h.astype(jnp.float32), router_w.astype(jnp.float32)))
    sel = s + router_b.astype(jnp.float32)
    grp_score = jnp.sum(lax.top_k(sel.reshape(n_tok, N_GROUPS, EXP_PER_GROUP), 2)[0], -1)
    best = jnp.argmax(grp_score, axis=-1)
    in_grp = (jnp.arange(N_EXPERTS) // EXP_PER_GROUP)[None, :] == best[:, None]
    _, idx = lax.top_k(jnp.where(in_grp, sel, -jnp.inf), TOP_K)
    wt = jnp.take_along_axis(s, idx, axis=-1)
    wt = wt / jnp.sum(wt, -1, keepdims=True)
    n_asg = n_tok * TOP_K
    flat_e = idx.reshape(-1).astype(jnp.int32)
    flat_t = jnp.repeat(jnp.arange(n_tok, dtype=jnp.int32), TOP_K)
    flat_w = wt.reshape(-1)
    order = jnp.argsort(flat_e)
    se, st, sw = flat_e[order], flat_t[order], flat_w[order]
    counts = jnp.bincount(flat_e, length=N_EXPERTS).astype(jnp.int32)
    starts = jnp.cumsum(counts) - counts
    padded = (counts + MOE_BLOCK - 1) // MOE_BLOCK * MOE_BLOCK
    pends = jnp.cumsum(padded)
    pstarts = pends - padded
    dest = pstarts[se] + jnp.arange(n_asg, dtype=jnp.int32) - starts[se]
    n_blk = -(-n_asg // MOE_BLOCK) + N_EXPERTS
    buf = n_blk * MOE_BLOCK
    buf_t = jnp.full((buf,), n_tok, jnp.int32).at[dest].set(st)
    buf_w = jnp.zeros((buf,), jnp.float32).at[dest].set(sw)
    blk_e = jnp.minimum(jnp.searchsorted(pends, jnp.arange(n_blk, dtype=jnp.int32) * MOE_BLOCK, side='right'), N_EXPERTS - 1)
    xb = jnp.concatenate([h, jnp.zeros((1, d), h.dtype)], 0)[buf_t].reshape(n_blk, MOE_BLOCK, d)

    def expert_block(args):
        xe, e = args
        return (jax.nn.silu(xe @ w_gate[e]) * (xe @ w_up[e])) @ w_down[e]

    yb = lax.map(expert_block, (xb, blk_e)).reshape(buf, d)
    y = jax.ops.segment_sum(yb * buf_w[:, None].astype(yb.dtype), buf_t, num_segments=n_tok + 1)
    return y[:n_tok]


def setup_inputs(seed: int = 0) -> dict:
    key = jax.random.key(seed)
    ks = jax.random.split(key, 24)

    def nrm(k, shape, std):
        return jax.random.normal(k, shape, jnp.float32) * std

    f_bias = jnp.linspace(3.0, 6.0, ML_HEADS, dtype=jnp.float32)
    zero_h = jnp.zeros((ML_HEADS,), jnp.float32)
    gate_base = jnp.stack([zero_h, f_bias, zero_h, f_bias])
    return {
        'x': nrm(ks[0], (BATCH, SEQ, D_MODEL), 1.0),
        'c': nrm(ks[1], (BATCH, D_MODEL), 1.0),
        'ctx': nrm(ks[2], (BATCH, CTX_LEN, D_MODEL), 1.0),
        'c_ctx': nrm(ks[3], (D_MODEL,), 1.0),
        'ada_w': nrm(ks[4], (DEPTH, D_MODEL, 6 * D_MODEL), 0.5 * D_MODEL ** -0.5),
        'ada_b': nrm(ks[5], (DEPTH, 6 * D_MODEL), 0.02),
        'norm1_g': 1.0 + nrm(ks[6], (DEPTH, D_MODEL), 0.02),
        'norm2_g': 1.0 + nrm(ks[7], (DEPTH, D_MODEL), 0.02),
        'w_in': nrm(ks[8], (DEPTH, D_MODEL, N_IN), D_MODEL ** -0.5),
        'w_out': nrm(ks[9], (DEPTH, D_MIX, D_MODEL), D_MIX ** -0.5),
        'gm_ws': nrm(ks[10], (DEPTH, GM_HEADS, GM_CHUNK, GM_CHUNK), GM_CHUNK ** -0.5),
        'gm_bs': 1.0 + nrm(ks[11], (DEPTH, GM_HEADS, GM_CHUNK), 0.1),
        'gm_norm_g': 1.0 + nrm(ks[12], (DEPTH, GM_WIDTH), 0.02),
        'ml_conv_w': nrm(ks[13], (DEPTH, ML_CONV, 2 * ML_WIDTH), ML_CONV ** -0.5),
        'ml_conv_b': nrm(ks[14], (DEPTH, 2 * ML_WIDTH), 0.02),
        'ml_gate_b': gate_base[None] + nrm(ks[15], (DEPTH, 4, ML_HEADS), 0.1),
        'ml_norm_g': 1.0 + nrm(ks[16], (DEPTH, ML_WIDTH), 0.02),
        'na_rpb': nrm(ks[17], (DEPTH, NA_HEADS, 2 * NA_ROWS - 1, 2 * NA_COLS - 1), 0.1),
        'router_w': nrm(ks[18], (D_MODEL, N_EXPERTS), D_MODEL ** -0.5),
        'router_b': nrm(ks[19], (N_EXPERTS,), 0.01),
        'moe_w_gate': nrm(ks[20], (DEPTH, N_EXPERTS, D_MODEL, D_EXPERT), D_MODEL ** -0.5),
        'moe_w_up': nrm(ks[21], (DEPTH, N_EXPERTS, D_MODEL, D_EXPERT), D_MODEL ** -0.5),
        'moe_w_down': nrm(ks[22], (DEPTH, N_EXPERTS, D_EXPERT, D_MODEL), D_EXPERT ** -0.5),
        'final_g': 1.0 + nrm(ks[23], (D_MODEL,), 0.02),
    }


def reference(x, c, ctx, c_ctx, ada_w, ada_b, norm1_g, norm2_g, w_in, w_out, gm_ws, gm_bs, gm_norm_g,
              ml_conv_w, ml_conv_b, ml_gate_b, ml_norm_g, na_rpb, router_w, router_b,
              moe_w_gate, moe_w_up, moe_w_down, final_g):
    B, T, D = x.shape
    Lc = ctx.shape[1]
    xc = ctx
    sc = jax.nn.silu(c)
    scc = jax.nn.silu(c_ctx)
    for l in range(DEPTH):
        need_ctx = l < DEPTH - 1
        mod = (sc @ ada_w[l] + ada_b[l])[:, None, :]
        modc = scc @ ada_w[l] + ada_b[l]
        sh1, s1, g1, sh2, s2, g2 = jnp.split(mod, 6, axis=-1)
        csh1, cs1, cg1, csh2, cs2, cg2 = jnp.split(modc, 6, axis=-1)
        h = _modulate(x, norm1_g[l], sh1, s1)
        hc = _modulate(xc, norm1_g[l], csh1, cs1)
        y, yc = _token_mixers(h, hc, w_in[l], gm_ws[l], gm_bs[l], gm_norm_g[l], ml_conv_w[l], ml_conv_b[l],
                              ml_gate_b[l], ml_norm_g[l], na_rpb[l], need_ctx)
        x = x + g1 * (y @ w_out[l])
        h2 = _modulate(x, norm2_g[l], sh2, s2).reshape(B * T, D)
        if need_ctx:
            xc = xc + cg1 * (yc @ w_out[l])
            h2c = _modulate(xc, norm2_g[l], csh2, cs2).reshape(B * Lc, D)
            f = _moe_ffn(jnp.concatenate([h2, h2c], 0), router_w, router_b, moe_w_gate[l], moe_w_up[l], moe_w_down[l])
            x = x + g2 * f[:B * T].reshape(B, T, D)
            xc = xc + cg2 * f[B * T:].reshape(B, Lc, D)
        else:
            f = _moe_ffn(h2, router_w, router_b, moe_w_gate[l], moe_w_up[l], moe_w_down[l])
            x = x + g2 * f.reshape(B, T, D)
    return _rmsnorm(x, final_g)
```

```python
import functools

import jax
import jax.numpy as jnp
import numpy as np
from jax import lax
from jax.experimental import pallas as pl
from jax.experimental.pallas import tpu as pltpu

F32 = jnp.float32
BF16 = jnp.bfloat16

D_MODEL = 1024
DEPTH = 4
GRID_W = 64
EPS = 1e-6

GM_HEADS = 4
GM_WIDTH = D_MODEL // 4
GM_HEAD_DIM = GM_WIDTH // GM_HEADS
GM_CHUNK = 128

ML_HEADS = 4
ML_WIDTH = D_MODEL // 4
ML_HEAD_DIM = ML_WIDTH // ML_HEADS
ML_CHUNK = 128

NA_HEADS = 8
NA_WIDTH = D_MODEL // 2
NA_HEAD_DIM = NA_WIDTH // NA_HEADS
NA_ROWS = 8
NA_COLS = 16

ROPE_BASE = 10000.0

OFF_ML = 2 * GM_WIDTH
OFF_GATES = OFF_ML + 4 * ML_WIDTH
OFF_NA = OFF_GATES + 4 * ML_HEADS
N_IN = OFF_NA + 3 * NA_WIDTH

N_EXPERTS = 16
N_GROUPS = 4
EXP_PER_GROUP = N_EXPERTS // N_GROUPS
D_EXPERT = D_MODEL

LANE = 128
SEG_GM = (0, 2 * GM_WIDTH)
SEG_ML = (SEG_GM[1], SEG_GM[1] + 4 * ML_WIDTH)
SEG_Q = (SEG_ML[1], SEG_ML[1] + NA_WIDTH)
SEG_K = (SEG_Q[1], SEG_Q[1] + NA_WIDTH)
SEG_V = (SEG_K[1], SEG_K[1] + NA_WIDTH)
SEG_GATES = (SEG_V[1], SEG_V[1] + LANE)
N_IN_PAD = SEG_GATES[1]

NA_QROWS = 8
NA_BAND = 16
NA_KBLK_ROWS = 4
MOE_TILE = 512
EXPERT_CHUNK = 512
MASK_NEG = -1e30
STATE_ROWS = 72
VMEM_LIMIT = 56 * 2 ** 20


def _cparams(*sem):
    return pltpu.CompilerParams(dimension_semantics=sem, vmem_limit_bytes=VMEM_LIMIT)


def _nt_dot(a, b, precision=None):
    return lax.dot_general(a, b, (((1,), (1,)), ((), ())), precision=precision,
                           preferred_element_type=F32)


def _tn_dot(a, b):
    return lax.dot_general(a, b, (((0,), (0,)), ((), ())), preferred_element_type=F32)


def _ada_kernel(c_ref, w_ref, b_ref, o_ref):
    c = c_ref[...]
    sc = c * jax.nn.sigmoid(c)
    o_ref[0] = jnp.dot(sc, w_ref[0], preferred_element_type=F32) + b_ref[0]


def _ada_all(cs, ada_w, ada_b):
    n_out = ada_w.shape[-1] // D_MODEL
    return pl.pallas_call(
        _ada_kernel,
        grid=(DEPTH, n_out),
        in_specs=[pl.BlockSpec((8, D_MODEL), lambda l, j: (0, 0)),
                  pl.BlockSpec((1, D_MODEL, D_MODEL), lambda l, j: (l, 0, j)),
                  pl.BlockSpec((1, 1, D_MODEL), lambda l, j: (l, 0, j))],
        out_specs=pl.BlockSpec((1, 8, D_MODEL), lambda l, j: (l, 0, j)),
        out_shape=jax.ShapeDtypeStruct((DEPTH, 8, n_out * D_MODEL), F32),
        compiler_params=_cparams("arbitrary", "arbitrary"),
        name="adaln",
    )(cs, ada_w, ada_b.reshape(DEPTH, 1, -1))


def _in_proj_kernel(x_ref, mod_ref, g_ref, w_ref, gm_ref, ml_ref, gt_ref, q_ref, k_ref, v_ref):
    x = x_ref[0]
    h = x * lax.rsqrt(jnp.mean(x * x, -1, keepdims=True) + EPS) * g_ref[...]
    h = h * (1.0 + mod_ref[0, 1:2, :]) + mod_ref[0, 0:1, :]
    hb = h.astype(BF16)

    def seg(s):
        return jnp.dot(hb, w_ref[:, s[0]:s[1]], preferred_element_type=F32)

    gm_ref[0] = seg(SEG_GM)
    ml_ref[0] = seg(SEG_ML)
    gt_ref[0] = seg(SEG_GATES)
    q_ref[0] = (seg(SEG_Q) * (NA_HEAD_DIM ** -0.5)).astype(BF16)
    k_ref[0] = seg(SEG_K).astype(BF16)
    v_ref[0] = seg(SEG_V).astype(BF16)


def _in_proj(x, mod, g, w):
    G, R, D = x.shape
    tm = min(R, 512)
    row = lambda b, i: (b, i, 0)
    widths = (SEG_GM[1] - SEG_GM[0], SEG_ML[1] - SEG_ML[0], LANE, NA_WIDTH, NA_WIDTH, NA_WIDTH)
    dtypes = (F32, F32, F32, BF16, BF16, BF16)
    return pl.pallas_call(
        _in_proj_kernel,
        grid=(G, R // tm),
        in_specs=[pl.BlockSpec((1, tm, D), row),
                  pl.BlockSpec((1, 2, D), lambda b, i: (b, 0, 0)),
                  pl.BlockSpec((1, D), lambda b, i: (0, 0)),
                  pl.BlockSpec((D, N_IN_PAD), lambda b, i: (0, 0))],
        out_specs=[pl.BlockSpec((1, tm, n), row) for n in widths],
        out_shape=[jax.ShapeDtypeStruct((G, R, n), dt) for n, dt in zip(widths, dtypes)],
        compiler_params=_cparams("arbitrary", "arbitrary"),
        name="in_proj",
    )(x, mod, g, w)


def _gmlp_kernel(n_chunks, p_ref, ws_ref, bs_ref, g_ref, o_ref):
    lane = lax.broadcasted_iota(jnp.int32, (1, GM_WIDTH), 1)

    def chunk(j, carry):
        r0 = pl.multiple_of(j * GM_CHUNK, GM_CHUNK)
        z = jax.nn.gelu(p_ref[0, pl.ds(r0, GM_CHUNK), :])
        u = z[:, :GM_WIDTH]
        v = z[:, GM_WIDTH:]
        mu = jnp.mean(v, -1, keepdims=True)
        vc = v - mu
        var = jnp.mean(vc * vc, -1, keepdims=True)
        vn = (vc * lax.rsqrt(var + EPS) * g_ref[...]).astype(BF16)
        sv = jnp.zeros((GM_CHUNK, GM_WIDTH), F32)
        for h in range(GM_HEADS):
            full = jnp.dot(ws_ref[h], vn, preferred_element_type=F32)
            in_head = (lane >= h * GM_HEAD_DIM) & (lane < (h + 1) * GM_HEAD_DIM)
            sv = jnp.where(in_head, full, sv)
        o_ref[0, pl.ds(r0, GM_CHUNK), :] = (u * (sv + bs_ref[...])).astype(BF16)
        return carry

    lax.fori_loop(0, n_chunks, chunk, 0)


def _gmlp(p_gm, ws, bs_full, g):
    G, R, _ = p_gm.shape
    tr = min(R, 1024)
    return pl.pallas_call(
        functools.partial(_gmlp_kernel, tr // GM_CHUNK),
        grid=(G, R // tr),
        in_specs=[pl.BlockSpec((1, tr, 2 * GM_WIDTH), lambda b, i: (b, i, 0)),
                  pl.BlockSpec((GM_HEADS, GM_CHUNK, GM_CHUNK), lambda b, i: (0, 0, 0)),
                  pl.BlockSpec((GM_CHUNK, GM_WIDTH), lambda b, i: (0, 0)),
                  pl.BlockSpec((1, GM_WIDTH), lambda b, i: (0, 0))],
        out_specs=pl.BlockSpec((1, tr, GM_WIDTH), lambda b, i: (b, i, 0)),
        out_shape=jax.ShapeDtypeStruct((G, R, GM_WIDTH), BF16),
        compiler_params=_cparams("arbitrary", "arbitrary"),
        name="gmlp",
    )(p_gm, ws, bs_full, g)


def _log_sigmoid(x):
    return jnp.minimum(x, 0.0) - jnp.log1p(jnp.exp(-jnp.abs(x)))


def _rope(x, cos, sin_signed):
    lane = lax.broadcasted_iota(jnp.int32, (1, LANE), 1)
    first_half = (lane & 16) == 0
    partner = jnp.where(first_half, pltpu.roll(x, LANE - 16, 1), pltpu.roll(x, 16, 1))
    return x * cos + partner * sin_signed


def _mlstm_direction(d, ci, n_chunks, tri_lo, tri_up, qk_ref, hp_ref, hn_ref, v_ref, gc_ref, gr_ref,
                     cos_ref, sin_ref, cw_ref, cb_ref, gbc_ref, gbr_ref, st_ref, out_ref):
    L = ML_CHUNK
    W = ML_WIDTH
    HD = ML_HEAD_DIM
    x = qk_ref[0]
    rid = lax.broadcasted_iota(jnp.int32, (L, 1), 0)
    ci_row = jnp.zeros((1, 2 * W), jnp.int32) + ci
    prev = jnp.where(ci_row == 0, 0.0, hp_ref[0, 7:8, :])
    nxt = jnp.where(ci_row == n_chunks - 1, 0.0, hn_ref[0, 0:1, :])
    xm1 = jnp.where(rid == 0, prev, pltpu.roll(x, 1, 0))
    xp1 = jnp.where(rid == L - 1, nxt, pltpu.roll(x, L - 1, 0))
    y = xm1 * cw_ref[0:1, :] + x * cw_ref[1:2, :] + xp1 * cw_ref[2:3, :] + cb_ref[...]
    y = y * jax.nn.sigmoid(y)
    cos = cos_ref[...]
    sin = sin_ref[...]
    parts = [_rope(y[:, i * LANE:(i + 1) * LANE], cos, sin) for i in range(2 * W // LANE)]
    q = jnp.concatenate(parts[:W // LANE], axis=1) * (HD ** -0.5)
    k = jnp.concatenate(parts[W // LANE:], axis=1)
    v = v_ref[0]

    gcol = gc_ref[0][:, :4 * ML_HEADS] + gbc_ref[...]
    grow = gr_ref[0] + gbr_ref[...]
    lf_col = _log_sigmoid(gcol)
    lf_row = _log_sigmoid(grow)
    hi = lax.Precision.HIGHEST
    if d == 0:
        b_col = jnp.dot(tri_lo, lf_col, precision=hi, preferred_element_type=F32)
        b_row = jnp.dot(lf_row, tri_up, precision=hi, preferred_element_type=F32)
        b_last_all = b_col[L - 1:L, :]
        valid = tri_lo > 0.5
    else:
        b_col = jnp.dot(tri_up, lf_col, precision=hi, preferred_element_type=F32)
        b_row = jnp.dot(lf_row, tri_lo, precision=hi, preferred_element_type=F32)
        b_last_all = b_col[0:1, :]
        valid = tri_up > 0.5

    for h in range(ML_HEADS):
        ji = 2 * ML_HEADS * d + h
        jf = ji + ML_HEADS
        bc = b_col[:, jf:jf + 1]
        br = b_row[jf:jf + 1, :]
        lic = gcol[:, ji:ji + 1]
        lir = grow[ji:ji + 1, :]
        b_last = b_last_all[:, jf:jf + 1]
        s = 4 * d + h
        c0 = st_ref[s, 0:HD, :]
        n0 = st_ref[s, HD:HD + 1, :]
        m0 = st_ref[s, HD + 1:HD + 2, 0:1]

        qh = q[:, h * HD:(h + 1) * HD]
        kh = k[:, h * HD:(h + 1) * HD]
        vh = v[:, h * HD:(h + 1) * HD]
        qb = qh.astype(BF16)
        kb = kh.astype(BF16)

        dm = jnp.where(valid, bc - br + lir, MASK_NEG)
        g_inter = bc + m0
        m_t = jnp.maximum(g_inter, jnp.max(dm, -1, keepdims=True))
        p = _nt_dot(qb, kb) * jnp.exp(dm - m_t)
        a_t = jnp.exp(g_inter - m_t)
        num = (jnp.dot(p.astype(BF16), vh.astype(BF16), preferred_element_type=F32)
               + a_t * jnp.dot(qb, c0.astype(BF16), preferred_element_type=F32))
        den = jnp.sum(p, -1, keepdims=True) + a_t * jnp.sum(qh * n0, -1, keepdims=True)
        out_ref[0, :, h * HD:(h + 1) * HD] = num / jnp.maximum(jnp.abs(den), jnp.exp(-m_t))

        w = b_last - bc + lic
        w_max = jnp.max(w, 0, keepdims=True)
        ew = jnp.exp(w - w_max)
        kv = _tn_dot(kb, (vh * ew).astype(BF16))
        ksum = jnp.sum(kh * ew, 0, keepdims=True)
        m_new = jnp.maximum(b_last + m0, w_max)
        a = jnp.exp(b_last + m0 - m_new)
        g = jnp.exp(w_max - m_new)
        st_ref[s, 0:HD, :] = a * c0 + g * kv
        st_ref[s, HD:HD + 1, :] = a * n0 + g * ksum
        st_ref[s, HD + 1:HD + 2, :] = jnp.broadcast_to(m_new, (1, HD))


def _mlstm_kernel(n_chunks, *refs):
    fwd = refs[0:8]
    bwd = refs[8:16]
    cw_ref, cb_ref, gbc_ref, gbr_ref, s0_ref = refs[16:21]
    hf_ref, hb_ref, sfin_ref, st_ref = refs[21:25]
    c = pl.program_id(1)

    @pl.when(c == 0)
    def _():
        st_ref[...] = s0_ref[0]

    L = ML_CHUNK
    row = lax.broadcasted_iota(jnp.int32, (L, L), 0)
    col = lax.broadcasted_iota(jnp.int32, (L, L), 1)
    tri_lo = (col <= row).astype(F32)
    tri_up = (col >= row).astype(F32)
    shared = (cw_ref, cb_ref, gbc_ref, gbr_ref, st_ref)
    _mlstm_direction(0, c, n_chunks, tri_lo, tri_up, *fwd, *shared, hf_ref)
    _mlstm_direction(1, n_chunks - 1 - c, n_chunks, tri_lo, tri_up, *bwd, *shared, hb_ref)

    @pl.when(c == n_chunks - 1)
    def _():
        sfin_ref[0] = st_ref[...]


def _mlstm(ml, gates, gates_t, cos, sin, conv_w, conv_b, gb_col, gb_row, state0):
    B, T, _ = ml.shape
    L = ML_CHUNK
    W = ML_WIDTH
    nc = T // L
    hb_per_chunk = L // 8
    n_hblk = T // 8

    def specs(chunk_of):
        ci = lambda c: chunk_of(c)
        return [
            pl.BlockSpec((1, L, 2 * W), lambda b, c: (b, ci(c), 0)),
            pl.BlockSpec((1, 8, 2 * W), lambda b, c: (b, jnp.maximum(ci(c) * hb_per_chunk - 1, 0), 0)),
            pl.BlockSpec((1, 8, 2 * W),
                         lambda b, c: (b, jnp.minimum((ci(c) + 1) * hb_per_chunk, n_hblk - 1), 0)),
            pl.BlockSpec((1, L, W), lambda b, c: (b, ci(c), 2)),
            pl.BlockSpec((1, L, LANE), lambda b, c: (b, ci(c), 0)),
            pl.BlockSpec((1, 4 * ML_HEADS, L), lambda b, c: (b, 0, ci(c))),
            pl.BlockSpec((L, LANE), lambda b, c: (ci(c), 0)),
            pl.BlockSpec((L, LANE), lambda b, c: (ci(c), 0)),
        ]

    dir_args = [ml, ml, ml, ml, gates, gates_t, cos, sin]
    whole = lambda shape: pl.BlockSpec(shape, lambda b, c: (0,) * len(shape))
    state_spec = pl.BlockSpec((1, 2 * ML_HEADS, STATE_ROWS, ML_HEAD_DIM), lambda b, c: (b, 0, 0, 0))
    return pl.pallas_call(
        functools.partial(_mlstm_kernel, nc),
        grid=(B, nc),
        in_specs=(specs(lambda c: c) + specs(lambda c: nc - 1 - c)
                  + [whole((3, 2 * W)), whole((1, 2 * W)), whole((1, 4 * ML_HEADS)),
                     whole((4 * ML_HEADS, 1)), state_spec]),
        out_specs=[pl.BlockSpec((1, L, W), lambda b, c: (b, c, 0)),
                   pl.BlockSpec((1, L, W), lambda b, c: (b, nc - 1 - c, 0)),
                   state_spec],
        out_shape=[jax.ShapeDtypeStruct((B, T, W), F32),
                   jax.ShapeDtypeStruct((B, T, W), F32),
                   jax.ShapeDtypeStruct(state0.shape, F32)],
        scratch_shapes=[pltpu.VMEM((2 * ML_HEADS, STATE_ROWS, ML_HEAD_DIM), F32)],
        compiler_params=_cparams("arbitrary", "arbitrary"),
        name="mlstm",
    )(*dir_args, *dir_args, conv_w, conv_b, gb_col, gb_row, state0)


def _softmax_pv(s_list, v_list):
    m = functools.reduce(jnp.maximum, [jnp.max(s, -1, keepdims=True) for s in s_list])
    ps = [jnp.exp(s - m) for s in s_list]
    l = functools.reduce(jnp.add, [jnp.sum(p, -1, keepdims=True) for p in ps])
    o = functools.reduce(jnp.add, [jnp.dot(p.astype(BF16), v, preferred_element_type=F32)
                                   for p, v in zip(ps, v_list)])
    return o / l


def _na_kernel(n_rows, q_ref, *refs):
    nb = NA_BAND // NA_KBLK_ROWS
    k_blks = refs[0:nb]
    v_blks = refs[nb:2 * nb]
    kc_ref, vc_ref, bt_ref, o_ref, kband, vband = refs[2 * nb:]
    i = pl.program_id(1)
    blk_tok = NA_KBLK_ROWS * GRID_W
    for j in range(nb):
        kband[j * blk_tok:(j + 1) * blk_tok, :] = k_blks[j][0]
        vband[j * blk_tok:(j + 1) * blk_tok, :] = v_blks[j][0]
    band_row0 = jnp.clip(i * NA_QROWS - NA_ROWS // 2, 0, n_rows - NA_BAND)
    n_win = NA_ROWS * GRID_W
    HD = NA_HEAD_DIM

    def row_body(a, carry):
        r = i * NA_QROWS + a
        r0 = jnp.clip(r - NA_ROWS // 2, 0, n_rows - NA_ROWS)
        koff = pl.multiple_of((r0 - band_row0) * GRID_W, GRID_W)
        dr_first = r0 - r + (NA_ROWS - 1)
        qoff = pl.multiple_of(a * GRID_W, GRID_W)
        for h in range(NA_HEADS):
            hs = slice(h * HD, (h + 1) * HD)
            qh = q_ref[0, pl.ds(qoff, GRID_W), hs]
            s_loc = _nt_dot(qh, kband[pl.ds(koff, n_win), hs]) + bt_ref[dr_first, h]
            s_ctx = _nt_dot(qh, kc_ref[0, :, hs])
            o = _softmax_pv([s_loc, s_ctx], [vband[pl.ds(koff, n_win), hs], vc_ref[0, :, hs]])
            o_ref[0, pl.ds(qoff, GRID_W), hs] = o.astype(BF16)
        return carry

    lax.fori_loop(0, NA_QROWS, row_body, 0)


def _na_latent(q, k, v, kc, vc, bias_tab):
    B, T, Wd = q.shape
    Lc = kc.shape[1]
    n_rows = T // GRID_W
    nb = NA_BAND // NA_KBLK_ROWS
    q_tok = NA_QROWS * GRID_W
    blk_tok = NA_KBLK_ROWS * GRID_W
    last_blk0 = (n_rows - NA_BAND) // NA_KBLK_ROWS

    def kv_spec(j):
        def idx(b, i):
            first = jnp.clip(i * (NA_QROWS // NA_KBLK_ROWS) - (NA_ROWS // 2) // NA_KBLK_ROWS, 0, last_blk0)
            return (b, first + j, 0)
        return pl.BlockSpec((1, blk_tok, Wd), idx)

    return pl.pallas_call(
        functools.partial(_na_kernel, n_rows),
        grid=(B, n_rows // NA_QROWS),
        in_specs=([pl.BlockSpec((1, q_tok, Wd), lambda b, i: (b, i, 0))]
                  + [kv_spec(j) for j in range(nb)] + [kv_spec(j) for j in range(nb)]
                  + [pl.BlockSpec((1, Lc, Wd), lambda b, i: (b, 0, 0)),
                     pl.BlockSpec((1, Lc, Wd), lambda b, i: (b, 0, 0)),
                     pl.BlockSpec(bias_tab.shape, lambda b, i: (0, 0, 0, 0))]),
        out_specs=pl.BlockSpec((1, q_tok, Wd), lambda b, i: (b, i, 0)),
        out_shape=jax.ShapeDtypeStruct((B, T, Wd), BF16),
        scratch_shapes=[pltpu.VMEM((NA_BAND * GRID_W, Wd), BF16),
                        pltpu.VMEM((NA_BAND * GRID_W, Wd), BF16)],
        compiler_params=_cparams("arbitrary", "arbitrary"),
        name="natten",
    )(q, *([k] * nb), *([v] * nb), kc, vc, bias_tab)


def _ctx_attn_kernel(q_ref, k_ref, v_ref, o_ref):
    HD = NA_HEAD_DIM
    for h in range(NA_HEADS):
        hs = slice(h * HD, (h + 1) * HD)
        s = _nt_dot(q_ref[0, :, hs], k_ref[0, :, hs])
        o_ref[0, :, hs] = _softmax_pv([s], [v_ref[0, :, hs]]).astype(BF16)


def _ctx_attn(q, k, v):
    B, Lc, Wd = q.shape
    spec = pl.BlockSpec((1, Lc, Wd), lambda b: (b, 0, 0))
    return pl.pallas_call(
        _ctx_attn_kernel, grid=(B,), in_specs=[spec, spec, spec], out_specs=spec,
        out_shape=jax.ShapeDtypeStruct((B, Lc, Wd), BF16),
        compiler_params=_cparams("arbitrary"), name="ctx_attn",
    )(q, k, v)


def _na_bias_table(rpb):
    qc = np.arange(GRID_W)[:, None]
    kcol = np.arange(GRID_W)[None, :]
    wstart = np.clip(qc - NA_COLS // 2, 0, GRID_W - NA_COLS)
    col_ok = (kcol >= wstart) & (kcol < wstart + NA_COLS)
    dc = np.clip(kcol - qc, -(NA_COLS - 1), NA_COLS - 1) + NA_COLS - 1
    per_dr = jnp.where(col_ok[None, None], rpb[:, :, dc], MASK_NEG)
    tabs = [jnp.transpose(per_dr[:, d:d + NA_ROWS], (0, 2, 1, 3)).reshape(NA_HEADS, GRID_W, NA_ROWS * GRID_W)
            for d in range(NA_ROWS)]
    return jnp.stack(tabs).astype(F32)


def _route(sel, s):
    E = EXP_PER_GROUP
    scores = []
    for g in range(N_GROUPS):
        a, b, c, d = sel[E * g:E * (g + 1)]
        scores.append(functools.reduce(jnp.maximum, [a + b, a + c, a + d, b + c, b + d, c + d]))
    best = jnp.zeros_like(scores[0], dtype=jnp.int32)
    best_score = scores[0]
    for g in range(1, N_GROUPS):
        upd = scores[g] > best_score
        best = jnp.where(upd, g, best)
        best_score = jnp.where(upd, scores[g], best_score)

    def pick(rows, j):
        out = rows[j]
        for g in range(1, N_GROUPS):
            out = jnp.where(best == g, rows[E * g + j], out)
        return out

    v = [pick(sel, j) for j in range(E)]
    sv = [pick(s, j) for j in range(E)]
    i1 = jnp.zeros_like(best)
    m1 = v[0]
    for j in range(1, E):
        upd = v[j] > m1
        i1 = jnp.where(upd, j, i1)
        m1 = jnp.where(upd, v[j], m1)
    i2 = jnp.where(i1 == 0, 1, 0)
    m2 = jnp.where(i1 == 0, v[1], v[0])
    for j in range(1, E):
        upd = (i1 != j) & (v[j] > m2)
        i2 = jnp.where(upd, j, i2)
        m2 = jnp.where(upd, v[j], m2)

    def at(rows, idx):
        out = rows[0]
        for j in range(1, E):
            out = jnp.where(idx == j, rows[j], out)
        return out

    s1 = at(sv, i1)
    s2 = at(sv, i2)
    tot = s1 + s2
    return best * E + i1, best * E + i2, s1 / tot, s2 / tot


def _out_proj_kernel(ygm_ref, hf_ref, hb_ref, o_ref, yna_ref, x_ref, mod_ref, w_ref, mlg_ref, n2g_ref,
                     rw_ref, rb_ref, xn_ref, h2_ref, re_ref, rwt_ref):
    hs = hf_ref[0] + hb_ref[0]
    hsq = hs * hs
    lane = lax.broadcasted_iota(jnp.int32, (1, ML_WIDTH), 1)
    scale = jnp.zeros_like(hs)
    for h in range(ML_HEADS):
        in_head = (lane >= h * ML_HEAD_DIM) & (lane < (h + 1) * ML_HEAD_DIM)
        ms = jnp.sum(jnp.where(in_head, hsq, 0.0), -1, keepdims=True) * (1.0 / ML_HEAD_DIM)
        scale = jnp.where(in_head, lax.rsqrt(ms + EPS), scale)
    yml = hs * scale * mlg_ref[...] * jax.nn.sigmoid(o_ref[0])
    o1 = GM_WIDTH
    o2 = GM_WIDTH + ML_WIDTH
    acc = (jnp.dot(ygm_ref[0], w_ref[0:o1, :], preferred_element_type=F32)
           + jnp.dot(yml.astype(BF16), w_ref[o1:o2, :], preferred_element_type=F32)
           + jnp.dot(yna_ref[0], w_ref[o2:, :], preferred_element_type=F32))
    xn = x_ref[0] + mod_ref[0, 0:1, :] * acc
    xn_ref[0] = xn
    h2 = xn * lax.rsqrt(jnp.mean(xn * xn, -1, keepdims=True) + EPS) * n2g_ref[...]
    h2 = h2 * (1.0 + mod_ref[0, 2:3, :]) + mod_ref[0, 1:2, :]
    h2_ref[0] = h2.astype(BF16)
    logits = _nt_dot(rw_ref[...], h2, precision=lax.Precision.HIGHEST)
    s = jax.nn.sigmoid(logits)
    sel = s + rb_ref[...]
    rows = lambda m: [m[e:e + 1, :] for e in range(N_EXPERTS)]
    e1, e2, w1, w2 = _route(rows(sel), rows(s))
    re_ref[0, 0:1, :] = e1
    re_ref[0, 1:2, :] = e2
    rwt_ref[0, 0:1, :] = w1
    rwt_ref[0, 1:2, :] = w2


def _out_proj(ygm, hf, hb, ml, yna, x, mod, w_out, ml_g, n2_g, router_wt, router_b):
    G, R, D = x.shape
    tm = min(R, 512)
    row = lambda b, i: (b, i, 0)
    whole2 = lambda shape: pl.BlockSpec(shape, lambda b, i: (0, 0))
    return pl.pallas_call(
        _out_proj_kernel,
        grid=(G, R // tm),
        in_specs=[pl.BlockSpec((1, tm, GM_WIDTH), row),
                  pl.BlockSpec((1, tm, ML_WIDTH), row),
                  pl.BlockSpec((1, tm, ML_WIDTH), row),
                  pl.BlockSpec((1, tm, ML_WIDTH), lambda b, i: (b, i, 3)),
                  pl.BlockSpec((1, tm, NA_WIDTH), row),
                  pl.BlockSpec((1, tm, D), row),
                  pl.BlockSpec((1, 3, D), lambda b, i: (b, 0, 0)),
                  whole2((D, D)), whole2((1, ML_WIDTH)), whole2((1, D)),
                  whole2((N_EXPERTS, D)), whole2((N_EXPERTS, 1))],
        out_specs=[pl.BlockSpec((1, tm, D), row), pl.BlockSpec((1, tm, D), row),
                   pl.BlockSpec((1, 2, tm), lambda b, i: (b, 0, i)),
                   pl.BlockSpec((1, 2, tm), lambda b, i: (b, 0, i))],
        out_shape=[jax.ShapeDtypeStruct((G, R, D), F32), jax.ShapeDtypeStruct((G, R, D), BF16),
                   jax.ShapeDtypeStruct((G, 2, R), jnp.int32), jax.ShapeDtypeStruct((G, 2, R), F32)],
        compiler_params=_cparams("arbitrary", "arbitrary"),
        name="out_proj",
    )(ygm, hf, hb, ml, yna, x, mod, w_out, ml_g, n2_g, router_wt, router_b)


def _experts_kernel(te_ref, nu_ref, x_ref, wg_ref, wu_ref, wd_ref, o_ref):
    i = pl.program_id(0)

    @pl.when(i < nu_ref[0])
    def _():
        x = x_ref[...]
        for j in range(D_EXPERT // EXPERT_CHUNK):
            cs = slice(j * EXPERT_CHUNK, (j + 1) * EXPERT_CHUNK)
            g = jnp.dot(x, wg_ref[0, :, cs], preferred_element_type=F32)
            u = jnp.dot(x, wu_ref[0, :, cs], preferred_element_type=F32)
            a = (g * jax.nn.sigmoid(g) * u).astype(BF16)
            y = jnp.dot(a, wd_ref[0, cs, :], preferred_element_type=F32)
            if j == 0:
                o_ref[...] = y
            else:
                o_ref[...] += y

    @pl.when(i >= nu_ref[0])
    def _():
        o_ref[...] = jnp.zeros_like(o_ref)


def _experts(tile_expert, n_used, xs, wg, wu, wd):
    buf, D = xs.shape
    n_tiles = buf // MOE_TILE
    wspec = lambda r, c: pl.BlockSpec((1, r, c), lambda i, te, nu: (te[i], 0, 0))
    return pl.pallas_call(
        _experts_kernel,
        grid_spec=pltpu.PrefetchScalarGridSpec(
            num_scalar_prefetch=2,
            grid=(n_tiles,),
            in_specs=[pl.BlockSpec((MOE_TILE, D), lambda i, te, nu: (i, 0)),
                      wspec(D, D_EXPERT), wspec(D, D_EXPERT), wspec(D_EXPERT, D)],
            out_specs=pl.BlockSpec((MOE_TILE, D), lambda i, te, nu: (i, 0))),
        out_shape=jax.ShapeDtypeStruct((buf, D), F32),
        compiler_params=_cparams("arbitrary"),
        name="experts",
    )(tile_expert, n_used, xs, wg, wu, wd)


def _dispatch(e_idx):
    n_tok = e_idx.shape[0]
    n_asg = 2 * n_tok
    n_tiles = -(-n_asg // MOE_TILE) + N_EXPERTS
    flat_e = e_idx.reshape(-1)
    onehot = (flat_e[:, None] == jnp.arange(N_EXPERTS, dtype=jnp.int32)[None, :]).astype(jnp.int32)
    csum = jnp.cumsum(onehot, axis=0)
    rank = jnp.take_along_axis(csum, flat_e[:, None], axis=1)[:, 0] - 1
    counts = csum[-1]
    tiles_e = (counts + MOE_TILE - 1) // MOE_TILE
    tile_end = jnp.cumsum(tiles_e)
    tile_start = tile_end - tiles_e
    dest = tile_start[flat_e] * MOE_TILE + rank
    n_used = tile_end[-1]
    tid = jnp.arange(n_tiles, dtype=jnp.int32)
    te = jnp.minimum(jnp.searchsorted(tile_end, tid, side="right"), N_EXPERTS - 1).astype(jnp.int32)
    te = jnp.where(tid < n_used, te, te[jnp.maximum(n_used - 1, 0)])
    flat_t = jnp.arange(n_asg, dtype=jnp.int32) // 2
    src = jnp.zeros((n_tiles * MOE_TILE,), jnp.int32).at[dest].set(flat_t)
    return dest.reshape(n_tok, 2), src, te, n_used.reshape(1).astype(jnp.int32)


def _combine_kernel(final, xn_ref, y0_ref, y1_ref, w_ref, g2_ref, fg_ref, o_ref):
    w = w_ref[0]
    f = y0_ref[0] * w[:, 0:1] + y1_ref[0] * w[:, 1:2]
    x = xn_ref[0] + g2_ref[0] * f
    if final:
        x = x * lax.rsqrt(jnp.mean(x * x, -1, keepdims=True) + EPS) * fg_ref[...]
    o_ref[0] = x


def _combine(xn, y0, y1, wt, g2, final_g, final):
    G, R, D = xn.shape
    tm = min(R, 512)
    row = lambda b, i: (b, i, 0)
    return pl.pallas_call(
        functools.partial(_combine_kernel, final),
        grid=(G, R // tm),
        in_specs=[pl.BlockSpec((1, tm, D), row), pl.BlockSpec((1, tm, D), row), pl.BlockSpec((1, tm, D), row),
                  pl.BlockSpec((1, tm, 2), row),
                  pl.BlockSpec((1, 1, D), lambda b, i: (b, 0, 0)),
                  pl.BlockSpec((1, D), lambda b, i: (0, 0))],
        out_specs=pl.BlockSpec((1, tm, D), row),
        out_shape=jax.ShapeDtypeStruct((G, R, D), F32),
        compiler_params=_cparams("arbitrary", "arbitrary"),
        name="combine_final" if final else "combine",
    )(xn, y0, y1, wt, g2, final_g)


def _rope_tables(T):
    lane = np.arange(LANE)
    half = ML_HEAD_DIM // 4
    inv = jnp.tile(ROPE_BASE ** (-jnp.arange(half, dtype=F32) / half), LANE // half)
    t = jnp.arange(T)
    pos = jnp.where(((lane // (2 * half)) % 2 == 0)[None, :], (t // GRID_W)[:, None], (t % GRID_W)[:, None])
    ang = pos.astype(F32) * inv[None, :]
    sign = np.where((lane // half) % 2 == 0, -1.0, 1.0).astype(np.float32)
    return jnp.cos(ang), jnp.sin(ang) * sign[None, :]


def _reorder_w_in(w_in):
    pad = jnp.zeros(w_in.shape[:2] + (LANE - 4 * ML_HEADS,), w_in.dtype)
    return jnp.concatenate([w_in[..., :OFF_GATES], w_in[..., OFF_NA:], w_in[..., OFF_GATES:OFF_NA], pad],
                           axis=-1).astype(BF16)


def _mixers(l, p, pc, prm, need_ctx):
    gm, ml, gt, q, k, v = p
    gmc, mlc, gtc, qc, kc, vc = pc
    B = ml.shape[0]
    gm_args = (prm["gm_ws"][l], prm["gm_bs_full"][l], prm["gm_g"][l])
    ml_args = (prm["conv_w"][l], prm["conv_b"][l], prm["gb_col"][l], prm["gb_row"][l])
    tr = lambda g: jnp.transpose(g[..., :4 * ML_HEADS], (0, 2, 1))
    zero_state = jnp.zeros((B, 2 * ML_HEADS, STATE_ROWS, ML_HEAD_DIM), F32)
    hfc, hbc, st = _mlstm(mlc, gtc, tr(gtc), prm["cos_c"], prm["sin_c"], *ml_args, zero_state)
    hf, hb, _ = _mlstm(ml, gt, tr(gt), prm["cos_l"], prm["sin_l"], *ml_args, st)
    y_na = _na_latent(q, k, v, kc, vc, prm["na_tab"][l])
    y = (_gmlp(gm, *gm_args), hf, hb, ml, y_na)
    if not need_ctx:
        return y, None
    return y, (_gmlp(gmc, *gm_args), hfc, hbc, mlc, _ctx_attn(qc, kc, vc))


def _moe(l, h2_all, e_all, prm):
    dest, src, te, n_used = _dispatch(e_all)
    xs = jnp.take(h2_all, src, axis=0)
    yb = _experts(te, n_used, xs, prm["wg"][l], prm["wu"][l], prm["wd"][l])
    return jnp.take(yb, dest[:, 0], axis=0), jnp.take(yb, dest[:, 1], axis=0)


def kernel(x, c, ctx, c_ctx, ada_w, ada_b, norm1_g, norm2_g, w_in, w_out, gm_ws, gm_bs, gm_norm_g,
           ml_conv_w, ml_conv_b, ml_gate_b, ml_norm_g, na_rpb, router_w, router_b,
           moe_w_gate, moe_w_up, moe_w_down, final_g):
    B, T, D = x.shape
    Lc = ctx.shape[1]
    cos_l, sin_l = _rope_tables(T)
    prm = dict(
        gm_ws=gm_ws.astype(BF16),
        gm_bs_full=jnp.repeat(jnp.transpose(gm_bs, (0, 2, 1)), GM_HEAD_DIM, axis=-1),
        gm_g=gm_norm_g[:, None, :],
        conv_w=ml_conv_w, conv_b=ml_conv_b[:, None, :],
        gb_col=ml_gate_b.reshape(DEPTH, 1, 4 * ML_HEADS), gb_row=ml_gate_b.reshape(DEPTH, 4 * ML_HEADS, 1),
        cos_l=cos_l, sin_l=sin_l,
        cos_c=jnp.ones((Lc, LANE), F32), sin_c=jnp.zeros((Lc, LANE), F32),
        na_tab=jnp.stack([_na_bias_table(na_rpb[l]) for l in range(DEPTH)]),
        wg=moe_w_gate.astype(BF16), wu=moe_w_up.astype(BF16), wd=moe_w_down.astype(BF16),
    )
    w_in_r = _reorder_w_in(w_in)
    w_out_b = w_out.astype(BF16)
    router_wt = jnp.transpose(router_w)
    router_bc = router_b[:, None]

    cs = jnp.concatenate([c, c_ctx[None, :], jnp.zeros((8 - B - 1, D), F32)], axis=0)
    mods = _ada_all(cs, ada_w, ada_b).reshape(DEPTH, 8, 6, D)

    xc = ctx
    for l in range(DEPTH):
        need_ctx = l < DEPTH - 1
        mod_l = mods[l, :B]
        mod_c = jnp.broadcast_to(mods[l, B:B + 1], (B, 6, D))
        n1 = norm1_g[l][None, :]
        p = _in_proj(x, mod_l[:, 0:2], n1, w_in_r[l])
        pc = _in_proj(xc, mod_c[:, 0:2], n1, w_in_r[l])
        y, yc = _mixers(l, p, pc, prm, need_ctx)
        op_args = (w_out_b[l], ml_norm_g[l][None, :], norm2_g[l][None, :], router_wt, router_bc)
        xn, h2, re, rw = _out_proj(*y, x, mod_l[:, 2:5], *op_args)
        h2_all = h2.reshape(B * T, D)
        e_all = jnp.transpose(re, (0, 2, 1)).reshape(B * T, 2)
        if need_ctx:
            xnc, h2c, rec, rwc = _out_proj(*yc, xc, mod_c[:, 2:5], *op_args)
            h2_all = jnp.concatenate([h2_all, h2c.reshape(B * Lc, D)], axis=0)
            e_all = jnp.concatenate([e_all, jnp.transpose(rec, (0, 2, 1)).reshape(B * Lc, 2)], axis=0)
        y0, y1 = _moe(l, h2_all, e_all, prm)
        final = l == DEPTH - 1
        fg = final_g[None, :]
        n_lat = B * T
        x = _combine(xn, y0[:n_lat].reshape(B, T, D), y1[:n_lat].reshape(B, T, D),
                     jnp.transpose(rw, (0, 2, 1)), mod_l[:, 5:6], fg, final)
        if need_ctx:
            xc = _combine(xnc, y0[n_lat:].reshape(B, Lc, D), y1[n_lat:].reshape(B, Lc, D),
                          jnp.transpose(rwc, (0, 2, 1)), mod_c[:, 5:6], fg, False)
    return x
```

```python
import functools

import jax
import jax.numpy as jnp
import numpy as np
from jax import lax
from jax.experimental import pallas as pl
from jax.experimental.pallas import tpu as pltpu

F32 = jnp.float32
BF16 = jnp.bfloat16

D_MODEL = 1024
DEPTH = 4
GRID_W = 64
EPS = 1e-6

GM_HEADS = 4
GM_WIDTH = D_MODEL // 4
GM_HEAD_DIM = GM_WIDTH // GM_HEADS
GM_CHUNK = 128

ML_HEADS = 4
ML_WIDTH = D_MODEL // 4
ML_HEAD_DIM = ML_WIDTH // ML_HEADS
ML_CHUNK = 128

NA_HEADS = 8
NA_WIDTH = D_MODEL // 2
NA_HEAD_DIM = NA_WIDTH // NA_HEADS
NA_ROWS = 8
NA_COLS = 16

ROPE_BASE = 10000.0

OFF_ML = 2 * GM_WIDTH
OFF_GATES = OFF_ML + 4 * ML_WIDTH
OFF_NA = OFF_GATES + 4 * ML_HEADS
N_IN = OFF_NA + 3 * NA_WIDTH

N_EXPERTS = 16
N_GROUPS = 4
EXP_PER_GROUP = N_EXPERTS // N_GROUPS
D_EXPERT = D_MODEL

LANE = 128
SEG_GM = (0, 2 * GM_WIDTH)
SEG_ML = (SEG_GM[1], SEG_GM[1] + 4 * ML_WIDTH)
SEG_Q = (SEG_ML[1], SEG_ML[1] + NA_WIDTH)
SEG_K = (SEG_Q[1], SEG_Q[1] + NA_WIDTH)
SEG_V = (SEG_K[1], SEG_K[1] + NA_WIDTH)
SEG_GATES = (SEG_V[1], SEG_V[1] + LANE)
N_IN_PAD = SEG_GATES[1]

NA_QROWS = 8
NA_BAND = 16
NA_KBLK_ROWS = 4
MOE_TILE = 512
EXPERT_CHUNK = 512
MASK_NEG = -1e30
N_GATE_ROWS = 6
VMEM_LIMIT = 56 * 2 ** 20


def _cparams(*sem):
    return pltpu.CompilerParams(dimension_semantics=sem, vmem_limit_bytes=VMEM_LIMIT)


def _nt_dot(a, b, precision=None):
    return lax.dot_general(a, b, (((1,), (1,)), ((), ())), precision=precision,
                           preferred_element_type=F32)


def _tn_dot(a, b):
    return lax.dot_general(a, b, (((0,), (0,)), ((), ())), preferred_element_type=F32)


def _ada_kernel(c_ref, w_ref, b_ref, o_ref):
    c = c_ref[...]
    sc = c * jax.nn.sigmoid(c)
    o_ref[0] = jnp.dot(sc, w_ref[0], preferred_element_type=F32) + b_ref[0]


def _ada_all(cs, ada_w, ada_b):
    n_out = ada_w.shape[-1] // D_MODEL
    return pl.pallas_call(
        _ada_kernel,
        grid=(DEPTH, n_out),
        in_specs=[pl.BlockSpec((8, D_MODEL), lambda l, j: (0, 0)),
                  pl.BlockSpec((1, D_MODEL, D_MODEL), lambda l, j: (l, 0, j)),
                  pl.BlockSpec((1, 1, D_MODEL), lambda l, j: (l, 0, j))],
        out_specs=pl.BlockSpec((1, 8, D_MODEL), lambda l, j: (l, 0, j)),
        out_shape=jax.ShapeDtypeStruct((DEPTH, 8, n_out * D_MODEL), F32),
        compiler_params=_cparams("arbitrary", "arbitrary"),
        name="adaln",
    )(cs, ada_w, ada_b.reshape(DEPTH, 1, -1))


def _in_proj_kernel(x_ref, mod_ref, g_ref, w_ref, gm_ref, ml_ref, gt_ref, q_ref, k_ref, v_ref):
    x = x_ref[0]
    h = x * lax.rsqrt(jnp.mean(x * x, -1, keepdims=True) + EPS) * g_ref[...]
    h = h * (1.0 + mod_ref[0, 1:2, :]) + mod_ref[0, 0:1, :]
    hb = h.astype(BF16)

    def seg(s):
        return jnp.dot(hb, w_ref[:, s[0]:s[1]], preferred_element_type=F32)

    gm_ref[0] = seg(SEG_GM)
    ml_ref[0] = seg(SEG_ML)
    gt_ref[0] = seg(SEG_GATES)
    q_ref[0] = (seg(SEG_Q) * (NA_HEAD_DIM ** -0.5)).astype(BF16)
    k_ref[0] = seg(SEG_K).astype(BF16)
    v_ref[0] = seg(SEG_V).astype(BF16)


def _in_proj(x, mod, g, w):
    G, R, D = x.shape
    tm = min(R, 512)
    row = lambda b, i: (b, i, 0)
    widths = (SEG_GM[1] - SEG_GM[0], SEG_ML[1] - SEG_ML[0], LANE, NA_WIDTH, NA_WIDTH, NA_WIDTH)
    dtypes = (F32, F32, F32, BF16, BF16, BF16)
    return pl.pallas_call(
        _in_proj_kernel,
        grid=(G, R // tm),
        in_specs=[pl.BlockSpec((1, tm, D), row),
                  pl.BlockSpec((1, 2, D), lambda b, i: (b, 0, 0)),
                  pl.BlockSpec((1, D), lambda b, i: (0, 0)),
                  pl.BlockSpec((D, N_IN_PAD), lambda b, i: (0, 0))],
        out_specs=[pl.BlockSpec((1, tm, n), row) for n in widths],
        out_shape=[jax.ShapeDtypeStruct((G, R, n), dt) for n, dt in zip(widths, dtypes)],
        compiler_params=_cparams("arbitrary", "arbitrary"),
        name="in_proj",
    )(x, mod, g, w)


def _gmlp_kernel(n_chunks, p_ref, ws_ref, bs_ref, g_ref, o_ref):
    lane = lax.broadcasted_iota(jnp.int32, (1, GM_WIDTH), 1)

    def chunk(j, carry):
        r0 = pl.multiple_of(j * GM_CHUNK, GM_CHUNK)
        z = jax.nn.gelu(p_ref[0, pl.ds(r0, GM_CHUNK), :])
        u = z[:, :GM_WIDTH]
        v = z[:, GM_WIDTH:]
        mu = jnp.mean(v, -1, keepdims=True)
        vc = v - mu
        var = jnp.mean(vc * vc, -1, keepdims=True)
        vn = (vc * lax.rsqrt(var + EPS) * g_ref[...]).astype(BF16)
        sv = jnp.zeros((GM_CHUNK, GM_WIDTH), F32)
        for h in range(GM_HEADS):
            full = jnp.dot(ws_ref[h], vn, preferred_element_type=F32)
            in_head = (lane >= h * GM_HEAD_DIM) & (lane < (h + 1) * GM_HEAD_DIM)
            sv = jnp.where(in_head, full, sv)
        o_ref[0, pl.ds(r0, GM_CHUNK), :] = (u * (sv + bs_ref[...])).astype(BF16)
        return carry

    lax.fori_loop(0, n_chunks, chunk, 0)


def _gmlp(p_gm, ws, bs_full, g):
    G, R, _ = p_gm.shape
    tr = min(R, 1024)
    return pl.pallas_call(
        functools.partial(_gmlp_kernel, tr // GM_CHUNK),
        grid=(G, R // tr),
        in_specs=[pl.BlockSpec((1, tr, 2 * GM_WIDTH), lambda b, i: (b, i, 0)),
                  pl.BlockSpec((GM_HEADS, GM_CHUNK, GM_CHUNK), lambda b, i: (0, 0, 0)),
                  pl.BlockSpec((GM_CHUNK, GM_WIDTH), lambda b, i: (0, 0)),
                  pl.BlockSpec((1, GM_WIDTH), lambda b, i: (0, 0))],
        out_specs=pl.BlockSpec((1, tr, GM_WIDTH), lambda b, i: (b, i, 0)),
        out_shape=jax.ShapeDtypeStruct((G, R, GM_WIDTH), BF16),
        compiler_params=_cparams("arbitrary", "arbitrary"),
        name="gmlp",
    )(p_gm, ws, bs_full, g)


def _log_sigmoid(x):
    return jnp.minimum(x, 0.0) - jnp.log1p(jnp.exp(-jnp.abs(x)))


def _rope(x, cos, sin_signed):
    lane = lax.broadcasted_iota(jnp.int32, (1, LANE), 1)
    first_half = (lane & 16) == 0
    partner = jnp.where(first_half, pltpu.roll(x, LANE - 16, 1), pltpu.roll(x, 16, 1))
    return x * cos + partner * sin_signed


def _chunk_scan(x, op, fill, reverse):
    pos = lax.broadcasted_iota(jnp.int32, (1, x.shape[1]), 1) % ML_CHUNK
    n = x.shape[1]
    sh = 1
    while sh < ML_CHUNK:
        if reverse:
            shifted = jnp.where(pos < ML_CHUNK - sh, pltpu.roll(x, n - sh, 1), fill)
        else:
            shifted = jnp.where(pos >= sh, pltpu.roll(x, sh, 1), fill)
        x = op(x, shifted)
        sh *= 2
    return x


def _mlstm_prep_kernel(n_blocks, x_ref, hp_ref, hn_ref, cos_ref, sin_ref, cw_ref, cb_ref, g_ref, gbias_ref,
                       q_ref, k_ref, go_ref):
    nh2 = 2 * ML_HEADS
    g = g_ref[0] + gbias_ref[...]
    is_fwd = lax.broadcasted_iota(jnp.int32, (nh2, 1), 0) < ML_HEADS
    li = g[0:nh2]
    lf = _log_sigmoid(g[nh2:])
    ps = _chunk_scan(lf, jnp.add, 0.0, False)
    ss = _chunk_scan(lf, jnp.add, 0.0, True)
    bcum = jnp.where(is_fwd, ps, ss)
    r = li - bcum
    pm = _chunk_scan(r, jnp.maximum, MASK_NEG, False)
    sm = _chunk_scan(r, jnp.maximum, MASK_NEG, True)
    rmax = jnp.maximum(pm, sm)
    groups = (bcum, r, jnp.where(is_fwd, pm, sm), jnp.exp(r - rmax), rmax, ps + ss - lf)
    for j, val in enumerate(groups):
        go_ref[0, nh2 * j:nh2 * (j + 1), :] = val

    i = pl.program_id(1)
    W = ML_WIDTH
    x = x_ref[0]
    tr = x.shape[0]
    rid = lax.broadcasted_iota(jnp.int32, (tr, 1), 0)
    i_row = jnp.zeros((1, 2 * W), jnp.int32) + i
    prev = jnp.where(i_row == 0, 0.0, hp_ref[0, 7:8, :])
    nxt = jnp.where(i_row == n_blocks - 1, 0.0, hn_ref[0, 0:1, :])
    xm1 = jnp.where(rid == 0, prev, pltpu.roll(x, 1, 0))
    xp1 = jnp.where(rid == tr - 1, nxt, pltpu.roll(x, tr - 1, 0))
    y = xm1 * cw_ref[0:1, :] + x * cw_ref[1:2, :] + xp1 * cw_ref[2:3, :] + cb_ref[...]
    y = y * jax.nn.sigmoid(y)
    cos = cos_ref[...]
    sin = sin_ref[...]
    parts = [_rope(y[:, j * LANE:(j + 1) * LANE], cos, sin) for j in range(2 * W // LANE)]
    q_ref[0] = (jnp.concatenate(parts[:W // LANE], axis=1) * (ML_HEAD_DIM ** -0.5)).astype(BF16)
    k_ref[0] = jnp.concatenate(parts[W // LANE:], axis=1).astype(BF16)


def _mlstm_prep(ml, cos, sin, conv_w, conv_b, gates_t, gbias):
    B, T, _ = ml.shape
    n_g = gates_t.shape[1]
    W = ML_WIDTH
    tr = min(T, 1024)
    nb = T // tr
    halo_per_blk = tr // 8
    n_hblk = T // 8
    whole = lambda shape: pl.BlockSpec(shape, lambda b, i: (0, 0))
    return pl.pallas_call(
        functools.partial(_mlstm_prep_kernel, nb),
        grid=(B, nb),
        in_specs=[pl.BlockSpec((1, tr, 2 * W), lambda b, i: (b, i, 0)),
                  pl.BlockSpec((1, 8, 2 * W), lambda b, i: (b, jnp.maximum(i * halo_per_blk - 1, 0), 0)),
                  pl.BlockSpec((1, 8, 2 * W),
                               lambda b, i: (b, jnp.minimum((i + 1) * halo_per_blk, n_hblk - 1), 0)),
                  pl.BlockSpec((tr, LANE), lambda b, i: (i, 0)),
                  pl.BlockSpec((tr, LANE), lambda b, i: (i, 0)),
                  whole((3, 2 * W)), whole((1, 2 * W)),
                  pl.BlockSpec((1, n_g, tr), lambda b, i: (b, 0, i)), whole((n_g, 1))],
        out_specs=[pl.BlockSpec((1, tr, W), lambda b, i: (b, i, 0)),
                   pl.BlockSpec((1, tr, W), lambda b, i: (b, i, 0)),
                   pl.BlockSpec((1, 2 * ML_HEADS * N_GATE_ROWS, tr), lambda b, i: (b, 0, i))],
        out_shape=[jax.ShapeDtypeStruct((B, T, W), BF16), jax.ShapeDtypeStruct((B, T, W), BF16),
                   jax.ShapeDtypeStruct((B, 2 * ML_HEADS * N_GATE_ROWS, T), F32)],
        compiler_params=_cparams("arbitrary", "arbitrary"),
        name="mlstm_prep",
    )(ml, ml, ml, cos, sin, conv_w, conv_b, gates_t, gbias)


def _mlstm_kernel(n_chunks, qf_ref, kf_ref, vf_ref, gf_ref, qb_ref, kb_ref, vb_ref, gb_ref,
                  m0_ref, s0_ref, hf_ref, hb_ref, mfin_ref, sfin_ref, m_scr, s_scr, bd_scr):
    c = pl.program_id(1)
    L = ML_CHUNK
    HD = ML_HEAD_DIM
    NH = ML_HEADS

    @pl.when(c == 0)
    def _():
        m_scr[...] = m0_ref[0]
        s_scr[...] = s0_ref[0]
        bd_scr[...] = jnp.zeros_like(bd_scr)

    is_fwd = lax.broadcasted_iota(jnp.int32, (2 * NH, 1), 0) < NH
    gq = lambda i: jnp.where(is_fwd, gf_ref[0, 2 * NH * i:2 * NH * (i + 1), :],
                             gb_ref[0, 2 * NH * i:2 * NH * (i + 1), :])
    bcum, r, rcmax, ew, rmax, b_last = (gq(i) for i in range(N_GATE_ROWS))
    m0 = m_scr[...]
    mu = jnp.maximum(m0, rcmax)
    a_t = jnp.exp(m0 - mu)
    emt = jnp.exp(-(bcum + mu))
    m_last = jnp.maximum(m0, rmax)
    a_st = jnp.exp(m0 - m_last)
    g_st = jnp.exp(rmax - m_last)
    m_scr[...] = b_last + m_last
    n_q = 4
    stack = jnp.concatenate([mu, a_t, emt, ew, jnp.zeros((L - n_q * 2 * NH, LANE), F32)], axis=0)
    cols = stack.T

    def col_bcast(q, j):
        return jnp.broadcast_to(cols[:, 8 * q + j:8 * q + j + 1], (L, LANE))

    def row_bcast(x, j):
        return jnp.broadcast_to(x[j:j + 1, :], (L, LANE))

    row = lax.broadcasted_iota(jnp.int32, (L, L), 0)
    col = lax.broadcasted_iota(jnp.int32, (L, L), 1)
    low_half = lax.broadcasted_iota(jnp.int32, (1, LANE), 1) < HD
    own_rows = jnp.concatenate([row < HD, row >= HD], axis=1)
    zero = jnp.zeros((), BF16)

    for d, (q_ref, k_ref, v_ref, out_ref) in enumerate(((qf_ref, kf_ref, vf_ref, hf_ref),
                                                        (qb_ref, kb_ref, vb_ref, hb_ref))):
        valid = (col <= row) if d == 0 else (col >= row)
        for pr in range(NH // 2):
            ps = slice(pr * LANE, (pr + 1) * LANE)
            ja = NH * d + 2 * pr
            jb = ja + 1
            sd = 2 * d + pr
            qp = q_ref[0, :, ps]
            kp = k_ref[0, :, ps]
            vp = v_ref[0, :, ps]
            st = s_scr[sd]
            kcat = jnp.concatenate([jnp.where(low_half, kp, zero), jnp.where(low_half, zero, kp)], axis=0)
            s2 = _nt_dot(qp, kcat)
            e2 = jnp.concatenate(
                [jnp.where(valid, jnp.exp(row_bcast(r, j) - col_bcast(0, j)), 0.0) for j in (ja, jb)], axis=1)
            p2 = (s2 * e2).astype(BF16)
            va = jnp.where(low_half, vp, 1.0)
            vb = jnp.where(low_half, 1.0, vp)
            bd_scr[0:L, 0:LANE] = va.astype(BF16)
            bd_scr[L:2 * L, LANE:2 * LANE] = vb.astype(BF16)
            intra = jnp.dot(p2, bd_scr[...], preferred_element_type=F32)
            inter = jnp.dot(qp, st.astype(BF16), preferred_element_type=F32)
            a2 = jnp.concatenate([col_bcast(1, ja), col_bcast(1, jb)], axis=1)
            tot = intra + a2 * inter
            tot_a = tot[:, 0:LANE]
            tot_b = tot[:, LANE:2 * LANE]
            num = jnp.where(low_half, tot_a, tot_b)
            den = pltpu.roll(jnp.where(low_half, tot_b, tot_a), HD, 1)
            floor = jnp.where(low_half, col_bcast(2, ja), col_bcast(2, jb))
            out_ref[0, :, ps] = num / jnp.maximum(jnp.abs(den), floor)

            vw = jnp.concatenate([va * col_bcast(3, ja), vb * col_bcast(3, jb)], axis=1).astype(BF16)
            upd = _tn_dot(kp, vw)
            a_s = jnp.concatenate([row_bcast(a_st, ja), row_bcast(a_st, jb)], axis=1)
            g_s = jnp.concatenate([row_bcast(g_st, ja), row_bcast(g_st, jb)], axis=1)
            s_scr[sd] = a_s * st + g_s * jnp.where(own_rows, upd, 0.0)

    @pl.when(c == n_chunks - 1)
    def _():
        mfin_ref[0] = m_scr[...]
        sfin_ref[0] = s_scr[...]


def _mlstm(q, k, ml, grows, m0, s0):
    B, T, W = q.shape
    L = ML_CHUNK
    nc = T // L

    def specs(ci):
        return [pl.BlockSpec((1, L, W), lambda b, c: (b, ci(c), 0)),
                pl.BlockSpec((1, L, W), lambda b, c: (b, ci(c), 0)),
                pl.BlockSpec((1, L, W), lambda b, c: (b, ci(c), 2)),
                pl.BlockSpec((1, grows.shape[1], L), lambda b, c: (b, 0, ci(c)))]

    m_spec = pl.BlockSpec((1,) + m0.shape[1:], lambda b, c: (b, 0, 0))
    s_spec = pl.BlockSpec((1,) + s0.shape[1:], lambda b, c: (b, 0, 0, 0))
    dir_args = [q, k, ml, grows]
    return pl.pallas_call(
        functools.partial(_mlstm_kernel, nc),
        grid=(B, nc),
        in_specs=specs(lambda c: c) + specs(lambda c: nc - 1 - c) + [m_spec, s_spec],
        out_specs=[pl.BlockSpec((1, L, W), lambda b, c: (b, c, 0)),
                   pl.BlockSpec((1, L, W), lambda b, c: (b, nc - 1 - c, 0)),
                   m_spec, s_spec],
        out_shape=[jax.ShapeDtypeStruct((B, T, W), F32),
                   jax.ShapeDtypeStruct((B, T, W), F32),
                   jax.ShapeDtypeStruct(m0.shape, F32),
                   jax.ShapeDtypeStruct(s0.shape, F32)],
        scratch_shapes=[pltpu.VMEM(m0.shape[1:], F32), pltpu.VMEM(s0.shape[1:], F32),
                        pltpu.VMEM((2 * L, 2 * LANE), BF16)],
        compiler_params=_cparams("arbitrary", "arbitrary"),
        name="mlstm",
    )(*dir_args, *dir_args, m0, s0)


def _softmax_pv(s_list, v_list):
    m = functools.reduce(jnp.maximum, [jnp.max(s, -1, keepdims=True) for s in s_list])
    ps = [jnp.exp(s - m) for s in s_list]
    l = functools.reduce(jnp.add, [jnp.sum(p, -1, keepdims=True) for p in ps])
    o = functools.reduce(jnp.add, [jnp.dot(p.astype(BF16), v, preferred_element_type=F32)
                                   for p, v in zip(ps, v_list)])
    return o / l


def _na_kernel(n_rows, q_ref, *refs):
    nb = NA_BAND // NA_KBLK_ROWS
    k_blks = refs[0:nb]
    v_blks = refs[nb:2 * nb]
    kc_ref, vc_ref, bt_ref, o_ref, kband, vband = refs[2 * nb:]
    i = pl.program_id(1)
    blk_tok = NA_KBLK_ROWS * GRID_W
    HD = NA_HEAD_DIM
    low_half = lax.broadcasted_iota(jnp.int32, (1, LANE), 1) < HD
    for j in range(nb):
        kband[j * blk_tok:(j + 1) * blk_tok, :] = k_blks[j][0]
        vband[j * blk_tok:(j + 1) * blk_tok, :] = v_blks[j][0]
    band_row0 = jnp.clip(i * NA_QROWS - NA_ROWS // 2, 0, n_rows - NA_BAND)
    n_win = NA_ROWS * GRID_W
    zero = jnp.zeros((), BF16)

    def row_body(a, carry):
        r = i * NA_QROWS + a
        r0 = jnp.clip(r - NA_ROWS // 2, 0, n_rows - NA_ROWS)
        koff = pl.multiple_of((r0 - band_row0) * GRID_W, GRID_W)
        dr_first = r0 - r + (NA_ROWS - 1)
        qoff = pl.multiple_of(a * GRID_W, GRID_W)
        for pr in range(NA_HEADS // 2):
            ps = slice(pr * LANE, (pr + 1) * LANE)
            qp = q_ref[0, pl.ds(qoff, GRID_W), ps]
            qm = jnp.concatenate([jnp.where(low_half, qp, zero), jnp.where(low_half, zero, qp)], axis=0)
            s_loc = _nt_dot(kband[pl.ds(koff, n_win), ps], qm) + bt_ref[dr_first, pr]
            s_ctx = _nt_dot(kc_ref[0, :, ps], qm)
            m = jnp.maximum(jnp.max(s_loc, 0, keepdims=True), jnp.max(s_ctx, 0, keepdims=True))
            e_loc = jnp.exp(s_loc - m)
            e_ctx = jnp.exp(s_ctx - m)
            inv = 1.0 / (jnp.sum(e_loc, 0, keepdims=True) + jnp.sum(e_ctx, 0, keepdims=True))
            res = (_tn_dot((e_loc * inv).astype(BF16), vband[pl.ds(koff, n_win), ps])
                   + _tn_dot((e_ctx * inv).astype(BF16), vc_ref[0, :, ps]))
            out = jnp.where(low_half, res[0:GRID_W], res[GRID_W:2 * GRID_W])
            o_ref[0, pl.ds(qoff, GRID_W), ps] = out.astype(BF16)
        return carry

    lax.fori_loop(0, NA_QROWS, row_body, 0)


def _na_latent(q, k, v, kc, vc, bias_tab):
    B, T, Wd = q.shape
    Lc = kc.shape[1]
    n_rows = T // GRID_W
    nb = NA_BAND // NA_KBLK_ROWS
    q_tok = NA_QROWS * GRID_W
    blk_tok = NA_KBLK_ROWS * GRID_W
    last_blk0 = (n_rows - NA_BAND) // NA_KBLK_ROWS

    def kv_spec(j):
        def idx(b, i):
            first = jnp.clip(i * (NA_QROWS // NA_KBLK_ROWS) - (NA_ROWS // 2) // NA_KBLK_ROWS, 0, last_blk0)
            return (b, first + j, 0)
        return pl.BlockSpec((1, blk_tok, Wd), idx)

    return pl.pallas_call(
        functools.partial(_na_kernel, n_rows),
        grid=(B, n_rows // NA_QROWS),
        in_specs=([pl.BlockSpec((1, q_tok, Wd), lambda b, i: (b, i, 0))]
                  + [kv_spec(j) for j in range(nb)] + [kv_spec(j) for j in range(nb)]
                  + [pl.BlockSpec((1, Lc, Wd), lambda b, i: (b, 0, 0)),
                     pl.BlockSpec((1, Lc, Wd), lambda b, i: (b, 0, 0)),
                     pl.BlockSpec(bias_tab.shape, lambda b, i: (0, 0, 0, 0))]),
        out_specs=pl.BlockSpec((1, q_tok, Wd), lambda b, i: (b, i, 0)),
        out_shape=jax.ShapeDtypeStruct((B, T, Wd), BF16),
        scratch_shapes=[pltpu.VMEM((NA_BAND * GRID_W, Wd), BF16),
                        pltpu.VMEM((NA_BAND * GRID_W, Wd), BF16)],
        compiler_params=_cparams("arbitrary", "arbitrary"),
        name="natten",
    )(q, *([k] * nb), *([v] * nb), kc, vc, bias_tab)


def _ctx_attn_kernel(q_ref, k_ref, v_ref, o_ref):
    HD = NA_HEAD_DIM
    for h in range(NA_HEADS):
        hs = slice(h * HD, (h + 1) * HD)
        s = _nt_dot(q_ref[0, :, hs], k_ref[0, :, hs])
        o_ref[0, :, hs] = _softmax_pv([s], [v_ref[0, :, hs]]).astype(BF16)


def _ctx_attn(q, k, v):
    B, Lc, Wd = q.shape
    spec = pl.BlockSpec((1, Lc, Wd), lambda b: (b, 0, 0))
    return pl.pallas_call(
        _ctx_attn_kernel, grid=(B,), in_specs=[spec, spec, spec], out_specs=spec,
        out_shape=jax.ShapeDtypeStruct((B, Lc, Wd), BF16),
        compiler_params=_cparams("arbitrary"), name="ctx_attn",
    )(q, k, v)


def _na_bias_table(rpb):
    qc = np.arange(GRID_W)[:, None]
    kcol = np.arange(GRID_W)[None, :]
    wstart = np.clip(qc - NA_COLS // 2, 0, GRID_W - NA_COLS)
    col_ok = (kcol >= wstart) & (kcol < wstart + NA_COLS)
    dc = np.clip(kcol - qc, -(NA_COLS - 1), NA_COLS - 1) + NA_COLS - 1
    per_dr = jnp.where(col_ok[None, None], rpb[:, :, dc], MASK_NEG)
    def tab(d):
        t = per_dr[:, d:d + NA_ROWS].reshape(NA_HEADS // 2, 2, NA_ROWS, GRID_W, GRID_W)
        return jnp.transpose(t, (0, 2, 4, 1, 3)).reshape(NA_HEADS // 2, NA_ROWS * GRID_W, 2 * GRID_W)

    return jnp.stack([tab(d) for d in range(NA_ROWS)]).astype(F32)


def _route(sel, s):
    E = EXP_PER_GROUP
    scores = []
    for g in range(N_GROUPS):
        a, b, c, d = sel[E * g:E * (g + 1)]
        scores.append(functools.reduce(jnp.maximum, [a + b, a + c, a + d, b + c, b + d, c + d]))
    best = jnp.zeros_like(scores[0], dtype=jnp.int32)
    best_score = scores[0]
    for g in range(1, N_GROUPS):
        upd = scores[g] > best_score
        best = jnp.where(upd, g, best)
        best_score = jnp.where(upd, scores[g], best_score)

    def pick(rows, j):
        out = rows[j]
        for g in range(1, N_GROUPS):
            out = jnp.where(best == g, rows[E * g + j], out)
        return out

    v = [pick(sel, j) for j in range(E)]
    sv = [pick(s, j) for j in range(E)]
    i1 = jnp.zeros_like(best)
    m1 = v[0]
    for j in range(1, E):
        upd = v[j] > m1
        i1 = jnp.where(upd, j, i1)
        m1 = jnp.where(upd, v[j], m1)
    i2 = jnp.where(i1 == 0, 1, 0)
    m2 = jnp.where(i1 == 0, v[1], v[0])
    for j in range(1, E):
        upd = (i1 != j) & (v[j] > m2)
        i2 = jnp.where(upd, j, i2)
        m2 = jnp.where(upd, v[j], m2)

    def at(rows, idx):
        out = rows[0]
        for j in range(1, E):
            out = jnp.where(idx == j, rows[j], out)
        return out

    s1 = at(sv, i1)
    s2 = at(sv, i2)
    tot = s1 + s2
    return best * E + i1, best * E + i2, s1 / tot, s2 / tot


def _out_proj_kernel(ygm_ref, hf_ref, hb_ref, o_ref, yna_ref, x_ref, mod_ref, w_ref, mlg_ref, n2g_ref,
                     rw_ref, rb_ref, xn_ref, h2_ref, re_ref, rwt_ref):
    hs = hf_ref[0] + hb_ref[0]
    hsq = hs * hs
    lane = lax.broadcasted_iota(jnp.int32, (1, ML_WIDTH), 1)
    scale = jnp.zeros_like(hs)
    for h in range(ML_HEADS):
        in_head = (lane >= h * ML_HEAD_DIM) & (lane < (h + 1) * ML_HEAD_DIM)
        ms = jnp.sum(jnp.where(in_head, hsq, 0.0), -1, keepdims=True) * (1.0 / ML_HEAD_DIM)
        scale = jnp.where(in_head, lax.rsqrt(ms + EPS), scale)
    yml = hs * scale * mlg_ref[...] * jax.nn.sigmoid(o_ref[0])
    o1 = GM_WIDTH
    o2 = GM_WIDTH + ML_WIDTH
    acc = (jnp.dot(ygm_ref[0], w_ref[0:o1, :], preferred_element_type=F32)
           + jnp.dot(yml.astype(BF16), w_ref[o1:o2, :], preferred_element_type=F32)
           + jnp.dot(yna_ref[0], w_ref[o2:, :], preferred_element_type=F32))
    xn = x_ref[0] + mod_ref[0, 0:1, :] * acc
    xn_ref[0] = xn
    h2 = xn * lax.rsqrt(jnp.mean(xn * xn, -1, keepdims=True) + EPS) * n2g_ref[...]
    h2 = h2 * (1.0 + mod_ref[0, 2:3, :]) + mod_ref[0, 1:2, :]
    h2_ref[0] = h2.astype(BF16)
    logits = _nt_dot(rw_ref[...], h2, precision=lax.Precision.HIGHEST)
    s = jax.nn.sigmoid(logits)
    sel = s + rb_ref[...]
    rows = lambda m: [m[e:e + 1, :] for e in range(N_EXPERTS)]
    e1, e2, w1, w2 = _route(rows(sel), rows(s))
    re_ref[0, 0:1, :] = e1
    re_ref[0, 1:2, :] = e2
    rwt_ref[0, 0:1, :] = w1
    rwt_ref[0, 1:2, :] = w2


def _out_proj(ygm, hf, hb, ml, yna, x, mod, w_out, ml_g, n2_g, router_wt, router_b):
    G, R, D = x.shape
    tm = min(R, 512)
    row = lambda b, i: (b, i, 0)
    whole2 = lambda shape: pl.BlockSpec(shape, lambda b, i: (0, 0))
    return pl.pallas_call(
        _out_proj_kernel,
        grid=(G, R // tm),
        in_specs=[pl.BlockSpec((1, tm, GM_WIDTH), row),
                  pl.BlockSpec((1, tm, ML_WIDTH), row),
                  pl.BlockSpec((1, tm, ML_WIDTH), row),
                  pl.BlockSpec((1, tm, ML_WIDTH), lambda b, i: (b, i, 3)),
                  pl.BlockSpec((1, tm, NA_WIDTH), row),
                  pl.BlockSpec((1, tm, D), row),
                  pl.BlockSpec((1, 3, D), lambda b, i: (b, 0, 0)),
                  whole2((D, D)), whole2((1, ML_WIDTH)), whole2((1, D)),
                  whole2((N_EXPERTS, D)), whole2((N_EXPERTS, 1))],
        out_specs=[pl.BlockSpec((1, tm, D), row), pl.BlockSpec((1, tm, D), row),
                   pl.BlockSpec((1, 2, tm), lambda b, i: (b, 0, i)),
                   pl.BlockSpec((1, 2, tm), lambda b, i: (b, 0, i))],
        out_shape=[jax.ShapeDtypeStruct((G, R, D), F32), jax.ShapeDtypeStruct((G, R, D), BF16),
                   jax.ShapeDtypeStruct((G, 2, R), jnp.int32), jax.ShapeDtypeStruct((G, 2, R), F32)],
        compiler_params=_cparams("arbitrary", "arbitrary"),
        name="out_proj",
    )(ygm, hf, hb, ml, yna, x, mod, w_out, ml_g, n2_g, router_wt, router_b)


def _experts_kernel(te_ref, nu_ref, x_ref, wg_ref, wu_ref, wd_ref, o_ref, acc_ref):
    i = pl.program_id(0)

    @pl.when(i < nu_ref[0])
    def _():
        x = x_ref[...]
        for j in range(D_EXPERT // EXPERT_CHUNK):
            cs = slice(j * EXPERT_CHUNK, (j + 1) * EXPERT_CHUNK)
            g = jnp.dot(x, wg_ref[0, :, cs], preferred_element_type=F32)
            u = jnp.dot(x, wu_ref[0, :, cs], preferred_element_type=F32)
            a = (g * jax.nn.sigmoid(g) * u).astype(BF16)
            y = jnp.dot(a, wd_ref[0, cs, :], preferred_element_type=F32)
            if j == 0:
                acc_ref[...] = y
            else:
                acc_ref[...] += y
        o_ref[...] = acc_ref[...].astype(BF16)

    @pl.when(i >= nu_ref[0])
    def _():
        o_ref[...] = jnp.zeros_like(o_ref)


def _experts(tile_expert, n_used, xs, wg, wu, wd):
    buf, D = xs.shape
    n_tiles = buf // MOE_TILE
    wspec = lambda r, c: pl.BlockSpec((1, r, c), lambda i, te, nu: (te[i], 0, 0))
    return pl.pallas_call(
        _experts_kernel,
        grid_spec=pltpu.PrefetchScalarGridSpec(
            num_scalar_prefetch=2,
            grid=(n_tiles,),
            in_specs=[pl.BlockSpec((MOE_TILE, D), lambda i, te, nu: (i, 0)),
                      wspec(D, D_EXPERT), wspec(D, D_EXPERT), wspec(D_EXPERT, D)],
            out_specs=pl.BlockSpec((MOE_TILE, D), lambda i, te, nu: (i, 0)),
            scratch_shapes=[pltpu.VMEM((MOE_TILE, D), F32)]),
        out_shape=jax.ShapeDtypeStruct((buf, D), BF16),
        compiler_params=_cparams("arbitrary"),
        name="experts",
    )(tile_expert, n_used, xs, wg, wu, wd)


def _dispatch(e_idx):
    n_tok = e_idx.shape[0]
    n_asg = 2 * n_tok
    n_tiles = -(-n_asg // MOE_TILE) + N_EXPERTS
    flat_e = e_idx.reshape(-1)
    onehot = (flat_e[:, None] == jnp.arange(N_EXPERTS, dtype=jnp.int32)[None, :]).astype(jnp.int32)
    csum = jnp.cumsum(onehot, axis=0)
    rank = jnp.take_along_axis(csum, flat_e[:, None], axis=1)[:, 0] - 1
    counts = csum[-1]
    tiles_e = (counts + MOE_TILE - 1) // MOE_TILE
    tile_end = jnp.cumsum(tiles_e)
    tile_start = tile_end - tiles_e
    dest = tile_start[flat_e] * MOE_TILE + rank
    n_used = tile_end[-1]
    tid = jnp.arange(n_tiles, dtype=jnp.int32)
    te = jnp.minimum(jnp.searchsorted(tile_end, tid, side="right"), N_EXPERTS - 1).astype(jnp.int32)
    te = jnp.where(tid < n_used, te, te[jnp.maximum(n_used - 1, 0)])
    flat_t = jnp.arange(n_asg, dtype=jnp.int32) // 2
    src = jnp.zeros((n_tiles * MOE_TILE,), jnp.int32).at[dest].set(
        flat_t, unique_indices=True, mode="promise_in_bounds")
    return dest.reshape(n_tok, 2), src, te, n_used.reshape(1).astype(jnp.int32)


def _combine_kernel(final, xn_ref, y0_ref, y1_ref, w_ref, g2_ref, fg_ref, o_ref):
    w = w_ref[0]
    f = y0_ref[0].astype(F32) * w[:, 0:1] + y1_ref[0].astype(F32) * w[:, 1:2]
    x = xn_ref[0] + g2_ref[0] * f
    if final:
        x = x * lax.rsqrt(jnp.mean(x * x, -1, keepdims=True) + EPS) * fg_ref[...]
    o_ref[0] = x


def _combine(xn, y0, y1, wt, g2, final_g, final):
    G, R, D = xn.shape
    tm = min(R, 512)
    row = lambda b, i: (b, i, 0)
    return pl.pallas_call(
        functools.partial(_combine_kernel, final),
        grid=(G, R // tm),
        in_specs=[pl.BlockSpec((1, tm, D), row), pl.BlockSpec((1, tm, D), row), pl.BlockSpec((1, tm, D), row),
                  pl.BlockSpec((1, tm, 2), row),
                  pl.BlockSpec((1, 1, D), lambda b, i: (b, 0, 0)),
                  pl.BlockSpec((1, D), lambda b, i: (0, 0))],
        out_specs=pl.BlockSpec((1, tm, D), row),
        out_shape=jax.ShapeDtypeStruct((G, R, D), F32),
        compiler_params=_cparams("arbitrary", "arbitrary"),
        name="combine_final" if final else "combine",
    )(xn, y0, y1, wt, g2, final_g)


def _rope_tables(T):
    lane = np.arange(LANE)
    half = ML_HEAD_DIM // 4
    inv = jnp.tile(ROPE_BASE ** (-jnp.arange(half, dtype=F32) / half), LANE // half)
    t = jnp.arange(T)
    pos = jnp.where(((lane // (2 * half)) % 2 == 0)[None, :], (t // GRID_W)[:, None], (t % GRID_W)[:, None])
    ang = pos.astype(F32) * inv[None, :]
    sign = np.where((lane // half) % 2 == 0, -1.0, 1.0).astype(np.float32)
    return jnp.cos(ang), jnp.sin(ang) * sign[None, :]


def _reorder_w_in(w_in):
    pad = jnp.zeros(w_in.shape[:2] + (LANE - 4 * ML_HEADS,), w_in.dtype)
    return jnp.concatenate([w_in[..., :OFF_GATES], w_in[..., OFF_NA:], w_in[..., OFF_GATES:OFF_NA], pad],
                           axis=-1).astype(BF16)


def _mixers(l, p, pc, prm, need_ctx):
    gm, ml, gt, q, k, v = p
    gmc, mlc, gtc, qc, kc, vc = pc
    B = ml.shape[0]
    gm_args = (prm["gm_ws"][l], prm["gm_bs_full"][l], prm["gm_g"][l])
    conv = (prm["conv_w"][l], prm["conv_b"][l])
    nh = ML_HEADS
    tr = lambda g: jnp.transpose(
        jnp.concatenate([g[..., 0:nh], g[..., 2 * nh:3 * nh], g[..., nh:2 * nh], g[..., 3 * nh:4 * nh]], -1),
        (0, 2, 1))
    m_zero = jnp.zeros((B, 2 * nh, LANE), F32)
    s_zero = jnp.zeros((B, nh, ML_CHUNK, 2 * LANE), F32)
    qmc, kmc, grc = _mlstm_prep(mlc, prm["cos_c"], prm["sin_c"], *conv, tr(gtc), prm["gbias"][l])
    hfc, hbc, m_st, s_st = _mlstm(qmc, kmc, mlc, grc, m_zero, s_zero)
    qm, km, gr = _mlstm_prep(ml, prm["cos_l"], prm["sin_l"], *conv, tr(gt), prm["gbias"][l])
    hf, hb, _, _ = _mlstm(qm, km, ml, gr, m_st, s_st)
    y_na = _na_latent(q, k, v, kc, vc, prm["na_tab"][l])
    y = (_gmlp(gm, *gm_args), hf, hb, ml, y_na)
    if not need_ctx:
        return y, None
    return y, (_gmlp(gmc, *gm_args), hfc, hbc, mlc, _ctx_attn(qc, kc, vc))


def _rows(a, idx):
    return a.at[idx].get(mode="promise_in_bounds")


def _moe(l, h2_all, e_all, prm):
    dest, src, te, n_used = _dispatch(e_all)
    yb = _experts(te, n_used, _rows(h2_all, src), prm["wg"][l], prm["wu"][l], prm["wd"][l])
    return yb, dest


def kernel(x, c, ctx, c_ctx, ada_w, ada_b, norm1_g, norm2_g, w_in, w_out, gm_ws, gm_bs, gm_norm_g,
           ml_conv_w, ml_conv_b, ml_gate_b, ml_norm_g, na_rpb, router_w, router_b,
           moe_w_gate, moe_w_up, moe_w_down, final_g):
    B, T, D = x.shape
    Lc = ctx.shape[1]
    cos_l, sin_l = _rope_tables(T)
    prm = dict(
        gm_ws=gm_ws.astype(BF16),
        gm_bs_full=jnp.repeat(jnp.transpose(gm_bs, (0, 2, 1)), GM_HEAD_DIM, axis=-1),
        gm_g=gm_norm_g[:, None, :],
        conv_w=ml_conv_w, conv_b=ml_conv_b[:, None, :],
        gbias=ml_gate_b[:, jnp.array([0, 2, 1, 3])].reshape(DEPTH, 4 * ML_HEADS, 1),
        cos_l=cos_l, sin_l=sin_l,
        cos_c=jnp.ones((Lc, LANE), F32), sin_c=jnp.zeros((Lc, LANE), F32),
        na_tab=jnp.stack([_na_bias_table(na_rpb[l]) for l in range(DEPTH)]),
        wg=moe_w_gate.astype(BF16), wu=moe_w_up.astype(BF16), wd=moe_w_down.astype(BF16),
    )
    w_in_r = _reorder_w_in(w_in)
    w_out_b = w_out.astype(BF16)
    router_wt = jnp.transpose(router_w)
    router_bc = router_b[:, None]

    cs = jnp.concatenate([c, c_ctx[None, :], jnp.zeros((8 - B - 1, D), F32)], axis=0)
    mods = _ada_all(cs, ada_w, ada_b).reshape(DEPTH, 8, 6, D)

    xc = ctx
    for l in range(DEPTH):
        need_ctx = l < DEPTH - 1
        mod_l = mods[l, :B]
        mod_c = jnp.broadcast_to(mods[l, B:B + 1], (B, 6, D))
        n1 = norm1_g[l][None, :]
        p = _in_proj(x, mod_l[:, 0:2], n1, w_in_r[l])
        pc = _in_proj(xc, mod_c[:, 0:2], n1, w_in_r[l])
        y, yc = _mixers(l, p, pc, prm, need_ctx)
        op_args = (w_out_b[l], ml_norm_g[l][None, :], norm2_g[l][None, :], router_wt, router_bc)
        xn, h2, re, rw = _out_proj(*y, x, mod_l[:, 2:5], *op_args)
        h2_all = h2.reshape(B * T, D)
        e_all = jnp.transpose(re, (0, 2, 1)).reshape(B * T, 2)
        if need_ctx:
            xnc, h2c, rec, rwc = _out_proj(*yc, xc, mod_c[:, 2:5], *op_args)
            h2_all = jnp.concatenate([h2_all, h2c.reshape(B * Lc, D)], axis=0)
            e_all = jnp.concatenate([e_all, jnp.transpose(rec, (0, 2, 1)).reshape(B * Lc, 2)], axis=0)
        yb, dest = _moe(l, h2_all, e_all, prm)
        final = l == DEPTH - 1
        fg = final_g[None, :]
        n_lat = B * T
        picked = lambda lo, hi, k, L: _rows(yb, dest[lo:hi, k]).reshape(B, L, D)
        x = _combine(xn, picked(0, n_lat, 0, T), picked(0, n_lat, 1, T),
                     jnp.transpose(rw, (0, 2, 1)), mod_l[:, 5:6], fg, final)
        if need_ctx:
            xc = _combine(xnc, picked(n_lat, None, 0, Lc), picked(n_lat, None, 1, Lc),
                          jnp.transpose(rwc, (0, 2, 1)), mod_c[:, 5:6], fg, False)
    return x
```

```python
import functools

import jax
import jax.numpy as jnp
import numpy as np
from jax import lax
from jax.experimental import pallas as pl
from jax.experimental.pallas import tpu as pltpu

F32 = jnp.float32
BF16 = jnp.bfloat16

D_MODEL = 1024
DEPTH = 4
GRID_W = 64
EPS = 1e-6

GM_HEADS = 4
GM_WIDTH = D_MODEL // 4
GM_HEAD_DIM = GM_WIDTH // GM_HEADS
GM_CHUNK = 128

ML_HEADS = 4
ML_WIDTH = D_MODEL // 4
ML_HEAD_DIM = ML_WIDTH // ML_HEADS
ML_CHUNK = 128

NA_HEADS = 8
NA_WIDTH = D_MODEL // 2
NA_HEAD_DIM = NA_WIDTH // NA_HEADS
NA_ROWS = 8
NA_COLS = 16

ROPE_BASE = 10000.0

OFF_ML = 2 * GM_WIDTH
OFF_GATES = OFF_ML + 4 * ML_WIDTH
OFF_NA = OFF_GATES + 4 * ML_HEADS
N_IN = OFF_NA + 3 * NA_WIDTH

N_EXPERTS = 16
N_GROUPS = 4
EXP_PER_GROUP = N_EXPERTS // N_GROUPS
D_EXPERT = D_MODEL

LANE = 128
SEG_GM = (0, 2 * GM_WIDTH)
SEG_ML = (SEG_GM[1], SEG_GM[1] + 4 * ML_WIDTH)
SEG_Q = (SEG_ML[1], SEG_ML[1] + NA_WIDTH)
SEG_K = (SEG_Q[1], SEG_Q[1] + NA_WIDTH)
SEG_V = (SEG_K[1], SEG_K[1] + NA_WIDTH)
SEG_GATES = (SEG_V[1], SEG_V[1] + LANE)
N_IN_PAD = SEG_GATES[1]

NA_QROWS = 8
NA_GROUP = 4
NA_UNROLL = 4
NA_BAND = 16
NA_KBLK_ROWS = 4
MOE_TILE = 512
EXPERT_CHUNK = 512
MASK_NEG = -1e30
N_GATE_ROWS = 6
VMEM_LIMIT = 56 * 2 ** 20


def _cparams(*sem):
    return pltpu.CompilerParams(dimension_semantics=sem, vmem_limit_bytes=VMEM_LIMIT)


def _nt_dot(a, b, precision=None):
    return lax.dot_general(a, b, (((1,), (1,)), ((), ())), precision=precision,
                           preferred_element_type=F32)


def _tn_dot(a, b):
    return lax.dot_general(a, b, (((0,), (0,)), ((), ())), preferred_element_type=F32)


def _ada_kernel(c_ref, w_ref, b_ref, o_ref):
    c = c_ref[...]
    sc = c * jax.nn.sigmoid(c)
    o_ref[0] = jnp.dot(sc, w_ref[0], preferred_element_type=F32) + b_ref[0]


def _ada_all(cs, ada_w, ada_b):
    n_out = ada_w.shape[-1] // D_MODEL
    return pl.pallas_call(
        _ada_kernel,
        grid=(DEPTH, n_out),
        in_specs=[pl.BlockSpec((8, D_MODEL), lambda l, j: (0, 0)),
                  pl.BlockSpec((1, D_MODEL, D_MODEL), lambda l, j: (l, 0, j)),
                  pl.BlockSpec((1, 1, D_MODEL), lambda l, j: (l, 0, j))],
        out_specs=pl.BlockSpec((1, 8, D_MODEL), lambda l, j: (l, 0, j)),
        out_shape=jax.ShapeDtypeStruct((DEPTH, 8, n_out * D_MODEL), F32),
        compiler_params=_cparams("arbitrary", "arbitrary"),
        name="adaln",
    )(cs, ada_w, ada_b.reshape(DEPTH, 1, -1))


def _in_proj_kernel(x_ref, mod_ref, g_ref, w_ref, gm_ref, ml_ref, gt_ref, q_ref, k_ref, v_ref):
    x = x_ref[0]
    h = x * lax.rsqrt(jnp.mean(x * x, -1, keepdims=True) + EPS) * g_ref[...]
    h = h * (1.0 + mod_ref[0, 1:2, :]) + mod_ref[0, 0:1, :]
    hb = h.astype(BF16)

    def seg(s):
        return jnp.dot(hb, w_ref[:, s[0]:s[1]], preferred_element_type=F32)

    gm_ref[0] = seg(SEG_GM)
    ml_ref[0] = seg(SEG_ML)
    gt_ref[0] = seg(SEG_GATES)
    q_ref[0] = (seg(SEG_Q) * (NA_HEAD_DIM ** -0.5)).astype(BF16)
    k_ref[0] = seg(SEG_K).astype(BF16)
    v_ref[0] = seg(SEG_V).astype(BF16)


def _in_proj(x, mod, g, w):
    G, R, D = x.shape
    tm = min(R, 512)
    row = lambda b, i: (b, i, 0)
    widths = (SEG_GM[1] - SEG_GM[0], SEG_ML[1] - SEG_ML[0], LANE, NA_WIDTH, NA_WIDTH, NA_WIDTH)
    dtypes = (F32, F32, F32, BF16, BF16, BF16)
    return pl.pallas_call(
        _in_proj_kernel,
        grid=(G, R // tm),
        in_specs=[pl.BlockSpec((1, tm, D), row),
                  pl.BlockSpec((1, 2, D), lambda b, i: (b, 0, 0)),
                  pl.BlockSpec((1, D), lambda b, i: (0, 0)),
                  pl.BlockSpec((D, N_IN_PAD), lambda b, i: (0, 0))],
        out_specs=[pl.BlockSpec((1, tm, n), row) for n in widths],
        out_shape=[jax.ShapeDtypeStruct((G, R, n), dt) for n, dt in zip(widths, dtypes)],
        compiler_params=_cparams("arbitrary", "arbitrary"),
        name="in_proj",
    )(x, mod, g, w)


def _gmlp_kernel(n_chunks, p_ref, ws_ref, bs_ref, g_ref, o_ref):
    lane = lax.broadcasted_iota(jnp.int32, (1, GM_WIDTH), 1)

    def chunk(j, carry):
        r0 = pl.multiple_of(j * GM_CHUNK, GM_CHUNK)
        z = jax.nn.gelu(p_ref[0, pl.ds(r0, GM_CHUNK), :])
        u = z[:, :GM_WIDTH]
        v = z[:, GM_WIDTH:]
        mu = jnp.mean(v, -1, keepdims=True)
        vc = v - mu
        var = jnp.mean(vc * vc, -1, keepdims=True)
        vn = (vc * lax.rsqrt(var + EPS) * g_ref[...]).astype(BF16)
        sv = jnp.zeros((GM_CHUNK, GM_WIDTH), F32)
        for h in range(GM_HEADS):
            full = jnp.dot(ws_ref[h], vn, preferred_element_type=F32)
            in_head = (lane >= h * GM_HEAD_DIM) & (lane < (h + 1) * GM_HEAD_DIM)
            sv = jnp.where(in_head, full, sv)
        o_ref[0, pl.ds(r0, GM_CHUNK), :] = (u * (sv + bs_ref[...])).astype(BF16)
        return carry

    lax.fori_loop(0, n_chunks, chunk, 0)


def _gmlp(p_gm, ws, bs_full, g):
    G, R, _ = p_gm.shape
    tr = min(R, 1024)
    return pl.pallas_call(
        functools.partial(_gmlp_kernel, tr // GM_CHUNK),
        grid=(G, R // tr),
        in_specs=[pl.BlockSpec((1, tr, 2 * GM_WIDTH), lambda b, i: (b, i, 0)),
                  pl.BlockSpec((GM_HEADS, GM_CHUNK, GM_CHUNK), lambda b, i: (0, 0, 0)),
                  pl.BlockSpec((GM_CHUNK, GM_WIDTH), lambda b, i: (0, 0)),
                  pl.BlockSpec((1, GM_WIDTH), lambda b, i: (0, 0))],
        out_specs=pl.BlockSpec((1, tr, GM_WIDTH), lambda b, i: (b, i, 0)),
        out_shape=jax.ShapeDtypeStruct((G, R, GM_WIDTH), BF16),
        compiler_params=_cparams("arbitrary", "arbitrary"),
        name="gmlp",
    )(p_gm, ws, bs_full, g)


def _log_sigmoid(x):
    return jnp.minimum(x, 0.0) - jnp.log1p(jnp.exp(-jnp.abs(x)))


def _rope(x, cos, sin_signed):
    lane = lax.broadcasted_iota(jnp.int32, (1, LANE), 1)
    first_half = (lane & 16) == 0
    partner = jnp.where(first_half, pltpu.roll(x, LANE - 16, 1), pltpu.roll(x, 16, 1))
    return x * cos + partner * sin_signed


def _chunk_scan(x, op, fill, reverse):
    pos = lax.broadcasted_iota(jnp.int32, (1, x.shape[1]), 1) % ML_CHUNK
    n = x.shape[1]
    sh = 1
    while sh < ML_CHUNK:
        if reverse:
            shifted = jnp.where(pos < ML_CHUNK - sh, pltpu.roll(x, n - sh, 1), fill)
        else:
            shifted = jnp.where(pos >= sh, pltpu.roll(x, sh, 1), fill)
        x = op(x, shifted)
        sh *= 2
    return x


def _mlstm_prep_kernel(n_blocks, x_ref, hp_ref, hn_ref, cos_ref, sin_ref, cw_ref, cb_ref, g_ref, gbias_ref,
                       q_ref, k_ref, go_ref):
    nh2 = 2 * ML_HEADS
    g = g_ref[0] + gbias_ref[...]
    is_fwd = lax.broadcasted_iota(jnp.int32, (nh2, 1), 0) < ML_HEADS
    li = g[0:nh2]
    lf = _log_sigmoid(g[nh2:])
    ps = _chunk_scan(lf, jnp.add, 0.0, False)
    ss = _chunk_scan(lf, jnp.add, 0.0, True)
    bcum = jnp.where(is_fwd, ps, ss)
    r = li - bcum
    pm = _chunk_scan(r, jnp.maximum, MASK_NEG, False)
    sm = _chunk_scan(r, jnp.maximum, MASK_NEG, True)
    rmax = jnp.maximum(pm, sm)
    groups = (bcum, r, jnp.where(is_fwd, pm, sm), jnp.exp(r - rmax), rmax, ps + ss - lf)
    for j, val in enumerate(groups):
        go_ref[0, nh2 * j:nh2 * (j + 1), :] = val

    i = pl.program_id(1)
    W = ML_WIDTH
    x = x_ref[0]
    tr = x.shape[0]
    rid = lax.broadcasted_iota(jnp.int32, (tr, 1), 0)
    i_row = jnp.zeros((1, 2 * W), jnp.int32) + i
    prev = jnp.where(i_row == 0, 0.0, hp_ref[0, 7:8, :])
    nxt = jnp.where(i_row == n_blocks - 1, 0.0, hn_ref[0, 0:1, :])
    xm1 = jnp.where(rid == 0, prev, pltpu.roll(x, 1, 0))
    xp1 = jnp.where(rid == tr - 1, nxt, pltpu.roll(x, tr - 1, 0))
    y = xm1 * cw_ref[0:1, :] + x * cw_ref[1:2, :] + xp1 * cw_ref[2:3, :] + cb_ref[...]
    y = y * jax.nn.sigmoid(y)
    cos = cos_ref[...]
    sin = sin_ref[...]
    parts = [_rope(y[:, j * LANE:(j + 1) * LANE], cos, sin) for j in range(2 * W // LANE)]
    q_ref[0] = (jnp.concatenate(parts[:W // LANE], axis=1) * (ML_HEAD_DIM ** -0.5)).astype(BF16)
    k_ref[0] = jnp.concatenate(parts[W // LANE:], axis=1).astype(BF16)


def _mlstm_prep(ml, cos, sin, conv_w, conv_b, gates_t, gbias):
    B, T, _ = ml.shape
    n_g = gates_t.shape[1]
    W = ML_WIDTH
    tr = min(T, 1024)
    nb = T // tr
    halo_per_blk = tr // 8
    n_hblk = T // 8
    whole = lambda shape: pl.BlockSpec(shape, lambda b, i: (0, 0))
    return pl.pallas_call(
        functools.partial(_mlstm_prep_kernel, nb),
        grid=(B, nb),
        in_specs=[pl.BlockSpec((1, tr, 2 * W), lambda b, i: (b, i, 0)),
                  pl.BlockSpec((1, 8, 2 * W), lambda b, i: (b, jnp.maximum(i * halo_per_blk - 1, 0), 0)),
                  pl.BlockSpec((1, 8, 2 * W),
                               lambda b, i: (b, jnp.minimum((i + 1) * halo_per_blk, n_hblk - 1), 0)),
                  pl.BlockSpec((tr, LANE), lambda b, i: (i, 0)),
                  pl.BlockSpec((tr, LANE), lambda b, i: (i, 0)),
                  whole((3, 2 * W)), whole((1, 2 * W)),
                  pl.BlockSpec((1, n_g, tr), lambda b, i: (b, 0, i)), whole((n_g, 1))],
        out_specs=[pl.BlockSpec((1, tr, W), lambda b, i: (b, i, 0)),
                   pl.BlockSpec((1, tr, W), lambda b, i: (b, i, 0)),
                   pl.BlockSpec((1, 2 * ML_HEADS * N_GATE_ROWS, tr), lambda b, i: (b, 0, i))],
        out_shape=[jax.ShapeDtypeStruct((B, T, W), BF16), jax.ShapeDtypeStruct((B, T, W), BF16),
                   jax.ShapeDtypeStruct((B, 2 * ML_HEADS * N_GATE_ROWS, T), F32)],
        compiler_params=_cparams("arbitrary", "arbitrary"),
        name="mlstm_prep",
    )(ml, ml, ml, cos, sin, conv_w, conv_b, gates_t, gbias)


def _mlstm_kernel(n_chunks, qf_ref, kf_ref, vf_ref, gf_ref, qb_ref, kb_ref, vb_ref, gb_ref,
                  m0_ref, s0_ref, hf_ref, hb_ref, mfin_ref, sfin_ref, m_scr, s_scr, bd_scr):
    c = pl.program_id(1)
    L = ML_CHUNK
    HD = ML_HEAD_DIM
    NH = ML_HEADS

    @pl.when(c == 0)
    def _():
        m_scr[...] = m0_ref[0]
        s_scr[...] = s0_ref[0]
        bd_scr[...] = jnp.zeros_like(bd_scr)

    is_fwd = lax.broadcasted_iota(jnp.int32, (2 * NH, 1), 0) < NH
    gq = lambda i: jnp.where(is_fwd, gf_ref[0, 2 * NH * i:2 * NH * (i + 1), :],
                             gb_ref[0, 2 * NH * i:2 * NH * (i + 1), :])
    bcum, r, rcmax, ew, rmax, b_last = (gq(i) for i in range(N_GATE_ROWS))
    m0 = m_scr[...]
    mu = jnp.maximum(m0, rcmax)
    a_t = jnp.exp(m0 - mu)
    emt = jnp.exp(-(bcum + mu))
    m_last = jnp.maximum(m0, rmax)
    a_st = jnp.exp(m0 - m_last)
    g_st = jnp.exp(rmax - m_last)
    m_scr[...] = b_last + m_last
    n_q = 4
    stack = jnp.concatenate([mu, a_t, emt, ew, jnp.zeros((L - n_q * 2 * NH, LANE), F32)], axis=0)
    cols = stack.T

    def col_bcast(q, j):
        return jnp.broadcast_to(cols[:, 8 * q + j:8 * q + j + 1], (L, LANE))

    def row_bcast(x, j):
        return jnp.broadcast_to(x[j:j + 1, :], (L, LANE))

    row = lax.broadcasted_iota(jnp.int32, (L, L), 0)
    col = lax.broadcasted_iota(jnp.int32, (L, L), 1)
    low_half = lax.broadcasted_iota(jnp.int32, (1, LANE), 1) < HD
    own_rows = jnp.concatenate([row < HD, row >= HD], axis=1)
    zero = jnp.zeros((), BF16)

    for d, (q_ref, k_ref, v_ref, out_ref) in enumerate(((qf_ref, kf_ref, vf_ref, hf_ref),
                                                        (qb_ref, kb_ref, vb_ref, hb_ref))):
        valid = (col <= row) if d == 0 else (col >= row)
        for pr in range(NH // 2):
            ps = slice(pr * LANE, (pr + 1) * LANE)
            ja = NH * d + 2 * pr
            jb = ja + 1
            sd = 2 * d + pr
            qp = q_ref[0, :, ps]
            kp = k_ref[0, :, ps]
            vp = v_ref[0, :, ps]
            st = s_scr[sd]
            kcat = jnp.concatenate([jnp.where(low_half, kp, zero), jnp.where(low_half, zero, kp)], axis=0)
            s2 = _nt_dot(qp, kcat)
            e2 = jnp.concatenate(
                [jnp.where(valid, jnp.exp(row_bcast(r, j) - col_bcast(0, j)), 0.0) for j in (ja, jb)], axis=1)
            p2 = (s2 * e2).astype(BF16)
            va = jnp.where(low_half, vp, 1.0)
            vb = jnp.where(low_half, 1.0, vp)
            bd_scr[0:L, 0:LANE] = va.astype(BF16)
            bd_scr[L:2 * L, LANE:2 * LANE] = vb.astype(BF16)
            intra = jnp.dot(p2, bd_scr[...], preferred_element_type=F32)
            inter = jnp.dot(qp, st.astype(BF16), preferred_element_type=F32)
            a2 = jnp.concatenate([col_bcast(1, ja), col_bcast(1, jb)], axis=1)
            tot = intra + a2 * inter
            tot_a = tot[:, 0:LANE]
            tot_b = tot[:, LANE:2 * LANE]
            num = jnp.where(low_half, tot_a, tot_b)
            den = pltpu.roll(jnp.where(low_half, tot_b, tot_a), HD, 1)
            floor = jnp.where(low_half, col_bcast(2, ja), col_bcast(2, jb))
            out_ref[0, :, ps] = num / jnp.maximum(jnp.abs(den), floor)

            vw = jnp.concatenate([va * col_bcast(3, ja), vb * col_bcast(3, jb)], axis=1).astype(BF16)
            upd = _tn_dot(kp, vw)
            a_s = jnp.concatenate([row_bcast(a_st, ja), row_bcast(a_st, jb)], axis=1)
            g_s = jnp.concatenate([row_bcast(g_st, ja), row_bcast(g_st, jb)], axis=1)
            s_scr[sd] = a_s * st + g_s * jnp.where(own_rows, upd, 0.0)

    @pl.when(c == n_chunks - 1)
    def _():
        mfin_ref[0] = m_scr[...]
        sfin_ref[0] = s_scr[...]


def _mlstm(q, k, ml, grows, m0, s0):
    B, T, W = q.shape
    L = ML_CHUNK
    nc = T // L

    def specs(ci):
        return [pl.BlockSpec((1, L, W), lambda b, c: (b, ci(c), 0)),
                pl.BlockSpec((1, L, W), lambda b, c: (b, ci(c), 0)),
                pl.BlockSpec((1, L, W), lambda b, c: (b, ci(c), 2)),
                pl.BlockSpec((1, grows.shape[1], L), lambda b, c: (b, 0, ci(c)))]

    m_spec = pl.BlockSpec((1,) + m0.shape[1:], lambda b, c: (b, 0, 0))
    s_spec = pl.BlockSpec((1,) + s0.shape[1:], lambda b, c: (b, 0, 0, 0))
    dir_args = [q, k, ml, grows]
    return pl.pallas_call(
        functools.partial(_mlstm_kernel, nc),
        grid=(B, nc),
        in_specs=specs(lambda c: c) + specs(lambda c: nc - 1 - c) + [m_spec, s_spec],
        out_specs=[pl.BlockSpec((1, L, W), lambda b, c: (b, c, 0)),
                   pl.BlockSpec((1, L, W), lambda b, c: (b, nc - 1 - c, 0)),
                   m_spec, s_spec],
        out_shape=[jax.ShapeDtypeStruct((B, T, W), F32),
                   jax.ShapeDtypeStruct((B, T, W), F32),
                   jax.ShapeDtypeStruct(m0.shape, F32),
                   jax.ShapeDtypeStruct(s0.shape, F32)],
        scratch_shapes=[pltpu.VMEM(m0.shape[1:], F32), pltpu.VMEM(s0.shape[1:], F32),
                        pltpu.VMEM((2 * L, 2 * LANE), BF16)],
        compiler_params=_cparams("arbitrary", "arbitrary"),
        name="mlstm",
    )(*dir_args, *dir_args, m0, s0)


def _softmax_pv(s_list, v_list):
    m = functools.reduce(jnp.maximum, [jnp.max(s, -1, keepdims=True) for s in s_list])
    ps = [jnp.exp(s - m) for s in s_list]
    l = functools.reduce(jnp.add, [jnp.sum(p, -1, keepdims=True) for p in ps])
    o = functools.reduce(jnp.add, [jnp.dot(p.astype(BF16), v, preferred_element_type=F32)
                                   for p, v in zip(ps, v_list)])
    return o / l


def _na_kernel(n_rows, q_ref, *refs):
    nb = NA_BAND // NA_KBLK_ROWS
    k_blks = refs[0:nb]
    v_blks = refs[nb:2 * nb]
    kc_ref, vc_ref, bt_ref, o_ref, kband, vband = refs[2 * nb:]
    i = pl.program_id(1)
    blk_tok = NA_KBLK_ROWS * GRID_W
    HD = NA_HEAD_DIM
    GW = NA_GROUP * HD
    head_of_lane = lax.broadcasted_iota(jnp.int32, (1, GW), 1) // HD
    for j in range(nb):
        kband[j * blk_tok:(j + 1) * blk_tok, :] = k_blks[j][0]
        vband[j * blk_tok:(j + 1) * blk_tok, :] = v_blks[j][0]
    band_row0 = jnp.clip(i * NA_QROWS - NA_ROWS // 2, 0, n_rows - NA_BAND)
    n_win = NA_ROWS * GRID_W
    zero = jnp.zeros((), BF16)

    def rows_body(t, carry):
        items = []
        for u in range(NA_UNROLL):
            a = t * NA_UNROLL + u
            r = i * NA_QROWS + a
            r0 = jnp.clip(r - NA_ROWS // 2, 0, n_rows - NA_ROWS)
            koff = pl.multiple_of((r0 - band_row0) * GRID_W, GRID_W)
            dr_first = r0 - r + (NA_ROWS - 1)
            qoff = pl.multiple_of(a * GRID_W, GRID_W)
            for gi in range(NA_HEADS // NA_GROUP):
                items.append((koff, dr_first, qoff, gi, slice(gi * GW, (gi + 1) * GW)))
        scores = []
        for koff, _, qoff, _, gs in items:
            qg = q_ref[0, pl.ds(qoff, GRID_W), gs]
            qm = jnp.concatenate([jnp.where(head_of_lane == g, qg, zero) for g in range(NA_GROUP)], axis=0)
            scores.append((_nt_dot(kband[pl.ds(koff, n_win), gs], qm),
                           _nt_dot(kc_ref[0, :, gs], qm)))
        probs = []
        for (_, dr_first, _, gi, _), (s_loc, s_ctx) in zip(items, scores):
            s_loc = s_loc + bt_ref[dr_first, gi]
            m = jnp.maximum(jnp.max(s_loc, 0, keepdims=True), jnp.max(s_ctx, 0, keepdims=True))
            e_loc = jnp.exp(s_loc - m)
            e_ctx = jnp.exp(s_ctx - m)
            inv = 1.0 / (jnp.sum(e_loc, 0, keepdims=True) + jnp.sum(e_ctx, 0, keepdims=True))
            probs.append(((e_loc * inv).astype(BF16), (e_ctx * inv).astype(BF16)))
        for (koff, _, qoff, _, gs), (p_loc, p_ctx) in zip(items, probs):
            res = (_tn_dot(p_loc, vband[pl.ds(koff, n_win), gs])
                   + _tn_dot(p_ctx, vc_ref[0, :, gs]))
            out = res[0:GRID_W]
            for g in range(1, NA_GROUP):
                out = jnp.where(head_of_lane == g, res[g * GRID_W:(g + 1) * GRID_W], out)
            o_ref[0, pl.ds(qoff, GRID_W), gs] = out.astype(BF16)
        return carry

    lax.fori_loop(0, NA_QROWS // NA_UNROLL, rows_body, 0)


def _na_latent(q, k, v, kc, vc, bias_tab):
    B, T, Wd = q.shape
    Lc = kc.shape[1]
    n_rows = T // GRID_W
    nb = NA_BAND // NA_KBLK_ROWS
    q_tok = NA_QROWS * GRID_W
    blk_tok = NA_KBLK_ROWS * GRID_W
    last_blk0 = (n_rows - NA_BAND) // NA_KBLK_ROWS

    def kv_spec(j):
        def idx(b, i):
            first = jnp.clip(i * (NA_QROWS // NA_KBLK_ROWS) - (NA_ROWS // 2) // NA_KBLK_ROWS, 0, last_blk0)
            return (b, first + j, 0)
        return pl.BlockSpec((1, blk_tok, Wd), idx)

    return pl.pallas_call(
        functools.partial(_na_kernel, n_rows),
        grid=(B, n_rows // NA_QROWS),
        in_specs=([pl.BlockSpec((1, q_tok, Wd), lambda b, i: (b, i, 0))]
                  + [kv_spec(j) for j in range(nb)] + [kv_spec(j) for j in range(nb)]
                  + [pl.BlockSpec((1, Lc, Wd), lambda b, i: (b, 0, 0)),
                     pl.BlockSpec((1, Lc, Wd), lambda b, i: (b, 0, 0)),
                     pl.BlockSpec(bias_tab.shape, lambda b, i: (0, 0, 0, 0))]),
        out_specs=pl.BlockSpec((1, q_tok, Wd), lambda b, i: (b, i, 0)),
        out_shape=jax.ShapeDtypeStruct((B, T, Wd), BF16),
        scratch_shapes=[pltpu.VMEM((NA_BAND * GRID_W, Wd), BF16),
                        pltpu.VMEM((NA_BAND * GRID_W, Wd), BF16)],
        compiler_params=_cparams("arbitrary", "arbitrary"),
        name="natten",
    )(q, *([k] * nb), *([v] * nb), kc, vc, bias_tab)


def _ctx_attn_kernel(q_ref, k_ref, v_ref, o_ref):
    HD = NA_HEAD_DIM
    for h in range(NA_HEADS):
        hs = slice(h * HD, (h + 1) * HD)
        s = _nt_dot(q_ref[0, :, hs], k_ref[0, :, hs])
        o_ref[0, :, hs] = _softmax_pv([s], [v_ref[0, :, hs]]).astype(BF16)


def _ctx_attn(q, k, v):
    B, Lc, Wd = q.shape
    spec = pl.BlockSpec((1, Lc, Wd), lambda b: (b, 0, 0))
    return pl.pallas_call(
        _ctx_attn_kernel, grid=(B,), in_specs=[spec, spec, spec], out_specs=spec,
        out_shape=jax.ShapeDtypeStruct((B, Lc, Wd), BF16),
        compiler_params=_cparams("arbitrary"), name="ctx_attn",
    )(q, k, v)


def _na_bias_table(rpb):
    qc = np.arange(GRID_W)[:, None]
    kcol = np.arange(GRID_W)[None, :]
    wstart = np.clip(qc - NA_COLS // 2, 0, GRID_W - NA_COLS)
    col_ok = (kcol >= wstart) & (kcol < wstart + NA_COLS)
    dc = np.clip(kcol - qc, -(NA_COLS - 1), NA_COLS - 1) + NA_COLS - 1
    per_dr = jnp.where(col_ok[None, None], rpb[:, :, dc], MASK_NEG)
    def tab(d):
        n_grp = NA_HEADS // NA_GROUP
        t = per_dr[:, d:d + NA_ROWS].reshape(n_grp, NA_GROUP, NA_ROWS, GRID_W, GRID_W)
        return jnp.transpose(t, (0, 2, 4, 1, 3)).reshape(n_grp, NA_ROWS * GRID_W, NA_GROUP * GRID_W)

    return jnp.stack([tab(d) for d in range(NA_ROWS)]).astype(F32)


def _route(sel, s):
    E = EXP_PER_GROUP
    scores = []
    for g in range(N_GROUPS):
        a, b, c, d = sel[E * g:E * (g + 1)]
        scores.append(functools.reduce(jnp.maximum, [a + b, a + c, a + d, b + c, b + d, c + d]))
    best = jnp.zeros_like(scores[0], dtype=jnp.int32)
    best_score = scores[0]
    for g in range(1, N_GROUPS):
        upd = scores[g] > best_score
        best = jnp.where(upd, g, best)
        best_score = jnp.where(upd, scores[g], best_score)

    def pick(rows, j):
        out = rows[j]
        for g in range(1, N_GROUPS):
            out = jnp.where(best == g, rows[E * g + j], out)
        return out

    v = [pick(sel, j) for j in range(E)]
    sv = [pick(s, j) for j in range(E)]
    i1 = jnp.zeros_like(best)
    m1 = v[0]
    for j in range(1, E):
        upd = v[j] > m1
        i1 = jnp.where(upd, j, i1)
        m1 = jnp.where(upd, v[j], m1)
    i2 = jnp.where(i1 == 0, 1, 0)
    m2 = jnp.where(i1 == 0, v[1], v[0])
    for j in range(1, E):
        upd = (i1 != j) & (v[j] > m2)
        i2 = jnp.where(upd, j, i2)
        m2 = jnp.where(upd, v[j], m2)

    def at(rows, idx):
        out = rows[0]
        for j in range(1, E):
            out = jnp.where(idx == j, rows[j], out)
        return out

    s1 = at(sv, i1)
    s2 = at(sv, i2)
    tot = s1 + s2
    return best * E + i1, best * E + i2, s1 / tot, s2 / tot


def _out_proj_kernel(ygm_ref, hf_ref, hb_ref, o_ref, yna_ref, x_ref, mod_ref, w_ref, mlg_ref, n2g_ref,
                     rw_ref, rb_ref, xn_ref, h2_ref, re_ref, rwt_ref):
    hs = hf_ref[0] + hb_ref[0]
    hsq = hs * hs
    lane = lax.broadcasted_iota(jnp.int32, (1, ML_WIDTH), 1)
    scale = jnp.zeros_like(hs)
    for h in range(ML_HEADS):
        in_head = (lane >= h * ML_HEAD_DIM) & (lane < (h + 1) * ML_HEAD_DIM)
        ms = jnp.sum(jnp.where(in_head, hsq, 0.0), -1, keepdims=True) * (1.0 / ML_HEAD_DIM)
        scale = jnp.where(in_head, lax.rsqrt(ms + EPS), scale)
    yml = hs * scale * mlg_ref[...] * jax.nn.sigmoid(o_ref[0])
    o1 = GM_WIDTH
    o2 = GM_WIDTH + ML_WIDTH
    acc = (jnp.dot(ygm_ref[0], w_ref[0:o1, :], preferred_element_type=F32)
           + jnp.dot(yml.astype(BF16), w_ref[o1:o2, :], preferred_element_type=F32)
           + jnp.dot(yna_ref[0], w_ref[o2:, :], preferred_element_type=F32))
    xn = x_ref[0] + mod_ref[0, 0:1, :] * acc
    xn_ref[0] = xn
    h2 = xn * lax.rsqrt(jnp.mean(xn * xn, -1, keepdims=True) + EPS) * n2g_ref[...]
    h2 = h2 * (1.0 + mod_ref[0, 2:3, :]) + mod_ref[0, 1:2, :]
    h2_ref[0] = h2.astype(BF16)
    logits = _nt_dot(rw_ref[...], h2, precision=lax.Precision.HIGHEST)
    s = jax.nn.sigmoid(logits)
    sel = s + rb_ref[...]
    rows = lambda m: [m[e:e + 1, :] for e in range(N_EXPERTS)]
    e1, e2, w1, w2 = _route(rows(sel), rows(s))
    re_ref[0, 0:1, :] = e1
    re_ref[0, 1:2, :] = e2
    rwt_ref[0, 0:1, :] = w1
    rwt_ref[0, 1:2, :] = w2


def _out_proj(ygm, hf, hb, ml, yna, x, mod, w_out, ml_g, n2_g, router_wt, router_b):
    G, R, D = x.shape
    tm = min(R, 512)
    row = lambda b, i: (b, i, 0)
    whole2 = lambda shape: pl.BlockSpec(shape, lambda b, i: (0, 0))
    return pl.pallas_call(
        _out_proj_kernel,
        grid=(G, R // tm),
        in_specs=[pl.BlockSpec((1, tm, GM_WIDTH), row),
                  pl.BlockSpec((1, tm, ML_WIDTH), row),
                  pl.BlockSpec((1, tm, ML_WIDTH), row),
                  pl.BlockSpec((1, tm, ML_WIDTH), lambda b, i: (b, i, 3)),
                  pl.BlockSpec((1, tm, NA_WIDTH), row),
                  pl.BlockSpec((1, tm, D), row),
                  pl.BlockSpec((1, 3, D), lambda b, i: (b, 0, 0)),
                  whole2((D, D)), whole2((1, ML_WIDTH)), whole2((1, D)),
                  whole2((N_EXPERTS, D)), whole2((N_EXPERTS, 1))],
        out_specs=[pl.BlockSpec((1, tm, D), row), pl.BlockSpec((1, tm, D), row),
                   pl.BlockSpec((1, 2, tm), lambda b, i: (b, 0, i)),
                   pl.BlockSpec((1, 2, tm), lambda b, i: (b, 0, i))],
        out_shape=[jax.ShapeDtypeStruct((G, R, D), F32), jax.ShapeDtypeStruct((G, R, D), BF16),
                   jax.ShapeDtypeStruct((G, 2, R), jnp.int32), jax.ShapeDtypeStruct((G, 2, R), F32)],
        compiler_params=_cparams("arbitrary", "arbitrary"),
        name="out_proj",
    )(ygm, hf, hb, ml, yna, x, mod, w_out, ml_g, n2_g, router_wt, router_b)


def _experts_kernel(te_ref, nu_ref, x_ref, wg_ref, wu_ref, wd_ref, o_ref, acc_ref, wgb, wub, wdb):
    i = pl.program_id(0)
    used = i < nu_ref[0]
    new_expert = (i == 0) | (te_ref[i] != te_ref[jnp.maximum(i - 1, 0)])

    @pl.when(used & new_expert)
    def _():
        def cast_rows(c, carry):
            rs = pl.ds(pl.multiple_of(c * LANE, LANE), LANE)
            wgb[rs, :] = wg_ref[0, 0, rs, :].astype(BF16)
            wub[rs, :] = wu_ref[0, 0, rs, :].astype(BF16)
            wdb[rs, :] = wd_ref[0, 0, rs, :].astype(BF16)
            return carry
        lax.fori_loop(0, D_MODEL // LANE, cast_rows, 0)

    @pl.when(used)
    def _():
        x = x_ref[...]
        for j in range(D_EXPERT // EXPERT_CHUNK):
            cs = slice(j * EXPERT_CHUNK, (j + 1) * EXPERT_CHUNK)
            g = jnp.dot(x, wgb[:, cs], preferred_element_type=F32)
            u = jnp.dot(x, wub[:, cs], preferred_element_type=F32)
            a = (g * jax.nn.sigmoid(g) * u).astype(BF16)
            y = jnp.dot(a, wdb[cs, :], preferred_element_type=F32)
            if j == 0:
                acc_ref[...] = y
            else:
                acc_ref[...] += y
        o_ref[...] = acc_ref[...].astype(BF16)

    @pl.when(jnp.logical_not(used))
    def _():
        o_ref[...] = jnp.zeros_like(o_ref)


def _experts(l, tile_expert, n_used, xs, wg, wu, wd):
    buf, D = xs.shape
    n_tiles = buf // MOE_TILE
    wspec = lambda r, c: pl.BlockSpec((1, 1, r, c), lambda i, te, nu: (l, te[i], 0, 0))
    return pl.pallas_call(
        _experts_kernel,
        grid_spec=pltpu.PrefetchScalarGridSpec(
            num_scalar_prefetch=2,
            grid=(n_tiles,),
            in_specs=[pl.BlockSpec((MOE_TILE, D), lambda i, te, nu: (i, 0)),
                      wspec(D, D_EXPERT), wspec(D, D_EXPERT), wspec(D_EXPERT, D)],
            out_specs=pl.BlockSpec((MOE_TILE, D), lambda i, te, nu: (i, 0)),
            scratch_shapes=[pltpu.VMEM((MOE_TILE, D), F32), pltpu.VMEM((D, D_EXPERT), BF16),
                            pltpu.VMEM((D, D_EXPERT), BF16), pltpu.VMEM((D_EXPERT, D), BF16)]),
        out_shape=jax.ShapeDtypeStruct((buf, D), BF16),
        compiler_params=_cparams("arbitrary"),
        name="experts",
    )(tile_expert, n_used, xs, wg, wu, wd)


def _dispatch(e_idx):
    n_tok = e_idx.shape[0]
    n_asg = 2 * n_tok
    n_tiles = -(-n_asg // MOE_TILE) + N_EXPERTS
    flat_e = e_idx.reshape(-1)
    onehot = (flat_e[:, None] == jnp.arange(N_EXPERTS, dtype=jnp.int32)[None, :]).astype(jnp.int32)
    csum = jnp.cumsum(onehot, axis=0)
    rank = jnp.take_along_axis(csum, flat_e[:, None], axis=1)[:, 0] - 1
    counts = csum[-1]
    tiles_e = (counts + MOE_TILE - 1) // MOE_TILE
    tile_end = jnp.cumsum(tiles_e)
    tile_start = tile_end - tiles_e
    dest = tile_start[flat_e] * MOE_TILE + rank
    n_used = tile_end[-1]
    tid = jnp.arange(n_tiles, dtype=jnp.int32)
    te = jnp.minimum(jnp.searchsorted(tile_end, tid, side="right"), N_EXPERTS - 1).astype(jnp.int32)
    te = jnp.where(tid < n_used, te, te[jnp.maximum(n_used - 1, 0)])
    flat_t = jnp.arange(n_asg, dtype=jnp.int32) // 2
    src = jnp.zeros((n_tiles * MOE_TILE,), jnp.int32).at[dest].set(
        flat_t, unique_indices=True, mode="promise_in_bounds")
    return dest.reshape(n_tok, 2), src, te, n_used.reshape(1).astype(jnp.int32)


def _combine_kernel(final, xn_ref, y0_ref, y1_ref, w_ref, g2_ref, fg_ref, o_ref):
    w = w_ref[0]
    f = y0_ref[0].astype(F32) * w[:, 0:1] + y1_ref[0].astype(F32) * w[:, 1:2]
    x = xn_ref[0] + g2_ref[0] * f
    if final:
        x = x * lax.rsqrt(jnp.mean(x * x, -1, keepdims=True) + EPS) * fg_ref[...]
    o_ref[0] = x


def _combine(xn, y0, y1, wt, g2, final_g, final):
    G, R, D = xn.shape
    tm = min(R, 512)
    row = lambda b, i: (b, i, 0)
    return pl.pallas_call(
        functools.partial(_combine_kernel, final),
        grid=(G, R // tm),
        in_specs=[pl.BlockSpec((1, tm, D), row), pl.BlockSpec((1, tm, D), row), pl.BlockSpec((1, tm, D), row),
                  pl.BlockSpec((1, tm, 2), row),
                  pl.BlockSpec((1, 1, D), lambda b, i: (b, 0, 0)),
                  pl.BlockSpec((1, D), lambda b, i: (0, 0))],
        out_specs=pl.BlockSpec((1, tm, D), row),
        out_shape=jax.ShapeDtypeStruct((G, R, D), F32),
        compiler_params=_cparams("arbitrary", "arbitrary"),
        name="combine_final" if final else "combine",
    )(xn, y0, y1, wt, g2, final_g)


def _rope_tables(T):
    lane = np.arange(LANE)
    half = ML_HEAD_DIM // 4
    inv = jnp.tile(ROPE_BASE ** (-jnp.arange(half, dtype=F32) / half), LANE // half)
    t = jnp.arange(T)
    pos = jnp.where(((lane // (2 * half)) % 2 == 0)[None, :], (t // GRID_W)[:, None], (t % GRID_W)[:, None])
    ang = pos.astype(F32) * inv[None, :]
    sign = np.where((lane // half) % 2 == 0, -1.0, 1.0).astype(np.float32)
    return jnp.cos(ang), jnp.sin(ang) * sign[None, :]


def _reorder_w_in(w_in):
    pad = jnp.zeros(w_in.shape[:2] + (LANE - 4 * ML_HEADS,), w_in.dtype)
    return jnp.concatenate([w_in[..., :OFF_GATES], w_in[..., OFF_NA:], w_in[..., OFF_GATES:OFF_NA], pad],
                           axis=-1).astype(BF16)


def _mixers(l, p, pc, prm, need_ctx):
    gm, ml, gt, q, k, v = p
    gmc, mlc, gtc, qc, kc, vc = pc
    B = ml.shape[0]
    gm_args = (prm["gm_ws"][l], prm["gm_bs_full"][l], prm["gm_g"][l])
    conv = (prm["conv_w"][l], prm["conv_b"][l])
    nh = ML_HEADS
    tr = lambda g: jnp.transpose(
        jnp.concatenate([g[..., 0:nh], g[..., 2 * nh:3 * nh], g[..., nh:2 * nh], g[..., 3 * nh:4 * nh]], -1),
        (0, 2, 1))
    m_zero = jnp.zeros((B, 2 * nh, LANE), F32)
    s_zero = jnp.zeros((B, nh, ML_CHUNK, 2 * LANE), F32)
    qmc, kmc, grc = _mlstm_prep(mlc, prm["cos_c"], prm["sin_c"], *conv, tr(gtc), prm["gbias"][l])
    hfc, hbc, m_st, s_st = _mlstm(qmc, kmc, mlc, grc, m_zero, s_zero)
    qm, km, gr = _mlstm_prep(ml, prm["cos_l"], prm["sin_l"], *conv, tr(gt), prm["gbias"][l])
    hf, hb, _, _ = _mlstm(qm, km, ml, gr, m_st, s_st)
    y_na = _na_latent(q, k, v, kc, vc, prm["na_tab"][l])
    y = (_gmlp(gm, *gm_args), hf, hb, ml, y_na)
    if not need_ctx:
        return y, None
    return y, (_gmlp(gmc, *gm_args), hfc, hbc, mlc, _ctx_attn(qc, kc, vc))


def _rows(a, idx):
    return a.at[idx].get(mode="promise_in_bounds")


def _moe(l, h2_all, e_all, prm):
    dest, src, te, n_used = _dispatch(e_all)
    yb = _experts(l, te, n_used, _rows(h2_all, src), prm["wg"], prm["wu"], prm["wd"])
    return yb, dest


def kernel(x, c, ctx, c_ctx, ada_w, ada_b, norm1_g, norm2_g, w_in, w_out, gm_ws, gm_bs, gm_norm_g,
           ml_conv_w, ml_conv_b, ml_gate_b, ml_norm_g, na_rpb, router_w, router_b,
           moe_w_gate, moe_w_up, moe_w_down, final_g):
    B, T, D = x.shape
    Lc = ctx.shape[1]
    cos_l, sin_l = _rope_tables(T)
    prm = dict(
        gm_ws=gm_ws.astype(BF16),
        gm_bs_full=jnp.repeat(jnp.transpose(gm_bs, (0, 2, 1)), GM_HEAD_DIM, axis=-1),
        gm_g=gm_norm_g[:, None, :],
        conv_w=ml_conv_w, conv_b=ml_conv_b[:, None, :],
        gbias=ml_gate_b[:, jnp.array([0, 2, 1, 3])].reshape(DEPTH, 4 * ML_HEADS, 1),
        cos_l=cos_l, sin_l=sin_l,
        cos_c=jnp.ones((Lc, LANE), F32), sin_c=jnp.zeros((Lc, LANE), F32),
        na_tab=jnp.stack([_na_bias_table(na_rpb[l]) for l in range(DEPTH)]),
        wg=moe_w_gate, wu=moe_w_up, wd=moe_w_down,
    )
    w_in_r = _reorder_w_in(w_in)
    w_out_b = w_out.astype(BF16)
    router_wt = jnp.transpose(router_w)
    router_bc = router_b[:, None]

    cs = jnp.concatenate([c, c_ctx[None, :], jnp.zeros((8 - B - 1, D), F32)], axis=0)
    mods = _ada_all(cs, ada_w, ada_b).reshape(DEPTH, 8, 6, D)

    xc = ctx
    for l in range(DEPTH):
        need_ctx = l < DEPTH - 1
        mod_l = mods[l, :B]
        mod_c = jnp.broadcast_to(mods[l, B:B + 1], (B, 6, D))
        n1 = norm1_g[l][None, :]
        p = _in_proj(x, mod_l[:, 0:2], n1, w_in_r[l])
        pc = _in_proj(xc, mod_c[:, 0:2], n1, w_in_r[l])
        y, yc = _mixers(l, p, pc, prm, need_ctx)
        op_args = (w_out_b[l], ml_norm_g[l][None, :], norm2_g[l][None, :], router_wt, router_bc)
        xn, h2, re, rw = _out_proj(*y, x, mod_l[:, 2:5], *op_args)
        h2_all = h2.reshape(B * T, D)
        e_all = jnp.transpose(re, (0, 2, 1)).reshape(B * T, 2)
        if need_ctx:
            xnc, h2c, rec, rwc = _out_proj(*yc, xc, mod_c[:, 2:5], *op_args)
            h2_all = jnp.concatenate([h2_all, h2c.reshape(B * Lc, D)], axis=0)
            e_all = jnp.concatenate([e_all, jnp.transpose(rec, (0, 2, 1)).reshape(B * Lc, 2)], axis=0)
        yb, dest = _moe(l, h2_all, e_all, prm)
        final = l == DEPTH - 1
        fg = final_g[None, :]
        n_lat = B * T
        picked = lambda lo, hi, k, L: _rows(yb, dest[lo:hi, k]).reshape(B, L, D)
        x = _combine(xn, picked(0, n_lat, 0, T), picked(0, n_lat, 1, T),
                     jnp.transpose(rw, (0, 2, 1)), mod_l[:, 5:6], fg, final)
        if need_ctx:
            xc = _combine(xnc, picked(n_lat, None, 0, Lc), picked(n_lat, None, 1, Lc),
                          jnp.transpose(rwc, (0, 2, 1)), mod_c[:, 5:6], fg, False)
    return x
```

```python
import functools

import jax
import jax.numpy as jnp
import numpy as np
from jax import lax
from jax.experimental import pallas as pl
from jax.experimental.pallas import tpu as pltpu

F32 = jnp.float32
BF16 = jnp.bfloat16

D_MODEL = 1024
DEPTH = 4
GRID_W = 64
EPS = 1e-6

GM_HEADS = 4
GM_WIDTH = D_MODEL // 4
GM_HEAD_DIM = GM_WIDTH // GM_HEADS
GM_CHUNK = 128

ML_HEADS = 4
ML_WIDTH = D_MODEL // 4
ML_HEAD_DIM = ML_WIDTH // ML_HEADS
ML_CHUNK = 128

NA_HEADS = 8
NA_WIDTH = D_MODEL // 2
NA_HEAD_DIM = NA_WIDTH // NA_HEADS
NA_ROWS = 8
NA_COLS = 16

ROPE_BASE = 10000.0

OFF_ML = 2 * GM_WIDTH
OFF_GATES = OFF_ML + 4 * ML_WIDTH
OFF_NA = OFF_GATES + 4 * ML_HEADS
N_IN = OFF_NA + 3 * NA_WIDTH

N_EXPERTS = 16
N_GROUPS = 4
EXP_PER_GROUP = N_EXPERTS // N_GROUPS
D_EXPERT = D_MODEL

LANE = 128
SEG_GM = (0, 2 * GM_WIDTH)
SEG_ML = (SEG_GM[1], SEG_GM[1] + 4 * ML_WIDTH)
SEG_Q = (SEG_ML[1], SEG_ML[1] + NA_WIDTH)
SEG_K = (SEG_Q[1], SEG_Q[1] + NA_WIDTH)
SEG_V = (SEG_K[1], SEG_K[1] + NA_WIDTH)
SEG_GATES = (SEG_V[1], SEG_V[1] + LANE)
N_IN_PAD = SEG_GATES[1]

NA_QROWS = 8
NA_GROUP = 4
NA_UNROLL = 4
NA_BAND = 16
NA_KBLK_ROWS = 4
MOE_TILE = 512
EXPERT_CHUNK = 512
MASK_NEG = -1e30
N_GATE_ROWS = 6
VMEM_LIMIT = 56 * 2 ** 20


def _cparams(*sem):
    return pltpu.CompilerParams(dimension_semantics=sem, vmem_limit_bytes=VMEM_LIMIT)


def _nt_dot(a, b, precision=None):
    return lax.dot_general(a, b, (((1,), (1,)), ((), ())), precision=precision,
                           preferred_element_type=F32)


def _tn_dot(a, b):
    return lax.dot_general(a, b, (((0,), (0,)), ((), ())), preferred_element_type=F32)


def _ada_kernel(c_ref, w_ref, b_ref, o_ref):
    c = c_ref[...]
    sc = c * jax.nn.sigmoid(c)
    o_ref[0] = jnp.dot(sc, w_ref[0], preferred_element_type=F32) + b_ref[0]


def _ada_all(cs, ada_w, ada_b):
    n_out = ada_w.shape[-1] // D_MODEL
    return pl.pallas_call(
        _ada_kernel,
        grid=(DEPTH, n_out),
        in_specs=[pl.BlockSpec((8, D_MODEL), lambda l, j: (0, 0)),
                  pl.BlockSpec((1, D_MODEL, D_MODEL), lambda l, j: (l, 0, j)),
                  pl.BlockSpec((1, 1, D_MODEL), lambda l, j: (l, 0, j))],
        out_specs=pl.BlockSpec((1, 8, D_MODEL), lambda l, j: (l, 0, j)),
        out_shape=jax.ShapeDtypeStruct((DEPTH, 8, n_out * D_MODEL), F32),
        compiler_params=_cparams("arbitrary", "arbitrary"),
        name="adaln",
    )(cs, ada_w, ada_b.reshape(DEPTH, 1, -1))


def _in_proj_kernel(x_ref, mod_ref, g_ref, w_ref, gm_ref, ml_ref, gt_ref, q_ref, k_ref, v_ref):
    x = x_ref[0]
    h = x * lax.rsqrt(jnp.mean(x * x, -1, keepdims=True) + EPS) * g_ref[...]
    h = h * (1.0 + mod_ref[0, 1:2, :]) + mod_ref[0, 0:1, :]
    hb = h.astype(BF16)

    def seg(s):
        return jnp.dot(hb, w_ref[:, s[0]:s[1]], preferred_element_type=F32)

    gm_ref[0] = seg(SEG_GM)
    ml_ref[0] = seg(SEG_ML)
    gt_ref[0] = seg(SEG_GATES)
    q_ref[0] = (seg(SEG_Q) * (NA_HEAD_DIM ** -0.5)).astype(BF16)
    k_ref[0] = seg(SEG_K).astype(BF16)
    v_ref[0] = seg(SEG_V).astype(BF16)


def _in_proj(x, mod, g, w):
    G, R, D = x.shape
    tm = min(R, 512)
    row = lambda b, i: (b, i, 0)
    widths = (SEG_GM[1] - SEG_GM[0], SEG_ML[1] - SEG_ML[0], LANE, NA_WIDTH, NA_WIDTH, NA_WIDTH)
    dtypes = (F32, F32, F32, BF16, BF16, BF16)
    return pl.pallas_call(
        _in_proj_kernel,
        grid=(G, R // tm),
        in_specs=[pl.BlockSpec((1, tm, D), row),
                  pl.BlockSpec((1, 2, D), lambda b, i: (b, 0, 0)),
                  pl.BlockSpec((1, D), lambda b, i: (0, 0)),
                  pl.BlockSpec((D, N_IN_PAD), lambda b, i: (0, 0))],
        out_specs=[pl.BlockSpec((1, tm, n), row) for n in widths],
        out_shape=[jax.ShapeDtypeStruct((G, R, n), dt) for n, dt in zip(widths, dtypes)],
        compiler_params=_cparams("arbitrary", "arbitrary"),
        name="in_proj",
    )(x, mod, g, w)


def _gmlp_kernel(n_chunks, p_ref, ws_ref, bs_ref, g_ref, o_ref):
    lane = lax.broadcasted_iota(jnp.int32, (1, GM_WIDTH), 1)

    def chunk(j, carry):
        r0 = pl.multiple_of(j * GM_CHUNK, GM_CHUNK)
        z = jax.nn.gelu(p_ref[0, pl.ds(r0, GM_CHUNK), :])
        u = z[:, :GM_WIDTH]
        v = z[:, GM_WIDTH:]
        mu = jnp.mean(v, -1, keepdims=True)
        vc = v - mu
        var = jnp.mean(vc * vc, -1, keepdims=True)
        vn = (vc * lax.rsqrt(var + EPS) * g_ref[...]).astype(BF16)
        sv = jnp.zeros((GM_CHUNK, GM_WIDTH), F32)
        for h in range(GM_HEADS):
            full = jnp.dot(ws_ref[h], vn, preferred_element_type=F32)
            in_head = (lane >= h * GM_HEAD_DIM) & (lane < (h + 1) * GM_HEAD_DIM)
            sv = jnp.where(in_head, full, sv)
        o_ref[0, pl.ds(r0, GM_CHUNK), :] = (u * (sv + bs_ref[...])).astype(BF16)
        return carry

    lax.fori_loop(0, n_chunks, chunk, 0)


def _gmlp(p_gm, ws, bs_full, g):
    G, R, _ = p_gm.shape
    tr = min(R, 1024)
    return pl.pallas_call(
        functools.partial(_gmlp_kernel, tr // GM_CHUNK),
        grid=(G, R // tr),
        in_specs=[pl.BlockSpec((1, tr, 2 * GM_WIDTH), lambda b, i: (b, i, 0)),
                  pl.BlockSpec((GM_HEADS, GM_CHUNK, GM_CHUNK), lambda b, i: (0, 0, 0)),
                  pl.BlockSpec((GM_CHUNK, GM_WIDTH), lambda b, i: (0, 0)),
                  pl.BlockSpec((1, GM_WIDTH), lambda b, i: (0, 0))],
        out_specs=pl.BlockSpec((1, tr, GM_WIDTH), lambda b, i: (b, i, 0)),
        out_shape=jax.ShapeDtypeStruct((G, R, GM_WIDTH), BF16),
        compiler_params=_cparams("arbitrary", "arbitrary"),
        name="gmlp",
    )(p_gm, ws, bs_full, g)


def _log_sigmoid(x):
    return jnp.minimum(x, 0.0) - jnp.log1p(jnp.exp(-jnp.abs(x)))


def _rope(x, cos, sin_signed):
    lane = lax.broadcasted_iota(jnp.int32, (1, LANE), 1)
    first_half = (lane & 16) == 0
    partner = jnp.where(first_half, pltpu.roll(x, LANE - 16, 1), pltpu.roll(x, 16, 1))
    return x * cos + partner * sin_signed


def _chunk_scan(x, op, fill, reverse):
    pos = lax.broadcasted_iota(jnp.int32, (1, x.shape[1]), 1) % ML_CHUNK
    n = x.shape[1]
    sh = 1
    while sh < ML_CHUNK:
        if reverse:
            shifted = jnp.where(pos < ML_CHUNK - sh, pltpu.roll(x, n - sh, 1), fill)
        else:
            shifted = jnp.where(pos >= sh, pltpu.roll(x, sh, 1), fill)
        x = op(x, shifted)
        sh *= 2
    return x


def _mlstm_prep_kernel(n_blocks, x_ref, hp_ref, hn_ref, cos_ref, sin_ref, cw_ref, cb_ref, g_ref, gbias_ref,
                       q_ref, k_ref, go_ref):
    nh2 = 2 * ML_HEADS
    g = g_ref[0] + gbias_ref[...]
    is_fwd = lax.broadcasted_iota(jnp.int32, (nh2, 1), 0) < ML_HEADS
    li = g[0:nh2]
    lf = _log_sigmoid(g[nh2:])
    ps = _chunk_scan(lf, jnp.add, 0.0, False)
    ss = _chunk_scan(lf, jnp.add, 0.0, True)
    bcum = jnp.where(is_fwd, ps, ss)
    r = li - bcum
    pm = _chunk_scan(r, jnp.maximum, MASK_NEG, False)
    sm = _chunk_scan(r, jnp.maximum, MASK_NEG, True)
    rmax = jnp.maximum(pm, sm)
    groups = (bcum, r, jnp.where(is_fwd, pm, sm), jnp.exp(r - rmax), rmax, ps + ss - lf)
    for j, val in enumerate(groups):
        go_ref[0, nh2 * j:nh2 * (j + 1), :] = val

    i = pl.program_id(1)
    W = ML_WIDTH
    x = x_ref[0]
    tr = x.shape[0]
    rid = lax.broadcasted_iota(jnp.int32, (tr, 1), 0)
    i_row = jnp.zeros((1, 2 * W), jnp.int32) + i
    prev = jnp.where(i_row == 0, 0.0, hp_ref[0, 7:8, :])
    nxt = jnp.where(i_row == n_blocks - 1, 0.0, hn_ref[0, 0:1, :])
    xm1 = jnp.where(rid == 0, prev, pltpu.roll(x, 1, 0))
    xp1 = jnp.where(rid == tr - 1, nxt, pltpu.roll(x, tr - 1, 0))
    y = xm1 * cw_ref[0:1, :] + x * cw_ref[1:2, :] + xp1 * cw_ref[2:3, :] + cb_ref[...]
    y = y * jax.nn.sigmoid(y)
    cos = cos_ref[...]
    sin = sin_ref[...]
    parts = [_rope(y[:, j * LANE:(j + 1) * LANE], cos, sin) for j in range(2 * W // LANE)]
    q_ref[0] = (jnp.concatenate(parts[:W // LANE], axis=1) * (ML_HEAD_DIM ** -0.5)).astype(BF16)
    k_ref[0] = jnp.concatenate(parts[W // LANE:], axis=1).astype(BF16)


def _mlstm_prep(ml, cos, sin, conv_w, conv_b, gates_t, gbias):
    B, T, _ = ml.shape
    n_g = gates_t.shape[1]
    W = ML_WIDTH
    tr = min(T, 1024)
    nb = T // tr
    halo_per_blk = tr // 8
    n_hblk = T // 8
    whole = lambda shape: pl.BlockSpec(shape, lambda b, i: (0, 0))
    return pl.pallas_call(
        functools.partial(_mlstm_prep_kernel, nb),
        grid=(B, nb),
        in_specs=[pl.BlockSpec((1, tr, 2 * W), lambda b, i: (b, i, 0)),
                  pl.BlockSpec((1, 8, 2 * W), lambda b, i: (b, jnp.maximum(i * halo_per_blk - 1, 0), 0)),
                  pl.BlockSpec((1, 8, 2 * W),
                               lambda b, i: (b, jnp.minimum((i + 1) * halo_per_blk, n_hblk - 1), 0)),
                  pl.BlockSpec((tr, LANE), lambda b, i: (i, 0)),
                  pl.BlockSpec((tr, LANE), lambda b, i: (i, 0)),
                  whole((3, 2 * W)), whole((1, 2 * W)),
                  pl.BlockSpec((1, n_g, tr), lambda b, i: (b, 0, i)), whole((n_g, 1))],
        out_specs=[pl.BlockSpec((1, tr, W), lambda b, i: (b, i, 0)),
                   pl.BlockSpec((1, tr, W), lambda b, i: (b, i, 0)),
                   pl.BlockSpec((1, 2 * ML_HEADS * N_GATE_ROWS, tr), lambda b, i: (b, 0, i))],
        out_shape=[jax.ShapeDtypeStruct((B, T, W), BF16), jax.ShapeDtypeStruct((B, T, W), BF16),
                   jax.ShapeDtypeStruct((B, 2 * ML_HEADS * N_GATE_ROWS, T), F32)],
        compiler_params=_cparams("arbitrary", "arbitrary"),
        name="mlstm_prep",
    )(ml, ml, ml, cos, sin, conv_w, conv_b, gates_t, gbias)


def _mlstm_kernel(n_chunks, qf_ref, kf_ref, vf_ref, gf_ref, qb_ref, kb_ref, vb_ref, gb_ref,
                  m0_ref, s0_ref, hf_ref, hb_ref, mfin_ref, sfin_ref, m_scr, s_scr, bd_scr):
    c = pl.program_id(1)
    L = ML_CHUNK
    HD = ML_HEAD_DIM
    NH = ML_HEADS

    @pl.when(c == 0)
    def _():
        m_scr[...] = m0_ref[0]
        s_scr[...] = s0_ref[0]
        bd_scr[...] = jnp.zeros_like(bd_scr)

    is_fwd = lax.broadcasted_iota(jnp.int32, (2 * NH, 1), 0) < NH
    gq = lambda i: jnp.where(is_fwd, gf_ref[0, 2 * NH * i:2 * NH * (i + 1), :],
                             gb_ref[0, 2 * NH * i:2 * NH * (i + 1), :])
    bcum, r, rcmax, ew, rmax, b_last = (gq(i) for i in range(N_GATE_ROWS))
    m0 = m_scr[...]
    mu = jnp.maximum(m0, rcmax)
    a_t = jnp.exp(m0 - mu)
    emt = jnp.exp(-(bcum + mu))
    m_last = jnp.maximum(m0, rmax)
    a_st = jnp.exp(m0 - m_last)
    g_st = jnp.exp(rmax - m_last)
    m_scr[...] = b_last + m_last
    n_q = 4
    stack = jnp.concatenate([mu, a_t, emt, ew, jnp.zeros((L - n_q * 2 * NH, LANE), F32)], axis=0)
    cols = stack.T

    def col_bcast(q, j):
        return jnp.broadcast_to(cols[:, 8 * q + j:8 * q + j + 1], (L, LANE))

    def row_bcast(x, j):
        return jnp.broadcast_to(x[j:j + 1, :], (L, LANE))

    row = lax.broadcasted_iota(jnp.int32, (L, L), 0)
    col = lax.broadcasted_iota(jnp.int32, (L, L), 1)
    low_half = lax.broadcasted_iota(jnp.int32, (1, LANE), 1) < HD
    own_rows = jnp.concatenate([row < HD, row >= HD], axis=1)
    zero = jnp.zeros((), BF16)

    for d, (q_ref, k_ref, v_ref, out_ref) in enumerate(((qf_ref, kf_ref, vf_ref, hf_ref),
                                                        (qb_ref, kb_ref, vb_ref, hb_ref))):
        valid = (col <= row) if d == 0 else (col >= row)
        for pr in range(NH // 2):
            ps = slice(pr * LANE, (pr + 1) * LANE)
            ja = NH * d + 2 * pr
            jb = ja + 1
            sd = 2 * d + pr
            qp = q_ref[0, :, ps]
            kp = k_ref[0, :, ps]
            vp = v_ref[0, :, ps]
            st = s_scr[sd]
            kcat = jnp.concatenate([jnp.where(low_half, kp, zero), jnp.where(low_half, zero, kp)], axis=0)
            s2 = _nt_dot(qp, kcat)
            e2 = jnp.concatenate(
                [jnp.where(valid, jnp.exp(row_bcast(r, j) - col_bcast(0, j)), 0.0) for j in (ja, jb)], axis=1)
            p2 = (s2 * e2).astype(BF16)
            va = jnp.where(low_half, vp, 1.0)
            vb = jnp.where(low_half, 1.0, vp)
            bd_scr[0:L, 0:LANE] = va.astype(BF16)
            bd_scr[L:2 * L, LANE:2 * LANE] = vb.astype(BF16)
            intra = jnp.dot(p2, bd_scr[...], preferred_element_type=F32)
            inter = jnp.dot(qp, st.astype(BF16), preferred_element_type=F32)
            a2 = jnp.concatenate([col_bcast(1, ja), col_bcast(1, jb)], axis=1)
            tot = intra + a2 * inter
            tot_a = tot[:, 0:LANE]
            tot_b = tot[:, LANE:2 * LANE]
            num = jnp.where(low_half, tot_a, tot_b)
            den = pltpu.roll(jnp.where(low_half, tot_b, tot_a), HD, 1)
            floor = jnp.where(low_half, col_bcast(2, ja), col_bcast(2, jb))
            out_ref[0, :, ps] = num / jnp.maximum(jnp.abs(den), floor)

            vw = jnp.concatenate([va * col_bcast(3, ja), vb * col_bcast(3, jb)], axis=1).astype(BF16)
            upd = _tn_dot(kp, vw)
            a_s = jnp.concatenate([row_bcast(a_st, ja), row_bcast(a_st, jb)], axis=1)
            g_s = jnp.concatenate([row_bcast(g_st, ja), row_bcast(g_st, jb)], axis=1)
            s_scr[sd] = a_s * st + g_s * jnp.where(own_rows, upd, 0.0)

    @pl.when(c == n_chunks - 1)
    def _():
        mfin_ref[0] = m_scr[...]
        sfin_ref[0] = s_scr[...]


def _mlstm(q, k, ml, grows, m0, s0):
    B, T, W = q.shape
    L = ML_CHUNK
    nc = T // L

    def specs(ci):
        return [pl.BlockSpec((1, L, W), lambda b, c: (b, ci(c), 0)),
                pl.BlockSpec((1, L, W), lambda b, c: (b, ci(c), 0)),
                pl.BlockSpec((1, L, W), lambda b, c: (b, ci(c), 2)),
                pl.BlockSpec((1, grows.shape[1], L), lambda b, c: (b, 0, ci(c)))]

    m_spec = pl.BlockSpec((1,) + m0.shape[1:], lambda b, c: (b, 0, 0))
    s_spec = pl.BlockSpec((1,) + s0.shape[1:], lambda b, c: (b, 0, 0, 0))
    dir_args = [q, k, ml, grows]
    return pl.pallas_call(
        functools.partial(_mlstm_kernel, nc),
        grid=(B, nc),
        in_specs=specs(lambda c: c) + specs(lambda c: nc - 1 - c) + [m_spec, s_spec],
        out_specs=[pl.BlockSpec((1, L, W), lambda b, c: (b, c, 0)),
                   pl.BlockSpec((1, L, W), lambda b, c: (b, nc - 1 - c, 0)),
                   m_spec, s_spec],
        out_shape=[jax.ShapeDtypeStruct((B, T, W), F32),
                   jax.ShapeDtypeStruct((B, T, W), F32),
                   jax.ShapeDtypeStruct(m0.shape, F32),
                   jax.ShapeDtypeStruct(s0.shape, F32)],
        scratch_shapes=[pltpu.VMEM(m0.shape[1:], F32), pltpu.VMEM(s0.shape[1:], F32),
                        pltpu.VMEM((2 * L, 2 * LANE), BF16)],
        compiler_params=_cparams("arbitrary", "arbitrary"),
        name="mlstm",
    )(*dir_args, *dir_args, m0, s0)


def _softmax_pv(s_list, v_list):
    m = functools.reduce(jnp.maximum, [jnp.max(s, -1, keepdims=True) for s in s_list])
    ps = [jnp.exp(s - m) for s in s_list]
    l = functools.reduce(jnp.add, [jnp.sum(p, -1, keepdims=True) for p in ps])
    o = functools.reduce(jnp.add, [jnp.dot(p.astype(BF16), v, preferred_element_type=F32)
                                   for p, v in zip(ps, v_list)])
    return o / l


def _na_kernel(n_rows, q_ref, *refs):
    nb = NA_BAND // NA_KBLK_ROWS
    k_blks = refs[0:nb]
    v_blks = refs[nb:2 * nb]
    kc_ref, vc_ref, bt_ref, o_ref, kband, vband = refs[2 * nb:]
    i = pl.program_id(1)
    blk_tok = NA_KBLK_ROWS * GRID_W
    HD = NA_HEAD_DIM
    GW = NA_GROUP * HD
    head_of_lane = lax.broadcasted_iota(jnp.int32, (1, GW), 1) // HD
    for j in range(nb):
        kband[j * blk_tok:(j + 1) * blk_tok, :] = k_blks[j][0]
        vband[j * blk_tok:(j + 1) * blk_tok, :] = v_blks[j][0]
    band_row0 = jnp.clip(i * NA_QROWS - NA_ROWS // 2, 0, n_rows - NA_BAND)
    n_win = NA_ROWS * GRID_W
    zero = jnp.zeros((), BF16)

    def rows_body(t, carry):
        items = []
        for u in range(NA_UNROLL):
            a = t * NA_UNROLL + u
            r = i * NA_QROWS + a
            r0 = jnp.clip(r - NA_ROWS // 2, 0, n_rows - NA_ROWS)
            koff = pl.multiple_of((r0 - band_row0) * GRID_W, GRID_W)
            dr_first = r0 - r + (NA_ROWS - 1)
            qoff = pl.multiple_of(a * GRID_W, GRID_W)
            for gi in range(NA_HEADS // NA_GROUP):
                items.append((koff, dr_first, qoff, gi, slice(gi * GW, (gi + 1) * GW)))
        scores = []
        for koff, _, qoff, _, gs in items:
            qg = q_ref[0, pl.ds(qoff, GRID_W), gs]
            qm = jnp.concatenate([jnp.where(head_of_lane == g, qg, zero) for g in range(NA_GROUP)], axis=0)
            scores.append((_nt_dot(kband[pl.ds(koff, n_win), gs], qm),
                           _nt_dot(kc_ref[0, :, gs], qm)))
        probs = []
        for (_, dr_first, _, gi, _), (s_loc, s_ctx) in zip(items, scores):
            s_loc = s_loc + bt_ref[dr_first, gi]
            m = jnp.maximum(jnp.max(s_loc, 0, keepdims=True), jnp.max(s_ctx, 0, keepdims=True))
            e_loc = jnp.exp(s_loc - m)
            e_ctx = jnp.exp(s_ctx - m)
            inv = 1.0 / (jnp.sum(e_loc, 0, keepdims=True) + jnp.sum(e_ctx, 0, keepdims=True))
            probs.append(((e_loc * inv).astype(BF16), (e_ctx * inv).astype(BF16)))
        for (koff, _, qoff, _, gs), (p_loc, p_ctx) in zip(items, probs):
            res = (_tn_dot(p_loc, vband[pl.ds(koff, n_win), gs])
                   + _tn_dot(p_ctx, vc_ref[0, :, gs]))
            out = res[0:GRID_W]
            for g in range(1, NA_GROUP):
                out = jnp.where(head_of_lane == g, res[g * GRID_W:(g + 1) * GRID_W], out)
            o_ref[0, pl.ds(qoff, GRID_W), gs] = out.astype(BF16)
        return carry

    lax.fori_loop(0, NA_QROWS // NA_UNROLL, rows_body, 0)


def _na_latent(q, k, v, kc, vc, bias_tab):
    B, T, Wd = q.shape
    Lc = kc.shape[1]
    n_rows = T // GRID_W
    nb = NA_BAND // NA_KBLK_ROWS
    q_tok = NA_QROWS * GRID_W
    blk_tok = NA_KBLK_ROWS * GRID_W
    last_blk0 = (n_rows - NA_BAND) // NA_KBLK_ROWS

    def kv_spec(j):
        def idx(b, i):
            first = jnp.clip(i * (NA_QROWS // NA_KBLK_ROWS) - (NA_ROWS // 2) // NA_KBLK_ROWS, 0, last_blk0)
            return (b, first + j, 0)
        return pl.BlockSpec((1, blk_tok, Wd), idx)

    return pl.pallas_call(
        functools.partial(_na_kernel, n_rows),
        grid=(B, n_rows // NA_QROWS),
        in_specs=([pl.BlockSpec((1, q_tok, Wd), lambda b, i: (b, i, 0))]
                  + [kv_spec(j) for j in range(nb)] + [kv_spec(j) for j in range(nb)]
                  + [pl.BlockSpec((1, Lc, Wd), lambda b, i: (b, 0, 0)),
                     pl.BlockSpec((1, Lc, Wd), lambda b, i: (b, 0, 0)),
                     pl.BlockSpec(bias_tab.shape, lambda b, i: (0, 0, 0, 0))]),
        out_specs=pl.BlockSpec((1, q_tok, Wd), lambda b, i: (b, i, 0)),
        out_shape=jax.ShapeDtypeStruct((B, T, Wd), BF16),
        scratch_shapes=[pltpu.VMEM((NA_BAND * GRID_W, Wd), BF16),
                        pltpu.VMEM((NA_BAND * GRID_W, Wd), BF16)],
        compiler_params=_cparams("arbitrary", "arbitrary"),
        name="natten",
    )(q, *([k] * nb), *([v] * nb), kc, vc, bias_tab)


def _ctx_attn_kernel(q_ref, k_ref, v_ref, o_ref):
    HD = NA_HEAD_DIM
    for h in range(NA_HEADS):
        hs = slice(h * HD, (h + 1) * HD)
        s = _nt_dot(q_ref[0, :, hs], k_ref[0, :, hs])
        o_ref[0, :, hs] = _softmax_pv([s], [v_ref[0, :, hs]]).astype(BF16)


def _ctx_attn(q, k, v):
    B, Lc, Wd = q.shape
    spec = pl.BlockSpec((1, Lc, Wd), lambda b: (b, 0, 0))
    return pl.pallas_call(
        _ctx_attn_kernel, grid=(B,), in_specs=[spec, spec, spec], out_specs=spec,
        out_shape=jax.ShapeDtypeStruct((B, Lc, Wd), BF16),
        compiler_params=_cparams("arbitrary"), name="ctx_attn",
    )(q, k, v)


def _na_bias_table(rpb):
    qc = np.arange(GRID_W)[:, None]
    kcol = np.arange(GRID_W)[None, :]
    wstart = np.clip(qc - NA_COLS // 2, 0, GRID_W - NA_COLS)
    col_ok = (kcol >= wstart) & (kcol < wstart + NA_COLS)
    dc = np.clip(kcol - qc, -(NA_COLS - 1), NA_COLS - 1) + NA_COLS - 1
    per_dr = jnp.where(col_ok[None, None], rpb[:, :, dc], MASK_NEG)
    def tab(d):
        n_grp = NA_HEADS // NA_GROUP
        t = per_dr[:, d:d + NA_ROWS].reshape(n_grp, NA_GROUP, NA_ROWS, GRID_W, GRID_W)
        return jnp.transpose(t, (0, 2, 4, 1, 3)).reshape(n_grp, NA_ROWS * GRID_W, NA_GROUP * GRID_W)

    return jnp.stack([tab(d) for d in range(NA_ROWS)]).astype(F32)


def _route(sel, s):
    E = EXP_PER_GROUP
    scores = []
    for g in range(N_GROUPS):
        a, b, c, d = sel[E * g:E * (g + 1)]
        scores.append(functools.reduce(jnp.maximum, [a + b, a + c, a + d, b + c, b + d, c + d]))
    best = jnp.zeros_like(scores[0], dtype=jnp.int32)
    best_score = scores[0]
    for g in range(1, N_GROUPS):
        upd = scores[g] > best_score
        best = jnp.where(upd, g, best)
        best_score = jnp.where(upd, scores[g], best_score)

    def pick(rows, j):
        out = rows[j]
        for g in range(1, N_GROUPS):
            out = jnp.where(best == g, rows[E * g + j], out)
        return out

    v = [pick(sel, j) for j in range(E)]
    sv = [pick(s, j) for j in range(E)]
    i1 = jnp.zeros_like(best)
    m1 = v[0]
    for j in range(1, E):
        upd = v[j] > m1
        i1 = jnp.where(upd, j, i1)
        m1 = jnp.where(upd, v[j], m1)
    i2 = jnp.where(i1 == 0, 1, 0)
    m2 = jnp.where(i1 == 0, v[1], v[0])
    for j in range(1, E):
        upd = (i1 != j) & (v[j] > m2)
        i2 = jnp.where(upd, j, i2)
        m2 = jnp.where(upd, v[j], m2)

    def at(rows, idx):
        out = rows[0]
        for j in range(1, E):
            out = jnp.where(idx == j, rows[j], out)
        return out

    s1 = at(sv, i1)
    s2 = at(sv, i2)
    tot = s1 + s2
    return best * E + i1, best * E + i2, s1 / tot, s2 / tot


def _out_proj_kernel(ygm_ref, hf_ref, hb_ref, o_ref, yna_ref, x_ref, mod_ref, w_ref, mlg_ref, n2g_ref,
                     rw_ref, rb_ref, xn_ref, h2_ref, re_ref, rwt_ref):
    hs = hf_ref[0] + hb_ref[0]
    hsq = hs * hs
    lane = lax.broadcasted_iota(jnp.int32, (1, ML_WIDTH), 1)
    scale = jnp.zeros_like(hs)
    for h in range(ML_HEADS):
        in_head = (lane >= h * ML_HEAD_DIM) & (lane < (h + 1) * ML_HEAD_DIM)
        ms = jnp.sum(jnp.where(in_head, hsq, 0.0), -1, keepdims=True) * (1.0 / ML_HEAD_DIM)
        scale = jnp.where(in_head, lax.rsqrt(ms + EPS), scale)
    yml = hs * scale * mlg_ref[...] * jax.nn.sigmoid(o_ref[0])
    o1 = GM_WIDTH
    o2 = GM_WIDTH + ML_WIDTH
    acc = (jnp.dot(ygm_ref[0], w_ref[0:o1, :], preferred_element_type=F32)
           + jnp.dot(yml.astype(BF16), w_ref[o1:o2, :], preferred_element_type=F32)
           + jnp.dot(yna_ref[0], w_ref[o2:, :], preferred_element_type=F32))
    xn = x_ref[0] + mod_ref[0, 0:1, :] * acc
    xn_ref[0] = xn
    h2 = xn * lax.rsqrt(jnp.mean(xn * xn, -1, keepdims=True) + EPS) * n2g_ref[...]
    h2 = h2 * (1.0 + mod_ref[0, 2:3, :]) + mod_ref[0, 1:2, :]
    h2_ref[0] = h2
    logits = _nt_dot(rw_ref[...], h2, precision=lax.Precision.HIGHEST)
    s = jax.nn.sigmoid(logits)
    sel = s + rb_ref[...]
    rows = lambda m: [m[e:e + 1, :] for e in range(N_EXPERTS)]
    e1, e2, w1, w2 = _route(rows(sel), rows(s))
    re_ref[0, 0:1, :] = e1
    re_ref[0, 1:2, :] = e2
    rwt_ref[0, 0:1, :] = w1
    rwt_ref[0, 1:2, :] = w2


def _out_proj(ygm, hf, hb, ml, yna, x, mod, w_out, ml_g, n2_g, router_wt, router_b):
    G, R, D = x.shape
    tm = min(R, 512)
    row = lambda b, i: (b, i, 0)
    whole2 = lambda shape: pl.BlockSpec(shape, lambda b, i: (0, 0))
    return pl.pallas_call(
        _out_proj_kernel,
        grid=(G, R // tm),
        in_specs=[pl.BlockSpec((1, tm, GM_WIDTH), row),
                  pl.BlockSpec((1, tm, ML_WIDTH), row),
                  pl.BlockSpec((1, tm, ML_WIDTH), row),
                  pl.BlockSpec((1, tm, ML_WIDTH), lambda b, i: (b, i, 3)),
                  pl.BlockSpec((1, tm, NA_WIDTH), row),
                  pl.BlockSpec((1, tm, D), row),
                  pl.BlockSpec((1, 3, D), lambda b, i: (b, 0, 0)),
                  whole2((D, D)), whole2((1, ML_WIDTH)), whole2((1, D)),
                  whole2((N_EXPERTS, D)), whole2((N_EXPERTS, 1))],
        out_specs=[pl.BlockSpec((1, tm, D), row), pl.BlockSpec((1, tm, D), row),
                   pl.BlockSpec((1, 2, tm), lambda b, i: (b, 0, i)),
                   pl.BlockSpec((1, 2, tm), lambda b, i: (b, 0, i))],
        out_shape=[jax.ShapeDtypeStruct((G, R, D), F32), jax.ShapeDtypeStruct((G, R, D), F32),
                   jax.ShapeDtypeStruct((G, 2, R), jnp.int32), jax.ShapeDtypeStruct((G, 2, R), F32)],
        compiler_params=_cparams("arbitrary", "arbitrary"),
        name="out_proj",
    )(ygm, hf, hb, ml, yna, x, mod, w_out, ml_g, n2_g, router_wt, router_b)


def _experts_kernel(te_ref, nu_ref, src_cur, src_nxt, h_hbm, wg_ref, wu_ref, wd_ref, o_ref,
                    acc_ref, wgb, wub, wdb, xbuf, sem):
    i = pl.program_id(0)
    n_used = nu_ref[0]
    used = i < n_used
    slot = lax.rem(i, 2)
    new_expert = (i == 0) | (te_ref[i] != te_ref[jnp.maximum(i - 1, 0)])

    def row_copy(src_ref, s, r):
        return pltpu.make_async_copy(h_hbm.at[pl.ds(src_ref[0, 0, r], 1), :], xbuf.at[s, pl.ds(r, 1), :],
                                     sem.at[s])

    def tile_wait(s):
        pltpu.make_async_copy(xbuf.at[s], xbuf.at[s], sem.at[s]).wait()

    @pl.when((i == 0) & used)
    def _():
        def body(r, carry):
            row_copy(src_cur, 0, r).start()
            return carry
        lax.fori_loop(0, MOE_TILE, body, 0, unroll=8)

    @pl.when(used & new_expert)
    def _():
        def cast_rows(c, carry):
            rs = pl.ds(pl.multiple_of(c * LANE, LANE), LANE)
            wgb[rs, :] = wg_ref[0, 0, rs, :].astype(BF16)
            wub[rs, :] = wu_ref[0, 0, rs, :].astype(BF16)
            wdb[rs, :] = wd_ref[0, 0, rs, :].astype(BF16)
            return carry
        lax.fori_loop(0, D_MODEL // LANE, cast_rows, 0)

    @pl.when(used)
    def _():
        tile_wait(slot)
        x = xbuf[slot].astype(BF16)
        nxt = 1 - slot
        n_chunks = D_EXPERT // EXPERT_CHUNK
        per = MOE_TILE // (2 * n_chunks)
        for j in range(n_chunks):
            cs = slice(j * EXPERT_CHUNK, (j + 1) * EXPERT_CHUNK)
            for r in range(2 * j * per, (2 * j + 1) * per):
                row_copy(src_nxt, nxt, r).start()
            g = jnp.dot(x, wgb[:, cs], preferred_element_type=F32)
            u = jnp.dot(x, wub[:, cs], preferred_element_type=F32)
            a = (g * jax.nn.sigmoid(g) * u).astype(BF16)
            for r in range((2 * j + 1) * per, (2 * j + 2) * per):
                row_copy(src_nxt, nxt, r).start()
            y = jnp.dot(a, wdb[cs, :], preferred_element_type=F32)
            if j == 0:
                acc_ref[...] = y
            else:
                acc_ref[...] += y
        o_ref[...] = acc_ref[...].astype(BF16)

        @pl.when(i + 1 >= n_used)
        def _():
            tile_wait(nxt)

    @pl.when(jnp.logical_not(used))
    def _():
        o_ref[...] = jnp.zeros_like(o_ref)


def _experts(l, tile_expert, n_used, src, h_all, wg, wu, wd):
    D = h_all.shape[1]
    n_tiles = src.shape[0] // MOE_TILE
    src3 = src.reshape(n_tiles, 1, MOE_TILE)
    wspec = lambda r, c: pl.BlockSpec((1, 1, r, c), lambda i, te, nu: (l, te[i], 0, 0))
    src_spec = lambda ahead: pl.BlockSpec(
        (1, 1, MOE_TILE), lambda i, te, nu: (jnp.minimum(i + ahead, n_tiles - 1), 0, 0),
        memory_space=pltpu.SMEM)
    return pl.pallas_call(
        _experts_kernel,
        grid_spec=pltpu.PrefetchScalarGridSpec(
            num_scalar_prefetch=2,
            grid=(n_tiles,),
            in_specs=[src_spec(0), src_spec(1), pl.BlockSpec(memory_space=pl.ANY),
                      wspec(D, D_EXPERT), wspec(D, D_EXPERT), wspec(D_EXPERT, D)],
            out_specs=pl.BlockSpec((MOE_TILE, D), lambda i, te, nu: (i, 0)),
            scratch_shapes=[pltpu.VMEM((MOE_TILE, D), F32), pltpu.VMEM((D, D_EXPERT), BF16),
                            pltpu.VMEM((D, D_EXPERT), BF16), pltpu.VMEM((D_EXPERT, D), BF16),
                            pltpu.VMEM((2, MOE_TILE, D), F32), pltpu.SemaphoreType.DMA((2,))]),
        out_shape=jax.ShapeDtypeStruct((n_tiles * MOE_TILE, D), BF16),
        compiler_params=_cparams("arbitrary"),
        name="experts",
    )(tile_expert, n_used, src3, src3, h_all, wg, wu, wd)


def _dispatch(e_idx):
    n_tok = e_idx.shape[0]
    n_asg = 2 * n_tok
    n_tiles = -(-n_asg // MOE_TILE) + N_EXPERTS
    flat_e = e_idx.reshape(-1)
    onehot = (flat_e[:, None] == jnp.arange(N_EXPERTS, dtype=jnp.int32)[None, :]).astype(jnp.int32)
    csum = jnp.cumsum(onehot, axis=0)
    rank = jnp.take_along_axis(csum, flat_e[:, None], axis=1)[:, 0] - 1
    counts = csum[-1]
    tiles_e = (counts + MOE_TILE - 1) // MOE_TILE
    tile_end = jnp.cumsum(tiles_e)
    tile_start = tile_end - tiles_e
    dest = tile_start[flat_e] * MOE_TILE + rank
    n_used = tile_end[-1]
    tid = jnp.arange(n_tiles, dtype=jnp.int32)
    te = jnp.minimum(jnp.searchsorted(tile_end, tid, side="right"), N_EXPERTS - 1).astype(jnp.int32)
    te = jnp.where(tid < n_used, te, te[jnp.maximum(n_used - 1, 0)])
    flat_t = jnp.arange(n_asg, dtype=jnp.int32) // 2
    src = jnp.zeros((n_tiles * MOE_TILE,), jnp.int32).at[dest].set(
        flat_t, unique_indices=True, mode="promise_in_bounds")
    return dest.reshape(n_tok, 2), src, te, n_used.reshape(1).astype(jnp.int32)


def _combine_kernel(final, xn_ref, y0_ref, y1_ref, w_ref, g2_ref, fg_ref, o_ref):
    w = w_ref[0]
    f = y0_ref[0].astype(F32) * w[:, 0:1] + y1_ref[0].astype(F32) * w[:, 1:2]
    x = xn_ref[0] + g2_ref[0] * f
    if final:
        x = x * lax.rsqrt(jnp.mean(x * x, -1, keepdims=True) + EPS) * fg_ref[...]
    o_ref[0] = x


def _combine(xn, y0, y1, wt, g2, final_g, final):
    G, R, D = xn.shape
    tm = min(R, 512)
    row = lambda b, i: (b, i, 0)
    return pl.pallas_call(
        functools.partial(_combine_kernel, final),
        grid=(G, R // tm),
        in_specs=[pl.BlockSpec((1, tm, D), row), pl.BlockSpec((1, tm, D), row), pl.BlockSpec((1, tm, D), row),
                  pl.BlockSpec((1, tm, 2), row),
                  pl.BlockSpec((1, 1, D), lambda b, i: (b, 0, 0)),
                  pl.BlockSpec((1, D), lambda b, i: (0, 0))],
        out_specs=pl.BlockSpec((1, tm, D), row),
        out_shape=jax.ShapeDtypeStruct((G, R, D), F32),
        compiler_params=_cparams("arbitrary", "arbitrary"),
        name="combine_final" if final else "combine",
    )(xn, y0, y1, wt, g2, final_g)


def _rope_tables(T):
    lane = np.arange(LANE)
    half = ML_HEAD_DIM // 4
    inv = jnp.tile(ROPE_BASE ** (-jnp.arange(half, dtype=F32) / half), LANE // half)
    t = jnp.arange(T)
    pos = jnp.where(((lane // (2 * half)) % 2 == 0)[None, :], (t // GRID_W)[:, None], (t % GRID_W)[:, None])
    ang = pos.astype(F32) * inv[None, :]
    sign = np.where((lane // half) % 2 == 0, -1.0, 1.0).astype(np.float32)
    return jnp.cos(ang), jnp.sin(ang) * sign[None, :]


def _reorder_w_in(w_in):
    pad = jnp.zeros(w_in.shape[:2] + (LANE - 4 * ML_HEADS,), w_in.dtype)
    return jnp.concatenate([w_in[..., :OFF_GATES], w_in[..., OFF_NA:], w_in[..., OFF_GATES:OFF_NA], pad],
                           axis=-1).astype(BF16)


def _mixers(l, p, pc, prm, need_ctx):
    gm, ml, gt, q, k, v = p
    gmc, mlc, gtc, qc, kc, vc = pc
    B = ml.shape[0]
    gm_args = (prm["gm_ws"][l], prm["gm_bs_full"][l], prm["gm_g"][l])
    conv = (prm["conv_w"][l], prm["conv_b"][l])
    nh = ML_HEADS
    tr = lambda g: jnp.transpose(
        jnp.concatenate([g[..., 0:nh], g[..., 2 * nh:3 * nh], g[..., nh:2 * nh], g[..., 3 * nh:4 * nh]], -1),
        (0, 2, 1))
    m_zero = jnp.zeros((B, 2 * nh, LANE), F32)
    s_zero = jnp.zeros((B, nh, ML_CHUNK, 2 * LANE), F32)
    qmc, kmc, grc = _mlstm_prep(mlc, prm["cos_c"], prm["sin_c"], *conv, tr(gtc), prm["gbias"][l])
    hfc, hbc, m_st, s_st = _mlstm(qmc, kmc, mlc, grc, m_zero, s_zero)
    qm, km, gr = _mlstm_prep(ml, prm["cos_l"], prm["sin_l"], *conv, tr(gt), prm["gbias"][l])
    hf, hb, _, _ = _mlstm(qm, km, ml, gr, m_st, s_st)
    y_na = _na_latent(q, k, v, kc, vc, prm["na_tab"][l])
    y = (_gmlp(gm, *gm_args), hf, hb, ml, y_na)
    if not need_ctx:
        return y, None
    return y, (_gmlp(gmc, *gm_args), hfc, hbc, mlc, _ctx_attn(qc, kc, vc))


def _rows(a, idx):
    return a.at[idx].get(mode="promise_in_bounds")


def _moe(l, h2_all, e_all, prm):
    dest, src, te, n_used = _dispatch(e_all)
    yb = _experts(l, te, n_used, src, h2_all, prm["wg"], prm["wu"], prm["wd"])
    return yb, dest


def kernel(x, c, ctx, c_ctx, ada_w, ada_b, norm1_g, norm2_g, w_in, w_out, gm_ws, gm_bs, gm_norm_g,
           ml_conv_w, ml_conv_b, ml_gate_b, ml_norm_g, na_rpb, router_w, router_b,
           moe_w_gate, moe_w_up, moe_w_down, final_g):
    B, T, D = x.shape
    Lc = ctx.shape[1]
    cos_l, sin_l = _rope_tables(T)
    prm = dict(
        gm_ws=gm_ws.astype(BF16),
        gm_bs_full=jnp.repeat(jnp.transpose(gm_bs, (0, 2, 1)), GM_HEAD_DIM, axis=-1),
        gm_g=gm_norm_g[:, None, :],
        conv_w=ml_conv_w, conv_b=ml_conv_b[:, None, :],
        gbias=ml_gate_b[:, jnp.array([0, 2, 1, 3])].reshape(DEPTH, 4 * ML_HEADS, 1),
        cos_l=cos_l, sin_l=sin_l,
        cos_c=jnp.ones((Lc, LANE), F32), sin_c=jnp.zeros((Lc, LANE), F32),
        na_tab=jnp.stack([_na_bias_table(na_rpb[l]) for l in range(DEPTH)]),
        wg=moe_w_gate, wu=moe_w_up, wd=moe_w_down,
    )
    w_in_r = _reorder_w_in(w_in)
    w_out_b = w_out.astype(BF16)
    router_wt = jnp.transpose(router_w)
    router_bc = router_b[:, None]

    cs = jnp.concatenate([c, c_ctx[None, :], jnp.zeros((8 - B - 1, D), F32)], axis=0)
    mods = _ada_all(cs, ada_w, ada_b).reshape(DEPTH, 8, 6, D)

    xc = ctx
    for l in range(DEPTH):
        need_ctx = l < DEPTH - 1
        mod_l = mods[l, :B]
        mod_c = jnp.broadcast_to(mods[l, B:B + 1], (B, 6, D))
        n1 = norm1_g[l][None, :]
        p = _in_proj(x, mod_l[:, 0:2], n1, w_in_r[l])
        pc = _in_proj(xc, mod_c[:, 0:2], n1, w_in_r[l])
        y, yc = _mixers(l, p, pc, prm, need_ctx)
        op_args = (w_out_b[l], ml_norm_g[l][None, :], norm2_g[l][None, :], router_wt, router_bc)
        xn, h2, re, rw = _out_proj(*y, x, mod_l[:, 2:5], *op_args)
        h2_all = h2.reshape(B * T, D)
        e_all = jnp.transpose(re, (0, 2, 1)).reshape(B * T, 2)
        if need_ctx:
            xnc, h2c, rec, rwc = _out_proj(*yc, xc, mod_c[:, 2:5], *op_args)
            h2_all = jnp.concatenate([h2_all, h2c.reshape(B * Lc, D)], axis=0)
            e_all = jnp.concatenate([e_all, jnp.transpose(rec, (0, 2, 1)).reshape(B * Lc, 2)], axis=0)
        yb, dest = _moe(l, h2_all, e_all, prm)
        final = l == DEPTH - 1
        fg = final_g[None, :]
        n_lat = B * T
        picked = lambda lo, hi, k, L: _rows(yb, dest[lo:hi, k]).reshape(B, L, D)
        x = _combine(xn, picked(0, n_lat, 0, T), picked(0, n_lat, 1, T),
                     jnp.transpose(rw, (0, 2, 1)), mod_l[:, 5:6], fg, final)
        if need_ctx:
            xc = _combine(xnc, picked(n_lat, None, 0, Lc), picked(n_lat, None, 1, Lc),
                          jnp.transpose(rwc, (0, 2, 1)), mod_c[:, 5:6], fg, False)
    return x
```

```python
import functools

import jax
import jax.numpy as jnp
import numpy as np
from jax import lax
from jax.experimental import pallas as pl
from jax.experimental.pallas import tpu as pltpu

F32 = jnp.float32
BF16 = jnp.bfloat16

D_MODEL = 1024
DEPTH = 4
GRID_W = 64
EPS = 1e-6

GM_HEADS = 4
GM_WIDTH = D_MODEL // 4
GM_HEAD_DIM = GM_WIDTH // GM_HEADS
GM_CHUNK = 128

ML_HEADS = 4
ML_WIDTH = D_MODEL // 4
ML_HEAD_DIM = ML_WIDTH // ML_HEADS
ML_CHUNK = 128

NA_HEADS = 8
NA_WIDTH = D_MODEL // 2
NA_HEAD_DIM = NA_WIDTH // NA_HEADS
NA_ROWS = 8
NA_COLS = 16

ROPE_BASE = 10000.0

OFF_ML = 2 * GM_WIDTH
OFF_GATES = OFF_ML + 4 * ML_WIDTH
OFF_NA = OFF_GATES + 4 * ML_HEADS
N_IN = OFF_NA + 3 * NA_WIDTH

N_EXPERTS = 16
N_GROUPS = 4
EXP_PER_GROUP = N_EXPERTS // N_GROUPS
D_EXPERT = D_MODEL

LANE = 128
SEG_GM = (0, 2 * GM_WIDTH)
SEG_ML = (SEG_GM[1], SEG_GM[1] + 4 * ML_WIDTH)
SEG_Q = (SEG_ML[1], SEG_ML[1] + NA_WIDTH)
SEG_K = (SEG_Q[1], SEG_Q[1] + NA_WIDTH)
SEG_V = (SEG_K[1], SEG_K[1] + NA_WIDTH)
SEG_GATES = (SEG_V[1], SEG_V[1] + LANE)
N_IN_PAD = SEG_GATES[1]

NA_QROWS = 8
NA_GROUP = 4
NA_UNROLL = 4
NA_BAND = 16
NA_KBLK_ROWS = 4
MOE_TILE = 512
EXPERT_CHUNK = 512
MASK_NEG = -1e30
N_GATE_ROWS = 6
VMEM_LIMIT = 56 * 2 ** 20


def _cparams(*sem):
    return pltpu.CompilerParams(dimension_semantics=sem, vmem_limit_bytes=VMEM_LIMIT)


def _nt_dot(a, b, precision=None):
    return lax.dot_general(a, b, (((1,), (1,)), ((), ())), precision=precision,
                           preferred_element_type=F32)


def _tn_dot(a, b):
    return lax.dot_general(a, b, (((0,), (0,)), ((), ())), preferred_element_type=F32)


def _ada_kernel(c_ref, w_ref, b_ref, o_ref):
    c = c_ref[...]
    sc = c * jax.nn.sigmoid(c)
    o_ref[0] = jnp.dot(sc, w_ref[0], preferred_element_type=F32) + b_ref[0]


def _ada_all(cs, ada_w, ada_b):
    n_out = ada_w.shape[-1] // D_MODEL
    return pl.pallas_call(
        _ada_kernel,
        grid=(DEPTH, n_out),
        in_specs=[pl.BlockSpec((8, D_MODEL), lambda l, j: (0, 0)),
                  pl.BlockSpec((1, D_MODEL, D_MODEL), lambda l, j: (l, 0, j)),
                  pl.BlockSpec((1, 1, D_MODEL), lambda l, j: (l, 0, j))],
        out_specs=pl.BlockSpec((1, 8, D_MODEL), lambda l, j: (l, 0, j)),
        out_shape=jax.ShapeDtypeStruct((DEPTH, 8, n_out * D_MODEL), F32),
        compiler_params=_cparams("arbitrary", "arbitrary"),
        name="adaln",
    )(cs, ada_w, ada_b.reshape(DEPTH, 1, -1))


def _in_proj_kernel(x_ref, mod_ref, g_ref, w_ref, gm_ref, ml_ref, gt_ref, q_ref, k_ref, v_ref):
    x = x_ref[0]
    h = x * lax.rsqrt(jnp.mean(x * x, -1, keepdims=True) + EPS) * g_ref[...]
    h = h * (1.0 + mod_ref[0, 1:2, :]) + mod_ref[0, 0:1, :]
    hb = h.astype(BF16)

    def seg(s):
        return jnp.dot(hb, w_ref[:, s[0]:s[1]], preferred_element_type=F32)

    gm_ref[0] = seg(SEG_GM)
    ml_ref[0] = seg(SEG_ML)
    gt_ref[0] = seg(SEG_GATES)
    q_ref[0] = (seg(SEG_Q) * (NA_HEAD_DIM ** -0.5)).astype(BF16)
    k_ref[0] = seg(SEG_K).astype(BF16)
    v_ref[0] = seg(SEG_V).astype(BF16)


def _in_proj(x, mod, g, w):
    G, R, D = x.shape
    tm = min(R, 512)
    row = lambda b, i: (b, i, 0)
    widths = (SEG_GM[1] - SEG_GM[0], SEG_ML[1] - SEG_ML[0], LANE, NA_WIDTH, NA_WIDTH, NA_WIDTH)
    dtypes = (F32, F32, F32, BF16, BF16, BF16)
    return pl.pallas_call(
        _in_proj_kernel,
        grid=(G, R // tm),
        in_specs=[pl.BlockSpec((1, tm, D), row),
                  pl.BlockSpec((1, 2, D), lambda b, i: (b, 0, 0)),
                  pl.BlockSpec((1, D), lambda b, i: (0, 0)),
                  pl.BlockSpec((D, N_IN_PAD), lambda b, i: (0, 0))],
        out_specs=[pl.BlockSpec((1, tm, n), row) for n in widths],
        out_shape=[jax.ShapeDtypeStruct((G, R, n), dt) for n, dt in zip(widths, dtypes)],
        compiler_params=_cparams("arbitrary", "arbitrary"),
        name="in_proj",
    )(x, mod, g, w)


def _gmlp_kernel(n_chunks, p_ref, ws_ref, bs_ref, g_ref, o_ref):
    lane = lax.broadcasted_iota(jnp.int32, (1, GM_WIDTH), 1)

    def chunk(j, carry):
        r0 = pl.multiple_of(j * GM_CHUNK, GM_CHUNK)
        z = jax.nn.gelu(p_ref[0, pl.ds(r0, GM_CHUNK), :])
        u = z[:, :GM_WIDTH]
        v = z[:, GM_WIDTH:]
        mu = jnp.mean(v, -1, keepdims=True)
        vc = v - mu
        var = jnp.mean(vc * vc, -1, keepdims=True)
        vn = (vc * lax.rsqrt(var + EPS) * g_ref[...]).astype(BF16)
        sv = jnp.zeros((GM_CHUNK, GM_WIDTH), F32)
        for h in range(GM_HEADS):
            full = jnp.dot(ws_ref[h], vn, preferred_element_type=F32)
            in_head = (lane >= h * GM_HEAD_DIM) & (lane < (h + 1) * GM_HEAD_DIM)
            sv = jnp.where(in_head, full, sv)
        o_ref[0, pl.ds(r0, GM_CHUNK), :] = (u * (sv + bs_ref[...])).astype(BF16)
        return carry

    lax.fori_loop(0, n_chunks, chunk, 0)


def _gmlp(p_gm, ws, bs_full, g):
    G, R, _ = p_gm.shape
    tr = min(R, 1024)
    return pl.pallas_call(
        functools.partial(_gmlp_kernel, tr // GM_CHUNK),
        grid=(G, R // tr),
        in_specs=[pl.BlockSpec((1, tr, 2 * GM_WIDTH), lambda b, i: (b, i, 0)),
                  pl.BlockSpec((GM_HEADS, GM_CHUNK, GM_CHUNK), lambda b, i: (0, 0, 0)),
                  pl.BlockSpec((GM_CHUNK, GM_WIDTH), lambda b, i: (0, 0)),
                  pl.BlockSpec((1, GM_WIDTH), lambda b, i: (0, 0))],
        out_specs=pl.BlockSpec((1, tr, GM_WIDTH), lambda b, i: (b, i, 0)),
        out_shape=jax.ShapeDtypeStruct((G, R, GM_WIDTH), BF16),
        compiler_params=_cparams("arbitrary", "arbitrary"),
        name="gmlp",
    )(p_gm, ws, bs_full, g)


def _log_sigmoid(x):
    return jnp.minimum(x, 0.0) - jnp.log1p(jnp.exp(-jnp.abs(x)))


def _rope(x, cos, sin_signed):
    lane = lax.broadcasted_iota(jnp.int32, (1, LANE), 1)
    first_half = (lane & 16) == 0
    partner = jnp.where(first_half, pltpu.roll(x, LANE - 16, 1), pltpu.roll(x, 16, 1))
    return x * cos + partner * sin_signed


def _chunk_scan(x, op, fill, reverse):
    pos = lax.broadcasted_iota(jnp.int32, (1, x.shape[1]), 1) % ML_CHUNK
    n = x.shape[1]
    sh = 1
    while sh < ML_CHUNK:
        if reverse:
            shifted = jnp.where(pos < ML_CHUNK - sh, pltpu.roll(x, n - sh, 1), fill)
        else:
            shifted = jnp.where(pos >= sh, pltpu.roll(x, sh, 1), fill)
        x = op(x, shifted)
        sh *= 2
    return x


def _mlstm_prep_kernel(n_blocks, x_ref, hp_ref, hn_ref, cos_ref, sin_ref, cw_ref, cb_ref, g_ref, gbias_ref,
                       q_ref, k_ref, go_ref):
    nh2 = 2 * ML_HEADS
    g = g_ref[0] + gbias_ref[...]
    is_fwd = lax.broadcasted_iota(jnp.int32, (nh2, 1), 0) < ML_HEADS
    li = g[0:nh2]
    lf = _log_sigmoid(g[nh2:])
    ps = _chunk_scan(lf, jnp.add, 0.0, False)
    ss = _chunk_scan(lf, jnp.add, 0.0, True)
    bcum = jnp.where(is_fwd, ps, ss)
    r = li - bcum
    pm = _chunk_scan(r, jnp.maximum, MASK_NEG, False)
    sm = _chunk_scan(r, jnp.maximum, MASK_NEG, True)
    rmax = jnp.maximum(pm, sm)
    groups = (bcum, r, jnp.where(is_fwd, pm, sm), jnp.exp(r - rmax), rmax, ps + ss - lf)
    for j, val in enumerate(groups):
        go_ref[0, nh2 * j:nh2 * (j + 1), :] = val

    i = pl.program_id(1)
    W = ML_WIDTH
    x = x_ref[0]
    tr = x.shape[0]
    rid = lax.broadcasted_iota(jnp.int32, (tr, 1), 0)
    i_row = jnp.zeros((1, 2 * W), jnp.int32) + i
    prev = jnp.where(i_row == 0, 0.0, hp_ref[0, 7:8, :])
    nxt = jnp.where(i_row == n_blocks - 1, 0.0, hn_ref[0, 0:1, :])
    xm1 = jnp.where(rid == 0, prev, pltpu.roll(x, 1, 0))
    xp1 = jnp.where(rid == tr - 1, nxt, pltpu.roll(x, tr - 1, 0))
    y = xm1 * cw_ref[0:1, :] + x * cw_ref[1:2, :] + xp1 * cw_ref[2:3, :] + cb_ref[...]
    y = y * jax.nn.sigmoid(y)
    cos = cos_ref[...]
    sin = sin_ref[...]
    parts = [_rope(y[:, j * LANE:(j + 1) * LANE], cos, sin) for j in range(2 * W // LANE)]
    q_ref[0] = (jnp.concatenate(parts[:W // LANE], axis=1) * (ML_HEAD_DIM ** -0.5)).astype(BF16)
    for j in range(W // LANE):
        k_ref[0, j * LANE:(j + 1) * LANE, :] = parts[W // LANE + j].T.astype(BF16)


def _mlstm_prep(ml, cos, sin, conv_w, conv_b, gates_t, gbias):
    B, T, _ = ml.shape
    n_g = gates_t.shape[1]
    W = ML_WIDTH
    tr = min(T, 1024)
    nb = T // tr
    halo_per_blk = tr // 8
    n_hblk = T // 8
    whole = lambda shape: pl.BlockSpec(shape, lambda b, i: (0, 0))
    return pl.pallas_call(
        functools.partial(_mlstm_prep_kernel, nb),
        grid=(B, nb),
        in_specs=[pl.BlockSpec((1, tr, 2 * W), lambda b, i: (b, i, 0)),
                  pl.BlockSpec((1, 8, 2 * W), lambda b, i: (b, jnp.maximum(i * halo_per_blk - 1, 0), 0)),
                  pl.BlockSpec((1, 8, 2 * W),
                               lambda b, i: (b, jnp.minimum((i + 1) * halo_per_blk, n_hblk - 1), 0)),
                  pl.BlockSpec((tr, LANE), lambda b, i: (i, 0)),
                  pl.BlockSpec((tr, LANE), lambda b, i: (i, 0)),
                  whole((3, 2 * W)), whole((1, 2 * W)),
                  pl.BlockSpec((1, n_g, tr), lambda b, i: (b, 0, i)), whole((n_g, 1))],
        out_specs=[pl.BlockSpec((1, tr, W), lambda b, i: (b, i, 0)),
                   pl.BlockSpec((1, W, tr), lambda b, i: (b, 0, i)),
                   pl.BlockSpec((1, 2 * ML_HEADS * N_GATE_ROWS, tr), lambda b, i: (b, 0, i))],
        out_shape=[jax.ShapeDtypeStruct((B, T, W), BF16), jax.ShapeDtypeStruct((B, W, T), BF16),
                   jax.ShapeDtypeStruct((B, 2 * ML_HEADS * N_GATE_ROWS, T), F32)],
        compiler_params=_cparams("arbitrary", "arbitrary"),
        name="mlstm_prep",
    )(ml, ml, ml, cos, sin, conv_w, conv_b, gates_t, gbias)


def _mlstm_kernel(n_chunks, qf_ref, ktf_ref, vf_ref, gf_ref, qb_ref, ktb_ref, vb_ref, gb_ref,
                  m0_ref, s0_ref, hf_ref, hb_ref, mfin_ref, sfin_ref, m_scr, s_scr, bd_scr):
    c = pl.program_id(0)
    L = ML_CHUNK
    HD = ML_HEAD_DIM
    NH = ML_HEADS
    n_batch = m_scr.shape[0]

    @pl.when(c == 0)
    def _():
        m_scr[...] = m0_ref[...]
        s_scr[...] = s0_ref[...]
        bd_scr[...] = jnp.zeros_like(bd_scr)

    def row_bcast(x, j):
        return jnp.broadcast_to(x[j:j + 1, :], (L, LANE))

    row = lax.broadcasted_iota(jnp.int32, (L, L), 0)
    col = lax.broadcasted_iota(jnp.int32, (L, L), 1)
    low_half = lax.broadcasted_iota(jnp.int32, (1, LANE), 1) < HD
    own_rows = jnp.concatenate([row < HD, row >= HD], axis=1)
    is_fwd = lax.broadcasted_iota(jnp.int32, (2 * NH, 1), 0) < NH
    zero = jnp.zeros((), BF16)

    gate = []
    for b in range(n_batch):
        gq = lambda i: jnp.where(is_fwd, gf_ref[b, 2 * NH * i:2 * NH * (i + 1), :],
                                 gb_ref[b, 2 * NH * i:2 * NH * (i + 1), :])
        bcum, r, rcmax, ew, rmax, b_last = (gq(i) for i in range(N_GATE_ROWS))
        m0 = m_scr[b]
        mu = jnp.maximum(m0, rcmax)
        emt = jnp.exp(-(bcum + mu))
        m_last = jnp.maximum(m0, rmax)
        m_scr[b] = b_last + m_last
        stack = jnp.concatenate([mu, emt, jnp.zeros((L - 4 * NH, LANE), F32)], axis=0)
        gate.append(dict(r=r, ew=ew, m0=m0, a_st=jnp.exp(m0 - m_last), g_st=jnp.exp(rmax - m_last),
                         cols=stack.T))

    dirs = ((qf_ref, ktf_ref, vf_ref, hf_ref), (qb_ref, ktb_ref, vb_ref, hb_ref))
    items = [(b, d, pr) for b in range(n_batch) for d in range(2) for pr in range(NH // 2)]

    pre = []
    for b, d, pr in items:
        q_ref, kt_ref, v_ref, _ = dirs[d]
        ps = slice(pr * LANE, (pr + 1) * LANE)
        ja = NH * d + 2 * pr
        g = gate[b]
        qp = q_ref[b, :, ps]
        kt = kt_ref[b, ps, :]
        vp = v_ref[b, :, ps]
        st = s_scr[b, 2 * d + pr]
        kt2 = jnp.concatenate([jnp.where(row < HD, kt, zero), jnp.where(row < HD, zero, kt)], axis=1)
        s2 = jnp.dot(qp, kt2, preferred_element_type=F32)
        inter = jnp.dot(qp, st.astype(BF16), preferred_element_type=F32)
        vab = jnp.concatenate([jnp.where(low_half, vp, 1.0), jnp.where(low_half, 1.0, vp)],
                              axis=1).astype(BF16)
        ktw = (kt.astype(F32) * jnp.where(row < HD, row_bcast(g["ew"], ja), row_bcast(g["ew"], ja + 1)))
        upd = jnp.dot(ktw.astype(BF16), vab, preferred_element_type=F32)
        pre.append((s2, inter, vab, upd, st))

    probs = []
    for (b, d, pr), (s2, _, _, _, _) in zip(items, pre):
        ja = NH * d + 2 * pr
        g = gate[b]
        valid = (col <= row) if d == 0 else (col >= row)
        mu_c = [jnp.broadcast_to(g["cols"][:, j:j + 1], (L, LANE)) for j in (ja, ja + 1)]
        e2 = jnp.concatenate([jnp.where(valid, jnp.exp(row_bcast(g["r"], j) - mc), 0.0)
                              for j, mc in zip((ja, ja + 1), mu_c)], axis=1)
        a2 = jnp.concatenate([jnp.exp(row_bcast(g["m0"], j) - mc) for j, mc in zip((ja, ja + 1), mu_c)],
                             axis=1)
        probs.append(((s2 * e2).astype(BF16), a2))

    for (b, d, pr), (_, inter, vab, upd, st), (p2, a2) in zip(items, pre, probs):
        out_ref = dirs[d][3]
        ps = slice(pr * LANE, (pr + 1) * LANE)
        ja = NH * d + 2 * pr
        g = gate[b]
        bd_scr[b, 0:L, 0:LANE] = vab[:, 0:LANE]
        bd_scr[b, L:2 * L, LANE:2 * LANE] = vab[:, LANE:2 * LANE]
        intra = jnp.dot(p2, bd_scr[b], preferred_element_type=F32)
        tot = intra + a2 * inter
        tot_a = tot[:, 0:LANE]
        tot_b = tot[:, LANE:2 * LANE]
        num = jnp.where(low_half, tot_a, tot_b)
        den = pltpu.roll(jnp.where(low_half, tot_b, tot_a), HD, 1)
        floor = jnp.where(low_half, jnp.broadcast_to(g["cols"][:, 8 + ja:8 + ja + 1], (L, LANE)),
                          jnp.broadcast_to(g["cols"][:, 8 + ja + 1:8 + ja + 2], (L, LANE)))
        out_ref[b, :, ps] = num / jnp.maximum(jnp.abs(den), floor)
        a_s = jnp.concatenate([row_bcast(g["a_st"], ja), row_bcast(g["a_st"], ja + 1)], axis=1)
        g_s = jnp.concatenate([row_bcast(g["g_st"], ja), row_bcast(g["g_st"], ja + 1)], axis=1)
        s_scr[b, 2 * d + pr] = a_s * st + g_s * jnp.where(own_rows, upd, 0.0)

    @pl.when(c == n_chunks - 1)
    def _():
        mfin_ref[...] = m_scr[...]
        sfin_ref[...] = s_scr[...]


def _mlstm(q, kt, ml, grows, m0, s0):
    B, T, W = q.shape
    L = ML_CHUNK
    nc = T // L

    def specs(ci):
        return [pl.BlockSpec((B, L, W), lambda c: (0, ci(c), 0)),
                pl.BlockSpec((B, W, L), lambda c: (0, 0, ci(c))),
                pl.BlockSpec((B, L, W), lambda c: (0, ci(c), 2)),
                pl.BlockSpec((B, grows.shape[1], L), lambda c: (0, 0, ci(c)))]

    m_spec = pl.BlockSpec(m0.shape, lambda c: (0, 0, 0))
    s_spec = pl.BlockSpec(s0.shape, lambda c: (0, 0, 0, 0))
    dir_args = [q, kt, ml, grows]
    return pl.pallas_call(
        functools.partial(_mlstm_kernel, nc),
        grid=(nc,),
        in_specs=specs(lambda c: c) + specs(lambda c: nc - 1 - c) + [m_spec, s_spec],
        out_specs=[pl.BlockSpec((B, L, W), lambda c: (0, c, 0)),
                   pl.BlockSpec((B, L, W), lambda c: (0, nc - 1 - c, 0)),
                   m_spec, s_spec],
        out_shape=[jax.ShapeDtypeStruct((B, T, W), F32),
                   jax.ShapeDtypeStruct((B, T, W), F32),
                   jax.ShapeDtypeStruct(m0.shape, F32),
                   jax.ShapeDtypeStruct(s0.shape, F32)],
        scratch_shapes=[pltpu.VMEM(m0.shape, F32), pltpu.VMEM(s0.shape, F32),
                        pltpu.VMEM((B, 2 * L, 2 * LANE), BF16)],
        compiler_params=_cparams("arbitrary"),
        name="mlstm",
    )(*dir_args, *dir_args, m0, s0)


def _softmax_pv(s_list, v_list):
    m = functools.reduce(jnp.maximum, [jnp.max(s, -1, keepdims=True) for s in s_list])
    ps = [jnp.exp(s - m) for s in s_list]
    l = functools.reduce(jnp.add, [jnp.sum(p, -1, keepdims=True) for p in ps])
    o = functools.reduce(jnp.add, [jnp.dot(p.astype(BF16), v, preferred_element_type=F32)
                                   for p, v in zip(ps, v_list)])
    return o / l


def _na_kernel(n_rows, q_ref, *refs):
    nb = NA_BAND // NA_KBLK_ROWS
    k_blks = refs[0:nb]
    v_blks = refs[nb:2 * nb]
    kc_ref, vc_ref, bt_ref, o_ref, kband, vband = refs[2 * nb:]
    i = pl.program_id(1)
    blk_tok = NA_KBLK_ROWS * GRID_W
    HD = NA_HEAD_DIM
    GW = NA_GROUP * HD
    head_of_lane = lax.broadcasted_iota(jnp.int32, (1, GW), 1) // HD
    for j in range(nb):
        kband[j * blk_tok:(j + 1) * blk_tok, :] = k_blks[j][0]
        vband[j * blk_tok:(j + 1) * blk_tok, :] = v_blks[j][0]
    band_row0 = jnp.clip(i * NA_QROWS - NA_ROWS // 2, 0, n_rows - NA_BAND)
    n_win = NA_ROWS * GRID_W
    zero = jnp.zeros((), BF16)

    def rows_body(t, carry):
        items = []
        for u in range(NA_UNROLL):
            a = t * NA_UNROLL + u
            r = i * NA_QROWS + a
            r0 = jnp.clip(r - NA_ROWS // 2, 0, n_rows - NA_ROWS)
            koff = pl.multiple_of((r0 - band_row0) * GRID_W, GRID_W)
            dr_first = r0 - r + (NA_ROWS - 1)
            qoff = pl.multiple_of(a * GRID_W, GRID_W)
            for gi in range(NA_HEADS // NA_GROUP):
                items.append((koff, dr_first, qoff, gi, slice(gi * GW, (gi + 1) * GW)))
        scores = []
        for koff, _, qoff, _, gs in items:
            qg = q_ref[0, pl.ds(qoff, GRID_W), gs]
            qm = jnp.concatenate([jnp.where(head_of_lane == g, qg, zero) for g in range(NA_GROUP)], axis=0)
            scores.append((_nt_dot(kband[pl.ds(koff, n_win), gs], qm),
                           _nt_dot(kc_ref[0, :, gs], qm)))
        probs = []
        for (_, dr_first, _, gi, _), (s_loc, s_ctx) in zip(items, scores):
            s_loc = s_loc + bt_ref[dr_first, gi]
            m = jnp.maximum(jnp.max(s_loc, 0, keepdims=True), jnp.max(s_ctx, 0, keepdims=True))
            e_loc = jnp.exp(s_loc - m)
            e_ctx = jnp.exp(s_ctx - m)
            inv = 1.0 / (jnp.sum(e_loc, 0, keepdims=True) + jnp.sum(e_ctx, 0, keepdims=True))
            probs.append(((e_loc * inv).astype(BF16), (e_ctx * inv).astype(BF16)))
        for (koff, _, qoff, _, gs), (p_loc, p_ctx) in zip(items, probs):
            res = (_tn_dot(p_loc, vband[pl.ds(koff, n_win), gs])
                   + _tn_dot(p_ctx, vc_ref[0, :, gs]))
            out = res[0:GRID_W]
            for g in range(1, NA_GROUP):
                out = jnp.where(head_of_lane == g, res[g * GRID_W:(g + 1) * GRID_W], out)
            o_ref[0, pl.ds(qoff, GRID_W), gs] = out.astype(BF16)
        return carry

    lax.fori_loop(0, NA_QROWS // NA_UNROLL, rows_body, 0)


def _na_latent(q, k, v, kc, vc, bias_tab):
    B, T, Wd = q.shape
    Lc = kc.shape[1]
    n_rows = T // GRID_W
    nb = NA_BAND // NA_KBLK_ROWS
    q_tok = NA_QROWS * GRID_W
    blk_tok = NA_KBLK_ROWS * GRID_W
    last_blk0 = (n_rows - NA_BAND) // NA_KBLK_ROWS

    def kv_spec(j):
        def idx(b, i):
            first = jnp.clip(i * (NA_QROWS // NA_KBLK_ROWS) - (NA_ROWS // 2) // NA_KBLK_ROWS, 0, last_blk0)
            return (b, first + j, 0)
        return pl.BlockSpec((1, blk_tok, Wd), idx)

    return pl.pallas_call(
        functools.partial(_na_kernel, n_rows),
        grid=(B, n_rows // NA_QROWS),
        in_specs=([pl.BlockSpec((1, q_tok, Wd), lambda b, i: (b, i, 0))]
                  + [kv_spec(j) for j in range(nb)] + [kv_spec(j) for j in range(nb)]
                  + [pl.BlockSpec((1, Lc, Wd), lambda b, i: (b, 0, 0)),
                     pl.BlockSpec((1, Lc, Wd), lambda b, i: (b, 0, 0)),
                     pl.BlockSpec(bias_tab.shape, lambda b, i: (0, 0, 0, 0))]),
        out_specs=pl.BlockSpec((1, q_tok, Wd), lambda b, i: (b, i, 0)),
        out_shape=jax.ShapeDtypeStruct((B, T, Wd), BF16),
        scratch_shapes=[pltpu.VMEM((NA_BAND * GRID_W, Wd), BF16),
                        pltpu.VMEM((NA_BAND * GRID_W, Wd), BF16)],
        compiler_params=_cparams("arbitrary", "arbitrary"),
        name="natten",
    )(q, *([k] * nb), *([v] * nb), kc, vc, bias_tab)


def _ctx_attn_kernel(q_ref, k_ref, v_ref, o_ref):
    HD = NA_HEAD_DIM
    for h in range(NA_HEADS):
        hs = slice(h * HD, (h + 1) * HD)
        s = _nt_dot(q_ref[0, :, hs], k_ref[0, :, hs])
        o_ref[0, :, hs] = _softmax_pv([s], [v_ref[0, :, hs]]).astype(BF16)


def _ctx_attn(q, k, v):
    B, Lc, Wd = q.shape
    spec = pl.BlockSpec((1, Lc, Wd), lambda b: (b, 0, 0))
    return pl.pallas_call(
        _ctx_attn_kernel, grid=(B,), in_specs=[spec, spec, spec], out_specs=spec,
        out_shape=jax.ShapeDtypeStruct((B, Lc, Wd), BF16),
        compiler_params=_cparams("arbitrary"), name="ctx_attn",
    )(q, k, v)


def _na_bias_table(rpb):
    qc = np.arange(GRID_W)[:, None]
    kcol = np.arange(GRID_W)[None, :]
    wstart = np.clip(qc - NA_COLS // 2, 0, GRID_W - NA_COLS)
    col_ok = (kcol >= wstart) & (kcol < wstart + NA_COLS)
    dc = np.clip(kcol - qc, -(NA_COLS - 1), NA_COLS - 1) + NA_COLS - 1
    per_dr = jnp.where(col_ok[None, None], rpb[:, :, dc], MASK_NEG)
    def tab(d):
        n_grp = NA_HEADS // NA_GROUP
        t = per_dr[:, d:d + NA_ROWS].reshape(n_grp, NA_GROUP, NA_ROWS, GRID_W, GRID_W)
        return jnp.transpose(t, (0, 2, 4, 1, 3)).reshape(n_grp, NA_ROWS * GRID_W, NA_GROUP * GRID_W)

    return jnp.stack([tab(d) for d in range(NA_ROWS)]).astype(F32)


def _route(sel, s):
    E = EXP_PER_GROUP
    scores = []
    for g in range(N_GROUPS):
        a, b, c, d = sel[E * g:E * (g + 1)]
        scores.append(functools.reduce(jnp.maximum, [a + b, a + c, a + d, b + c, b + d, c + d]))
    best = jnp.zeros_like(scores[0], dtype=jnp.int32)
    best_score = scores[0]
    for g in range(1, N_GROUPS):
        upd = scores[g] > best_score
        best = jnp.where(upd, g, best)
        best_score = jnp.where(upd, scores[g], best_score)

    def pick(rows, j):
        out = rows[j]
        for g in range(1, N_GROUPS):
            out = jnp.where(best == g, rows[E * g + j], out)
        return out

    v = [pick(sel, j) for j in range(E)]
    sv = [pick(s, j) for j in range(E)]
    i1 = jnp.zeros_like(best)
    m1 = v[0]
    for j in range(1, E):
        upd = v[j] > m1
        i1 = jnp.where(upd, j, i1)
        m1 = jnp.where(upd, v[j], m1)
    i2 = jnp.where(i1 == 0, 1, 0)
    m2 = jnp.where(i1 == 0, v[1], v[0])
    for j in range(1, E):
        upd = (i1 != j) & (v[j] > m2)
        i2 = jnp.where(upd, j, i2)
        m2 = jnp.where(upd, v[j], m2)

    def at(rows, idx):
        out = rows[0]
        for j in range(1, E):
            out = jnp.where(idx == j, rows[j], out)
        return out

    s1 = at(sv, i1)
    s2 = at(sv, i2)
    tot = s1 + s2
    return best * E + i1, best * E + i2, s1 / tot, s2 / tot


def _out_proj_kernel(ygm_ref, hf_ref, hb_ref, o_ref, yna_ref, x_ref, mod_ref, w_ref, mlg_ref, n2g_ref,
                     rw_ref, rb_ref, xn_ref, h2_ref, re_ref, rwt_ref):
    hs = hf_ref[0] + hb_ref[0]
    hsq = hs * hs
    lane = lax.broadcasted_iota(jnp.int32, (1, ML_WIDTH), 1)
    scale = jnp.zeros_like(hs)
    for h in range(ML_HEADS):
        in_head = (lane >= h * ML_HEAD_DIM) & (lane < (h + 1) * ML_HEAD_DIM)
        ms = jnp.sum(jnp.where(in_head, hsq, 0.0), -1, keepdims=True) * (1.0 / ML_HEAD_DIM)
        scale = jnp.where(in_head, lax.rsqrt(ms + EPS), scale)
    yml = hs * scale * mlg_ref[...] * jax.nn.sigmoid(o_ref[0])
    o1 = GM_WIDTH
    o2 = GM_WIDTH + ML_WIDTH
    acc = (jnp.dot(ygm_ref[0], w_ref[0:o1, :], preferred_element_type=F32)
           + jnp.dot(yml.astype(BF16), w_ref[o1:o2, :], preferred_element_type=F32)
           + jnp.dot(yna_ref[0], w_ref[o2:, :], preferred_element_type=F32))
    xn = x_ref[0] + mod_ref[0, 0:1, :] * acc
    xn_ref[0] = xn
    h2 = xn * lax.rsqrt(jnp.mean(xn * xn, -1, keepdims=True) + EPS) * n2g_ref[...]
    h2 = h2 * (1.0 + mod_ref[0, 2:3, :]) + mod_ref[0, 1:2, :]
    h2_ref[0] = h2
    logits = _nt_dot(rw_ref[...], h2, precision=lax.Precision.HIGHEST)
    s = jax.nn.sigmoid(logits)
    sel = s + rb_ref[...]
    rows = lambda m: [m[e:e + 1, :] for e in range(N_EXPERTS)]
    e1, e2, w1, w2 = _route(rows(sel), rows(s))
    re_ref[0, 0:1, :] = e1
    re_ref[0, 1:2, :] = e2
    rwt_ref[0, 0:1, :] = w1
    rwt_ref[0, 1:2, :] = w2


def _out_proj(ygm, hf, hb, ml, yna, x, mod, w_out, ml_g, n2_g, router_wt, router_b):
    G, R, D = x.shape
    tm = min(R, 512)
    row = lambda b, i: (b, i, 0)
    whole2 = lambda shape: pl.BlockSpec(shape, lambda b, i: (0, 0))
    return pl.pallas_call(
        _out_proj_kernel,
        grid=(G, R // tm),
        in_specs=[pl.BlockSpec((1, tm, GM_WIDTH), row),
                  pl.BlockSpec((1, tm, ML_WIDTH), row),
                  pl.BlockSpec((1, tm, ML_WIDTH), row),
                  pl.BlockSpec((1, tm, ML_WIDTH), lambda b, i: (b, i, 3)),
                  pl.BlockSpec((1, tm, NA_WIDTH), row),
                  pl.BlockSpec((1, tm, D), row),
                  pl.BlockSpec((1, 3, D), lambda b, i: (b, 0, 0)),
                  whole2((D, D)), whole2((1, ML_WIDTH)), whole2((1, D)),
                  whole2((N_EXPERTS, D)), whole2((N_EXPERTS, 1))],
        out_specs=[pl.BlockSpec((1, tm, D), row), pl.BlockSpec((1, tm, D), row),
                   pl.BlockSpec((1, 2, tm), lambda b, i: (b, 0, i)),
                   pl.BlockSpec((1, 2, tm), lambda b, i: (b, 0, i))],
        out_shape=[jax.ShapeDtypeStruct((G, R, D), F32), jax.ShapeDtypeStruct((G, R, D), F32),
                   jax.ShapeDtypeStruct((G, 2, R), jnp.int32), jax.ShapeDtypeStruct((G, 2, R), F32)],
        compiler_params=_cparams("arbitrary", "arbitrary"),
        name="out_proj",
    )(ygm, hf, hb, ml, yna, x, mod, w_out, ml_g, n2_g, router_wt, router_b)


def _experts_kernel(te_ref, nu_ref, src_cur, src_nxt, h_hbm, wg_ref, wu_ref, wd_ref, o_ref,
                    acc_ref, wgb, wub, wdb, xbuf, sem):
    i = pl.program_id(0)
    n_used = nu_ref[0]
    used = i < n_used
    slot = lax.rem(i, 2)
    new_expert = (i == 0) | (te_ref[i] != te_ref[jnp.maximum(i - 1, 0)])

    def row_copy(src_ref, s, r):
        return pltpu.make_async_copy(h_hbm.at[pl.ds(src_ref[0, 0, r], 1), :], xbuf.at[s, pl.ds(r, 1), :],
                                     sem.at[s])

    def tile_wait(s):
        pltpu.make_async_copy(xbuf.at[s], xbuf.at[s], sem.at[s]).wait()

    @pl.when((i == 0) & used)
    def _():
        def body(r, carry):
            row_copy(src_cur, 0, r).start()
            return carry
        lax.fori_loop(0, MOE_TILE, body, 0, unroll=8)

    @pl.when(used & new_expert)
    def _():
        def cast_rows(c, carry):
            rs = pl.ds(pl.multiple_of(c * LANE, LANE), LANE)
            wgb[rs, :] = wg_ref[0, 0, rs, :].astype(BF16)
            wub[rs, :] = wu_ref[0, 0, rs, :].astype(BF16)
            wdb[rs, :] = wd_ref[0, 0, rs, :].astype(BF16)
            return carry
        lax.fori_loop(0, D_MODEL // LANE, cast_rows, 0)

    @pl.when(used)
    def _():
        tile_wait(slot)
        x = xbuf[slot].astype(BF16)
        nxt = 1 - slot
        per = MOE_TILE // 4
        issued = [0]

        def request_rows():
            for r in range(issued[0], min(issued[0] + per, MOE_TILE)):
                row_copy(src_nxt, nxt, r).start(priority=r % 2)
            issued[0] += per

        for j in range(D_EXPERT // EXPERT_CHUNK):
            cs = slice(j * EXPERT_CHUNK, (j + 1) * EXPERT_CHUNK)
            request_rows()
            g = jnp.dot(x, wgb[:, cs], preferred_element_type=F32)
            request_rows()
            u = jnp.dot(x, wub[:, cs], preferred_element_type=F32)
            a = (g * jax.nn.sigmoid(g) * u).astype(BF16)
            request_rows()
            y = jnp.dot(a, wdb[cs, :], preferred_element_type=F32)
            if j == 0:
                acc_ref[...] = y
            else:
                acc_ref[...] += y
        o_ref[...] = acc_ref[...].astype(BF16)

        @pl.when(i + 1 >= n_used)
        def _():
            tile_wait(nxt)

    @pl.when(jnp.logical_not(used))
    def _():
        o_ref[...] = jnp.zeros_like(o_ref)


def _experts(l, tile_expert, n_used, src, h_all, wg, wu, wd):
    D = h_all.shape[1]
    n_tiles = src.shape[0] // MOE_TILE
    src3 = src.reshape(n_tiles, 1, MOE_TILE)
    wspec = lambda r, c: pl.BlockSpec((1, 1, r, c), lambda i, te, nu: (l, te[i], 0, 0))
    src_spec = lambda ahead: pl.BlockSpec(
        (1, 1, MOE_TILE), lambda i, te, nu: (jnp.minimum(i + ahead, n_tiles - 1), 0, 0),
        memory_space=pltpu.SMEM)
    return pl.pallas_call(
        _experts_kernel,
        grid_spec=pltpu.PrefetchScalarGridSpec(
            num_scalar_prefetch=2,
            grid=(n_tiles,),
            in_specs=[src_spec(0), src_spec(1), pl.BlockSpec(memory_space=pl.ANY),
                      wspec(D, D_EXPERT), wspec(D, D_EXPERT), wspec(D_EXPERT, D)],
            out_specs=pl.BlockSpec((MOE_TILE, D), lambda i, te, nu: (i, 0)),
            scratch_shapes=[pltpu.VMEM((MOE_TILE, D), F32), pltpu.VMEM((D, D_EXPERT), BF16),
                            pltpu.VMEM((D, D_EXPERT), BF16), pltpu.VMEM((D_EXPERT, D), BF16),
                            pltpu.VMEM((2, MOE_TILE, D), F32), pltpu.SemaphoreType.DMA((2,))]),
        out_shape=jax.ShapeDtypeStruct((n_tiles * MOE_TILE, D), BF16),
        compiler_params=_cparams("arbitrary"),
        name="experts",
    )(tile_expert, n_used, src3, src3, h_all, wg, wu, wd)


def _dispatch(e_idx):
    n_tok = e_idx.shape[0]
    n_asg = 2 * n_tok
    n_tiles = -(-n_asg // MOE_TILE) + N_EXPERTS
    flat_e = e_idx.reshape(-1)
    onehot = (flat_e[:, None] == jnp.arange(N_EXPERTS, dtype=jnp.int32)[None, :]).astype(jnp.int32)
    csum = jnp.cumsum(onehot, axis=0)
    rank = jnp.take_along_axis(csum, flat_e[:, None], axis=1)[:, 0] - 1
    counts = csum[-1]
    tiles_e = (counts + MOE_TILE - 1) // MOE_TILE
    tile_end = jnp.cumsum(tiles_e)
    tile_start = tile_end - tiles_e
    dest = tile_start[flat_e] * MOE_TILE + rank
    n_used = tile_end[-1]
    tid = jnp.arange(n_tiles, dtype=jnp.int32)
    te = jnp.minimum(jnp.searchsorted(tile_end, tid, side="right"), N_EXPERTS - 1).astype(jnp.int32)
    te = jnp.where(tid < n_used, te, te[jnp.maximum(n_used - 1, 0)])
    order = jnp.argsort(flat_e, stable=True).astype(jnp.int32)
    starts = jnp.cumsum(counts) - counts
    row = jnp.arange(n_tiles * MOE_TILE, dtype=jnp.int32)
    e_row = te[row // MOE_TILE]
    k_row = row - tile_start[e_row] * MOE_TILE
    live = (row // MOE_TILE < n_used) & (k_row < counts[e_row])
    src = jnp.where(live, _rows(order, jnp.clip(starts[e_row] + k_row, 0, n_asg - 1)) // 2, 0)
    return dest.reshape(n_tok, 2), src, te, n_used.reshape(1).astype(jnp.int32)


def _combine_kernel(final, xn_ref, y0_ref, y1_ref, w_ref, g2_ref, fg_ref, o_ref):
    w = w_ref[0]
    f = y0_ref[0].astype(F32) * w[:, 0:1] + y1_ref[0].astype(F32) * w[:, 1:2]
    x = xn_ref[0] + g2_ref[0] * f
    if final:
        x = x * lax.rsqrt(jnp.mean(x * x, -1, keepdims=True) + EPS) * fg_ref[...]
    o_ref[0] = x


def _combine(xn, y0, y1, wt, g2, final_g, final):
    G, R, D = xn.shape
    tm = min(R, 512)
    row = lambda b, i: (b, i, 0)
    return pl.pallas_call(
        functools.partial(_combine_kernel, final),
        grid=(G, R // tm),
        in_specs=[pl.BlockSpec((1, tm, D), row), pl.BlockSpec((1, tm, D), row), pl.BlockSpec((1, tm, D), row),
                  pl.BlockSpec((1, tm, 2), row),
                  pl.BlockSpec((1, 1, D), lambda b, i: (b, 0, 0)),
                  pl.BlockSpec((1, D), lambda b, i: (0, 0))],
        out_specs=pl.BlockSpec((1, tm, D), row),
        out_shape=jax.ShapeDtypeStruct((G, R, D), F32),
        compiler_params=_cparams("arbitrary", "arbitrary"),
        name="combine_final" if final else "combine",
    )(xn, y0, y1, wt, g2, final_g)


def _rope_tables(T):
    lane = np.arange(LANE)
    half = ML_HEAD_DIM // 4
    inv = jnp.tile(ROPE_BASE ** (-jnp.arange(half, dtype=F32) / half), LANE // half)
    t = jnp.arange(T)
    pos = jnp.where(((lane // (2 * half)) % 2 == 0)[None, :], (t // GRID_W)[:, None], (t % GRID_W)[:, None])
    ang = pos.astype(F32) * inv[None, :]
    sign = np.where((lane // half) % 2 == 0, -1.0, 1.0).astype(np.float32)
    return jnp.cos(ang), jnp.sin(ang) * sign[None, :]


def _reorder_w_in(w_in):
    pad = jnp.zeros(w_in.shape[:2] + (LANE - 4 * ML_HEADS,), w_in.dtype)
    return jnp.concatenate([w_in[..., :OFF_GATES], w_in[..., OFF_NA:], w_in[..., OFF_GATES:OFF_NA], pad],
                           axis=-1).astype(BF16)


def _mixers(l, p, pc, prm, need_ctx):
    gm, ml, gt, q, k, v = p
    gmc, mlc, gtc, qc, kc, vc = pc
    B = ml.shape[0]
    gm_args = (prm["gm_ws"][l], prm["gm_bs_full"][l], prm["gm_g"][l])
    conv = (prm["conv_w"][l], prm["conv_b"][l])
    nh = ML_HEADS
    tr = lambda g: jnp.transpose(
        jnp.concatenate([g[..., 0:nh], g[..., 2 * nh:3 * nh], g[..., nh:2 * nh], g[..., 3 * nh:4 * nh]], -1),
        (0, 2, 1))
    m_zero = jnp.zeros((B, 2 * nh, LANE), F32)
    s_zero = jnp.zeros((B, nh, ML_CHUNK, 2 * LANE), F32)
    qmc, kmc, grc = _mlstm_prep(mlc, prm["cos_c"], prm["sin_c"], *conv, tr(gtc), prm["gbias"][l])
    hfc, hbc, m_st, s_st = _mlstm(qmc, kmc, mlc, grc, m_zero, s_zero)
    qm, km, gr = _mlstm_prep(ml, prm["cos_l"], prm["sin_l"], *conv, tr(gt), prm["gbias"][l])
    hf, hb, _, _ = _mlstm(qm, km, ml, gr, m_st, s_st)
    y_na = _na_latent(q, k, v, kc, vc, prm["na_tab"][l])
    y = (_gmlp(gm, *gm_args), hf, hb, ml, y_na)
    if not need_ctx:
        return y, None
    return y, (_gmlp(gmc, *gm_args), hfc, hbc, mlc, _ctx_attn(qc, kc, vc))


def _rows(a, idx):
    return a.at[idx].get(mode="promise_in_bounds")


def _moe(l, h2_all, e_all, prm):
    dest, src, te, n_used = _dispatch(e_all)
    yb = _experts(l, te, n_used, src, h2_all, prm["wg"], prm["wu"], prm["wd"])
    return yb, dest


def kernel(x, c, ctx, c_ctx, ada_w, ada_b, norm1_g, norm2_g, w_in, w_out, gm_ws, gm_bs, gm_norm_g,
           ml_conv_w, ml_conv_b, ml_gate_b, ml_norm_g, na_rpb, router_w, router_b,
           moe_w_gate, moe_w_up, moe_w_down, final_g):
    B, T, D = x.shape
    Lc = ctx.shape[1]
    cos_l, sin_l = _rope_tables(T)
    prm = dict(
        gm_ws=gm_ws.astype(BF16),
        gm_bs_full=jnp.repeat(jnp.transpose(gm_bs, (0, 2, 1)), GM_HEAD_DIM, axis=-1),
        gm_g=gm_norm_g[:, None, :],
        conv_w=ml_conv_w, conv_b=ml_conv_b[:, None, :],
        gbias=ml_gate_b[:, jnp.array([0, 2, 1, 3])].reshape(DEPTH, 4 * ML_HEADS, 1),
        cos_l=cos_l, sin_l=sin_l,
        cos_c=jnp.ones((Lc, LANE), F32), sin_c=jnp.zeros((Lc, LANE), F32),
        na_tab=jnp.stack([_na_bias_table(na_rpb[l]) for l in range(DEPTH)]),
        wg=moe_w_gate, wu=moe_w_up, wd=moe_w_down,
    )
    w_in_r = _reorder_w_in(w_in)
    w_out_b = w_out.astype(BF16)
    router_wt = jnp.transpose(router_w)
    router_bc = router_b[:, None]

    cs = jnp.concatenate([c, c_ctx[None, :], jnp.zeros((8 - B - 1, D), F32)], axis=0)
    mods = _ada_all(cs, ada_w, ada_b).reshape(DEPTH, 8, 6, D)

    xc = ctx
    for l in range(DEPTH):
        need_ctx = l < DEPTH - 1
        mod_l = mods[l, :B]
        mod_c = jnp.broadcast_to(mods[l, B:B + 1], (B, 6, D))
        n1 = norm1_g[l][None, :]
        p = _in_proj(x, mod_l[:, 0:2], n1, w_in_r[l])
        pc = _in_proj(xc, mod_c[:, 0:2], n1, w_in_r[l])
        y, yc = _mixers(l, p, pc, prm, need_ctx)
        op_args = (w_out_b[l], ml_norm_g[l][None, :], norm2_g[l][None, :], router_wt, router_bc)
        xn, h2, re, rw = _out_proj(*y, x, mod_l[:, 2:5], *op_args)
        h2_all = h2.reshape(B * T, D)
        e_all = jnp.transpose(re, (0, 2, 1)).reshape(B * T, 2)
        if need_ctx:
            xnc, h2c, rec, rwc = _out_proj(*yc, xc, mod_c[:, 2:5], *op_args)
            h2_all = jnp.concatenate([h2_all, h2c.reshape(B * Lc, D)], axis=0)
            e_all = jnp.concatenate([e_all, jnp.transpose(rec, (0, 2, 1)).reshape(B * Lc, 2)], axis=0)
        yb, dest = _moe(l, h2_all, e_all, prm)
        final = l == DEPTH - 1
        fg = final_g[None, :]
        n_lat = B * T
        picked = lambda lo, hi, k, L: _rows(yb, dest[lo:hi, k]).reshape(B, L, D)
        x = _combine(xn, picked(0, n_lat, 0, T), picked(0, n_lat, 1, T),
                     jnp.transpose(rw, (0, 2, 1)), mod_l[:, 5:6], fg, final)
        if need_ctx:
            xc = _combine(xnc, picked(n_lat, None, 0, Lc), picked(n_lat, None, 1, Lc),
                          jnp.transpose(rwc, (0, 2, 1)), mod_c[:, 5:6], fg, False)
    return x
```

```python
import functools

import jax
import jax.numpy as jnp
import numpy as np
from jax import lax
from jax.experimental import pallas as pl
from jax.experimental.pallas import tpu as pltpu

F32 = jnp.float32
BF16 = jnp.bfloat16

D_MODEL = 1024
DEPTH = 4
GRID_W = 64
EPS = 1e-6

GM_HEADS = 4
GM_WIDTH = D_MODEL // 4
GM_HEAD_DIM = GM_WIDTH // GM_HEADS
GM_CHUNK = 128

ML_HEADS = 4
ML_WIDTH = D_MODEL // 4
ML_HEAD_DIM = ML_WIDTH // ML_HEADS
ML_CHUNK = 128

NA_HEADS = 8
NA_WIDTH = D_MODEL // 2
NA_HEAD_DIM = NA_WIDTH // NA_HEADS
NA_ROWS = 8
NA_COLS = 16

ROPE_BASE = 10000.0

OFF_ML = 2 * GM_WIDTH
OFF_GATES = OFF_ML + 4 * ML_WIDTH
OFF_NA = OFF_GATES + 4 * ML_HEADS
N_IN = OFF_NA + 3 * NA_WIDTH

N_EXPERTS = 16
N_GROUPS = 4
EXP_PER_GROUP = N_EXPERTS // N_GROUPS
D_EXPERT = D_MODEL

LANE = 128
SEG_GM = (0, 2 * GM_WIDTH)
SEG_ML = (SEG_GM[1], SEG_GM[1] + 4 * ML_WIDTH)
SEG_Q = (SEG_ML[1], SEG_ML[1] + NA_WIDTH)
SEG_K = (SEG_Q[1], SEG_Q[1] + NA_WIDTH)
SEG_V = (SEG_K[1], SEG_K[1] + NA_WIDTH)
SEG_GATES = (SEG_V[1], SEG_V[1] + LANE)
N_IN_PAD = SEG_GATES[1]

NA_QROWS = 8
NA_GROUP = 4
NA_UNROLL = 4
NA_BAND = 16
NA_KBLK_ROWS = 4
MOE_TILE = 512
EXPERT_CHUNK = 512
MASK_NEG = -1e30
N_GATE_ROWS = 6
VMEM_LIMIT = 56 * 2 ** 20


def _cparams(*sem):
    return pltpu.CompilerParams(dimension_semantics=sem, vmem_limit_bytes=VMEM_LIMIT)


def _nt_dot(a, b, precision=None):
    return lax.dot_general(a, b, (((1,), (1,)), ((), ())), precision=precision,
                           preferred_element_type=F32)


def _tn_dot(a, b):
    return lax.dot_general(a, b, (((0,), (0,)), ((), ())), preferred_element_type=F32)


def _ada_kernel(c_ref, w_ref, b_ref, o_ref):
    c = c_ref[...]
    sc = c * jax.nn.sigmoid(c)
    o_ref[0] = jnp.dot(sc, w_ref[0], preferred_element_type=F32) + b_ref[0]


def _ada_all(cs, ada_w, ada_b):
    n_out = ada_w.shape[-1] // D_MODEL
    return pl.pallas_call(
        _ada_kernel,
        grid=(DEPTH, n_out),
        in_specs=[pl.BlockSpec((8, D_MODEL), lambda l, j: (0, 0)),
                  pl.BlockSpec((1, D_MODEL, D_MODEL), lambda l, j: (l, 0, j)),
                  pl.BlockSpec((1, 1, D_MODEL), lambda l, j: (l, 0, j))],
        out_specs=pl.BlockSpec((1, 8, D_MODEL), lambda l, j: (l, 0, j)),
        out_shape=jax.ShapeDtypeStruct((DEPTH, 8, n_out * D_MODEL), F32),
        compiler_params=_cparams("arbitrary", "arbitrary"),
        name="adaln",
    )(cs, ada_w, ada_b.reshape(DEPTH, 1, -1))


def _in_proj_kernel(x_ref, mod_ref, g_ref, w_ref, gm_ref, ml_ref, gt_ref, q_ref, k_ref, v_ref):
    x = x_ref[0]
    h = x * lax.rsqrt(jnp.mean(x * x, -1, keepdims=True) + EPS) * g_ref[...]
    h = h * (1.0 + mod_ref[0, 1:2, :]) + mod_ref[0, 0:1, :]
    hb = h.astype(BF16)

    def seg(s):
        return jnp.dot(hb, w_ref[:, s[0]:s[1]], preferred_element_type=F32)

    gm_ref[0] = seg(SEG_GM)
    ml_ref[0] = seg(SEG_ML)
    gt_ref[0] = seg(SEG_GATES)
    q_ref[0] = (seg(SEG_Q) * (NA_HEAD_DIM ** -0.5)).astype(BF16)
    k_ref[0] = seg(SEG_K).astype(BF16)
    v_ref[0] = seg(SEG_V).astype(BF16)


def _in_proj(x, mod, g, w):
    G, R, D = x.shape
    tm = min(R, 512)
    row = lambda b, i: (b, i, 0)
    widths = (SEG_GM[1] - SEG_GM[0], SEG_ML[1] - SEG_ML[0], LANE, NA_WIDTH, NA_WIDTH, NA_WIDTH)
    dtypes = (F32, F32, F32, BF16, BF16, BF16)
    return pl.pallas_call(
        _in_proj_kernel,
        grid=(G, R // tm),
        in_specs=[pl.BlockSpec((1, tm, D), row),
                  pl.BlockSpec((1, 2, D), lambda b, i: (b, 0, 0)),
                  pl.BlockSpec((1, D), lambda b, i: (0, 0)),
                  pl.BlockSpec((D, N_IN_PAD), lambda b, i: (0, 0))],
        out_specs=[pl.BlockSpec((1, tm, n), row) for n in widths],
        out_shape=[jax.ShapeDtypeStruct((G, R, n), dt) for n, dt in zip(widths, dtypes)],
        compiler_params=_cparams("arbitrary", "arbitrary"),
        name="in_proj",
    )(x, mod, g, w)


def _gmlp_kernel(n_chunks, p_ref, ws_ref, bs_ref, g_ref, o_ref):
    lane = lax.broadcasted_iota(jnp.int32, (1, GM_WIDTH), 1)

    def chunk(j, carry):
        r0 = pl.multiple_of(j * GM_CHUNK, GM_CHUNK)
        z = jax.nn.gelu(p_ref[0, pl.ds(r0, GM_CHUNK), :])
        u = z[:, :GM_WIDTH]
        v = z[:, GM_WIDTH:]
        mu = jnp.mean(v, -1, keepdims=True)
        vc = v - mu
        var = jnp.mean(vc * vc, -1, keepdims=True)
        vn = (vc * lax.rsqrt(var + EPS) * g_ref[...]).astype(BF16)
        sv = jnp.zeros((GM_CHUNK, GM_WIDTH), F32)
        for h in range(GM_HEADS):
            full = jnp.dot(ws_ref[h], vn, preferred_element_type=F32)
            in_head = (lane >= h * GM_HEAD_DIM) & (lane < (h + 1) * GM_HEAD_DIM)
            sv = jnp.where(in_head, full, sv)
        o_ref[0, pl.ds(r0, GM_CHUNK), :] = (u * (sv + bs_ref[...])).astype(BF16)
        return carry

    lax.fori_loop(0, n_chunks, chunk, 0)


def _gmlp(p_gm, ws, bs_full, g):
    G, R, _ = p_gm.shape
    tr = min(R, 1024)
    return pl.pallas_call(
        functools.partial(_gmlp_kernel, tr // GM_CHUNK),
        grid=(G, R // tr),
        in_specs=[pl.BlockSpec((1, tr, 2 * GM_WIDTH), lambda b, i: (b, i, 0)),
                  pl.BlockSpec((GM_HEADS, GM_CHUNK, GM_CHUNK), lambda b, i: (0, 0, 0)),
                  pl.BlockSpec((GM_CHUNK, GM_WIDTH), lambda b, i: (0, 0)),
                  pl.BlockSpec((1, GM_WIDTH), lambda b, i: (0, 0))],
        out_specs=pl.BlockSpec((1, tr, GM_WIDTH), lambda b, i: (b, i, 0)),
        out_shape=jax.ShapeDtypeStruct((G, R, GM_WIDTH), BF16),
        compiler_params=_cparams("arbitrary", "arbitrary"),
        name="gmlp",
    )(p_gm, ws, bs_full, g)


def _log_sigmoid(x):
    return jnp.minimum(x, 0.0) - jnp.log1p(jnp.exp(-jnp.abs(x)))


def _rope(x, cos, sin_signed):
    lane = lax.broadcasted_iota(jnp.int32, (1, LANE), 1)
    first_half = (lane & 16) == 0
    partner = jnp.where(first_half, pltpu.roll(x, LANE - 16, 1), pltpu.roll(x, 16, 1))
    return x * cos + partner * sin_signed


def _chunk_scan(x, op, fill, reverse):
    pos = lax.broadcasted_iota(jnp.int32, (1, x.shape[1]), 1) % ML_CHUNK
    n = x.shape[1]
    sh = 1
    while sh < ML_CHUNK:
        if reverse:
            shifted = jnp.where(pos < ML_CHUNK - sh, pltpu.roll(x, n - sh, 1), fill)
        else:
            shifted = jnp.where(pos >= sh, pltpu.roll(x, sh, 1), fill)
        x = op(x, shifted)
        sh *= 2
    return x


def _mlstm_prep_kernel(n_blocks, x_ref, hp_ref, hn_ref, cos_ref, sin_ref, cw_ref, cb_ref, g_ref, gbias_ref,
                       q_ref, k_ref, go_ref):
    nh2 = 2 * ML_HEADS
    g = g_ref[0] + gbias_ref[...]
    is_fwd = lax.broadcasted_iota(jnp.int32, (nh2, 1), 0) < ML_HEADS
    li = g[0:nh2]
    lf = _log_sigmoid(g[nh2:])
    ps = _chunk_scan(lf, jnp.add, 0.0, False)
    ss = _chunk_scan(lf, jnp.add, 0.0, True)
    bcum = jnp.where(is_fwd, ps, ss)
    r = li - bcum
    pm = _chunk_scan(r, jnp.maximum, MASK_NEG, False)
    sm = _chunk_scan(r, jnp.maximum, MASK_NEG, True)
    rmax = jnp.maximum(pm, sm)
    groups = (bcum, r, jnp.where(is_fwd, pm, sm), jnp.exp(r - rmax), rmax, ps + ss - lf)
    for j, val in enumerate(groups):
        go_ref[0, nh2 * j:nh2 * (j + 1), :] = val

    i = pl.program_id(1)
    W = ML_WIDTH
    x = x_ref[0]
    tr = x.shape[0]
    rid = lax.broadcasted_iota(jnp.int32, (tr, 1), 0)
    i_row = jnp.zeros((1, 2 * W), jnp.int32) + i
    prev = jnp.where(i_row == 0, 0.0, hp_ref[0, 7:8, :])
    nxt = jnp.where(i_row == n_blocks - 1, 0.0, hn_ref[0, 0:1, :])
    xm1 = jnp.where(rid == 0, prev, pltpu.roll(x, 1, 0))
    xp1 = jnp.where(rid == tr - 1, nxt, pltpu.roll(x, tr - 1, 0))
    y = xm1 * cw_ref[0:1, :] + x * cw_ref[1:2, :] + xp1 * cw_ref[2:3, :] + cb_ref[...]
    y = y * jax.nn.sigmoid(y)
    cos = cos_ref[...]
    sin = sin_ref[...]
    parts = [_rope(y[:, j * LANE:(j + 1) * LANE], cos, sin) for j in range(2 * W // LANE)]
    q_ref[0] = (jnp.concatenate(parts[:W // LANE], axis=1) * (ML_HEAD_DIM ** -0.5)).astype(BF16)
    for j in range(W // LANE):
        k_ref[0, j * LANE:(j + 1) * LANE, :] = parts[W // LANE + j].T.astype(BF16)


def _mlstm_prep(ml, cos, sin, conv_w, conv_b, gates_t, gbias):
    B, T, _ = ml.shape
    n_g = gates_t.shape[1]
    W = ML_WIDTH
    tr = min(T, 1024)
    nb = T // tr
    halo_per_blk = tr // 8
    n_hblk = T // 8
    whole = lambda shape: pl.BlockSpec(shape, lambda b, i: (0, 0))
    return pl.pallas_call(
        functools.partial(_mlstm_prep_kernel, nb),
        grid=(B, nb),
        in_specs=[pl.BlockSpec((1, tr, 2 * W), lambda b, i: (b, i, 0)),
                  pl.BlockSpec((1, 8, 2 * W), lambda b, i: (b, jnp.maximum(i * halo_per_blk - 1, 0), 0)),
                  pl.BlockSpec((1, 8, 2 * W),
                               lambda b, i: (b, jnp.minimum((i + 1) * halo_per_blk, n_hblk - 1), 0)),
                  pl.BlockSpec((tr, LANE), lambda b, i: (i, 0)),
                  pl.BlockSpec((tr, LANE), lambda b, i: (i, 0)),
                  whole((3, 2 * W)), whole((1, 2 * W)),
                  pl.BlockSpec((1, n_g, tr), lambda b, i: (b, 0, i)), whole((n_g, 1))],
        out_specs=[pl.BlockSpec((1, tr, W), lambda b, i: (b, i, 0)),
                   pl.BlockSpec((1, W, tr), lambda b, i: (b, 0, i)),
                   pl.BlockSpec((1, 2 * ML_HEADS * N_GATE_ROWS, tr), lambda b, i: (b, 0, i))],
        out_shape=[jax.ShapeDtypeStruct((B, T, W), BF16), jax.ShapeDtypeStruct((B, W, T), BF16),
                   jax.ShapeDtypeStruct((B, 2 * ML_HEADS * N_GATE_ROWS, T), F32)],
        compiler_params=_cparams("arbitrary", "arbitrary"),
        name="mlstm_prep",
    )(ml, ml, ml, cos, sin, conv_w, conv_b, gates_t, gbias)


def _mlstm_kernel(n_chunks, qf_ref, ktf_ref, vf_ref, gf_ref, qb_ref, ktb_ref, vb_ref, gb_ref,
                  m0_ref, s0_ref, hf_ref, hb_ref, mfin_ref, sfin_ref, m_scr, s_scr, bd_scr):
    c = pl.program_id(0)
    L = ML_CHUNK
    HD = ML_HEAD_DIM
    NH = ML_HEADS
    n_batch = m_scr.shape[0]

    @pl.when(c == 0)
    def _():
        m_scr[...] = m0_ref[...]
        s_scr[...] = s0_ref[...]
        bd_scr[...] = jnp.zeros_like(bd_scr)

    def row_bcast(x, j):
        return jnp.broadcast_to(x[j:j + 1, :], (L, LANE))

    row = lax.broadcasted_iota(jnp.int32, (L, L), 0)
    col = lax.broadcasted_iota(jnp.int32, (L, L), 1)
    low_half = lax.broadcasted_iota(jnp.int32, (1, LANE), 1) < HD
    own_rows = jnp.concatenate([row < HD, row >= HD], axis=1)
    is_fwd = lax.broadcasted_iota(jnp.int32, (2 * NH, 1), 0) < NH

    gate = []
    for b in range(n_batch):
        gq = lambda i: jnp.where(is_fwd, gf_ref[b, 2 * NH * i:2 * NH * (i + 1), :],
                                 gb_ref[b, 2 * NH * i:2 * NH * (i + 1), :])
        bcum, r, rcmax, ew, rmax, b_last = (gq(i) for i in range(N_GATE_ROWS))
        m0 = m_scr[b]
        mu = jnp.maximum(m0, rcmax)
        emt = jnp.exp(-(bcum + mu))
        m_last = jnp.maximum(m0, rmax)
        m_scr[b] = b_last + m_last
        stack = jnp.concatenate([mu, emt, jnp.zeros((L - 4 * NH, LANE), F32)], axis=0)
        gate.append(dict(r=r, ew=ew, m0=m0, a_st=jnp.exp(m0 - m_last), g_st=jnp.exp(rmax - m_last),
                         cols=stack.T))

    dirs = ((qf_ref, ktf_ref, vf_ref, hf_ref), (qb_ref, ktb_ref, vb_ref, hb_ref))
    items = [(b, d, pr) for b in range(n_batch) for d in range(2) for pr in range(NH // 2)]

    pre = []
    for b, d, pr in items:
        q_ref, kt_ref, v_ref, _ = dirs[d]
        ps = slice(pr * LANE, (pr + 1) * LANE)
        ja = NH * d + 2 * pr
        g = gate[b]
        qp = q_ref[b, :, ps]
        kt = kt_ref[b, ps, :]
        vp = v_ref[b, :, ps]
        st = s_scr[b, 2 * d + pr]
        ktf = kt.astype(F32)
        kt2 = jnp.concatenate([jnp.where(row < HD, ktf, 0.0), jnp.where(row < HD, 0.0, ktf)],
                              axis=1).astype(BF16)
        s2 = jnp.dot(qp, kt2, preferred_element_type=F32)
        inter = jnp.dot(qp, st.astype(BF16), preferred_element_type=F32)
        vab = jnp.concatenate([jnp.where(low_half, vp, 1.0), jnp.where(low_half, 1.0, vp)],
                              axis=1).astype(BF16)
        ktw = ktf * jnp.where(row < HD, row_bcast(g["ew"], ja), row_bcast(g["ew"], ja + 1))
        upd = jnp.dot(ktw.astype(BF16), vab, preferred_element_type=F32)
        pre.append((s2, inter, vab, upd, st))

    probs = []
    for (b, d, pr), (s2, _, _, _, _) in zip(items, pre):
        ja = NH * d + 2 * pr
        g = gate[b]
        valid = (col <= row) if d == 0 else (col >= row)
        mu_c = [jnp.broadcast_to(g["cols"][:, j:j + 1], (L, LANE)) for j in (ja, ja + 1)]
        e2 = jnp.concatenate([jnp.where(valid, jnp.exp(row_bcast(g["r"], j) - mc), 0.0)
                              for j, mc in zip((ja, ja + 1), mu_c)], axis=1)
        a2 = jnp.concatenate([jnp.exp(row_bcast(g["m0"], j) - mc) for j, mc in zip((ja, ja + 1), mu_c)],
                             axis=1)
        probs.append(((s2 * e2).astype(BF16), a2))

    for it, ((b, d, pr), (_, inter, vab, upd, st), (p2, a2)) in enumerate(zip(items, pre, probs)):
        out_ref = dirs[d][3]
        ps = slice(pr * LANE, (pr + 1) * LANE)
        ja = NH * d + 2 * pr
        g = gate[b]
        bd_scr[it, 0:L, 0:LANE] = vab[:, 0:LANE]
        bd_scr[it, L:2 * L, LANE:2 * LANE] = vab[:, LANE:2 * LANE]
        intra = jnp.dot(p2, bd_scr[it], preferred_element_type=F32)
        tot = intra + a2 * inter
        tot_a = tot[:, 0:LANE]
        tot_b = tot[:, LANE:2 * LANE]
        num = jnp.where(low_half, tot_a, tot_b)
        den = pltpu.roll(jnp.where(low_half, tot_b, tot_a), HD, 1)
        floor = jnp.where(low_half, jnp.broadcast_to(g["cols"][:, 8 + ja:8 + ja + 1], (L, LANE)),
                          jnp.broadcast_to(g["cols"][:, 8 + ja + 1:8 + ja + 2], (L, LANE)))
        out_ref[b, :, ps] = num / jnp.maximum(jnp.abs(den), floor)
        a_s = jnp.concatenate([row_bcast(g["a_st"], ja), row_bcast(g["a_st"], ja + 1)], axis=1)
        g_s = jnp.concatenate([row_bcast(g["g_st"], ja), row_bcast(g["g_st"], ja + 1)], axis=1)
        s_scr[b, 2 * d + pr] = a_s * st + g_s * jnp.where(own_rows, upd, 0.0)

    @pl.when(c == n_chunks - 1)
    def _():
        mfin_ref[...] = m_scr[...]
        sfin_ref[...] = s_scr[...]


def _mlstm(q, kt, ml, grows, m0, s0):
    B, T, W = q.shape
    L = ML_CHUNK
    nc = T // L

    def specs(ci):
        return [pl.BlockSpec((B, L, W), lambda c: (0, ci(c), 0)),
                pl.BlockSpec((B, W, L), lambda c: (0, 0, ci(c))),
                pl.BlockSpec((B, L, W), lambda c: (0, ci(c), 2)),
                pl.BlockSpec((B, grows.shape[1], L), lambda c: (0, 0, ci(c)))]

    m_spec = pl.BlockSpec(m0.shape, lambda c: (0, 0, 0))
    s_spec = pl.BlockSpec(s0.shape, lambda c: (0, 0, 0, 0))
    dir_args = [q, kt, ml, grows]
    return pl.pallas_call(
        functools.partial(_mlstm_kernel, nc),
        grid=(nc,),
        in_specs=specs(lambda c: c) + specs(lambda c: nc - 1 - c) + [m_spec, s_spec],
        out_specs=[pl.BlockSpec((B, L, W), lambda c: (0, c, 0)),
                   pl.BlockSpec((B, L, W), lambda c: (0, nc - 1 - c, 0)),
                   m_spec, s_spec],
        out_shape=[jax.ShapeDtypeStruct((B, T, W), F32),
                   jax.ShapeDtypeStruct((B, T, W), F32),
                   jax.ShapeDtypeStruct(m0.shape, F32),
                   jax.ShapeDtypeStruct(s0.shape, F32)],
        scratch_shapes=[pltpu.VMEM(m0.shape, F32), pltpu.VMEM(s0.shape, F32),
                        pltpu.VMEM((B * ML_HEADS, 2 * L, 2 * LANE), BF16)],
        compiler_params=_cparams("arbitrary"),
        name="mlstm",
    )(*dir_args, *dir_args, m0, s0)


def _softmax_pv(s_list, v_list):
    m = functools.reduce(jnp.maximum, [jnp.max(s, -1, keepdims=True) for s in s_list])
    ps = [jnp.exp(s - m) for s in s_list]
    l = functools.reduce(jnp.add, [jnp.sum(p, -1, keepdims=True) for p in ps])
    o = functools.reduce(jnp.add, [jnp.dot(p.astype(BF16), v, preferred_element_type=F32)
                                   for p, v in zip(ps, v_list)])
    return o / l


def _na_kernel(n_rows, q_ref, *refs):
    nb = NA_BAND // NA_KBLK_ROWS
    k_blks = refs[0:nb]
    v_blks = refs[nb:2 * nb]
    kc_ref, vc_ref, bt_ref, o_ref, kband, vband = refs[2 * nb:]
    i = pl.program_id(1)
    blk_tok = NA_KBLK_ROWS * GRID_W
    HD = NA_HEAD_DIM
    GW = NA_GROUP * HD
    head_of_lane = lax.broadcasted_iota(jnp.int32, (1, GW), 1) // HD
    for j in range(nb):
        kband[j * blk_tok:(j + 1) * blk_tok, :] = k_blks[j][0]
        vband[j * blk_tok:(j + 1) * blk_tok, :] = v_blks[j][0]
    band_row0 = jnp.clip(i * NA_QROWS - NA_ROWS // 2, 0, n_rows - NA_BAND)
    n_win = NA_ROWS * GRID_W
    zero = jnp.zeros((), BF16)

    def rows_body(t, carry):
        items = []
        for u in range(NA_UNROLL):
            a = t * NA_UNROLL + u
            r = i * NA_QROWS + a
            r0 = jnp.clip(r - NA_ROWS // 2, 0, n_rows - NA_ROWS)
            koff = pl.multiple_of((r0 - band_row0) * GRID_W, GRID_W)
            dr_first = r0 - r + (NA_ROWS - 1)
            qoff = pl.multiple_of(a * GRID_W, GRID_W)
            for gi in range(NA_HEADS // NA_GROUP):
                items.append((koff, dr_first, qoff, gi, slice(gi * GW, (gi + 1) * GW)))
        scores = []
        for koff, _, qoff, _, gs in items:
            qg = q_ref[0, pl.ds(qoff, GRID_W), gs]
            qm = jnp.concatenate([jnp.where(head_of_lane == g, qg, zero) for g in range(NA_GROUP)], axis=0)
            scores.append((_nt_dot(kband[pl.ds(koff, n_win), gs], qm),
                           _nt_dot(kc_ref[0, :, gs], qm)))
        probs = []
        for (_, dr_first, _, gi, _), (s_loc, s_ctx) in zip(items, scores):
            s_loc = s_loc + bt_ref[dr_first, gi]
            m = jnp.maximum(jnp.max(s_loc, 0, keepdims=True), jnp.max(s_ctx, 0, keepdims=True))
            e_loc = jnp.exp(s_loc - m)
            e_ctx = jnp.exp(s_ctx - m)
            inv = 1.0 / (jnp.sum(e_loc, 0, keepdims=True) + jnp.sum(e_ctx, 0, keepdims=True))
            probs.append(((e_loc * inv).astype(BF16), (e_ctx * inv).astype(BF16)))
        for (koff, _, qoff, _, gs), (p_loc, p_ctx) in zip(items, probs):
            res = (_tn_dot(p_loc, vband[pl.ds(koff, n_win), gs])
                   + _tn_dot(p_ctx, vc_ref[0, :, gs]))
            out = res[0:GRID_W]
            for g in range(1, NA_GROUP):
                out = jnp.where(head_of_lane == g, res[g * GRID_W:(g + 1) * GRID_W], out)
            o_ref[0, pl.ds(qoff, GRID_W), gs] = out.astype(BF16)
        return carry

    lax.fori_loop(0, NA_QROWS // NA_UNROLL, rows_body, 0)


def _na_latent(q, k, v, kc, vc, bias_tab):
    B, T, Wd = q.shape
    Lc = kc.shape[1]
    n_rows = T // GRID_W
    nb = NA_BAND // NA_KBLK_ROWS
    q_tok = NA_QROWS * GRID_W
    blk_tok = NA_KBLK_ROWS * GRID_W
    last_blk0 = (n_rows - NA_BAND) // NA_KBLK_ROWS

    def kv_spec(j):
        def idx(b, i):
            first = jnp.clip(i * (NA_QROWS // NA_KBLK_ROWS) - (NA_ROWS // 2) // NA_KBLK_ROWS, 0, last_blk0)
            return (b, first + j, 0)
        return pl.BlockSpec((1, blk_tok, Wd), idx)

    return pl.pallas_call(
        functools.partial(_na_kernel, n_rows),
        grid=(B, n_rows // NA_QROWS),
        in_specs=([pl.BlockSpec((1, q_tok, Wd), lambda b, i: (b, i, 0))]
                  + [kv_spec(j) for j in range(nb)] + [kv_spec(j) for j in range(nb)]
                  + [pl.BlockSpec((1, Lc, Wd), lambda b, i: (b, 0, 0)),
                     pl.BlockSpec((1, Lc, Wd), lambda b, i: (b, 0, 0)),
                     pl.BlockSpec(bias_tab.shape, lambda b, i: (0, 0, 0, 0))]),
        out_specs=pl.BlockSpec((1, q_tok, Wd), lambda b, i: (b, i, 0)),
        out_shape=jax.ShapeDtypeStruct((B, T, Wd), BF16),
        scratch_shapes=[pltpu.VMEM((NA_BAND * GRID_W, Wd), BF16),
                        pltpu.VMEM((NA_BAND * GRID_W, Wd), BF16)],
        compiler_params=_cparams("arbitrary", "arbitrary"),
        name="natten",
    )(q, *([k] * nb), *([v] * nb), kc, vc, bias_tab)


def _ctx_attn_kernel(q_ref, k_ref, v_ref, o_ref):
    HD = NA_HEAD_DIM
    for h in range(NA_HEADS):
        hs = slice(h * HD, (h + 1) * HD)
        s = _nt_dot(q_ref[0, :, hs], k_ref[0, :, hs])
        o_ref[0, :, hs] = _softmax_pv([s], [v_ref[0, :, hs]]).astype(BF16)


def _ctx_attn(q, k, v):
    B, Lc, Wd = q.shape
    spec = pl.BlockSpec((1, Lc, Wd), lambda b: (b, 0, 0))
    return pl.pallas_call(
        _ctx_attn_kernel, grid=(B,), in_specs=[spec, spec, spec], out_specs=spec,
        out_shape=jax.ShapeDtypeStruct((B, Lc, Wd), BF16),
        compiler_params=_cparams("arbitrary"), name="ctx_attn",
    )(q, k, v)


def _na_bias_table(rpb):
    qc = np.arange(GRID_W)[:, None]
    kcol = np.arange(GRID_W)[None, :]
    wstart = np.clip(qc - NA_COLS // 2, 0, GRID_W - NA_COLS)
    col_ok = (kcol >= wstart) & (kcol < wstart + NA_COLS)
    dc = np.clip(kcol - qc, -(NA_COLS - 1), NA_COLS - 1) + NA_COLS - 1
    per_dr = jnp.where(col_ok[None, None], rpb[:, :, dc], MASK_NEG)
    def tab(d):
        n_grp = NA_HEADS // NA_GROUP
        t = per_dr[:, d:d + NA_ROWS].reshape(n_grp, NA_GROUP, NA_ROWS, GRID_W, GRID_W)
        return jnp.transpose(t, (0, 2, 4, 1, 3)).reshape(n_grp, NA_ROWS * GRID_W, NA_GROUP * GRID_W)

    return jnp.stack([tab(d) for d in range(NA_ROWS)]).astype(F32)


def _route(sel, s):
    E = EXP_PER_GROUP
    scores = []
    for g in range(N_GROUPS):
        a, b, c, d = sel[E * g:E * (g + 1)]
        scores.append(functools.reduce(jnp.maximum, [a + b, a + c, a + d, b + c, b + d, c + d]))
    best = jnp.zeros_like(scores[0], dtype=jnp.int32)
    best_score = scores[0]
    for g in range(1, N_GROUPS):
        upd = scores[g] > best_score
        best = jnp.where(upd, g, best)
        best_score = jnp.where(upd, scores[g], best_score)

    def pick(rows, j):
        out = rows[j]
        for g in range(1, N_GROUPS):
            out = jnp.where(best == g, rows[E * g + j], out)
        return out

    v = [pick(sel, j) for j in range(E)]
    sv = [pick(s, j) for j in range(E)]
    i1 = jnp.zeros_like(best)
    m1 = v[0]
    for j in range(1, E):
        upd = v[j] > m1
        i1 = jnp.where(upd, j, i1)
        m1 = jnp.where(upd, v[j], m1)
    i2 = jnp.where(i1 == 0, 1, 0)
    m2 = jnp.where(i1 == 0, v[1], v[0])
    for j in range(1, E):
        upd = (i1 != j) & (v[j] > m2)
        i2 = jnp.where(upd, j, i2)
        m2 = jnp.where(upd, v[j], m2)

    def at(rows, idx):
        out = rows[0]
        for j in range(1, E):
            out = jnp.where(idx == j, rows[j], out)
        return out

    s1 = at(sv, i1)
    s2 = at(sv, i2)
    tot = s1 + s2
    return best * E + i1, best * E + i2, s1 / tot, s2 / tot


def _out_proj_kernel(ygm_ref, hf_ref, hb_ref, o_ref, yna_ref, x_ref, mod_ref, w_ref, mlg_ref, n2g_ref,
                     rw_ref, rb_ref, xn_ref, h2_ref, re_ref, rwt_ref):
    hs = hf_ref[0] + hb_ref[0]
    hsq = hs * hs
    lane = lax.broadcasted_iota(jnp.int32, (1, ML_WIDTH), 1)
    scale = jnp.zeros_like(hs)
    for h in range(ML_HEADS):
        in_head = (lane >= h * ML_HEAD_DIM) & (lane < (h + 1) * ML_HEAD_DIM)
        ms = jnp.sum(jnp.where(in_head, hsq, 0.0), -1, keepdims=True) * (1.0 / ML_HEAD_DIM)
        scale = jnp.where(in_head, lax.rsqrt(ms + EPS), scale)
    yml = hs * scale * mlg_ref[...] * jax.nn.sigmoid(o_ref[0])
    o1 = GM_WIDTH
    o2 = GM_WIDTH + ML_WIDTH
    acc = (jnp.dot(ygm_ref[0], w_ref[0:o1, :], preferred_element_type=F32)
           + jnp.dot(yml.astype(BF16), w_ref[o1:o2, :], preferred_element_type=F32)
           + jnp.dot(yna_ref[0], w_ref[o2:, :], preferred_element_type=F32))
    xn = x_ref[0] + mod_ref[0, 0:1, :] * acc
    xn_ref[0] = xn
    h2 = xn * lax.rsqrt(jnp.mean(xn * xn, -1, keepdims=True) + EPS) * n2g_ref[...]
    h2 = h2 * (1.0 + mod_ref[0, 2:3, :]) + mod_ref[0, 1:2, :]
    h2_ref[0] = h2
    logits = _nt_dot(rw_ref[...], h2, precision=lax.Precision.HIGHEST)
    s = jax.nn.sigmoid(logits)
    sel = s + rb_ref[...]
    rows = lambda m: [m[e:e + 1, :] for e in range(N_EXPERTS)]
    e1, e2, w1, w2 = _route(rows(sel), rows(s))
    re_ref[0, 0:1, :] = e1
    re_ref[0, 1:2, :] = e2
    rwt_ref[0, 0:1, :] = w1
    rwt_ref[0, 1:2, :] = w2


def _out_proj(ygm, hf, hb, ml, yna, x, mod, w_out, ml_g, n2_g, router_wt, router_b):
    G, R, D = x.shape
    tm = min(R, 512)
    row = lambda b, i: (b, i, 0)
    whole2 = lambda shape: pl.BlockSpec(shape, lambda b, i: (0, 0))
    return pl.pallas_call(
        _out_proj_kernel,
        grid=(G, R // tm),
        in_specs=[pl.BlockSpec((1, tm, GM_WIDTH), row),
                  pl.BlockSpec((1, tm, ML_WIDTH), row),
                  pl.BlockSpec((1, tm, ML_WIDTH), row),
                  pl.BlockSpec((1, tm, ML_WIDTH), lambda b, i: (b, i, 3)),
                  pl.BlockSpec((1, tm, NA_WIDTH), row),
                  pl.BlockSpec((1, tm, D), row),
                  pl.BlockSpec((1, 3, D), lambda b, i: (b, 0, 0)),
                  whole2((D, D)), whole2((1, ML_WIDTH)), whole2((1, D)),
                  whole2((N_EXPERTS, D)), whole2((N_EXPERTS, 1))],
        out_specs=[pl.BlockSpec((1, tm, D), row), pl.BlockSpec((1, tm, D), row),
                   pl.BlockSpec((1, 2, tm), lambda b, i: (b, 0, i)),
                   pl.BlockSpec((1, 2, tm), lambda b, i: (b, 0, i))],
        out_shape=[jax.ShapeDtypeStruct((G, R, D), F32), jax.ShapeDtypeStruct((G, R, D), F32),
                   jax.ShapeDtypeStruct((G, 2, R), jnp.int32), jax.ShapeDtypeStruct((G, 2, R), F32)],
        compiler_params=_cparams("arbitrary", "arbitrary"),
        name="out_proj",
    )(ygm, hf, hb, ml, yna, x, mod, w_out, ml_g, n2_g, router_wt, router_b)


def _experts_kernel(te_ref, nu_ref, src_cur, src_nxt, h_hbm, wg_ref, wu_ref, wd_ref, o_ref,
                    acc_ref, wgb, wub, wdb, xbuf, sem):
    i = pl.program_id(0)
    n_used = nu_ref[0]
    used = i < n_used
    slot = lax.rem(i, 2)
    new_expert = (i == 0) | (te_ref[i] != te_ref[jnp.maximum(i - 1, 0)])

    def row_copy(src_ref, s, r):
        return pltpu.make_async_copy(h_hbm.at[pl.ds(src_ref[0, 0, r], 1), :], xbuf.at[s, pl.ds(r, 1), :],
                                     sem.at[s])

    def tile_wait(s):
        pltpu.make_async_copy(xbuf.at[s], xbuf.at[s], sem.at[s]).wait()

    @pl.when((i == 0) & used)
    def _():
        def body(r, carry):
            row_copy(src_cur, 0, r).start()
            return carry
        lax.fori_loop(0, MOE_TILE, body, 0, unroll=8)

    @pl.when(used & new_expert)
    def _():
        def cast_rows(c, carry):
            rs = pl.ds(pl.multiple_of(c * LANE, LANE), LANE)
            wgb[rs, :] = wg_ref[0, 0, rs, :].astype(BF16)
            wub[rs, :] = wu_ref[0, 0, rs, :].astype(BF16)
            wdb[rs, :] = wd_ref[0, 0, rs, :].astype(BF16)
            return carry
        lax.fori_loop(0, D_MODEL // LANE, cast_rows, 0)

    @pl.when(used)
    def _():
        tile_wait(slot)
        x = xbuf[slot].astype(BF16)
        nxt = 1 - slot
        n_chunks = D_EXPERT // EXPERT_CHUNK
        per = MOE_TILE // (2 * n_chunks)
        for j in range(n_chunks):
            cs = slice(j * EXPERT_CHUNK, (j + 1) * EXPERT_CHUNK)
            for r in range(2 * j * per, (2 * j + 1) * per):
                row_copy(src_nxt, nxt, r).start()
            g = jnp.dot(x, wgb[:, cs], preferred_element_type=F32)
            u = jnp.dot(x, wub[:, cs], preferred_element_type=F32)
            a = (g * jax.nn.sigmoid(g) * u).astype(BF16)
            for r in range((2 * j + 1) * per, (2 * j + 2) * per):
                row_copy(src_nxt, nxt, r).start()
            y = jnp.dot(a, wdb[cs, :], preferred_element_type=F32)
            if j == 0:
                acc_ref[...] = y
            else:
                acc_ref[...] += y
        o_ref[...] = acc_ref[...].astype(BF16)

        @pl.when(i + 1 >= n_used)
        def _():
            tile_wait(nxt)

    @pl.when(jnp.logical_not(used))
    def _():
        o_ref[...] = jnp.zeros_like(o_ref)


def _experts(l, tile_expert, n_used, src, h_all, wg, wu, wd):
    D = h_all.shape[1]
    n_tiles = src.shape[0] // MOE_TILE
    src3 = src.reshape(n_tiles, 1, MOE_TILE)
    wspec = lambda r, c: pl.BlockSpec((1, 1, r, c), lambda i, te, nu: (l, te[i], 0, 0))
    src_spec = lambda ahead: pl.BlockSpec(
        (1, 1, MOE_TILE), lambda i, te, nu: (jnp.minimum(i + ahead, n_tiles - 1), 0, 0),
        memory_space=pltpu.SMEM)
    return pl.pallas_call(
        _experts_kernel,
        grid_spec=pltpu.PrefetchScalarGridSpec(
            num_scalar_prefetch=2,
            grid=(n_tiles,),
            in_specs=[src_spec(0), src_spec(1), pl.BlockSpec(memory_space=pl.ANY),
                      wspec(D, D_EXPERT), wspec(D, D_EXPERT), wspec(D_EXPERT, D)],
            out_specs=pl.BlockSpec((MOE_TILE, D), lambda i, te, nu: (i, 0)),
            scratch_shapes=[pltpu.VMEM((MOE_TILE, D), F32), pltpu.VMEM((D, D_EXPERT), BF16),
                            pltpu.VMEM((D, D_EXPERT), BF16), pltpu.VMEM((D_EXPERT, D), BF16),
                            pltpu.VMEM((2, MOE_TILE, D), F32), pltpu.SemaphoreType.DMA((2,))]),
        out_shape=jax.ShapeDtypeStruct((n_tiles * MOE_TILE, D), BF16),
        compiler_params=_cparams("arbitrary"),
        name="experts",
    )(tile_expert, n_used, src3, src3, h_all, wg, wu, wd)


def _dispatch(e_idx):
    n_tok = e_idx.shape[0]
    n_asg = 2 * n_tok
    n_tiles = -(-n_asg // MOE_TILE) + N_EXPERTS
    flat_e = e_idx.reshape(-1)
    onehot = (flat_e[:, None] == jnp.arange(N_EXPERTS, dtype=jnp.int32)[None, :]).astype(jnp.int32)
    csum = jnp.cumsum(onehot, axis=0)
    rank = jnp.take_along_axis(csum, flat_e[:, None], axis=1)[:, 0] - 1
    counts = csum[-1]
    tiles_e = (counts + MOE_TILE - 1) // MOE_TILE
    tile_end = jnp.cumsum(tiles_e)
    tile_start = tile_end - tiles_e
    dest = tile_start[flat_e] * MOE_TILE + rank
    n_used = tile_end[-1]
    tid = jnp.arange(n_tiles, dtype=jnp.int32)
    te = jnp.minimum(jnp.searchsorted(tile_end, tid, side="right"), N_EXPERTS - 1).astype(jnp.int32)
    te = jnp.where(tid < n_used, te, te[jnp.maximum(n_used - 1, 0)])
    flat_t = jnp.arange(n_asg, dtype=jnp.int32) // 2
    src = jnp.zeros((n_tiles * MOE_TILE,), jnp.int32).at[dest].set(
        flat_t, unique_indices=True, mode="promise_in_bounds")
    return dest.reshape(n_tok, 2), src, te, n_used.reshape(1).astype(jnp.int32)


def _combine_kernel(final, xn_ref, y0_ref, y1_ref, w_ref, g2_ref, fg_ref, o_ref):
    w = w_ref[0]
    f = y0_ref[0].astype(F32) * w[:, 0:1] + y1_ref[0].astype(F32) * w[:, 1:2]
    x = xn_ref[0] + g2_ref[0] * f
    if final:
        x = x * lax.rsqrt(jnp.mean(x * x, -1, keepdims=True) + EPS) * fg_ref[...]
    o_ref[0] = x


def _combine(xn, y0, y1, wt, g2, final_g, final):
    G, R, D = xn.shape
    tm = min(R, 512)
    row = lambda b, i: (b, i, 0)
    return pl.pallas_call(
        functools.partial(_combine_kernel, final),
        grid=(G, R // tm),
        in_specs=[pl.BlockSpec((1, tm, D), row), pl.BlockSpec((1, tm, D), row), pl.BlockSpec((1, tm, D), row),
                  pl.BlockSpec((1, tm, 2), row),
                  pl.BlockSpec((1, 1, D), lambda b, i: (b, 0, 0)),
                  pl.BlockSpec((1, D), lambda b, i: (0, 0))],
        out_specs=pl.BlockSpec((1, tm, D), row),
        out_shape=jax.ShapeDtypeStruct((G, R, D), F32),
        compiler_params=_cparams("arbitrary", "arbitrary"),
        name="combine_final" if final else "combine",
    )(xn, y0, y1, wt, g2, final_g)


def _rope_tables(T):
    lane = np.arange(LANE)
    half = ML_HEAD_DIM // 4
    inv = jnp.tile(ROPE_BASE ** (-jnp.arange(half, dtype=F32) / half), LANE // half)
    t = jnp.arange(T)
    pos = jnp.where(((lane // (2 * half)) % 2 == 0)[None, :], (t // GRID_W)[:, None], (t % GRID_W)[:, None])
    ang = pos.astype(F32) * inv[None, :]
    sign = np.where((lane // half) % 2 == 0, -1.0, 1.0).astype(np.float32)
    return jnp.cos(ang), jnp.sin(ang) * sign[None, :]


def _reorder_w_in(w_in):
    pad = jnp.zeros(w_in.shape[:2] + (LANE - 4 * ML_HEADS,), w_in.dtype)
    return jnp.concatenate([w_in[..., :OFF_GATES], w_in[..., OFF_NA:], w_in[..., OFF_GATES:OFF_NA], pad],
                           axis=-1).astype(BF16)


def _mixers(l, p, pc, prm, need_ctx):
    gm, ml, gt, q, k, v = p
    gmc, mlc, gtc, qc, kc, vc = pc
    B = ml.shape[0]
    gm_args = (prm["gm_ws"][l], prm["gm_bs_full"][l], prm["gm_g"][l])
    conv = (prm["conv_w"][l], prm["conv_b"][l])
    nh = ML_HEADS
    tr = lambda g: jnp.transpose(
        jnp.concatenate([g[..., 0:nh], g[..., 2 * nh:3 * nh], g[..., nh:2 * nh], g[..., 3 * nh:4 * nh]], -1),
        (0, 2, 1))
    m_zero = jnp.zeros((B, 2 * nh, LANE), F32)
    s_zero = jnp.zeros((B, nh, ML_CHUNK, 2 * LANE), F32)
    qmc, kmc, grc = _mlstm_prep(mlc, prm["cos_c"], prm["sin_c"], *conv, tr(gtc), prm["gbias"][l])
    hfc, hbc, m_st, s_st = _mlstm(qmc, kmc, mlc, grc, m_zero, s_zero)
    qm, km, gr = _mlstm_prep(ml, prm["cos_l"], prm["sin_l"], *conv, tr(gt), prm["gbias"][l])
    hf, hb, _, _ = _mlstm(qm, km, ml, gr, m_st, s_st)
    y_na = _na_latent(q, k, v, kc, vc, prm["na_tab"][l])
    y = (_gmlp(gm, *gm_args), hf, hb, ml, y_na)
    if not need_ctx:
        return y, None
    return y, (_gmlp(gmc, *gm_args), hfc, hbc, mlc, _ctx_attn(qc, kc, vc))


def _rows(a, idx):
    return a.at[idx].get(mode="promise_in_bounds")


def _moe(l, h2_all, e_all, prm):
    dest, src, te, n_used = _dispatch(e_all)
    yb = _experts(l, te, n_used, src, h2_all, prm["wg"], prm["wu"], prm["wd"])
    return yb, dest


def kernel(x, c, ctx, c_ctx, ada_w, ada_b, norm1_g, norm2_g, w_in, w_out, gm_ws, gm_bs, gm_norm_g,
           ml_conv_w, ml_conv_b, ml_gate_b, ml_norm_g, na_rpb, router_w, router_b,
           moe_w_gate, moe_w_up, moe_w_down, final_g):
    B, T, D = x.shape
    Lc = ctx.shape[1]
    cos_l, sin_l = _rope_tables(T)
    prm = dict(
        gm_ws=gm_ws.astype(BF16),
        gm_bs_full=jnp.repeat(jnp.transpose(gm_bs, (0, 2, 1)), GM_HEAD_DIM, axis=-1),
        gm_g=gm_norm_g[:, None, :],
        conv_w=ml_conv_w, conv_b=ml_conv_b[:, None, :],
        gbias=ml_gate_b[:, jnp.array([0, 2, 1, 3])].reshape(DEPTH, 4 * ML_HEADS, 1),
        cos_l=cos_l, sin_l=sin_l,
        cos_c=jnp.ones((Lc, LANE), F32), sin_c=jnp.zeros((Lc, LANE), F32),
        na_tab=jnp.stack([_na_bias_table(na_rpb[l]) for l in range(DEPTH)]),
        wg=moe_w_gate, wu=moe_w_up, wd=moe_w_down,
    )
    w_in_r = _reorder_w_in(w_in)
    w_out_b = w_out.astype(BF16)
    router_wt = jnp.transpose(router_w)
    router_bc = router_b[:, None]

    cs = jnp.concatenate([c, c_ctx[None, :], jnp.zeros((8 - B - 1, D), F32)], axis=0)
    mods = _ada_all(cs, ada_w, ada_b).reshape(DEPTH, 8, 6, D)

    xc = ctx
    for l in range(DEPTH):
        need_ctx = l < DEPTH - 1
        mod_l = mods[l, :B]
        mod_c = jnp.broadcast_to(mods[l, B:B + 1], (B, 6, D))
        n1 = norm1_g[l][None, :]
        p = _in_proj(x, mod_l[:, 0:2], n1, w_in_r[l])
        pc = _in_proj(xc, mod_c[:, 0:2], n1, w_in_r[l])
        y, yc = _mixers(l, p, pc, prm, need_ctx)
        op_args = (w_out_b[l], ml_norm_g[l][None, :], norm2_g[l][None, :], router_wt, router_bc)
        xn, h2, re, rw = _out_proj(*y, x, mod_l[:, 2:5], *op_args)
        h2_all = h2.reshape(B * T, D)
        e_all = jnp.transpose(re, (0, 2, 1)).reshape(B * T, 2)
        if need_ctx:
            xnc, h2c, rec, rwc = _out_proj(*yc, xc, mod_c[:, 2:5], *op_args)
            h2_all = jnp.concatenate([h2_all, h2c.reshape(B * Lc, D)], axis=0)
            e_all = jnp.concatenate([e_all, jnp.transpose(rec, (0, 2, 1)).reshape(B * Lc, 2)], axis=0)
        yb, dest = _moe(l, h2_all, e_all, prm)
        final = l == DEPTH - 1
        fg = final_g[None, :]
        n_lat = B * T
        picked = lambda lo, hi, k, L: _rows(yb, dest[lo:hi, k]).reshape(B, L, D)
        x = _combine(xn, picked(0, n_lat, 0, T), picked(0, n_lat, 1, T),
                     jnp.transpose(rw, (0, 2, 1)), mod_l[:, 5:6], fg, final)
        if need_ctx:
            xc = _combine(xnc, picked(n_lat, None, 0, Lc), picked(n_lat, None, 1, Lc),
                          jnp.transpose(rwc, (0, 2, 1)), mod_c[:, 5:6], fg, False)
    return x
```

```python
import functools

import jax
import jax.numpy as jnp
import numpy as np
from jax import lax
from jax.experimental import pallas as pl
from jax.experimental.pallas import tpu as pltpu

F32 = jnp.float32
BF16 = jnp.bfloat16

D_MODEL = 1024
DEPTH = 4
GRID_W = 64
EPS = 1e-6

GM_HEADS = 4
GM_WIDTH = D_MODEL // 4
GM_HEAD_DIM = GM_WIDTH // GM_HEADS
GM_CHUNK = 128

ML_HEADS = 4
ML_WIDTH = D_MODEL // 4
ML_HEAD_DIM = ML_WIDTH // ML_HEADS
ML_CHUNK = 128

NA_HEADS = 8
NA_WIDTH = D_MODEL // 2
NA_HEAD_DIM = NA_WIDTH // NA_HEADS
NA_ROWS = 8
NA_COLS = 16

ROPE_BASE = 10000.0

OFF_ML = 2 * GM_WIDTH
OFF_GATES = OFF_ML + 4 * ML_WIDTH
OFF_NA = OFF_GATES + 4 * ML_HEADS
N_IN = OFF_NA + 3 * NA_WIDTH

N_EXPERTS = 16
N_GROUPS = 4
EXP_PER_GROUP = N_EXPERTS // N_GROUPS
D_EXPERT = D_MODEL

LANE = 128
SEG_GM = (0, 2 * GM_WIDTH)
SEG_ML = (SEG_GM[1], SEG_GM[1] + 4 * ML_WIDTH)
SEG_Q = (SEG_ML[1], SEG_ML[1] + NA_WIDTH)
SEG_K = (SEG_Q[1], SEG_Q[1] + NA_WIDTH)
SEG_V = (SEG_K[1], SEG_K[1] + NA_WIDTH)
SEG_GATES = (SEG_V[1], SEG_V[1] + LANE)
N_IN_PAD = SEG_GATES[1]

NA_QROWS = 8
NA_GROUP = 4
NA_UNROLL = 4
NA_BAND = 16
NA_KBLK_ROWS = 4
MOE_TILE = 512
EXPERT_CHUNK = 256
MASK_NEG = -1e30
N_GATE_ROWS = 6
VMEM_LIMIT = 56 * 2 ** 20


def _cparams(*sem):
    return pltpu.CompilerParams(dimension_semantics=sem, vmem_limit_bytes=VMEM_LIMIT)


def _nt_dot(a, b, precision=None):
    return lax.dot_general(a, b, (((1,), (1,)), ((), ())), precision=precision,
                           preferred_element_type=F32)


def _tn_dot(a, b):
    return lax.dot_general(a, b, (((0,), (0,)), ((), ())), preferred_element_type=F32)


def _ada_kernel(c_ref, w_ref, b_ref, o_ref):
    c = c_ref[...]
    sc = c * jax.nn.sigmoid(c)
    o_ref[0] = jnp.dot(sc, w_ref[0], preferred_element_type=F32) + b_ref[0]


def _ada_all(cs, ada_w, ada_b):
    n_out = ada_w.shape[-1] // D_MODEL
    return pl.pallas_call(
        _ada_kernel,
        grid=(DEPTH, n_out),
        in_specs=[pl.BlockSpec((8, D_MODEL), lambda l, j: (0, 0)),
                  pl.BlockSpec((1, D_MODEL, D_MODEL), lambda l, j: (l, 0, j)),
                  pl.BlockSpec((1, 1, D_MODEL), lambda l, j: (l, 0, j))],
        out_specs=pl.BlockSpec((1, 8, D_MODEL), lambda l, j: (l, 0, j)),
        out_shape=jax.ShapeDtypeStruct((DEPTH, 8, n_out * D_MODEL), F32),
        compiler_params=_cparams("arbitrary", "arbitrary"),
        name="adaln",
    )(cs, ada_w, ada_b.reshape(DEPTH, 1, -1))


def _gmlp_chunk(z_pre, ws_ref, bs, g):
    lane = lax.broadcasted_iota(jnp.int32, (1, GM_WIDTH), 1)
    z = jax.nn.gelu(z_pre)
    u = z[:, :GM_WIDTH]
    v = z[:, GM_WIDTH:]
    mu = jnp.mean(v, -1, keepdims=True)
    vc = v - mu
    var = jnp.mean(vc * vc, -1, keepdims=True)
    vn = (vc * lax.rsqrt(var + EPS) * g).astype(BF16)
    sv = jnp.zeros((GM_CHUNK, GM_WIDTH), F32)
    for h in range(GM_HEADS):
        full = jnp.dot(ws_ref[h], vn, preferred_element_type=F32)
        in_head = (lane >= h * GM_HEAD_DIM) & (lane < (h + 1) * GM_HEAD_DIM)
        sv = jnp.where(in_head, full, sv)
    return u * (sv + bs)


def _in_proj_kernel(x_ref, mod_ref, g_ref, w_ref, ws_ref, bs_ref, gmg_ref,
                    ygm_ref, ml_ref, gt_ref, q_ref, k_ref, v_ref):
    x = x_ref[0]
    h = x * lax.rsqrt(jnp.mean(x * x, -1, keepdims=True) + EPS) * g_ref[...]
    h = h * (1.0 + mod_ref[0, 1:2, :]) + mod_ref[0, 0:1, :]
    hb = h.astype(BF16)

    def seg(s):
        return jnp.dot(hb, w_ref[:, s[0]:s[1]], preferred_element_type=F32)

    gm = seg(SEG_GM)
    ml_ref[0] = seg(SEG_ML)
    for c in range(gm.shape[0] // GM_CHUNK):
        rows = slice(c * GM_CHUNK, (c + 1) * GM_CHUNK)
        ygm_ref[0, rows, :] = _gmlp_chunk(gm[rows], ws_ref, bs_ref[...], gmg_ref[...]).astype(BF16)
    gt_ref[0] = seg(SEG_GATES)
    q_ref[0] = (seg(SEG_Q) * (NA_HEAD_DIM ** -0.5)).astype(BF16)
    k_ref[0] = seg(SEG_K).astype(BF16)
    v_ref[0] = seg(SEG_V).astype(BF16)


def _in_proj(x, mod, g, w, gm_ws, gm_bs_full, gm_g):
    G, R, D = x.shape
    tm = min(R, 512)
    row = lambda b, i: (b, i, 0)
    whole = lambda shape: pl.BlockSpec(shape, lambda b, i: (0,) * len(shape))
    widths = (GM_WIDTH, SEG_ML[1] - SEG_ML[0], LANE, NA_WIDTH, NA_WIDTH, NA_WIDTH)
    dtypes = (BF16, F32, F32, BF16, BF16, BF16)
    return pl.pallas_call(
        _in_proj_kernel,
        grid=(G, R // tm),
        in_specs=[pl.BlockSpec((1, tm, D), row),
                  pl.BlockSpec((1, 2, D), lambda b, i: (b, 0, 0)),
                  whole((1, D)), whole((D, N_IN_PAD)),
                  whole((GM_HEADS, GM_CHUNK, GM_CHUNK)), whole((GM_CHUNK, GM_WIDTH)), whole((1, GM_WIDTH))],
        out_specs=[pl.BlockSpec((1, tm, n), row) for n in widths],
        out_shape=[jax.ShapeDtypeStruct((G, R, n), dt) for n, dt in zip(widths, dtypes)],
        compiler_params=_cparams("arbitrary", "arbitrary"),
        name="in_proj",
    )(x, mod, g, w, gm_ws, gm_bs_full, gm_g)


def _log_sigmoid(x):
    return jnp.minimum(x, 0.0) - jnp.log1p(jnp.exp(-jnp.abs(x)))


def _rope(x, cos, sin_signed):
    lane = lax.broadcasted_iota(jnp.int32, (1, LANE), 1)
    first_half = (lane & 16) == 0
    partner = jnp.where(first_half, pltpu.roll(x, LANE - 16, 1), pltpu.roll(x, 16, 1))
    return x * cos + partner * sin_signed


def _chunk_scan(x, op, fill, reverse):
    pos = lax.broadcasted_iota(jnp.int32, (1, x.shape[1]), 1) % ML_CHUNK
    n = x.shape[1]
    sh = 1
    while sh < ML_CHUNK:
        if reverse:
            shifted = jnp.where(pos < ML_CHUNK - sh, pltpu.roll(x, n - sh, 1), fill)
        else:
            shifted = jnp.where(pos >= sh, pltpu.roll(x, sh, 1), fill)
        x = op(x, shifted)
        sh *= 2
    return x


def _mlstm_prep_kernel(n_blocks, x_ref, hp_ref, hn_ref, cos_ref, sin_ref, cw_ref, cb_ref, g_ref, gbias_ref,
                       q_ref, k_ref, go_ref):
    nh2 = 2 * ML_HEADS
    g = g_ref[0] + gbias_ref[...]
    is_fwd = lax.broadcasted_iota(jnp.int32, (nh2, 1), 0) < ML_HEADS
    li = g[0:nh2]
    lf = _log_sigmoid(g[nh2:])
    ps = _chunk_scan(lf, jnp.add, 0.0, False)
    ss = _chunk_scan(lf, jnp.add, 0.0, True)
    bcum = jnp.where(is_fwd, ps, ss)
    r = li - bcum
    pm = _chunk_scan(r, jnp.maximum, MASK_NEG, False)
    sm = _chunk_scan(r, jnp.maximum, MASK_NEG, True)
    rmax = jnp.maximum(pm, sm)
    groups = (bcum, r, jnp.where(is_fwd, pm, sm), jnp.exp(r - rmax), rmax, ps + ss - lf)
    for j, val in enumerate(groups):
        go_ref[0, nh2 * j:nh2 * (j + 1), :] = val

    i = pl.program_id(1)
    W = ML_WIDTH
    x = x_ref[0]
    tr = x.shape[0]
    rid = lax.broadcasted_iota(jnp.int32, (tr, 1), 0)
    i_row = jnp.zeros((1, 2 * W), jnp.int32) + i
    prev = jnp.where(i_row == 0, 0.0, hp_ref[0, 7:8, :])
    nxt = jnp.where(i_row == n_blocks - 1, 0.0, hn_ref[0, 0:1, :])
    xm1 = jnp.where(rid == 0, prev, pltpu.roll(x, 1, 0))
    xp1 = jnp.where(rid == tr - 1, nxt, pltpu.roll(x, tr - 1, 0))
    y = xm1 * cw_ref[0:1, :] + x * cw_ref[1:2, :] + xp1 * cw_ref[2:3, :] + cb_ref[...]
    y = y * jax.nn.sigmoid(y)
    cos = cos_ref[...]
    sin = sin_ref[...]
    parts = [_rope(y[:, j * LANE:(j + 1) * LANE], cos, sin) for j in range(2 * W // LANE)]
    q_ref[0] = (jnp.concatenate(parts[:W // LANE], axis=1) * (ML_HEAD_DIM ** -0.5)).astype(BF16)
    for j in range(W // LANE):
        k_ref[0, j * LANE:(j + 1) * LANE, :] = parts[W // LANE + j].T.astype(BF16)


def _mlstm_prep(ml, cos, sin, conv_w, conv_b, gates_t, gbias):
    B, T, _ = ml.shape
    n_g = gates_t.shape[1]
    W = ML_WIDTH
    tr = min(T, 1024)
    nb = T // tr
    halo_per_blk = tr // 8
    n_hblk = T // 8
    whole = lambda shape: pl.BlockSpec(shape, lambda b, i: (0, 0))
    return pl.pallas_call(
        functools.partial(_mlstm_prep_kernel, nb),
        grid=(B, nb),
        in_specs=[pl.BlockSpec((1, tr, 2 * W), lambda b, i: (b, i, 0)),
                  pl.BlockSpec((1, 8, 2 * W), lambda b, i: (b, jnp.maximum(i * halo_per_blk - 1, 0), 0)),
                  pl.BlockSpec((1, 8, 2 * W),
                               lambda b, i: (b, jnp.minimum((i + 1) * halo_per_blk, n_hblk - 1), 0)),
                  pl.BlockSpec((tr, LANE), lambda b, i: (i, 0)),
                  pl.BlockSpec((tr, LANE), lambda b, i: (i, 0)),
                  whole((3, 2 * W)), whole((1, 2 * W)),
                  pl.BlockSpec((1, n_g, tr), lambda b, i: (b, 0, i)), whole((n_g, 1))],
        out_specs=[pl.BlockSpec((1, tr, W), lambda b, i: (b, i, 0)),
                   pl.BlockSpec((1, W, tr), lambda b, i: (b, 0, i)),
                   pl.BlockSpec((1, 2 * ML_HEADS * N_GATE_ROWS, tr), lambda b, i: (b, 0, i))],
        out_shape=[jax.ShapeDtypeStruct((B, T, W), BF16), jax.ShapeDtypeStruct((B, W, T), BF16),
                   jax.ShapeDtypeStruct((B, 2 * ML_HEADS * N_GATE_ROWS, T), F32)],
        compiler_params=_cparams("arbitrary", "arbitrary"),
        name="mlstm_prep",
    )(ml, ml, ml, cos, sin, conv_w, conv_b, gates_t, gbias)


def _mlstm_kernel(n_chunks, qf_ref, ktf_ref, vf_ref, gf_ref, qb_ref, ktb_ref, vb_ref, gb_ref,
                  m0_ref, s0_ref, hf_ref, hb_ref, mfin_ref, sfin_ref, m_scr, s_scr, bd_scr):
    c = pl.program_id(0)
    L = ML_CHUNK
    HD = ML_HEAD_DIM
    NH = ML_HEADS
    n_batch = m_scr.shape[0]

    @pl.when(c == 0)
    def _():
        m_scr[...] = m0_ref[...]
        s_scr[...] = s0_ref[...]
        bd_scr[...] = jnp.zeros_like(bd_scr)

    def row_bcast(x, j):
        return jnp.broadcast_to(x[j:j + 1, :], (L, LANE))

    row = lax.broadcasted_iota(jnp.int32, (L, L), 0)
    col = lax.broadcasted_iota(jnp.int32, (L, L), 1)
    low_half = lax.broadcasted_iota(jnp.int32, (1, LANE), 1) < HD
    own_rows = jnp.concatenate([row < HD, row >= HD], axis=1)
    is_fwd = lax.broadcasted_iota(jnp.int32, (2 * NH, 1), 0) < NH

    gate = []
    for b in range(n_batch):
        gq = lambda i: jnp.where(is_fwd, gf_ref[b, 2 * NH * i:2 * NH * (i + 1), :],
                                 gb_ref[b, 2 * NH * i:2 * NH * (i + 1), :])
        bcum, r, rcmax, ew, rmax, b_last = (gq(i) for i in range(N_GATE_ROWS))
        m0 = m_scr[b]
        mu = jnp.maximum(m0, rcmax)
        emt = jnp.exp(-(bcum + mu))
        m_last = jnp.maximum(m0, rmax)
        m_scr[b] = b_last + m_last
        stack = jnp.concatenate([mu, emt, jnp.zeros((L - 4 * NH, LANE), F32)], axis=0)
        gate.append(dict(r=r, ew=ew, m0=m0, a_st=jnp.exp(m0 - m_last), g_st=jnp.exp(rmax - m_last),
                         cols=stack.T))

    dirs = ((qf_ref, ktf_ref, vf_ref, hf_ref), (qb_ref, ktb_ref, vb_ref, hb_ref))
    items = [(b, d, pr) for b in range(n_batch) for d in range(2) for pr in range(NH // 2)]

    pre = []
    for b, d, pr in items:
        q_ref, kt_ref, v_ref, _ = dirs[d]
        ps = slice(pr * LANE, (pr + 1) * LANE)
        ja = NH * d + 2 * pr
        g = gate[b]
        qp = q_ref[b, :, ps]
        kt = kt_ref[b, ps, :]
        vp = v_ref[b, :, ps]
        st = s_scr[b, 2 * d + pr]
        ktf = kt.astype(F32)
        kt2 = jnp.concatenate([jnp.where(row < HD, ktf, 0.0), jnp.where(row < HD, 0.0, ktf)],
                              axis=1).astype(BF16)
        s2 = jnp.dot(qp, kt2, preferred_element_type=F32)
        inter = jnp.dot(qp, st.astype(BF16), preferred_element_type=F32)
        vab = jnp.concatenate([jnp.where(low_half, vp, 1.0), jnp.where(low_half, 1.0, vp)],
                              axis=1).astype(BF16)
        ktw = ktf * jnp.where(row < HD, row_bcast(g["ew"], ja), row_bcast(g["ew"], ja + 1))
        upd = jnp.dot(ktw.astype(BF16), vab, preferred_element_type=F32)
        pre.append((s2, inter, vab, upd, st))

    probs = []
    for (b, d, pr), (s2, _, _, _, _) in zip(items, pre):
        ja = NH * d + 2 * pr
        g = gate[b]
        valid = (col <= row) if d == 0 else (col >= row)
        mu_c = [jnp.broadcast_to(g["cols"][:, j:j + 1], (L, LANE)) for j in (ja, ja + 1)]
        e2 = jnp.concatenate([jnp.where(valid, jnp.exp(row_bcast(g["r"], j) - mc), 0.0)
                              for j, mc in zip((ja, ja + 1), mu_c)], axis=1)
        a2 = jnp.concatenate([jnp.exp(row_bcast(g["m0"], j) - mc) for j, mc in zip((ja, ja + 1), mu_c)],
                             axis=1)
        probs.append(((s2 * e2).astype(BF16), a2))

    for it, ((b, d, pr), (_, inter, vab, upd, st), (p2, a2)) in enumerate(zip(items, pre, probs)):
        out_ref = dirs[d][3]
        ps = slice(pr * LANE, (pr + 1) * LANE)
        ja = NH * d + 2 * pr
        g = gate[b]
        bd_scr[it, 0:L, 0:LANE] = vab[:, 0:LANE]
        bd_scr[it, L:2 * L, LANE:2 * LANE] = vab[:, LANE:2 * LANE]
        intra = jnp.dot(p2, bd_scr[it], preferred_element_type=F32)
        tot = intra + a2 * inter
        tot_a = tot[:, 0:LANE]
        tot_b = tot[:, LANE:2 * LANE]
        num = jnp.where(low_half, tot_a, tot_b)
        den = pltpu.roll(jnp.where(low_half, tot_b, tot_a), HD, 1)
        floor = jnp.where(low_half, jnp.broadcast_to(g["cols"][:, 8 + ja:8 + ja + 1], (L, LANE)),
                          jnp.broadcast_to(g["cols"][:, 8 + ja + 1:8 + ja + 2], (L, LANE)))
        out_ref[b, :, ps] = num / jnp.maximum(jnp.abs(den), floor)
        a_s = jnp.concatenate([row_bcast(g["a_st"], ja), row_bcast(g["a_st"], ja + 1)], axis=1)
        g_s = jnp.concatenate([row_bcast(g["g_st"], ja), row_bcast(g["g_st"], ja + 1)], axis=1)
        s_scr[b, 2 * d + pr] = a_s * st + g_s * jnp.where(own_rows, upd, 0.0)

    @pl.when(c == n_chunks - 1)
    def _():
        mfin_ref[...] = m_scr[...]
        sfin_ref[...] = s_scr[...]


def _mlstm(q, kt, ml, grows, m0, s0):
    B, T, W = q.shape
    L = ML_CHUNK
    nc = T // L

    def specs(ci):
        return [pl.BlockSpec((B, L, W), lambda c: (0, ci(c), 0)),
                pl.BlockSpec((B, W, L), lambda c: (0, 0, ci(c))),
                pl.BlockSpec((B, L, W), lambda c: (0, ci(c), 2)),
                pl.BlockSpec((B, grows.shape[1], L), lambda c: (0, 0, ci(c)))]

    m_spec = pl.BlockSpec(m0.shape, lambda c: (0, 0, 0))
    s_spec = pl.BlockSpec(s0.shape, lambda c: (0, 0, 0, 0))
    dir_args = [q, kt, ml, grows]
    return pl.pallas_call(
        functools.partial(_mlstm_kernel, nc),
        grid=(nc,),
        in_specs=specs(lambda c: c) + specs(lambda c: nc - 1 - c) + [m_spec, s_spec],
        out_specs=[pl.BlockSpec((B, L, W), lambda c: (0, c, 0)),
                   pl.BlockSpec((B, L, W), lambda c: (0, nc - 1 - c, 0)),
                   m_spec, s_spec],
        out_shape=[jax.ShapeDtypeStruct((B, T, W), F32),
                   jax.ShapeDtypeStruct((B, T, W), F32),
                   jax.ShapeDtypeStruct(m0.shape, F32),
                   jax.ShapeDtypeStruct(s0.shape, F32)],
        scratch_shapes=[pltpu.VMEM(m0.shape, F32), pltpu.VMEM(s0.shape, F32),
                        pltpu.VMEM((B * ML_HEADS, 2 * L, 2 * LANE), BF16)],
        compiler_params=_cparams("arbitrary"),
        name="mlstm",
    )(*dir_args, *dir_args, m0, s0)


def _softmax_pv(s_list, v_list):
    m = functools.reduce(jnp.maximum, [jnp.max(s, -1, keepdims=True) for s in s_list])
    ps = [jnp.exp(s - m) for s in s_list]
    l = functools.reduce(jnp.add, [jnp.sum(p, -1, keepdims=True) for p in ps])
    o = functools.reduce(jnp.add, [jnp.dot(p.astype(BF16), v, preferred_element_type=F32)
                                   for p, v in zip(ps, v_list)])
    return o / l


def _na_kernel(n_rows, q_ref, *refs):
    nb = NA_BAND // NA_KBLK_ROWS
    k_blks = refs[0:nb]
    v_blks = refs[nb:2 * nb]
    kc_ref, vc_ref, bt_ref, o_ref, kband, vband = refs[2 * nb:]
    i = pl.program_id(1)
    blk_tok = NA_KBLK_ROWS * GRID_W
    HD = NA_HEAD_DIM
    GW = NA_GROUP * HD
    head_of_lane = lax.broadcasted_iota(jnp.int32, (1, GW), 1) // HD
    for j in range(nb):
        kband[j * blk_tok:(j + 1) * blk_tok, :] = k_blks[j][0]
        vband[j * blk_tok:(j + 1) * blk_tok, :] = v_blks[j][0]
    band_row0 = jnp.clip(i * NA_QROWS - NA_ROWS // 2, 0, n_rows - NA_BAND)
    n_win = NA_ROWS * GRID_W
    zero = jnp.zeros((), BF16)

    def rows_body(t, carry):
        items = []
        for u in range(NA_UNROLL):
            a = t * NA_UNROLL + u
            r = i * NA_QROWS + a
            r0 = jnp.clip(r - NA_ROWS // 2, 0, n_rows - NA_ROWS)
            koff = pl.multiple_of((r0 - band_row0) * GRID_W, GRID_W)
            dr_first = r0 - r + (NA_ROWS - 1)
            qoff = pl.multiple_of(a * GRID_W, GRID_W)
            for gi in range(NA_HEADS // NA_GROUP):
                items.append((koff, dr_first, qoff, gi, slice(gi * GW, (gi + 1) * GW)))
        scores = []
        for koff, _, qoff, _, gs in items:
            qg = q_ref[0, pl.ds(qoff, GRID_W), gs]
            qm = jnp.concatenate([jnp.where(head_of_lane == g, qg, zero) for g in range(NA_GROUP)], axis=0)
            scores.append((_nt_dot(kband[pl.ds(koff, n_win), gs], qm),
                           _nt_dot(kc_ref[0, :, gs], qm)))
        probs = []
        for (_, dr_first, _, gi, _), (s_loc, s_ctx) in zip(items, scores):
            s_loc = s_loc + bt_ref[dr_first, gi]
            m = jnp.maximum(jnp.max(s_loc, 0, keepdims=True), jnp.max(s_ctx, 0, keepdims=True))
            e_loc = jnp.exp(s_loc - m)
            e_ctx = jnp.exp(s_ctx - m)
            inv = 1.0 / (jnp.sum(e_loc, 0, keepdims=True) + jnp.sum(e_ctx, 0, keepdims=True))
            probs.append(((e_loc * inv).astype(BF16), (e_ctx * inv).astype(BF16)))
        for (koff, _, qoff, _, gs), (p_loc, p_ctx) in zip(items, probs):
            res = (_tn_dot(p_loc, vband[pl.ds(koff, n_win), gs])
                   + _tn_dot(p_ctx, vc_ref[0, :, gs]))
            out = res[0:GRID_W]
            for g in range(1, NA_GROUP):
                out = jnp.where(head_of_lane == g, res[g * GRID_W:(g + 1) * GRID_W], out)
            o_ref[0, pl.ds(qoff, GRID_W), gs] = out.astype(BF16)
        return carry

    lax.fori_loop(0, NA_QROWS // NA_UNROLL, rows_body, 0)


def _na_latent(q, k, v, kc, vc, bias_tab):
    B, T, Wd = q.shape
    Lc = kc.shape[1]
    n_rows = T // GRID_W
    nb = NA_BAND // NA_KBLK_ROWS
    q_tok = NA_QROWS * GRID_W
    blk_tok = NA_KBLK_ROWS * GRID_W
    last_blk0 = (n_rows - NA_BAND) // NA_KBLK_ROWS

    def kv_spec(j):
        def idx(b, i):
            first = jnp.clip(i * (NA_QROWS // NA_KBLK_ROWS) - (NA_ROWS // 2) // NA_KBLK_ROWS, 0, last_blk0)
            return (b, first + j, 0)
        return pl.BlockSpec((1, blk_tok, Wd), idx)

    return pl.pallas_call(
        functools.partial(_na_kernel, n_rows),
        grid=(B, n_rows // NA_QROWS),
        in_specs=([pl.BlockSpec((1, q_tok, Wd), lambda b, i: (b, i, 0))]
                  + [kv_spec(j) for j in range(nb)] + [kv_spec(j) for j in range(nb)]
                  + [pl.BlockSpec((1, Lc, Wd), lambda b, i: (b, 0, 0)),
                     pl.BlockSpec((1, Lc, Wd), lambda b, i: (b, 0, 0)),
                     pl.BlockSpec(bias_tab.shape, lambda b, i: (0, 0, 0, 0))]),
        out_specs=pl.BlockSpec((1, q_tok, Wd), lambda b, i: (b, i, 0)),
        out_shape=jax.ShapeDtypeStruct((B, T, Wd), BF16),
        scratch_shapes=[pltpu.VMEM((NA_BAND * GRID_W, Wd), BF16),
                        pltpu.VMEM((NA_BAND * GRID_W, Wd), BF16)],
        compiler_params=_cparams("arbitrary", "arbitrary"),
        name="natten",
    )(q, *([k] * nb), *([v] * nb), kc, vc, bias_tab)


def _ctx_attn_kernel(q_ref, k_ref, v_ref, o_ref):
    HD = NA_HEAD_DIM
    for h in range(NA_HEADS):
        hs = slice(h * HD, (h + 1) * HD)
        s = _nt_dot(q_ref[0, :, hs], k_ref[0, :, hs])
        o_ref[0, :, hs] = _softmax_pv([s], [v_ref[0, :, hs]]).astype(BF16)


def _ctx_attn(q, k, v):
    B, Lc, Wd = q.shape
    spec = pl.BlockSpec((1, Lc, Wd), lambda b: (b, 0, 0))
    return pl.pallas_call(
        _ctx_attn_kernel, grid=(B,), in_specs=[spec, spec, spec], out_specs=spec,
        out_shape=jax.ShapeDtypeStruct((B, Lc, Wd), BF16),
        compiler_params=_cparams("arbitrary"), name="ctx_attn",
    )(q, k, v)


def _na_bias_table(rpb):
    qc = np.arange(GRID_W)[:, None]
    kcol = np.arange(GRID_W)[None, :]
    wstart = np.clip(qc - NA_COLS // 2, 0, GRID_W - NA_COLS)
    col_ok = (kcol >= wstart) & (kcol < wstart + NA_COLS)
    dc = np.clip(kcol - qc, -(NA_COLS - 1), NA_COLS - 1) + NA_COLS - 1
    per_dr = jnp.where(col_ok[None, None], rpb[:, :, dc], MASK_NEG)
    def tab(d):
        n_grp = NA_HEADS // NA_GROUP
        t = per_dr[:, d:d + NA_ROWS].reshape(n_grp, NA_GROUP, NA_ROWS, GRID_W, GRID_W)
        return jnp.transpose(t, (0, 2, 4, 1, 3)).reshape(n_grp, NA_ROWS * GRID_W, NA_GROUP * GRID_W)

    return jnp.stack([tab(d) for d in range(NA_ROWS)]).astype(F32)


def _route(sel, s):
    E = EXP_PER_GROUP
    scores = []
    for g in range(N_GROUPS):
        a, b, c, d = sel[E * g:E * (g + 1)]
        scores.append(functools.reduce(jnp.maximum, [a + b, a + c, a + d, b + c, b + d, c + d]))
    best = jnp.zeros_like(scores[0], dtype=jnp.int32)
    best_score = scores[0]
    for g in range(1, N_GROUPS):
        upd = scores[g] > best_score
        best = jnp.where(upd, g, best)
        best_score = jnp.where(upd, scores[g], best_score)

    def pick(rows, j):
        out = rows[j]
        for g in range(1, N_GROUPS):
            out = jnp.where(best == g, rows[E * g + j], out)
        return out

    v = [pick(sel, j) for j in range(E)]
    sv = [pick(s, j) for j in range(E)]
    i1 = jnp.zeros_like(best)
    m1 = v[0]
    for j in range(1, E):
        upd = v[j] > m1
        i1 = jnp.where(upd, j, i1)
        m1 = jnp.where(upd, v[j], m1)
    i2 = jnp.where(i1 == 0, 1, 0)
    m2 = jnp.where(i1 == 0, v[1], v[0])
    for j in range(1, E):
        upd = (i1 != j) & (v[j] > m2)
        i2 = jnp.where(upd, j, i2)
        m2 = jnp.where(upd, v[j], m2)

    def at(rows, idx):
        out = rows[0]
        for j in range(1, E):
            out = jnp.where(idx == j, rows[j], out)
        return out

    s1 = at(sv, i1)
    s2 = at(sv, i2)
    tot = s1 + s2
    return best * E + i1, best * E + i2, s1 / tot, s2 / tot


def _out_proj_kernel(ygm_ref, hf_ref, hb_ref, o_ref, yna_ref, x_ref, mod_ref, w_ref, mlg_ref, n2g_ref,
                     rw_ref, rb_ref, xn_ref, h2_ref, re_ref, rwt_ref):
    hs = hf_ref[0] + hb_ref[0]
    hsq = hs * hs
    lane = lax.broadcasted_iota(jnp.int32, (1, ML_WIDTH), 1)
    scale = jnp.zeros_like(hs)
    for h in range(ML_HEADS):
        in_head = (lane >= h * ML_HEAD_DIM) & (lane < (h + 1) * ML_HEAD_DIM)
        ms = jnp.sum(jnp.where(in_head, hsq, 0.0), -1, keepdims=True) * (1.0 / ML_HEAD_DIM)
        scale = jnp.where(in_head, lax.rsqrt(ms + EPS), scale)
    yml = hs * scale * mlg_ref[...] * jax.nn.sigmoid(o_ref[0])
    o1 = GM_WIDTH
    o2 = GM_WIDTH + ML_WIDTH
    acc = (jnp.dot(ygm_ref[0], w_ref[0:o1, :], preferred_element_type=F32)
           + jnp.dot(yml.astype(BF16), w_ref[o1:o2, :], preferred_element_type=F32)
           + jnp.dot(yna_ref[0], w_ref[o2:, :], preferred_element_type=F32))
    xn = x_ref[0] + mod_ref[0, 0:1, :] * acc
    xn_ref[0] = xn
    h2 = xn * lax.rsqrt(jnp.mean(xn * xn, -1, keepdims=True) + EPS) * n2g_ref[...]
    h2 = h2 * (1.0 + mod_ref[0, 2:3, :]) + mod_ref[0, 1:2, :]
    h2_ref[0] = h2
    logits = _nt_dot(rw_ref[...], h2, precision=lax.Precision.HIGHEST)
    s = jax.nn.sigmoid(logits)
    sel = s + rb_ref[...]
    rows = lambda m: [m[e:e + 1, :] for e in range(N_EXPERTS)]
    e1, e2, w1, w2 = _route(rows(sel), rows(s))
    re_ref[0, 0:1, :] = e1
    re_ref[0, 1:2, :] = e2
    rwt_ref[0, 0:1, :] = w1
    rwt_ref[0, 1:2, :] = w2


def _out_proj(ygm, hf, hb, ml, yna, x, mod, w_out, ml_g, n2_g, router_wt, router_b):
    G, R, D = x.shape
    tm = min(R, 512)
    row = lambda b, i: (b, i, 0)
    whole2 = lambda shape: pl.BlockSpec(shape, lambda b, i: (0, 0))
    return pl.pallas_call(
        _out_proj_kernel,
        grid=(G, R // tm),
        in_specs=[pl.BlockSpec((1, tm, GM_WIDTH), row),
                  pl.BlockSpec((1, tm, ML_WIDTH), row),
                  pl.BlockSpec((1, tm, ML_WIDTH), row),
                  pl.BlockSpec((1, tm, ML_WIDTH), lambda b, i: (b, i, 3)),
                  pl.BlockSpec((1, tm, NA_WIDTH), row),
                  pl.BlockSpec((1, tm, D), row),
                  pl.BlockSpec((1, 3, D), lambda b, i: (b, 0, 0)),
                  whole2((D, D)), whole2((1, ML_WIDTH)), whole2((1, D)),
                  whole2((N_EXPERTS, D)), whole2((N_EXPERTS, 1))],
        out_specs=[pl.BlockSpec((1, tm, D), row), pl.BlockSpec((1, tm, D), row),
                   pl.BlockSpec((1, 2, tm), lambda b, i: (b, 0, i)),
                   pl.BlockSpec((1, 2, tm), lambda b, i: (b, 0, i))],
        out_shape=[jax.ShapeDtypeStruct((G, R, D), F32), jax.ShapeDtypeStruct((G, R, D), F32),
                   jax.ShapeDtypeStruct((G, 2, R), jnp.int32), jax.ShapeDtypeStruct((G, 2, R), F32)],
        compiler_params=_cparams("arbitrary", "arbitrary"),
        name="out_proj",
    )(ygm, hf, hb, ml, yna, x, mod, w_out, ml_g, n2_g, router_wt, router_b)


def _experts_kernel(te_ref, nu_ref, src_cur, src_nxt, h_hbm, wg_ref, wu_ref, wd_ref, o_ref,
                    acc_ref, wgb, wub, wdb, xbuf, sem):
    i = pl.program_id(0)
    n_used = nu_ref[0]
    used = i < n_used
    slot = lax.rem(i, 2)
    new_expert = (i == 0) | (te_ref[i] != te_ref[jnp.maximum(i - 1, 0)])

    def row_copy(src_ref, s, r):
        return pltpu.make_async_copy(h_hbm.at[pl.ds(src_ref[0, 0, r], 1), :], xbuf.at[s, pl.ds(r, 1), :],
                                     sem.at[s])

    def tile_wait(s):
        pltpu.make_async_copy(xbuf.at[s], xbuf.at[s], sem.at[s]).wait()

    @pl.when((i == 0) & used)
    def _():
        def body(r, carry):
            row_copy(src_cur, 0, r).start()
            return carry
        lax.fori_loop(0, MOE_TILE, body, 0, unroll=8)

    @pl.when(used & new_expert)
    def _():
        def cast_rows(c, carry):
            rs = pl.ds(pl.multiple_of(c * LANE, LANE), LANE)
            wgb[rs, :] = wg_ref[0, 0, rs, :].astype(BF16)
            wub[rs, :] = wu_ref[0, 0, rs, :].astype(BF16)
            wdb[rs, :] = wd_ref[0, 0, rs, :].astype(BF16)
            return carry
        lax.fori_loop(0, D_MODEL // LANE, cast_rows, 0)

    @pl.when(used)
    def _():
        tile_wait(slot)
        x = xbuf[slot].astype(BF16)
        nxt = 1 - slot
        n_chunks = D_EXPERT // EXPERT_CHUNK
        per = -(-MOE_TILE // (3 * n_chunks))
        issued = [0]

        def request_rows():
            for r in range(issued[0], min(issued[0] + per, MOE_TILE)):
                row_copy(src_nxt, nxt, r).start(priority=r % 2)
            issued[0] += per

        for j in range(n_chunks):
            cs = slice(j * EXPERT_CHUNK, (j + 1) * EXPERT_CHUNK)
            request_rows()
            g = jnp.dot(x, wgb[:, cs], preferred_element_type=F32)
            request_rows()
            u = jnp.dot(x, wub[:, cs], preferred_element_type=F32)
            a = (g * jax.nn.sigmoid(g) * u).astype(BF16)
            request_rows()
            y = jnp.dot(a, wdb[cs, :], preferred_element_type=F32)
            if j == 0:
                acc_ref[...] = y
            else:
                acc_ref[...] += y
        o_ref[...] = acc_ref[...].astype(BF16)

        @pl.when(i + 1 >= n_used)
        def _():
            tile_wait(nxt)

    @pl.when(jnp.logical_not(used))
    def _():
        o_ref[...] = jnp.zeros_like(o_ref)


def _experts(l, tile_expert, n_used, src, h_all, wg, wu, wd):
    D = h_all.shape[1]
    n_tiles = src.shape[0] // MOE_TILE
    src3 = src.reshape(n_tiles, 1, MOE_TILE)
    wspec = lambda r, c: pl.BlockSpec((1, 1, r, c), lambda i, te, nu: (l, te[i], 0, 0))
    src_spec = lambda ahead: pl.BlockSpec(
        (1, 1, MOE_TILE), lambda i, te, nu: (jnp.minimum(i + ahead, n_tiles - 1), 0, 0),
        memory_space=pltpu.SMEM)
    return pl.pallas_call(
        _experts_kernel,
        grid_spec=pltpu.PrefetchScalarGridSpec(
            num_scalar_prefetch=2,
            grid=(n_tiles,),
            in_specs=[src_spec(0), src_spec(1), pl.BlockSpec(memory_space=pl.ANY),
                      wspec(D, D_EXPERT), wspec(D, D_EXPERT), wspec(D_EXPERT, D)],
            out_specs=pl.BlockSpec((MOE_TILE, D), lambda i, te, nu: (i, 0)),
            scratch_shapes=[pltpu.VMEM((MOE_TILE, D), F32), pltpu.VMEM((D, D_EXPERT), BF16),
                            pltpu.VMEM((D, D_EXPERT), BF16), pltpu.VMEM((D_EXPERT, D), BF16),
                            pltpu.VMEM((2, MOE_TILE, D), F32), pltpu.SemaphoreType.DMA((2,))]),
        out_shape=jax.ShapeDtypeStruct((n_tiles * MOE_TILE, D), BF16),
        compiler_params=_cparams("arbitrary"),
        name="experts",
    )(tile_expert, n_used, src3, src3, h_all, wg, wu, wd)


def _dispatch(e_idx):
    n_tok = e_idx.shape[0]
    n_asg = 2 * n_tok
    n_tiles = -(-n_asg // MOE_TILE) + N_EXPERTS
    flat_e = e_idx.reshape(-1)
    onehot = (flat_e[:, None] == jnp.arange(N_EXPERTS, dtype=jnp.int32)[None, :]).astype(jnp.int32)
    csum = jnp.cumsum(onehot, axis=0)
    rank = jnp.take_along_axis(csum, flat_e[:, None], axis=1)[:, 0] - 1
    counts = csum[-1]
    tiles_e = (counts + MOE_TILE - 1) // MOE_TILE
    tile_end = jnp.cumsum(tiles_e)
    tile_start = tile_end - tiles_e
    dest = tile_start[flat_e] * MOE_TILE + rank
    n_used = tile_end[-1]
    tid = jnp.arange(n_tiles, dtype=jnp.int32)
    te = jnp.minimum(jnp.searchsorted(tile_end, tid, side="right"), N_EXPERTS - 1).astype(jnp.int32)
    te = jnp.where(tid < n_used, te, te[jnp.maximum(n_used - 1, 0)])
    flat_t = jnp.arange(n_asg, dtype=jnp.int32) // 2
    src = jnp.zeros((n_tiles * MOE_TILE,), jnp.int32).at[dest].set(
        flat_t, unique_indices=True, mode="promise_in_bounds")
    return dest.reshape(n_tok, 2), src, te, n_used.reshape(1).astype(jnp.int32)


def _combine_kernel(final, xn_ref, y0_ref, y1_ref, w_ref, g2_ref, fg_ref, o_ref):
    w = w_ref[0]
    f = y0_ref[0].astype(F32) * w[:, 0:1] + y1_ref[0].astype(F32) * w[:, 1:2]
    x = xn_ref[0] + g2_ref[0] * f
    if final:
        x = x * lax.rsqrt(jnp.mean(x * x, -1, keepdims=True) + EPS) * fg_ref[...]
    o_ref[0] = x


def _combine(xn, y0, y1, wt, g2, final_g, final):
    G, R, D = xn.shape
    tm = min(R, 512)
    row = lambda b, i: (b, i, 0)
    return pl.pallas_call(
        functools.partial(_combine_kernel, final),
        grid=(G, R // tm),
        in_specs=[pl.BlockSpec((1, tm, D), row), pl.BlockSpec((1, tm, D), row), pl.BlockSpec((1, tm, D), row),
                  pl.BlockSpec((1, tm, 2), row),
                  pl.BlockSpec((1, 1, D), lambda b, i: (b, 0, 0)),
                  pl.BlockSpec((1, D), lambda b, i: (0, 0))],
        out_specs=pl.BlockSpec((1, tm, D), row),
        out_shape=jax.ShapeDtypeStruct((G, R, D), F32),
        compiler_params=_cparams("arbitrary", "arbitrary"),
        name="combine_final" if final else "combine",
    )(xn, y0, y1, wt, g2, final_g)


def _rope_tables(T):
    lane = np.arange(LANE)
    half = ML_HEAD_DIM // 4
    inv = jnp.tile(ROPE_BASE ** (-jnp.arange(half, dtype=F32) / half), LANE // half)
    t = jnp.arange(T)
    pos = jnp.where(((lane // (2 * half)) % 2 == 0)[None, :], (t // GRID_W)[:, None], (t % GRID_W)[:, None])
    ang = pos.astype(F32) * inv[None, :]
    sign = np.where((lane // half) % 2 == 0, -1.0, 1.0).astype(np.float32)
    return jnp.cos(ang), jnp.sin(ang) * sign[None, :]


def _reorder_w_in(w_in):
    pad = jnp.zeros(w_in.shape[:2] + (LANE - 4 * ML_HEADS,), w_in.dtype)
    return jnp.concatenate([w_in[..., :OFF_GATES], w_in[..., OFF_NA:], w_in[..., OFF_GATES:OFF_NA], pad],
                           axis=-1).astype(BF16)


def _mixers(l, p, pc, prm, need_ctx):
    ygm, ml, gt, q, k, v = p
    ygmc, mlc, gtc, qc, kc, vc = pc
    B = ml.shape[0]
    conv = (prm["conv_w"][l], prm["conv_b"][l])
    nh = ML_HEADS
    tr = lambda g: jnp.transpose(
        jnp.concatenate([g[..., 0:nh], g[..., 2 * nh:3 * nh], g[..., nh:2 * nh], g[..., 3 * nh:4 * nh]], -1),
        (0, 2, 1))
    m_zero = jnp.zeros((B, 2 * nh, LANE), F32)
    s_zero = jnp.zeros((B, nh, ML_CHUNK, 2 * LANE), F32)
    qmc, kmc, grc = _mlstm_prep(mlc, prm["cos_c"], prm["sin_c"], *conv, tr(gtc), prm["gbias"][l])
    hfc, hbc, m_st, s_st = _mlstm(qmc, kmc, mlc, grc, m_zero, s_zero)
    qm, km, gr = _mlstm_prep(ml, prm["cos_l"], prm["sin_l"], *conv, tr(gt), prm["gbias"][l])
    hf, hb, _, _ = _mlstm(qm, km, ml, gr, m_st, s_st)
    y_na = _na_latent(q, k, v, kc, vc, prm["na_tab"][l])
    y = (ygm, hf, hb, ml, y_na)
    if not need_ctx:
        return y, None
    return y, (ygmc, hfc, hbc, mlc, _ctx_attn(qc, kc, vc))


def _rows(a, idx):
    return a.at[idx].get(mode="promise_in_bounds")


def _moe(l, h2_all, e_all, prm):
    dest, src, te, n_used = _dispatch(e_all)
    yb = _experts(l, te, n_used, src, h2_all, prm["wg"], prm["wu"], prm["wd"])
    return yb, dest


def kernel(x, c, ctx, c_ctx, ada_w, ada_b, norm1_g, norm2_g, w_in, w_out, gm_ws, gm_bs, gm_norm_g,
           ml_conv_w, ml_conv_b, ml_gate_b, ml_norm_g, na_rpb, router_w, router_b,
           moe_w_gate, moe_w_up, moe_w_down, final_g):
    B, T, D = x.shape
    Lc = ctx.shape[1]
    cos_l, sin_l = _rope_tables(T)
    prm = dict(
        gm_ws=gm_ws.astype(BF16),
        gm_bs_full=jnp.repeat(jnp.transpose(gm_bs, (0, 2, 1)), GM_HEAD_DIM, axis=-1),
        gm_g=gm_norm_g[:, None, :],
        conv_w=ml_conv_w, conv_b=ml_conv_b[:, None, :],
        gbias=ml_gate_b[:, jnp.array([0, 2, 1, 3])].reshape(DEPTH, 4 * ML_HEADS, 1),
        cos_l=cos_l, sin_l=sin_l,
        cos_c=jnp.ones((Lc, LANE), F32), sin_c=jnp.zeros((Lc, LANE), F32),
        na_tab=jnp.stack([_na_bias_table(na_rpb[l]) for l in range(DEPTH)]),
        wg=moe_w_gate, wu=moe_w_up, wd=moe_w_down,
    )
    w_in_r = _reorder_w_in(w_in)
    w_out_b = w_out.astype(BF16)
    router_wt = jnp.transpose(router_w)
    router_bc = router_b[:, None]

    cs = jnp.concatenate([c, c_ctx[None, :], jnp.zeros((8 - B - 1, D), F32)], axis=0)
    mods = _ada_all(cs, ada_w, ada_b).reshape(DEPTH, 8, 6, D)

    xc = ctx
    for l in range(DEPTH):
        need_ctx = l < DEPTH - 1
        mod_l = mods[l, :B]
        mod_c = jnp.broadcast_to(mods[l, B:B + 1], (B, 6, D))
        n1 = norm1_g[l][None, :]
        gm_args = (prm["gm_ws"][l], prm["gm_bs_full"][l], prm["gm_g"][l])
        p = _in_proj(x, mod_l[:, 0:2], n1, w_in_r[l], *gm_args)
        pc = _in_proj(xc, mod_c[:, 0:2], n1, w_in_r[l], *gm_args)
        y, yc = _mixers(l, p, pc, prm, need_ctx)
        op_args = (w_out_b[l], ml_norm_g[l][None, :], norm2_g[l][None, :], router_wt, router_bc)
        xn, h2, re, rw = _out_proj(*y, x, mod_l[:, 2:5], *op_args)
        h2_all = h2.reshape(B * T, D)
        e_all = jnp.transpose(re, (0, 2, 1)).reshape(B * T, 2)
        if need_ctx:
            xnc, h2c, rec, rwc = _out_proj(*yc, xc, mod_c[:, 2:5], *op_args)
            h2_all = jnp.concatenate([h2_all, h2c.reshape(B * Lc, D)], axis=0)
            e_all = jnp.concatenate([e_all, jnp.transpose(rec, (0, 2, 1)).reshape(B * Lc, 2)], axis=0)
        yb, dest = _moe(l, h2_all, e_all, prm)
        final = l == DEPTH - 1
        fg = final_g[None, :]
        n_lat = B * T
        picked = lambda lo, hi, k, L: _rows(yb, dest[lo:hi, k]).reshape(B, L, D)
        x = _combine(xn, picked(0, n_lat, 0, T), picked(0, n_lat, 1, T),
                     jnp.transpose(rw, (0, 2, 1)), mod_l[:, 5:6], fg, final)
        if need_ctx:
            xc = _combine(xnc, picked(n_lat, None, 0, Lc), picked(n_lat, None, 1, Lc),
                          jnp.transpose(rwc, (0, 2, 1)), mod_c[:, 5:6], fg, False)
    return x
```

```python
import functools

import jax
import jax.numpy as jnp
import numpy as np
from jax import lax
from jax.experimental import pallas as pl
from jax.experimental.pallas import tpu as pltpu

F32 = jnp.float32
BF16 = jnp.bfloat16

D_MODEL = 1024
DEPTH = 4
GRID_W = 64
EPS = 1e-6

GM_HEADS = 4
GM_WIDTH = D_MODEL // 4
GM_HEAD_DIM = GM_WIDTH // GM_HEADS
GM_CHUNK = 128

ML_HEADS = 4
ML_WIDTH = D_MODEL // 4
ML_HEAD_DIM = ML_WIDTH // ML_HEADS
ML_CHUNK = 128

NA_HEADS = 8
NA_WIDTH = D_MODEL // 2
NA_HEAD_DIM = NA_WIDTH // NA_HEADS
NA_ROWS = 8
NA_COLS = 16

ROPE_BASE = 10000.0

OFF_ML = 2 * GM_WIDTH
OFF_GATES = OFF_ML + 4 * ML_WIDTH
OFF_NA = OFF_GATES + 4 * ML_HEADS
N_IN = OFF_NA + 3 * NA_WIDTH

N_EXPERTS = 16
N_GROUPS = 4
EXP_PER_GROUP = N_EXPERTS // N_GROUPS
D_EXPERT = D_MODEL

LANE = 128
SEG_GM = (0, 2 * GM_WIDTH)
SEG_ML = (SEG_GM[1], SEG_GM[1] + 4 * ML_WIDTH)
SEG_Q = (SEG_ML[1], SEG_ML[1] + NA_WIDTH)
SEG_K = (SEG_Q[1], SEG_Q[1] + NA_WIDTH)
SEG_V = (SEG_K[1], SEG_K[1] + NA_WIDTH)
SEG_GATES = (SEG_V[1], SEG_V[1] + LANE)
N_IN_PAD = SEG_GATES[1]

NA_QROWS = 8
NA_GROUP = 4
NA_UNROLL = 4
NA_BAND = 16
NA_KBLK_ROWS = 4
MOE_TILE = 512
EXPERT_CHUNK = 256
MASK_NEG = -1e30
N_GATE_ROWS = 6
VMEM_LIMIT = 56 * 2 ** 20


def _cparams(*sem):
    return pltpu.CompilerParams(dimension_semantics=sem, vmem_limit_bytes=VMEM_LIMIT)


def _nt_dot(a, b, precision=None):
    return lax.dot_general(a, b, (((1,), (1,)), ((), ())), precision=precision,
                           preferred_element_type=F32)


def _tn_dot(a, b):
    return lax.dot_general(a, b, (((0,), (0,)), ((), ())), preferred_element_type=F32)


def _ada_kernel(c_ref, w_ref, b_ref, o_ref):
    c = c_ref[...]
    sc = c * jax.nn.sigmoid(c)
    o_ref[0] = jnp.dot(sc, w_ref[0], preferred_element_type=F32) + b_ref[0]


def _ada_all(cs, ada_w, ada_b):
    n_out = ada_w.shape[-1] // D_MODEL
    return pl.pallas_call(
        _ada_kernel,
        grid=(DEPTH, n_out),
        in_specs=[pl.BlockSpec((8, D_MODEL), lambda l, j: (0, 0)),
                  pl.BlockSpec((1, D_MODEL, D_MODEL), lambda l, j: (l, 0, j)),
                  pl.BlockSpec((1, 1, D_MODEL), lambda l, j: (l, 0, j))],
        out_specs=pl.BlockSpec((1, 8, D_MODEL), lambda l, j: (l, 0, j)),
        out_shape=jax.ShapeDtypeStruct((DEPTH, 8, n_out * D_MODEL), F32),
        compiler_params=_cparams("arbitrary", "arbitrary"),
        name="adaln",
    )(cs, ada_w, ada_b.reshape(DEPTH, 1, -1))


def _gmlp_chunk(z_pre, ws_ref, bs, g):
    lane = lax.broadcasted_iota(jnp.int32, (1, GM_WIDTH), 1)
    z = jax.nn.gelu(z_pre)
    u = z[:, :GM_WIDTH]
    v = z[:, GM_WIDTH:]
    mu = jnp.mean(v, -1, keepdims=True)
    vc = v - mu
    var = jnp.mean(vc * vc, -1, keepdims=True)
    vn = (vc * lax.rsqrt(var + EPS) * g).astype(BF16)
    sv = jnp.zeros((GM_CHUNK, GM_WIDTH), F32)
    for h in range(GM_HEADS):
        full = jnp.dot(ws_ref[h], vn, preferred_element_type=F32)
        in_head = (lane >= h * GM_HEAD_DIM) & (lane < (h + 1) * GM_HEAD_DIM)
        sv = jnp.where(in_head, full, sv)
    return u * (sv + bs)


def _in_proj_kernel(x_ref, mod_ref, g_ref, w_ref, ws_ref, bs_ref, gmg_ref,
                    ygm_ref, ml_ref, gt_ref, q_ref, k_ref, v_ref):
    x = x_ref[0]
    h = x * lax.rsqrt(jnp.mean(x * x, -1, keepdims=True) + EPS) * g_ref[...]
    h = h * (1.0 + mod_ref[0, 1:2, :]) + mod_ref[0, 0:1, :]
    hb = h.astype(BF16)

    def seg(s):
        return jnp.dot(hb, w_ref[:, s[0]:s[1]], preferred_element_type=F32)

    gm = seg(SEG_GM)
    ml_ref[0] = seg(SEG_ML)
    for c in range(gm.shape[0] // GM_CHUNK):
        rows = slice(c * GM_CHUNK, (c + 1) * GM_CHUNK)
        ygm_ref[0, rows, :] = _gmlp_chunk(gm[rows], ws_ref, bs_ref[...], gmg_ref[...]).astype(BF16)
    gt_ref[0] = seg(SEG_GATES)
    q_ref[0] = (seg(SEG_Q) * (NA_HEAD_DIM ** -0.5)).astype(BF16)
    k_ref[0] = seg(SEG_K).astype(BF16)
    v_ref[0] = seg(SEG_V).astype(BF16)


def _in_proj(x, mod, g, w, gm_ws, gm_bs_full, gm_g):
    G, R, D = x.shape
    tm = min(R, 512)
    row = lambda b, i: (b, i, 0)
    whole = lambda shape: pl.BlockSpec(shape, lambda b, i: (0,) * len(shape))
    widths = (GM_WIDTH, SEG_ML[1] - SEG_ML[0], LANE, NA_WIDTH, NA_WIDTH, NA_WIDTH)
    dtypes = (BF16, F32, F32, BF16, BF16, BF16)
    return pl.pallas_call(
        _in_proj_kernel,
        grid=(G, R // tm),
        in_specs=[pl.BlockSpec((1, tm, D), row),
                  pl.BlockSpec((1, 2, D), lambda b, i: (b, 0, 0)),
                  whole((1, D)), whole((D, N_IN_PAD)),
                  whole((GM_HEADS, GM_CHUNK, GM_CHUNK)), whole((GM_CHUNK, GM_WIDTH)), whole((1, GM_WIDTH))],
        out_specs=[pl.BlockSpec((1, tm, n), row) for n in widths],
        out_shape=[jax.ShapeDtypeStruct((G, R, n), dt) for n, dt in zip(widths, dtypes)],
        compiler_params=_cparams("arbitrary", "arbitrary"),
        name="in_proj",
    )(x, mod, g, w, gm_ws, gm_bs_full, gm_g)


def _log_sigmoid(x):
    return jnp.minimum(x, 0.0) - jnp.log1p(jnp.exp(-jnp.abs(x)))


def _rope(x, cos, sin_signed):
    lane = lax.broadcasted_iota(jnp.int32, (1, LANE), 1)
    first_half = (lane & 16) == 0
    partner = jnp.where(first_half, pltpu.roll(x, LANE - 16, 1), pltpu.roll(x, 16, 1))
    return x * cos + partner * sin_signed


def _chunk_scan(x, op, fill, reverse):
    pos = lax.broadcasted_iota(jnp.int32, (1, x.shape[1]), 1) % ML_CHUNK
    n = x.shape[1]
    sh = 1
    while sh < ML_CHUNK:
        if reverse:
            shifted = jnp.where(pos < ML_CHUNK - sh, pltpu.roll(x, n - sh, 1), fill)
        else:
            shifted = jnp.where(pos >= sh, pltpu.roll(x, sh, 1), fill)
        x = op(x, shifted)
        sh *= 2
    return x


def _mlstm_prep_kernel(n_blocks, x_ref, hp_ref, hn_ref, cos_ref, sin_ref, cw_ref, cb_ref, g_ref, gbias_ref,
                       q_ref, k_ref, go_ref):
    nh2 = 2 * ML_HEADS
    g = g_ref[0] + gbias_ref[...]
    is_fwd = lax.broadcasted_iota(jnp.int32, (nh2, 1), 0) < ML_HEADS
    li = g[0:nh2]
    lf = _log_sigmoid(g[nh2:])
    ps = _chunk_scan(lf, jnp.add, 0.0, False)
    ss = _chunk_scan(lf, jnp.add, 0.0, True)
    bcum = jnp.where(is_fwd, ps, ss)
    r = li - bcum
    pm = _chunk_scan(r, jnp.maximum, MASK_NEG, False)
    sm = _chunk_scan(r, jnp.maximum, MASK_NEG, True)
    rmax = jnp.maximum(pm, sm)
    groups = (bcum, r, jnp.where(is_fwd, pm, sm), jnp.exp(r - rmax), rmax, ps + ss - lf)
    for j, val in enumerate(groups):
        go_ref[0, nh2 * j:nh2 * (j + 1), :] = val

    i = pl.program_id(1)
    W = ML_WIDTH
    x = x_ref[0]
    tr = x.shape[0]
    rid = lax.broadcasted_iota(jnp.int32, (tr, 1), 0)
    i_row = jnp.zeros((1, 2 * W), jnp.int32) + i
    prev = jnp.where(i_row == 0, 0.0, hp_ref[0, 7:8, :])
    nxt = jnp.where(i_row == n_blocks - 1, 0.0, hn_ref[0, 0:1, :])
    xm1 = jnp.where(rid == 0, prev, pltpu.roll(x, 1, 0))
    xp1 = jnp.where(rid == tr - 1, nxt, pltpu.roll(x, tr - 1, 0))
    y = xm1 * cw_ref[0:1, :] + x * cw_ref[1:2, :] + xp1 * cw_ref[2:3, :] + cb_ref[...]
    y = y * jax.nn.sigmoid(y)
    cos = cos_ref[...]
    sin = sin_ref[...]
    parts = [_rope(y[:, j * LANE:(j + 1) * LANE], cos, sin) for j in range(2 * W // LANE)]
    q_ref[0] = (jnp.concatenate(parts[:W // LANE], axis=1) * (ML_HEAD_DIM ** -0.5)).astype(BF16)
    for j in range(W // LANE):
        k_ref[0, j * LANE:(j + 1) * LANE, :] = parts[W // LANE + j].T.astype(BF16)


def _mlstm_prep(ml, cos, sin, conv_w, conv_b, gates_t, gbias):
    B, T, _ = ml.shape
    n_g = gates_t.shape[1]
    W = ML_WIDTH
    tr = min(T, 1024)
    nb = T // tr
    halo_per_blk = tr // 8
    n_hblk = T // 8
    whole = lambda shape: pl.BlockSpec(shape, lambda b, i: (0, 0))
    return pl.pallas_call(
        functools.partial(_mlstm_prep_kernel, nb),
        grid=(B, nb),
        in_specs=[pl.BlockSpec((1, tr, 2 * W), lambda b, i: (b, i, 0)),
                  pl.BlockSpec((1, 8, 2 * W), lambda b, i: (b, jnp.maximum(i * halo_per_blk - 1, 0), 0)),
                  pl.BlockSpec((1, 8, 2 * W),
                               lambda b, i: (b, jnp.minimum((i + 1) * halo_per_blk, n_hblk - 1), 0)),
                  pl.BlockSpec((tr, LANE), lambda b, i: (i, 0)),
                  pl.BlockSpec((tr, LANE), lambda b, i: (i, 0)),
                  whole((3, 2 * W)), whole((1, 2 * W)),
                  pl.BlockSpec((1, n_g, tr), lambda b, i: (b, 0, i)), whole((n_g, 1))],
        out_specs=[pl.BlockSpec((1, tr, W), lambda b, i: (b, i, 0)),
                   pl.BlockSpec((1, W, tr), lambda b, i: (b, 0, i)),
                   pl.BlockSpec((1, 2 * ML_HEADS * N_GATE_ROWS, tr), lambda b, i: (b, 0, i))],
        out_shape=[jax.ShapeDtypeStruct((B, T, W), BF16), jax.ShapeDtypeStruct((B, W, T), BF16),
                   jax.ShapeDtypeStruct((B, 2 * ML_HEADS * N_GATE_ROWS, T), F32)],
        compiler_params=_cparams("arbitrary", "arbitrary"),
        name="mlstm_prep",
    )(ml, ml, ml, cos, sin, conv_w, conv_b, gates_t, gbias)


def _mlstm_kernel(n_chunks, qf_ref, ktf_ref, vf_ref, gf_ref, qb_ref, ktb_ref, vb_ref, gb_ref,
                  m0_ref, s0_ref, hf_ref, hb_ref, mfin_ref, sfin_ref, m_scr, s_scr, bd_scr):
    c = pl.program_id(0)
    L = ML_CHUNK
    HD = ML_HEAD_DIM
    NH = ML_HEADS
    n_batch = m_scr.shape[0]

    @pl.when(c == 0)
    def _():
        m_scr[...] = m0_ref[...]
        s_scr[...] = s0_ref[...]
        bd_scr[...] = jnp.zeros_like(bd_scr)

    def row_bcast(x, j):
        return jnp.broadcast_to(x[j:j + 1, :], (L, LANE))

    row = lax.broadcasted_iota(jnp.int32, (L, L), 0)
    col = lax.broadcasted_iota(jnp.int32, (L, L), 1)
    low_half = lax.broadcasted_iota(jnp.int32, (1, LANE), 1) < HD
    own_rows = jnp.concatenate([row < HD, row >= HD], axis=1)
    is_fwd = lax.broadcasted_iota(jnp.int32, (2 * NH, 1), 0) < NH

    gate = []
    for b in range(n_batch):
        gq = lambda i: jnp.where(is_fwd, gf_ref[b, 2 * NH * i:2 * NH * (i + 1), :],
                                 gb_ref[b, 2 * NH * i:2 * NH * (i + 1), :])
        bcum, r, rcmax, ew, rmax, b_last = (gq(i) for i in range(N_GATE_ROWS))
        m0 = m_scr[b]
        mu = jnp.maximum(m0, rcmax)
        emt = jnp.exp(-(bcum + mu))
        m_last = jnp.maximum(m0, rmax)
        m_scr[b] = b_last + m_last
        stack = jnp.concatenate([mu, emt, jnp.zeros((L - 4 * NH, LANE), F32)], axis=0)
        gate.append(dict(r=r, ew=ew, m0=m0, a_st=jnp.exp(m0 - m_last), g_st=jnp.exp(rmax - m_last),
                         cols=stack.T))

    dirs = ((qf_ref, ktf_ref, vf_ref, hf_ref), (qb_ref, ktb_ref, vb_ref, hb_ref))
    items = [(b, d, pr) for b in range(n_batch) for d in range(2) for pr in range(NH // 2)]

    pre = []
    for b, d, pr in items:
        q_ref, kt_ref, v_ref, _ = dirs[d]
        ps = slice(pr * LANE, (pr + 1) * LANE)
        ja = NH * d + 2 * pr
        g = gate[b]
        qp = q_ref[b, :, ps]
        kt = kt_ref[b, ps, :]
        vp = v_ref[b, :, ps]
        st = s_scr[b, 2 * d + pr]
        ktf = kt.astype(F32)
        kt2 = jnp.concatenate([jnp.where(row < HD, ktf, 0.0), jnp.where(row < HD, 0.0, ktf)],
                              axis=1).astype(BF16)
        s2 = jnp.dot(qp, kt2, preferred_element_type=F32)
        inter = jnp.dot(qp, st.astype(BF16), preferred_element_type=F32)
        vab = jnp.concatenate([jnp.where(low_half, vp, 1.0), jnp.where(low_half, 1.0, vp)],
                              axis=1).astype(BF16)
        ktw = ktf * jnp.where(row < HD, row_bcast(g["ew"], ja), row_bcast(g["ew"], ja + 1))
        upd = jnp.dot(ktw.astype(BF16), vab, preferred_element_type=F32)
        pre.append((s2, inter, vab, upd, st))

    probs = []
    for (b, d, pr), (s2, _, _, _, _) in zip(items, pre):
        ja = NH * d + 2 * pr
        g = gate[b]
        valid = (col <= row) if d == 0 else (col >= row)
        mu_c = [jnp.broadcast_to(g["cols"][:, j:j + 1], (L, LANE)) for j in (ja, ja + 1)]
        e2 = jnp.concatenate([jnp.where(valid, jnp.exp(row_bcast(g["r"], j) - mc), 0.0)
                              for j, mc in zip((ja, ja + 1), mu_c)], axis=1)
        a2 = jnp.concatenate([jnp.exp(row_bcast(g["m0"], j) - mc) for j, mc in zip((ja, ja + 1), mu_c)],
                             axis=1)
        probs.append(((s2 * e2).astype(BF16), a2))

    for it, ((b, d, pr), (_, inter, vab, upd, st), (p2, a2)) in enumerate(zip(items, pre, probs)):
        out_ref = dirs[d][3]
        ps = slice(pr * LANE, (pr + 1) * LANE)
        ja = NH * d + 2 * pr
        g = gate[b]
        bd_scr[it, 0:L, 0:LANE] = vab[:, 0:LANE]
        bd_scr[it, L:2 * L, LANE:2 * LANE] = vab[:, LANE:2 * LANE]
        intra = jnp.dot(p2, bd_scr[it], preferred_element_type=F32)
        tot = intra + a2 * inter
        tot_a = tot[:, 0:LANE]
        tot_b = tot[:, LANE:2 * LANE]
        num = jnp.where(low_half, tot_a, tot_b)
        den = pltpu.roll(jnp.where(low_half, tot_b, tot_a), HD, 1)
        floor = jnp.where(low_half, jnp.broadcast_to(g["cols"][:, 8 + ja:8 + ja + 1], (L, LANE)),
                          jnp.broadcast_to(g["cols"][:, 8 + ja + 1:8 + ja + 2], (L, LANE)))
        out_ref[b, :, ps] = num / jnp.maximum(jnp.abs(den), floor)
        a_s = jnp.concatenate([row_bcast(g["a_st"], ja), row_bcast(g["a_st"], ja + 1)], axis=1)
        g_s = jnp.concatenate([row_bcast(g["g_st"], ja), row_bcast(g["g_st"], ja + 1)], axis=1)
        s_scr[b, 2 * d + pr] = a_s * st + g_s * jnp.where(own_rows, upd, 0.0)

    @pl.when(c == n_chunks - 1)
    def _():
        mfin_ref[...] = m_scr[...]
        sfin_ref[...] = s_scr[...]


def _mlstm(q, kt, ml, grows, m0, s0):
    B, T, W = q.shape
    L = ML_CHUNK
    nc = T // L

    def specs(ci):
        return [pl.BlockSpec((B, L, W), lambda c: (0, ci(c), 0)),
                pl.BlockSpec((B, W, L), lambda c: (0, 0, ci(c))),
                pl.BlockSpec((B, L, W), lambda c: (0, ci(c), 2)),
                pl.BlockSpec((B, grows.shape[1], L), lambda c: (0, 0, ci(c)))]

    m_spec = pl.BlockSpec(m0.shape, lambda c: (0, 0, 0))
    s_spec = pl.BlockSpec(s0.shape, lambda c: (0, 0, 0, 0))
    dir_args = [q, kt, ml, grows]
    return pl.pallas_call(
        functools.partial(_mlstm_kernel, nc),
        grid=(nc,),
        in_specs=specs(lambda c: c) + specs(lambda c: nc - 1 - c) + [m_spec, s_spec],
        out_specs=[pl.BlockSpec((B, L, W), lambda c: (0, c, 0)),
                   pl.BlockSpec((B, L, W), lambda c: (0, nc - 1 - c, 0)),
                   m_spec, s_spec],
        out_shape=[jax.ShapeDtypeStruct((B, T, W), F32),
                   jax.ShapeDtypeStruct((B, T, W), F32),
                   jax.ShapeDtypeStruct(m0.shape, F32),
                   jax.ShapeDtypeStruct(s0.shape, F32)],
        scratch_shapes=[pltpu.VMEM(m0.shape, F32), pltpu.VMEM(s0.shape, F32),
                        pltpu.VMEM((B * ML_HEADS, 2 * L, 2 * LANE), BF16)],
        compiler_params=_cparams("arbitrary"),
        name="mlstm",
    )(*dir_args, *dir_args, m0, s0)


def _softmax_pv(s_list, v_list):
    m = functools.reduce(jnp.maximum, [jnp.max(s, -1, keepdims=True) for s in s_list])
    ps = [jnp.exp(s - m) for s in s_list]
    l = functools.reduce(jnp.add, [jnp.sum(p, -1, keepdims=True) for p in ps])
    o = functools.reduce(jnp.add, [jnp.dot(p.astype(BF16), v, preferred_element_type=F32)
                                   for p, v in zip(ps, v_list)])
    return o / l


def _na_kernel(n_rows, q_ref, *refs):
    nb = NA_BAND // NA_KBLK_ROWS
    k_blks = refs[0:nb]
    v_blks = refs[nb:2 * nb]
    kc_ref, vc_ref, bt_ref, o_ref, kband, vband = refs[2 * nb:]
    i = pl.program_id(1)
    blk_tok = NA_KBLK_ROWS * GRID_W
    HD = NA_HEAD_DIM
    GW = NA_GROUP * HD
    head_of_lane = lax.broadcasted_iota(jnp.int32, (1, GW), 1) // HD
    for j in range(nb):
        kband[j * blk_tok:(j + 1) * blk_tok, :] = k_blks[j][0]
        vband[j * blk_tok:(j + 1) * blk_tok, :] = v_blks[j][0]
    band_row0 = jnp.clip(i * NA_QROWS - NA_ROWS // 2, 0, n_rows - NA_BAND)
    n_win = NA_ROWS * GRID_W
    zero = jnp.zeros((), BF16)

    def rows_body(t, carry):
        items = []
        for u in range(NA_UNROLL):
            a = t * NA_UNROLL + u
            r = i * NA_QROWS + a
            r0 = jnp.clip(r - NA_ROWS // 2, 0, n_rows - NA_ROWS)
            koff = pl.multiple_of((r0 - band_row0) * GRID_W, GRID_W)
            dr_first = r0 - r + (NA_ROWS - 1)
            qoff = pl.multiple_of(a * GRID_W, GRID_W)
            for gi in range(NA_HEADS // NA_GROUP):
                items.append((koff, dr_first, qoff, gi, slice(gi * GW, (gi + 1) * GW)))
        scores = []
        for koff, _, qoff, _, gs in items:
            qg = q_ref[0, pl.ds(qoff, GRID_W), gs]
            qm = jnp.concatenate([jnp.where(head_of_lane == g, qg, zero) for g in range(NA_GROUP)], axis=0)
            scores.append((_nt_dot(kband[pl.ds(koff, n_win), gs], qm),
                           _nt_dot(kc_ref[0, :, gs], qm)))
        probs = []
        for (_, dr_first, _, gi, _), (s_loc, s_ctx) in zip(items, scores):
            s_loc = s_loc + bt_ref[dr_first, gi]
            m = jnp.maximum(jnp.max(s_loc, 0, keepdims=True), jnp.max(s_ctx, 0, keepdims=True))
            e_loc = jnp.exp(s_loc - m)
            e_ctx = jnp.exp(s_ctx - m)
            inv = 1.0 / (jnp.sum(e_loc, 0, keepdims=True) + jnp.sum(e_ctx, 0, keepdims=True))
            probs.append(((e_loc * inv).astype(BF16), (e_ctx * inv).astype(BF16)))
        for (koff, _, qoff, _, gs), (p_loc, p_ctx) in zip(items, probs):
            res = (_tn_dot(p_loc, vband[pl.ds(koff, n_win), gs])
                   + _tn_dot(p_ctx, vc_ref[0, :, gs]))
            out = res[0:GRID_W]
            for g in range(1, NA_GROUP):
                out = jnp.where(head_of_lane == g, res[g * GRID_W:(g + 1) * GRID_W], out)
            o_ref[0, pl.ds(qoff, GRID_W), gs] = out.astype(BF16)
        return carry

    lax.fori_loop(0, NA_QROWS // NA_UNROLL, rows_body, 0)


def _na_latent(q, k, v, kc, vc, bias_tab):
    B, T, Wd = q.shape
    Lc = kc.shape[1]
    n_rows = T // GRID_W
    nb = NA_BAND // NA_KBLK_ROWS
    q_tok = NA_QROWS * GRID_W
    blk_tok = NA_KBLK_ROWS * GRID_W
    last_blk0 = (n_rows - NA_BAND) // NA_KBLK_ROWS

    def kv_spec(j):
        def idx(b, i):
            first = jnp.clip(i * (NA_QROWS // NA_KBLK_ROWS) - (NA_ROWS // 2) // NA_KBLK_ROWS, 0, last_blk0)
            return (b, first + j, 0)
        return pl.BlockSpec((1, blk_tok, Wd), idx)

    return pl.pallas_call(
        functools.partial(_na_kernel, n_rows),
        grid=(B, n_rows // NA_QROWS),
        in_specs=([pl.BlockSpec((1, q_tok, Wd), lambda b, i: (b, i, 0))]
                  + [kv_spec(j) for j in range(nb)] + [kv_spec(j) for j in range(nb)]
                  + [pl.BlockSpec((1, Lc, Wd), lambda b, i: (b, 0, 0)),
                     pl.BlockSpec((1, Lc, Wd), lambda b, i: (b, 0, 0)),
                     pl.BlockSpec(bias_tab.shape, lambda b, i: (0, 0, 0, 0))]),
        out_specs=pl.BlockSpec((1, q_tok, Wd), lambda b, i: (b, i, 0)),
        out_shape=jax.ShapeDtypeStruct((B, T, Wd), BF16),
        scratch_shapes=[pltpu.VMEM((NA_BAND * GRID_W, Wd), BF16),
                        pltpu.VMEM((NA_BAND * GRID_W, Wd), BF16)],
        compiler_params=_cparams("arbitrary", "arbitrary"),
        name="natten",
    )(q, *([k] * nb), *([v] * nb), kc, vc, bias_tab)


def _ctx_attn_kernel(q_ref, k_ref, v_ref, o_ref):
    HD = NA_HEAD_DIM
    for h in range(NA_HEADS):
        hs = slice(h * HD, (h + 1) * HD)
        s = _nt_dot(q_ref[0, :, hs], k_ref[0, :, hs])
        o_ref[0, :, hs] = _softmax_pv([s], [v_ref[0, :, hs]]).astype(BF16)


def _ctx_attn(q, k, v):
    B, Lc, Wd = q.shape
    spec = pl.BlockSpec((1, Lc, Wd), lambda b: (b, 0, 0))
    return pl.pallas_call(
        _ctx_attn_kernel, grid=(B,), in_specs=[spec, spec, spec], out_specs=spec,
        out_shape=jax.ShapeDtypeStruct((B, Lc, Wd), BF16),
        compiler_params=_cparams("arbitrary"), name="ctx_attn",
    )(q, k, v)


def _na_bias_table(rpb):
    qc = np.arange(GRID_W)[:, None]
    kcol = np.arange(GRID_W)[None, :]
    wstart = np.clip(qc - NA_COLS // 2, 0, GRID_W - NA_COLS)
    col_ok = (kcol >= wstart) & (kcol < wstart + NA_COLS)
    dc = np.clip(kcol - qc, -(NA_COLS - 1), NA_COLS - 1) + NA_COLS - 1
    per_dr = jnp.where(col_ok[None, None], rpb[:, :, dc], MASK_NEG)
    def tab(d):
        n_grp = NA_HEADS // NA_GROUP
        t = per_dr[:, d:d + NA_ROWS].reshape(n_grp, NA_GROUP, NA_ROWS, GRID_W, GRID_W)
        return jnp.transpose(t, (0, 2, 4, 1, 3)).reshape(n_grp, NA_ROWS * GRID_W, NA_GROUP * GRID_W)

    return jnp.stack([tab(d) for d in range(NA_ROWS)]).astype(F32)


def _route(sel, s):
    E = EXP_PER_GROUP
    scores = []
    for g in range(N_GROUPS):
        a, b, c, d = sel[E * g:E * (g + 1)]
        scores.append(functools.reduce(jnp.maximum, [a + b, a + c, a + d, b + c, b + d, c + d]))
    best = jnp.zeros_like(scores[0], dtype=jnp.int32)
    best_score = scores[0]
    for g in range(1, N_GROUPS):
        upd = scores[g] > best_score
        best = jnp.where(upd, g, best)
        best_score = jnp.where(upd, scores[g], best_score)

    def pick(rows, j):
        out = rows[j]
        for g in range(1, N_GROUPS):
            out = jnp.where(best == g, rows[E * g + j], out)
        return out

    v = [pick(sel, j) for j in range(E)]
    sv = [pick(s, j) for j in range(E)]
    i1 = jnp.zeros_like(best)
    m1 = v[0]
    for j in range(1, E):
        upd = v[j] > m1
        i1 = jnp.where(upd, j, i1)
        m1 = jnp.where(upd, v[j], m1)
    i2 = jnp.where(i1 == 0, 1, 0)
    m2 = jnp.where(i1 == 0, v[1], v[0])
    for j in range(1, E):
        upd = (i1 != j) & (v[j] > m2)
        i2 = jnp.where(upd, j, i2)
        m2 = jnp.where(upd, v[j], m2)

    def at(rows, idx):
        out = rows[0]
        for j in range(1, E):
            out = jnp.where(idx == j, rows[j], out)
        return out

    s1 = at(sv, i1)
    s2 = at(sv, i2)
    tot = s1 + s2
    return best * E + i1, best * E + i2, s1 / tot, s2 / tot


def _out_proj_kernel(ygm_ref, hf_ref, hb_ref, o_ref, yna_ref, x_ref, mod_ref, w_ref, mlg_ref, n2g_ref,
                     rw_ref, rb_ref, xn_ref, h2_ref, re_ref, rwt_ref):
    hs = hf_ref[0] + hb_ref[0]
    hsq = hs * hs
    lane = lax.broadcasted_iota(jnp.int32, (1, ML_WIDTH), 1)
    scale = jnp.zeros_like(hs)
    for h in range(ML_HEADS):
        in_head = (lane >= h * ML_HEAD_DIM) & (lane < (h + 1) * ML_HEAD_DIM)
        ms = jnp.sum(jnp.where(in_head, hsq, 0.0), -1, keepdims=True) * (1.0 / ML_HEAD_DIM)
        scale = jnp.where(in_head, lax.rsqrt(ms + EPS), scale)
    yml = hs * scale * mlg_ref[...] * jax.nn.sigmoid(o_ref[0])
    o1 = GM_WIDTH
    o2 = GM_WIDTH + ML_WIDTH
    acc = (jnp.dot(ygm_ref[0], w_ref[0:o1, :], preferred_element_type=F32)
           + jnp.dot(yml.astype(BF16), w_ref[o1:o2, :], preferred_element_type=F32)
           + jnp.dot(yna_ref[0], w_ref[o2:, :], preferred_element_type=F32))
    xn = x_ref[0] + mod_ref[0, 0:1, :] * acc
    xn_ref[0] = xn
    h2 = xn * lax.rsqrt(jnp.mean(xn * xn, -1, keepdims=True) + EPS) * n2g_ref[...]
    h2 = h2 * (1.0 + mod_ref[0, 2:3, :]) + mod_ref[0, 1:2, :]
    h2_ref[0] = h2
    logits = _nt_dot(rw_ref[...], h2, precision=lax.Precision.HIGHEST)
    s = jax.nn.sigmoid(logits)
    sel = s + rb_ref[...]
    rows = lambda m: [m[e:e + 1, :] for e in range(N_EXPERTS)]
    e1, e2, w1, w2 = _route(rows(sel), rows(s))
    re_ref[0, 0:1, :] = e1
    re_ref[0, 1:2, :] = e2
    rwt_ref[0, 0:1, :] = w1
    rwt_ref[0, 1:2, :] = w2


def _out_proj(ygm, hf, hb, ml, yna, x, mod, w_out, ml_g, n2_g, router_wt, router_b):
    G, R, D = x.shape
    tm = min(R, 512)
    row = lambda b, i: (b, i, 0)
    whole2 = lambda shape: pl.BlockSpec(shape, lambda b, i: (0, 0))
    return pl.pallas_call(
        _out_proj_kernel,
        grid=(G, R // tm),
        in_specs=[pl.BlockSpec((1, tm, GM_WIDTH), row),
                  pl.BlockSpec((1, tm, ML_WIDTH), row),
                  pl.BlockSpec((1, tm, ML_WIDTH), row),
                  pl.BlockSpec((1, tm, ML_WIDTH), lambda b, i: (b, i, 3)),
                  pl.BlockSpec((1, tm, NA_WIDTH), row),
                  pl.BlockSpec((1, tm, D), row),
                  pl.BlockSpec((1, 3, D), lambda b, i: (b, 0, 0)),
                  whole2((D, D)), whole2((1, ML_WIDTH)), whole2((1, D)),
                  whole2((N_EXPERTS, D)), whole2((N_EXPERTS, 1))],
        out_specs=[pl.BlockSpec((1, tm, D), row), pl.BlockSpec((1, tm, D), row),
                   pl.BlockSpec((1, 2, tm), lambda b, i: (b, 0, i)),
                   pl.BlockSpec((1, 2, tm), lambda b, i: (b, 0, i))],
        out_shape=[jax.ShapeDtypeStruct((G, R, D), F32), jax.ShapeDtypeStruct((G, R, D), F32),
                   jax.ShapeDtypeStruct((G, 2, R), jnp.int32), jax.ShapeDtypeStruct((G, 2, R), F32)],
        compiler_params=_cparams("arbitrary", "arbitrary"),
        name="out_proj",
    )(ygm, hf, hb, ml, yna, x, mod, w_out, ml_g, n2_g, router_wt, router_b)


def _experts_kernel(te_ref, nu_ref, src_cur, src_nxt, h_hbm, wg_ref, wu_ref, wd_ref, o_ref,
                    acc_ref, wgb, wub, wdb, xbuf, sem):
    i = pl.program_id(0)
    n_used = nu_ref[0]
    used = i < n_used
    slot = lax.rem(i, 2)
    new_expert = (i == 0) | (te_ref[i] != te_ref[jnp.maximum(i - 1, 0)])

    def row_copy(src_ref, s, r):
        return pltpu.make_async_copy(h_hbm.at[pl.ds(src_ref[0, 0, r], 1)], xbuf.at[s, pl.ds(r, 1)],
                                     sem.at[s])

    def tile_wait(s):
        pltpu.make_async_copy(xbuf.at[s], xbuf.at[s], sem.at[s]).wait()

    @pl.when((i == 0) & used)
    def _():
        def body(r, carry):
            row_copy(src_cur, 0, r).start()
            return carry
        lax.fori_loop(0, MOE_TILE, body, 0, unroll=8)

    @pl.when(used & new_expert)
    def _():
        def cast_rows(c, carry):
            rs = pl.ds(pl.multiple_of(c * LANE, LANE), LANE)
            wgb[rs, :] = wg_ref[0, 0, rs, :].astype(BF16)
            wub[rs, :] = wu_ref[0, 0, rs, :].astype(BF16)
            wdb[rs, :] = wd_ref[0, 0, rs, :].astype(BF16)
            return carry
        lax.fori_loop(0, D_MODEL // LANE, cast_rows, 0)

    @pl.when(used)
    def _():
        tile_wait(slot)
        x = jnp.concatenate([xbuf[slot, :, j, :] for j in range(D_MODEL // LANE)], axis=1).astype(BF16)
        nxt = 1 - slot
        n_chunks = D_EXPERT // EXPERT_CHUNK
        per = -(-MOE_TILE // (3 * n_chunks))
        issued = [0]

        def request_rows():
            for r in range(issued[0], min(issued[0] + per, MOE_TILE)):
                row_copy(src_nxt, nxt, r).start(priority=r % 2)
            issued[0] += per

        for j in range(n_chunks):
            cs = slice(j * EXPERT_CHUNK, (j + 1) * EXPERT_CHUNK)
            request_rows()
            g = jnp.dot(x, wgb[:, cs], preferred_element_type=F32)
            request_rows()
            u = jnp.dot(x, wub[:, cs], preferred_element_type=F32)
            a = (g * jax.nn.sigmoid(g) * u).astype(BF16)
            request_rows()
            y = jnp.dot(a, wdb[cs, :], preferred_element_type=F32)
            if j == 0:
                acc_ref[...] = y
            else:
                acc_ref[...] += y
        o_ref[...] = acc_ref[...].astype(BF16)

        @pl.when(i + 1 >= n_used)
        def _():
            tile_wait(nxt)

    @pl.when(jnp.logical_not(used))
    def _():
        o_ref[...] = jnp.zeros_like(o_ref)


def _experts(l, tile_expert, n_used, src, h_all, wg, wu, wd):
    D = h_all.shape[1]
    n_tiles = src.shape[0] // MOE_TILE
    src3 = src.reshape(n_tiles, 1, MOE_TILE)
    wspec = lambda r, c: pl.BlockSpec((1, 1, r, c), lambda i, te, nu: (l, te[i], 0, 0))
    src_spec = lambda ahead: pl.BlockSpec(
        (1, 1, MOE_TILE), lambda i, te, nu: (jnp.minimum(i + ahead, n_tiles - 1), 0, 0),
        memory_space=pltpu.SMEM)
    return pl.pallas_call(
        _experts_kernel,
        grid_spec=pltpu.PrefetchScalarGridSpec(
            num_scalar_prefetch=2,
            grid=(n_tiles,),
            in_specs=[src_spec(0), src_spec(1), pl.BlockSpec(memory_space=pl.ANY),
                      wspec(D, D_EXPERT), wspec(D, D_EXPERT), wspec(D_EXPERT, D)],
            out_specs=pl.BlockSpec((MOE_TILE, D), lambda i, te, nu: (i, 0)),
            scratch_shapes=[pltpu.VMEM((MOE_TILE, D), F32), pltpu.VMEM((D, D_EXPERT), BF16),
                            pltpu.VMEM((D, D_EXPERT), BF16), pltpu.VMEM((D_EXPERT, D), BF16),
                            pltpu.VMEM((2, MOE_TILE, D // LANE, LANE), F32),
                            pltpu.SemaphoreType.DMA((2,))]),
        out_shape=jax.ShapeDtypeStruct((n_tiles * MOE_TILE, D), BF16),
        compiler_params=_cparams("arbitrary"),
        name="experts",
    )(tile_expert, n_used, src3, src3, h_all.reshape(-1, D // LANE, LANE), wg, wu, wd)


def _dispatch(e_idx):
    n_tok = e_idx.shape[0]
    n_asg = 2 * n_tok
    n_tiles = -(-n_asg // MOE_TILE) + N_EXPERTS
    flat_e = e_idx.reshape(-1)
    onehot = (flat_e[:, None] == jnp.arange(N_EXPERTS, dtype=jnp.int32)[None, :]).astype(jnp.int32)
    csum = jnp.cumsum(onehot, axis=0)
    rank = jnp.take_along_axis(csum, flat_e[:, None], axis=1)[:, 0] - 1
    counts = csum[-1]
    tiles_e = (counts + MOE_TILE - 1) // MOE_TILE
    tile_end = jnp.cumsum(tiles_e)
    tile_start = tile_end - tiles_e
    dest = tile_start[flat_e] * MOE_TILE + rank
    n_used = tile_end[-1]
    tid = jnp.arange(n_tiles, dtype=jnp.int32)
    te = jnp.minimum(jnp.searchsorted(tile_end, tid, side="right"), N_EXPERTS - 1).astype(jnp.int32)
    te = jnp.where(tid < n_used, te, te[jnp.maximum(n_used - 1, 0)])
    flat_t = jnp.arange(n_asg, dtype=jnp.int32) // 2
    src = jnp.zeros((n_tiles * MOE_TILE,), jnp.int32).at[dest].set(
        flat_t, unique_indices=True, mode="promise_in_bounds")
    return dest.reshape(n_tok, 2), src, te, n_used.reshape(1).astype(jnp.int32)


def _combine_kernel(final, xn_ref, y0_ref, y1_ref, w_ref, g2_ref, fg_ref, o_ref):
    w = w_ref[0]
    f = y0_ref[0].astype(F32) * w[:, 0:1] + y1_ref[0].astype(F32) * w[:, 1:2]
    x = xn_ref[0] + g2_ref[0] * f
    if final:
        x = x * lax.rsqrt(jnp.mean(x * x, -1, keepdims=True) + EPS) * fg_ref[...]
    o_ref[0] = x


def _combine(xn, y0, y1, wt, g2, final_g, final):
    G, R, D = xn.shape
    tm = min(R, 512)
    row = lambda b, i: (b, i, 0)
    return pl.pallas_call(
        functools.partial(_combine_kernel, final),
        grid=(G, R // tm),
        in_specs=[pl.BlockSpec((1, tm, D), row), pl.BlockSpec((1, tm, D), row), pl.BlockSpec((1, tm, D), row),
                  pl.BlockSpec((1, tm, 2), row),
                  pl.BlockSpec((1, 1, D), lambda b, i: (b, 0, 0)),
                  pl.BlockSpec((1, D), lambda b, i: (0, 0))],
        out_specs=pl.BlockSpec((1, tm, D), row),
        out_shape=jax.ShapeDtypeStruct((G, R, D), F32),
        compiler_params=_cparams("arbitrary", "arbitrary"),
        name="combine_final" if final else "combine",
    )(xn, y0, y1, wt, g2, final_g)


def _rope_tables(T):
    lane = np.arange(LANE)
    half = ML_HEAD_DIM // 4
    inv = jnp.tile(ROPE_BASE ** (-jnp.arange(half, dtype=F32) / half), LANE // half)
    t = jnp.arange(T)
    pos = jnp.where(((lane // (2 * half)) % 2 == 0)[None, :], (t // GRID_W)[:, None], (t % GRID_W)[:, None])
    ang = pos.astype(F32) * inv[None, :]
    sign = np.where((lane // half) % 2 == 0, -1.0, 1.0).astype(np.float32)
    return jnp.cos(ang), jnp.sin(ang) * sign[None, :]


def _reorder_w_in(w_in):
    pad = jnp.zeros(w_in.shape[:2] + (LANE - 4 * ML_HEADS,), w_in.dtype)
    return jnp.concatenate([w_in[..., :OFF_GATES], w_in[..., OFF_NA:], w_in[..., OFF_GATES:OFF_NA], pad],
                           axis=-1).astype(BF16)


def _mixers(l, p, pc, prm, need_ctx):
    ygm, ml, gt, q, k, v = p
    ygmc, mlc, gtc, qc, kc, vc = pc
    B = ml.shape[0]
    conv = (prm["conv_w"][l], prm["conv_b"][l])
    nh = ML_HEADS
    tr = lambda g: jnp.transpose(
        jnp.concatenate([g[..., 0:nh], g[..., 2 * nh:3 * nh], g[..., nh:2 * nh], g[..., 3 * nh:4 * nh]], -1),
        (0, 2, 1))
    m_zero = jnp.zeros((B, 2 * nh, LANE), F32)
    s_zero = jnp.zeros((B, nh, ML_CHUNK, 2 * LANE), F32)
    qmc, kmc, grc = _mlstm_prep(mlc, prm["cos_c"], prm["sin_c"], *conv, tr(gtc), prm["gbias"][l])
    hfc, hbc, m_st, s_st = _mlstm(qmc, kmc, mlc, grc, m_zero, s_zero)
    qm, km, gr = _mlstm_prep(ml, prm["cos_l"], prm["sin_l"], *conv, tr(gt), prm["gbias"][l])
    hf, hb, _, _ = _mlstm(qm, km, ml, gr, m_st, s_st)
    y_na = _na_latent(q, k, v, kc, vc, prm["na_tab"][l])
    y = (ygm, hf, hb, ml, y_na)
    if not need_ctx:
        return y, None
    return y, (ygmc, hfc, hbc, mlc, _ctx_attn(qc, kc, vc))


def _rows(a, idx):
    return a.at[idx].get(mode="promise_in_bounds")


def _moe(l, h2_all, e_all, prm):
    dest, src, te, n_used = _dispatch(e_all)
    yb = _experts(l, te, n_used, src, h2_all, prm["wg"], prm["wu"], prm["wd"])
    return yb, dest


def kernel(x, c, ctx, c_ctx, ada_w, ada_b, norm1_g, norm2_g, w_in, w_out, gm_ws, gm_bs, gm_norm_g,
           ml_conv_w, ml_conv_b, ml_gate_b, ml_norm_g, na_rpb, router_w, router_b,
           moe_w_gate, moe_w_up, moe_w_down, final_g):
    B, T, D = x.shape
    Lc = ctx.shape[1]
    cos_l, sin_l = _rope_tables(T)
    prm = dict(
        gm_ws=gm_ws.astype(BF16),
        gm_bs_full=jnp.repeat(jnp.transpose(gm_bs, (0, 2, 1)), GM_HEAD_DIM, axis=-1),
        gm_g=gm_norm_g[:, None, :],
        conv_w=ml_conv_w, conv_b=ml_conv_b[:, None, :],
        gbias=ml_gate_b[:, jnp.array([0, 2, 1, 3])].reshape(DEPTH, 4 * ML_HEADS, 1),
        cos_l=cos_l, sin_l=sin_l,
        cos_c=jnp.ones((Lc, LANE), F32), sin_c=jnp.zeros((Lc, LANE), F32),
        na_tab=jnp.stack([_na_bias_table(na_rpb[l]) for l in range(DEPTH)]),
        wg=moe_w_gate, wu=moe_w_up, wd=moe_w_down,
    )
    w_in_r = _reorder_w_in(w_in)
    w_out_b = w_out.astype(BF16)
    router_wt = jnp.transpose(router_w)
    router_bc = router_b[:, None]

    cs = jnp.concatenate([c, c_ctx[None, :], jnp.zeros((8 - B - 1, D), F32)], axis=0)
    mods = _ada_all(cs, ada_w, ada_b).reshape(DEPTH, 8, 6, D)

    xc = ctx
    for l in range(DEPTH):
        need_ctx = l < DEPTH - 1
        mod_l = mods[l, :B]
        mod_c = jnp.broadcast_to(mods[l, B:B + 1], (B, 6, D))
        n1 = norm1_g[l][None, :]
        gm_args = (prm["gm_ws"][l], prm["gm_bs_full"][l], prm["gm_g"][l])
        p = _in_proj(x, mod_l[:, 0:2], n1, w_in_r[l], *gm_args)
        pc = _in_proj(xc, mod_c[:, 0:2], n1, w_in_r[l], *gm_args)
        y, yc = _mixers(l, p, pc, prm, need_ctx)
        op_args = (w_out_b[l], ml_norm_g[l][None, :], norm2_g[l][None, :], router_wt, router_bc)
        xn, h2, re, rw = _out_proj(*y, x, mod_l[:, 2:5], *op_args)
        h2_all = h2.reshape(B * T, D)
        e_all = jnp.transpose(re, (0, 2, 1)).reshape(B * T, 2)
        if need_ctx:
            xnc, h2c, rec, rwc = _out_proj(*yc, xc, mod_c[:, 2:5], *op_args)
            h2_all = jnp.concatenate([h2_all, h2c.reshape(B * Lc, D)], axis=0)
            e_all = jnp.concatenate([e_all, jnp.transpose(rec, (0, 2, 1)).reshape(B * Lc, 2)], axis=0)
        yb, dest = _moe(l, h2_all, e_all, prm)
        final = l == DEPTH - 1
        fg = final_g[None, :]
        n_lat = B * T
        picked = lambda lo, hi, k, L: _rows(yb, dest[lo:hi, k]).reshape(B, L, D)
        x = _combine(xn, picked(0, n_lat, 0, T), picked(0, n_lat, 1, T),
                     jnp.transpose(rw, (0, 2, 1)), mod_l[:, 5:6], fg, final)
        if need_ctx:
            xc = _combine(xnc, picked(n_lat, None, 0, Lc), picked(n_lat, None, 1, Lc),
                          jnp.transpose(rwc, (0, 2, 1)), mod_c[:, 5:6], fg, False)
    return x
```

```python
import functools

import jax
import jax.numpy as jnp
import numpy as np
from jax import lax
from jax.experimental import pallas as pl
from jax.experimental.pallas import tpu as pltpu

F32 = jnp.float32
BF16 = jnp.bfloat16

D_MODEL = 1024
DEPTH = 4
GRID_W = 64
EPS = 1e-6

GM_HEADS = 4
GM_WIDTH = D_MODEL // 4
GM_HEAD_DIM = GM_WIDTH // GM_HEADS
GM_CHUNK = 128

ML_HEADS = 4
ML_WIDTH = D_MODEL // 4
ML_HEAD_DIM = ML_WIDTH // ML_HEADS
ML_CHUNK = 128

NA_HEADS = 8
NA_WIDTH = D_MODEL // 2
NA_HEAD_DIM = NA_WIDTH // NA_HEADS
NA_ROWS = 8
NA_COLS = 16

ROPE_BASE = 10000.0

OFF_ML = 2 * GM_WIDTH
OFF_GATES = OFF_ML + 4 * ML_WIDTH
OFF_NA = OFF_GATES + 4 * ML_HEADS
N_IN = OFF_NA + 3 * NA_WIDTH

N_EXPERTS = 16
N_GROUPS = 4
EXP_PER_GROUP = N_EXPERTS // N_GROUPS
D_EXPERT = D_MODEL

LANE = 128
SEG_GM = (0, 2 * GM_WIDTH)
SEG_ML = (SEG_GM[1], SEG_GM[1] + 4 * ML_WIDTH)
SEG_Q = (SEG_ML[1], SEG_ML[1] + NA_WIDTH)
SEG_K = (SEG_Q[1], SEG_Q[1] + NA_WIDTH)
SEG_V = (SEG_K[1], SEG_K[1] + NA_WIDTH)
SEG_GATES = (SEG_V[1], SEG_V[1] + LANE)
N_IN_PAD = SEG_GATES[1]

NA_QROWS = 8
NA_GROUP = 4
NA_UNROLL = 4
NA_BAND = 16
NA_KBLK_ROWS = 4
MOE_TILE = 512
EXPERT_CHUNK = 256
MASK_NEG = -1e30
N_GATE_ROWS = 6
VMEM_LIMIT = 56 * 2 ** 20


def _cparams(*sem):
    return pltpu.CompilerParams(dimension_semantics=sem, vmem_limit_bytes=VMEM_LIMIT)


def _nt_dot(a, b, precision=None):
    return lax.dot_general(a, b, (((1,), (1,)), ((), ())), precision=precision,
                           preferred_element_type=F32)


def _tn_dot(a, b):
    return lax.dot_general(a, b, (((0,), (0,)), ((), ())), preferred_element_type=F32)


def _ada_kernel(c_ref, w_ref, b_ref, o_ref):
    c = c_ref[...]
    sc = c * jax.nn.sigmoid(c)
    o_ref[0] = jnp.dot(sc, w_ref[0], preferred_element_type=F32) + b_ref[0]


def _ada_all(cs, ada_w, ada_b):
    n_out = ada_w.shape[-1] // D_MODEL
    return pl.pallas_call(
        _ada_kernel,
        grid=(DEPTH, n_out),
        in_specs=[pl.BlockSpec((8, D_MODEL), lambda l, j: (0, 0)),
                  pl.BlockSpec((1, D_MODEL, D_MODEL), lambda l, j: (l, 0, j)),
                  pl.BlockSpec((1, 1, D_MODEL), lambda l, j: (l, 0, j))],
        out_specs=pl.BlockSpec((1, 8, D_MODEL), lambda l, j: (l, 0, j)),
        out_shape=jax.ShapeDtypeStruct((DEPTH, 8, n_out * D_MODEL), F32),
        compiler_params=_cparams("arbitrary", "arbitrary"),
        name="adaln",
    )(cs, ada_w, ada_b.reshape(DEPTH, 1, -1))


def _gmlp_chunk(z_pre, ws_ref, bs, g):
    lane = lax.broadcasted_iota(jnp.int32, (1, GM_WIDTH), 1)
    z = jax.nn.gelu(z_pre)
    u = z[:, :GM_WIDTH]
    v = z[:, GM_WIDTH:]
    mu = jnp.mean(v, -1, keepdims=True)
    vc = v - mu
    var = jnp.mean(vc * vc, -1, keepdims=True)
    vn = (vc * lax.rsqrt(var + EPS) * g).astype(BF16)
    sv = jnp.zeros((GM_CHUNK, GM_WIDTH), F32)
    for h in range(GM_HEADS):
        full = jnp.dot(ws_ref[h], vn, preferred_element_type=F32)
        in_head = (lane >= h * GM_HEAD_DIM) & (lane < (h + 1) * GM_HEAD_DIM)
        sv = jnp.where(in_head, full, sv)
    return u * (sv + bs)


def _in_proj_kernel(x_ref, mod_ref, g_ref, w_ref, ws_ref, bs_ref, gmg_ref,
                    ygm_ref, ml_ref, gt_ref, q_ref, k_ref, v_ref):
    x = x_ref[0]
    h = x * lax.rsqrt(jnp.mean(x * x, -1, keepdims=True) + EPS) * g_ref[...]
    h = h * (1.0 + mod_ref[0, 1:2, :]) + mod_ref[0, 0:1, :]
    hb = h.astype(BF16)

    def seg(s):
        return jnp.dot(hb, w_ref[:, s[0]:s[1]], preferred_element_type=F32)

    gm = seg(SEG_GM)
    ml_ref[0] = seg(SEG_ML)
    for c in range(gm.shape[0] // GM_CHUNK):
        rows = slice(c * GM_CHUNK, (c + 1) * GM_CHUNK)
        ygm_ref[0, rows, :] = _gmlp_chunk(gm[rows], ws_ref, bs_ref[...], gmg_ref[...]).astype(BF16)
    gt_ref[0] = seg(SEG_GATES)
    q_ref[0] = (seg(SEG_Q) * (NA_HEAD_DIM ** -0.5)).astype(BF16)
    k_ref[0] = seg(SEG_K).astype(BF16)
    v_ref[0] = seg(SEG_V).astype(BF16)


def _in_proj(x, mod, g, w, gm_ws, gm_bs_full, gm_g):
    G, R, D = x.shape
    tm = min(R, 512)
    row = lambda b, i: (b, i, 0)
    whole = lambda shape: pl.BlockSpec(shape, lambda b, i: (0,) * len(shape))
    widths = (GM_WIDTH, SEG_ML[1] - SEG_ML[0], LANE, NA_WIDTH, NA_WIDTH, NA_WIDTH)
    dtypes = (BF16, F32, F32, BF16, BF16, BF16)
    return pl.pallas_call(
        _in_proj_kernel,
        grid=(G, R // tm),
        in_specs=[pl.BlockSpec((1, tm, D), row),
                  pl.BlockSpec((1, 2, D), lambda b, i: (b, 0, 0)),
                  whole((1, D)), whole((D, N_IN_PAD)),
                  whole((GM_HEADS, GM_CHUNK, GM_CHUNK)), whole((GM_CHUNK, GM_WIDTH)), whole((1, GM_WIDTH))],
        out_specs=[pl.BlockSpec((1, tm, n), row) for n in widths],
        out_shape=[jax.ShapeDtypeStruct((G, R, n), dt) for n, dt in zip(widths, dtypes)],
        compiler_params=_cparams("arbitrary", "arbitrary"),
        name="in_proj",
    )(x, mod, g, w, gm_ws, gm_bs_full, gm_g)


def _log_sigmoid(x):
    return jnp.minimum(x, 0.0) - jnp.log1p(jnp.exp(-jnp.abs(x)))


def _rope(x, cos, sin_signed):
    lane = lax.broadcasted_iota(jnp.int32, (1, LANE), 1)
    first_half = (lane & 16) == 0
    partner = jnp.where(first_half, pltpu.roll(x, LANE - 16, 1), pltpu.roll(x, 16, 1))
    return x * cos + partner * sin_signed


def _chunk_scan(x, op, fill, reverse):
    pos = lax.broadcasted_iota(jnp.int32, (1, x.shape[1]), 1) % ML_CHUNK
    n = x.shape[1]
    sh = 1
    while sh < ML_CHUNK:
        if reverse:
            shifted = jnp.where(pos < ML_CHUNK - sh, pltpu.roll(x, n - sh, 1), fill)
        else:
            shifted = jnp.where(pos >= sh, pltpu.roll(x, sh, 1), fill)
        x = op(x, shifted)
        sh *= 2
    return x


def _mlstm_prep_kernel(n_blocks, x_ref, hp_ref, hn_ref, cos_ref, sin_ref, cw_ref, cb_ref, g_ref, gbias_ref,
                       q_ref, k_ref, go_ref):
    nh2 = 2 * ML_HEADS
    g = g_ref[0] + gbias_ref[...]
    is_fwd = lax.broadcasted_iota(jnp.int32, (nh2, 1), 0) < ML_HEADS
    li = g[0:nh2]
    lf = _log_sigmoid(g[nh2:])
    ps = _chunk_scan(lf, jnp.add, 0.0, False)
    ss = _chunk_scan(lf, jnp.add, 0.0, True)
    bcum = jnp.where(is_fwd, ps, ss)
    r = li - bcum
    pm = _chunk_scan(r, jnp.maximum, MASK_NEG, False)
    sm = _chunk_scan(r, jnp.maximum, MASK_NEG, True)
    rmax = jnp.maximum(pm, sm)
    groups = (bcum, r, jnp.where(is_fwd, pm, sm), jnp.exp(r - rmax), rmax, ps + ss - lf)
    for j, val in enumerate(groups):
        go_ref[0, nh2 * j:nh2 * (j + 1), :] = val

    i = pl.program_id(1)
    W = ML_WIDTH
    x = x_ref[0]
    tr = x.shape[0]
    rid = lax.broadcasted_iota(jnp.int32, (tr, 1), 0)
    i_row = jnp.zeros((1, 2 * W), jnp.int32) + i
    prev = jnp.where(i_row == 0, 0.0, hp_ref[0, 7:8, :])
    nxt = jnp.where(i_row == n_blocks - 1, 0.0, hn_ref[0, 0:1, :])
    xm1 = jnp.where(rid == 0, prev, pltpu.roll(x, 1, 0))
    xp1 = jnp.where(rid == tr - 1, nxt, pltpu.roll(x, tr - 1, 0))
    y = xm1 * cw_ref[0:1, :] + x * cw_ref[1:2, :] + xp1 * cw_ref[2:3, :] + cb_ref[...]
    y = y * jax.nn.sigmoid(y)
    cos = cos_ref[...]
    sin = sin_ref[...]
    parts = [_rope(y[:, j * LANE:(j + 1) * LANE], cos, sin) for j in range(2 * W // LANE)]
    q_ref[0] = (jnp.concatenate(parts[:W // LANE], axis=1) * (ML_HEAD_DIM ** -0.5)).astype(BF16)
    for j in range(W // LANE):
        k_ref[0, j * LANE:(j + 1) * LANE, :] = parts[W // LANE + j].T.astype(BF16)


def _mlstm_prep(ml, cos, sin, conv_w, conv_b, gates_t, gbias):
    B, T, _ = ml.shape
    n_g = gates_t.shape[1]
    W = ML_WIDTH
    tr = min(T, 1024)
    nb = T // tr
    halo_per_blk = tr // 8
    n_hblk = T // 8
    whole = lambda shape: pl.BlockSpec(shape, lambda b, i: (0, 0))
    return pl.pallas_call(
        functools.partial(_mlstm_prep_kernel, nb),
        grid=(B, nb),
        in_specs=[pl.BlockSpec((1, tr, 2 * W), lambda b, i: (b, i, 0)),
                  pl.BlockSpec((1, 8, 2 * W), lambda b, i: (b, jnp.maximum(i * halo_per_blk - 1, 0), 0)),
                  pl.BlockSpec((1, 8, 2 * W),
                               lambda b, i: (b, jnp.minimum((i + 1) * halo_per_blk, n_hblk - 1), 0)),
                  pl.BlockSpec((tr, LANE), lambda b, i: (i, 0)),
                  pl.BlockSpec((tr, LANE), lambda b, i: (i, 0)),
                  whole((3, 2 * W)), whole((1, 2 * W)),
                  pl.BlockSpec((1, n_g, tr), lambda b, i: (b, 0, i)), whole((n_g, 1))],
        out_specs=[pl.BlockSpec((1, tr, W), lambda b, i: (b, i, 0)),
                   pl.BlockSpec((1, W, tr), lambda b, i: (b, 0, i)),
                   pl.BlockSpec((1, 2 * ML_HEADS * N_GATE_ROWS, tr), lambda b, i: (b, 0, i))],
        out_shape=[jax.ShapeDtypeStruct((B, T, W), BF16), jax.ShapeDtypeStruct((B, W, T), BF16),
                   jax.ShapeDtypeStruct((B, 2 * ML_HEADS * N_GATE_ROWS, T), F32)],
        compiler_params=_cparams("arbitrary", "arbitrary"),
        name="mlstm_prep",
    )(ml, ml, ml, cos, sin, conv_w, conv_b, gates_t, gbias)


def _mlstm_kernel(n_chunks, qf_ref, ktf_ref, vf_ref, gf_ref, qb_ref, ktb_ref, vb_ref, gb_ref,
                  m0_ref, s0_ref, hf_ref, hb_ref, mfin_ref, sfin_ref, m_scr, s_scr, bd_scr):
    c = pl.program_id(0)
    L = ML_CHUNK
    HD = ML_HEAD_DIM
    NH = ML_HEADS
    n_batch = m_scr.shape[0]

    @pl.when(c == 0)
    def _():
        m_scr[...] = m0_ref[...]
        s_scr[...] = s0_ref[...]
        bd_scr[...] = jnp.zeros_like(bd_scr)

    def row_bcast(x, j):
        return jnp.broadcast_to(x[j:j + 1, :], (L, LANE))

    row = lax.broadcasted_iota(jnp.int32, (L, L), 0)
    col = lax.broadcasted_iota(jnp.int32, (L, L), 1)
    low_half = lax.broadcasted_iota(jnp.int32, (1, LANE), 1) < HD
    own_rows = jnp.concatenate([row < HD, row >= HD], axis=1)
    is_fwd = lax.broadcasted_iota(jnp.int32, (2 * NH, 1), 0) < NH

    gate = []
    for b in range(n_batch):
        gq = lambda i: jnp.where(is_fwd, gf_ref[b, 2 * NH * i:2 * NH * (i + 1), :],
                                 gb_ref[b, 2 * NH * i:2 * NH * (i + 1), :])
        bcum, r, rcmax, ew, rmax, b_last = (gq(i) for i in range(N_GATE_ROWS))
        m0 = m_scr[b]
        mu = jnp.maximum(m0, rcmax)
        emt = jnp.exp(-(bcum + mu))
        m_last = jnp.maximum(m0, rmax)
        m_scr[b] = b_last + m_last
        stack = jnp.concatenate([mu, emt, jnp.zeros((L - 4 * NH, LANE), F32)], axis=0)
        gate.append(dict(r=r, ew=ew, m0=m0, a_st=jnp.exp(m0 - m_last), g_st=jnp.exp(rmax - m_last),
                         cols=stack.T))

    dirs = ((qf_ref, ktf_ref, vf_ref, hf_ref), (qb_ref, ktb_ref, vb_ref, hb_ref))
    items = [(b, d, pr) for b in range(n_batch) for d in range(2) for pr in range(NH // 2)]

    pre = []
    for b, d, pr in items:
        q_ref, kt_ref, v_ref, _ = dirs[d]
        ps = slice(pr * LANE, (pr + 1) * LANE)
        ja = NH * d + 2 * pr
        g = gate[b]
        qp = q_ref[b, :, ps]
        kt = kt_ref[b, ps, :]
        vp = v_ref[b, :, ps]
        st = s_scr[b, 2 * d + pr]
        ktf = kt.astype(F32)
        kt2 = jnp.concatenate([jnp.where(row < HD, ktf, 0.0), jnp.where(row < HD, 0.0, ktf)],
                              axis=1).astype(BF16)
        s2 = jnp.dot(qp, kt2, preferred_element_type=F32)
        inter = jnp.dot(qp, st.astype(BF16), preferred_element_type=F32)
        vab = jnp.concatenate([jnp.where(low_half, vp, 1.0), jnp.where(low_half, 1.0, vp)],
                              axis=1).astype(BF16)
        ktw = ktf * jnp.where(row < HD, row_bcast(g["ew"], ja), row_bcast(g["ew"], ja + 1))
        upd = jnp.dot(ktw.astype(BF16), vab, preferred_element_type=F32)
        pre.append((s2, inter, vab, upd, st))

    probs = []
    for (b, d, pr), (s2, _, _, _, _) in zip(items, pre):
        ja = NH * d + 2 * pr
        g = gate[b]
        valid = (col <= row) if d == 0 else (col >= row)
        mu_c = [jnp.broadcast_to(g["cols"][:, j:j + 1], (L, LANE)) for j in (ja, ja + 1)]
        e2 = jnp.concatenate([jnp.where(valid, jnp.exp(row_bcast(g["r"], j) - mc), 0.0)
                              for j, mc in zip((ja, ja + 1), mu_c)], axis=1)
        a2 = jnp.concatenate([jnp.exp(row_bcast(g["m0"], j) - mc) for j, mc in zip((ja, ja + 1), mu_c)],
                             axis=1)
        probs.append(((s2 * e2).astype(BF16), a2))

    for it, ((b, d, pr), (_, inter, vab, upd, st), (p2, a2)) in enumerate(zip(items, pre, probs)):
        out_ref = dirs[d][3]
        ps = slice(pr * LANE, (pr + 1) * LANE)
        ja = NH * d + 2 * pr
        g = gate[b]
        bd_scr[it, 0:L, 0:LANE] = vab[:, 0:LANE]
        bd_scr[it, L:2 * L, LANE:2 * LANE] = vab[:, LANE:2 * LANE]
        intra = jnp.dot(p2, bd_scr[it], preferred_element_type=F32)
        tot = intra + a2 * inter
        tot_a = tot[:, 0:LANE]
        tot_b = tot[:, LANE:2 * LANE]
        num = jnp.where(low_half, tot_a, tot_b)
        den = pltpu.roll(jnp.where(low_half, tot_b, tot_a), HD, 1)
        floor = jnp.where(low_half, jnp.broadcast_to(g["cols"][:, 8 + ja:8 + ja + 1], (L, LANE)),
                          jnp.broadcast_to(g["cols"][:, 8 + ja + 1:8 + ja + 2], (L, LANE)))
        out_ref[b, :, ps] = num / jnp.maximum(jnp.abs(den), floor)
        a_s = jnp.concatenate([row_bcast(g["a_st"], ja), row_bcast(g["a_st"], ja + 1)], axis=1)
        g_s = jnp.concatenate([row_bcast(g["g_st"], ja), row_bcast(g["g_st"], ja + 1)], axis=1)
        s_scr[b, 2 * d + pr] = a_s * st + g_s * jnp.where(own_rows, upd, 0.0)

    @pl.when(c == n_chunks - 1)
    def _():
        mfin_ref[...] = m_scr[...]
        sfin_ref[...] = s_scr[...]


def _mlstm(q, kt, ml, grows, m0, s0):
    B, T, W = q.shape
    L = ML_CHUNK
    nc = T // L

    def specs(ci):
        return [pl.BlockSpec((B, L, W), lambda c: (0, ci(c), 0)),
                pl.BlockSpec((B, W, L), lambda c: (0, 0, ci(c))),
                pl.BlockSpec((B, L, W), lambda c: (0, ci(c), 2)),
                pl.BlockSpec((B, grows.shape[1], L), lambda c: (0, 0, ci(c)))]

    m_spec = pl.BlockSpec(m0.shape, lambda c: (0, 0, 0))
    s_spec = pl.BlockSpec(s0.shape, lambda c: (0, 0, 0, 0))
    dir_args = [q, kt, ml, grows]
    return pl.pallas_call(
        functools.partial(_mlstm_kernel, nc),
        grid=(nc,),
        in_specs=specs(lambda c: c) + specs(lambda c: nc - 1 - c) + [m_spec, s_spec],
        out_specs=[pl.BlockSpec((B, L, W), lambda c: (0, c, 0)),
                   pl.BlockSpec((B, L, W), lambda c: (0, nc - 1 - c, 0)),
                   m_spec, s_spec],
        out_shape=[jax.ShapeDtypeStruct((B, T, W), F32),
                   jax.ShapeDtypeStruct((B, T, W), F32),
                   jax.ShapeDtypeStruct(m0.shape, F32),
                   jax.ShapeDtypeStruct(s0.shape, F32)],
        scratch_shapes=[pltpu.VMEM(m0.shape, F32), pltpu.VMEM(s0.shape, F32),
                        pltpu.VMEM((B * ML_HEADS, 2 * L, 2 * LANE), BF16)],
        compiler_params=_cparams("arbitrary"),
        name="mlstm",
    )(*dir_args, *dir_args, m0, s0)


def _softmax_pv(s_list, v_list):
    m = functools.reduce(jnp.maximum, [jnp.max(s, -1, keepdims=True) for s in s_list])
    ps = [jnp.exp(s - m) for s in s_list]
    l = functools.reduce(jnp.add, [jnp.sum(p, -1, keepdims=True) for p in ps])
    o = functools.reduce(jnp.add, [jnp.dot(p.astype(BF16), v, preferred_element_type=F32)
                                   for p, v in zip(ps, v_list)])
    return o / l


def _na_kernel(n_rows, q_ref, *refs):
    nb = NA_BAND // NA_KBLK_ROWS
    k_blks = refs[0:nb]
    v_blks = refs[nb:2 * nb]
    kc_ref, vc_ref, bt_ref, o_ref, kband, vband = refs[2 * nb:]
    i = pl.program_id(1)
    blk_tok = NA_KBLK_ROWS * GRID_W
    HD = NA_HEAD_DIM
    GW = NA_GROUP * HD
    head_of_lane = lax.broadcasted_iota(jnp.int32, (1, GW), 1) // HD
    for j in range(nb):
        kband[j * blk_tok:(j + 1) * blk_tok, :] = k_blks[j][0]
        vband[j * blk_tok:(j + 1) * blk_tok, :] = v_blks[j][0]
    band_row0 = jnp.clip(i * NA_QROWS - NA_ROWS // 2, 0, n_rows - NA_BAND)
    n_win = NA_ROWS * GRID_W
    zero = jnp.zeros((), BF16)

    def rows_body(t, carry):
        items = []
        for u in range(NA_UNROLL):
            a = t * NA_UNROLL + u
            r = i * NA_QROWS + a
            r0 = jnp.clip(r - NA_ROWS // 2, 0, n_rows - NA_ROWS)
            koff = pl.multiple_of((r0 - band_row0) * GRID_W, GRID_W)
            dr_first = r0 - r + (NA_ROWS - 1)
            qoff = pl.multiple_of(a * GRID_W, GRID_W)
            for gi in range(NA_HEADS // NA_GROUP):
                items.append((koff, dr_first, qoff, gi, slice(gi * GW, (gi + 1) * GW)))
        scores = []
        for koff, _, qoff, _, gs in items:
            qg = q_ref[0, pl.ds(qoff, GRID_W), gs]
            qm = jnp.concatenate([jnp.where(head_of_lane == g, qg, zero) for g in range(NA_GROUP)], axis=0)
            scores.append((_nt_dot(kband[pl.ds(koff, n_win), gs], qm),
                           _nt_dot(kc_ref[0, :, gs], qm)))
        probs = []
        for (_, dr_first, _, gi, _), (s_loc, s_ctx) in zip(items, scores):
            s_loc = s_loc + bt_ref[dr_first, gi]
            m = jnp.maximum(jnp.max(s_loc, 0, keepdims=True), jnp.max(s_ctx, 0, keepdims=True))
            e_loc = jnp.exp(s_loc - m)
            e_ctx = jnp.exp(s_ctx - m)
            inv = 1.0 / (jnp.sum(e_loc, 0, keepdims=True) + jnp.sum(e_ctx, 0, keepdims=True))
            probs.append(((e_loc * inv).astype(BF16), (e_ctx * inv).astype(BF16)))
        for (koff, _, qoff, _, gs), (p_loc, p_ctx) in zip(items, probs):
            res = (_tn_dot(p_loc, vband[pl.ds(koff, n_win), gs])
                   + _tn_dot(p_ctx, vc_ref[0, :, gs]))
            out = res[0:GRID_W]
            for g in range(1, NA_GROUP):
                out = jnp.where(head_of_lane == g, res[g * GRID_W:(g + 1) * GRID_W], out)
            o_ref[0, pl.ds(qoff, GRID_W), gs] = out.astype(BF16)
        return carry

    lax.fori_loop(0, NA_QROWS // NA_UNROLL, rows_body, 0)


def _na_latent(q, k, v, kc, vc, bias_tab):
    B, T, Wd = q.shape
    Lc = kc.shape[1]
    n_rows = T // GRID_W
    nb = NA_BAND // NA_KBLK_ROWS
    q_tok = NA_QROWS * GRID_W
    blk_tok = NA_KBLK_ROWS * GRID_W
    last_blk0 = (n_rows - NA_BAND) // NA_KBLK_ROWS

    def kv_spec(j):
        def idx(b, i):
            first = jnp.clip(i * (NA_QROWS // NA_KBLK_ROWS) - (NA_ROWS // 2) // NA_KBLK_ROWS, 0, last_blk0)
            return (b, first + j, 0)
        return pl.BlockSpec((1, blk_tok, Wd), idx)

    return pl.pallas_call(
        functools.partial(_na_kernel, n_rows),
        grid=(B, n_rows // NA_QROWS),
        in_specs=([pl.BlockSpec((1, q_tok, Wd), lambda b, i: (b, i, 0))]
                  + [kv_spec(j) for j in range(nb)] + [kv_spec(j) for j in range(nb)]
                  + [pl.BlockSpec((1, Lc, Wd), lambda b, i: (b, 0, 0)),
                     pl.BlockSpec((1, Lc, Wd), lambda b, i: (b, 0, 0)),
                     pl.BlockSpec(bias_tab.shape, lambda b, i: (0, 0, 0, 0))]),
        out_specs=pl.BlockSpec((1, q_tok, Wd), lambda b, i: (b, i, 0)),
        out_shape=jax.ShapeDtypeStruct((B, T, Wd), BF16),
        scratch_shapes=[pltpu.VMEM((NA_BAND * GRID_W, Wd), BF16),
                        pltpu.VMEM((NA_BAND * GRID_W, Wd), BF16)],
        compiler_params=_cparams("arbitrary", "arbitrary"),
        name="natten",
    )(q, *([k] * nb), *([v] * nb), kc, vc, bias_tab)


def _ctx_attn_kernel(q_ref, k_ref, v_ref, o_ref):
    HD = NA_HEAD_DIM
    for h in range(NA_HEADS):
        hs = slice(h * HD, (h + 1) * HD)
        s = _nt_dot(q_ref[0, :, hs], k_ref[0, :, hs])
        o_ref[0, :, hs] = _softmax_pv([s], [v_ref[0, :, hs]]).astype(BF16)


def _ctx_attn(q, k, v):
    B, Lc, Wd = q.shape
    spec = pl.BlockSpec((1, Lc, Wd), lambda b: (b, 0, 0))
    return pl.pallas_call(
        _ctx_attn_kernel, grid=(B,), in_specs=[spec, spec, spec], out_specs=spec,
        out_shape=jax.ShapeDtypeStruct((B, Lc, Wd), BF16),
        compiler_params=_cparams("arbitrary"), name="ctx_attn",
    )(q, k, v)


def _na_bias_table(rpb):
    qc = np.arange(GRID_W)[:, None]
    kcol = np.arange(GRID_W)[None, :]
    wstart = np.clip(qc - NA_COLS // 2, 0, GRID_W - NA_COLS)
    col_ok = (kcol >= wstart) & (kcol < wstart + NA_COLS)
    dc = np.clip(kcol - qc, -(NA_COLS - 1), NA_COLS - 1) + NA_COLS - 1
    per_dr = jnp.where(col_ok[None, None], rpb[:, :, dc], MASK_NEG)
    def tab(d):
        n_grp = NA_HEADS // NA_GROUP
        t = per_dr[:, d:d + NA_ROWS].reshape(n_grp, NA_GROUP, NA_ROWS, GRID_W, GRID_W)
        return jnp.transpose(t, (0, 2, 4, 1, 3)).reshape(n_grp, NA_ROWS * GRID_W, NA_GROUP * GRID_W)

    return jnp.stack([tab(d) for d in range(NA_ROWS)]).astype(F32)


def _route(sel, s):
    E = EXP_PER_GROUP
    scores = []
    for g in range(N_GROUPS):
        a, b, c, d = sel[E * g:E * (g + 1)]
        scores.append(functools.reduce(jnp.maximum, [a + b, a + c, a + d, b + c, b + d, c + d]))
    best = jnp.zeros_like(scores[0], dtype=jnp.int32)
    best_score = scores[0]
    for g in range(1, N_GROUPS):
        upd = scores[g] > best_score
        best = jnp.where(upd, g, best)
        best_score = jnp.where(upd, scores[g], best_score)

    def pick(rows, j):
        out = rows[j]
        for g in range(1, N_GROUPS):
            out = jnp.where(best == g, rows[E * g + j], out)
        return out

    v = [pick(sel, j) for j in range(E)]
    sv = [pick(s, j) for j in range(E)]
    i1 = jnp.zeros_like(best)
    m1 = v[0]
    for j in range(1, E):
        upd = v[j] > m1
        i1 = jnp.where(upd, j, i1)
        m1 = jnp.where(upd, v[j], m1)
    i2 = jnp.where(i1 == 0, 1, 0)
    m2 = jnp.where(i1 == 0, v[1], v[0])
    for j in range(1, E):
        upd = (i1 != j) & (v[j] > m2)
        i2 = jnp.where(upd, j, i2)
        m2 = jnp.where(upd, v[j], m2)

    def at(rows, idx):
        out = rows[0]
        for j in range(1, E):
            out = jnp.where(idx == j, rows[j], out)
        return out

    s1 = at(sv, i1)
    s2 = at(sv, i2)
    tot = s1 + s2
    return best * E + i1, best * E + i2, s1 / tot, s2 / tot


def _out_proj_kernel(ygm_ref, hf_ref, hb_ref, o_ref, yna_ref, x_ref, mod_ref, w_ref, mlg_ref, n2g_ref,
                     rw_ref, rb_ref, xn_ref, h2_ref, re_ref, rwt_ref):
    hs = hf_ref[0] + hb_ref[0]
    hsq = hs * hs
    lane = lax.broadcasted_iota(jnp.int32, (1, ML_WIDTH), 1)
    scale = jnp.zeros_like(hs)
    for h in range(ML_HEADS):
        in_head = (lane >= h * ML_HEAD_DIM) & (lane < (h + 1) * ML_HEAD_DIM)
        ms = jnp.sum(jnp.where(in_head, hsq, 0.0), -1, keepdims=True) * (1.0 / ML_HEAD_DIM)
        scale = jnp.where(in_head, lax.rsqrt(ms + EPS), scale)
    yml = hs * scale * mlg_ref[...] * jax.nn.sigmoid(o_ref[0])
    o1 = GM_WIDTH
    o2 = GM_WIDTH + ML_WIDTH
    acc = (jnp.dot(ygm_ref[0], w_ref[0:o1, :], preferred_element_type=F32)
           + jnp.dot(yml.astype(BF16), w_ref[o1:o2, :], preferred_element_type=F32)
           + jnp.dot(yna_ref[0], w_ref[o2:, :], preferred_element_type=F32))
    xn = x_ref[0] + mod_ref[0, 0:1, :] * acc
    xn_ref[0] = xn
    h2 = xn * lax.rsqrt(jnp.mean(xn * xn, -1, keepdims=True) + EPS) * n2g_ref[...]
    h2 = h2 * (1.0 + mod_ref[0, 2:3, :]) + mod_ref[0, 1:2, :]
    h2_ref[0] = h2
    h_hi = h2.astype(BF16)
    h_lo = (h2 - h_hi.astype(F32)).astype(BF16)
    rw_hi = rw_ref[0]
    logits = _nt_dot(rw_hi, h_hi) + _nt_dot(rw_hi, h_lo) + _nt_dot(rw_ref[1], h_hi)
    s = jax.nn.sigmoid(logits)
    sel = s + rb_ref[...]
    rows = lambda m: [m[e:e + 1, :] for e in range(N_EXPERTS)]
    e1, e2, w1, w2 = _route(rows(sel), rows(s))
    re_ref[0, 0:1, :] = e1
    re_ref[0, 1:2, :] = e2
    rwt_ref[0, 0:1, :] = w1
    rwt_ref[0, 1:2, :] = w2


def _out_proj(ygm, hf, hb, ml, yna, x, mod, w_out, ml_g, n2_g, router_wt, router_b):
    G, R, D = x.shape
    tm = min(R, 512)
    row = lambda b, i: (b, i, 0)
    whole2 = lambda shape: pl.BlockSpec(shape, lambda b, i: (0, 0))
    return pl.pallas_call(
        _out_proj_kernel,
        grid=(G, R // tm),
        in_specs=[pl.BlockSpec((1, tm, GM_WIDTH), row),
                  pl.BlockSpec((1, tm, ML_WIDTH), row),
                  pl.BlockSpec((1, tm, ML_WIDTH), row),
                  pl.BlockSpec((1, tm, ML_WIDTH), lambda b, i: (b, i, 3)),
                  pl.BlockSpec((1, tm, NA_WIDTH), row),
                  pl.BlockSpec((1, tm, D), row),
                  pl.BlockSpec((1, 3, D), lambda b, i: (b, 0, 0)),
                  whole2((D, D)), whole2((1, ML_WIDTH)), whole2((1, D)),
                  pl.BlockSpec((2, N_EXPERTS, D), lambda b, i: (0, 0, 0)), whole2((N_EXPERTS, 1))],
        out_specs=[pl.BlockSpec((1, tm, D), row), pl.BlockSpec((1, tm, D), row),
                   pl.BlockSpec((1, 2, tm), lambda b, i: (b, 0, i)),
                   pl.BlockSpec((1, 2, tm), lambda b, i: (b, 0, i))],
        out_shape=[jax.ShapeDtypeStruct((G, R, D), F32), jax.ShapeDtypeStruct((G, R, D), F32),
                   jax.ShapeDtypeStruct((G, 2, R), jnp.int32), jax.ShapeDtypeStruct((G, 2, R), F32)],
        compiler_params=_cparams("arbitrary", "arbitrary"),
        name="out_proj",
    )(ygm, hf, hb, ml, yna, x, mod, w_out, ml_g, n2_g, router_wt, router_b)


def _experts_kernel(te_ref, nu_ref, src_cur, src_nxt, h_hbm, wg_ref, wu_ref, wd_ref, o_ref,
                    acc_ref, wgb, wub, wdb, xbuf, sem):
    i = pl.program_id(0)
    n_used = nu_ref[0]
    used = i < n_used
    slot = lax.rem(i, 2)
    new_expert = (i == 0) | (te_ref[i] != te_ref[jnp.maximum(i - 1, 0)])

    def row_copy(src_ref, s, r):
        return pltpu.make_async_copy(h_hbm.at[pl.ds(src_ref[0, 0, r], 1), :], xbuf.at[s, pl.ds(r, 1), :],
                                     sem.at[s])

    def tile_wait(s):
        pltpu.make_async_copy(xbuf.at[s], xbuf.at[s], sem.at[s]).wait()

    @pl.when((i == 0) & used)
    def _():
        def body(r, carry):
            row_copy(src_cur, 0, r).start()
            return carry
        lax.fori_loop(0, MOE_TILE, body, 0, unroll=8)

    @pl.when(used & new_expert)
    def _():
        def cast_rows(c, carry):
            rs = pl.ds(pl.multiple_of(c * LANE, LANE), LANE)
            wgb[rs, :] = wg_ref[0, 0, rs, :].astype(BF16)
            wub[rs, :] = wu_ref[0, 0, rs, :].astype(BF16)
            wdb[rs, :] = wd_ref[0, 0, rs, :].astype(BF16)
            return carry
        lax.fori_loop(0, D_MODEL // LANE, cast_rows, 0)

    @pl.when(used)
    def _():
        tile_wait(slot)
        x = xbuf[slot].astype(BF16)
        nxt = 1 - slot
        n_chunks = D_EXPERT // EXPERT_CHUNK
        per = -(-MOE_TILE // (3 * n_chunks))
        issued = [0]

        def request_rows():
            for r in range(issued[0], min(issued[0] + per, MOE_TILE)):
                row_copy(src_nxt, nxt, r).start(priority=r % 2)
            issued[0] += per

        for j in range(n_chunks):
            cs = slice(j * EXPERT_CHUNK, (j + 1) * EXPERT_CHUNK)
            request_rows()
            g = jnp.dot(x, wgb[:, cs], preferred_element_type=F32)
            request_rows()
            u = jnp.dot(x, wub[:, cs], preferred_element_type=F32)
            a = (g * jax.nn.sigmoid(g) * u).astype(BF16)
            request_rows()
            y = jnp.dot(a, wdb[cs, :], preferred_element_type=F32)
            if j == 0:
                acc_ref[...] = y
            else:
                acc_ref[...] += y
        o_ref[...] = acc_ref[...].astype(BF16)

        @pl.when(i + 1 >= n_used)
        def _():
            tile_wait(nxt)

    @pl.when(jnp.logical_not(used))
    def _():
        o_ref[...] = jnp.zeros_like(o_ref)


def _experts(l, tile_expert, n_used, src, h_all, wg, wu, wd):
    D = h_all.shape[1]
    n_tiles = src.shape[0] // MOE_TILE
    src3 = src.reshape(n_tiles, 1, MOE_TILE)
    wspec = lambda r, c: pl.BlockSpec((1, 1, r, c), lambda i, te, nu: (l, te[i], 0, 0))
    src_spec = lambda ahead: pl.BlockSpec(
        (1, 1, MOE_TILE), lambda i, te, nu: (jnp.minimum(i + ahead, n_tiles - 1), 0, 0),
        memory_space=pltpu.SMEM)
    return pl.pallas_call(
        _experts_kernel,
        grid_spec=pltpu.PrefetchScalarGridSpec(
            num_scalar_prefetch=2,
            grid=(n_tiles,),
            in_specs=[src_spec(0), src_spec(1), pl.BlockSpec(memory_space=pl.ANY),
                      wspec(D, D_EXPERT), wspec(D, D_EXPERT), wspec(D_EXPERT, D)],
            out_specs=pl.BlockSpec((MOE_TILE, D), lambda i, te, nu: (i, 0)),
            scratch_shapes=[pltpu.VMEM((MOE_TILE, D), F32), pltpu.VMEM((D, D_EXPERT), BF16),
                            pltpu.VMEM((D, D_EXPERT), BF16), pltpu.VMEM((D_EXPERT, D), BF16),
                            pltpu.VMEM((2, MOE_TILE, D), F32), pltpu.SemaphoreType.DMA((2,))]),
        out_shape=jax.ShapeDtypeStruct((n_tiles * MOE_TILE, D), BF16),
        compiler_params=_cparams("arbitrary"),
        name="experts",
    )(tile_expert, n_used, src3, src3, h_all, wg, wu, wd)


def _dispatch(e_idx):
    n_tok = e_idx.shape[0]
    n_asg = 2 * n_tok
    n_tiles = -(-n_asg // MOE_TILE) + N_EXPERTS
    flat_e = e_idx.reshape(-1)
    onehot = (flat_e[:, None] == jnp.arange(N_EXPERTS, dtype=jnp.int32)[None, :]).astype(jnp.int32)
    csum = jnp.cumsum(onehot, axis=0)
    rank = jnp.take_along_axis(csum, flat_e[:, None], axis=1)[:, 0] - 1
    counts = csum[-1]
    tiles_e = (counts + MOE_TILE - 1) // MOE_TILE
    tile_end = jnp.cumsum(tiles_e)
    tile_start = tile_end - tiles_e
    dest = tile_start[flat_e] * MOE_TILE + rank
    n_used = tile_end[-1]
    tid = jnp.arange(n_tiles, dtype=jnp.int32)
    te = jnp.minimum(jnp.searchsorted(tile_end, tid, side="right"), N_EXPERTS - 1).astype(jnp.int32)
    te = jnp.where(tid < n_used, te, te[jnp.maximum(n_used - 1, 0)])
    flat_t = jnp.arange(n_asg, dtype=jnp.int32) // 2
    src = jnp.zeros((n_tiles * MOE_TILE,), jnp.int32).at[dest].set(
        flat_t, unique_indices=True, mode="promise_in_bounds")
    return dest.reshape(n_tok, 2), src, te, n_used.reshape(1).astype(jnp.int32)


def _combine_kernel(final, xn_ref, y0_ref, y1_ref, w_ref, g2_ref, fg_ref, o_ref):
    w = w_ref[0]
    f = y0_ref[0].astype(F32) * w[:, 0:1] + y1_ref[0].astype(F32) * w[:, 1:2]
    x = xn_ref[0] + g2_ref[0] * f
    if final:
        x = x * lax.rsqrt(jnp.mean(x * x, -1, keepdims=True) + EPS) * fg_ref[...]
    o_ref[0] = x


def _combine(xn, y0, y1, wt, g2, final_g, final):
    G, R, D = xn.shape
    tm = min(R, 512)
    row = lambda b, i: (b, i, 0)
    return pl.pallas_call(
        functools.partial(_combine_kernel, final),
        grid=(G, R // tm),
        in_specs=[pl.BlockSpec((1, tm, D), row), pl.BlockSpec((1, tm, D), row), pl.BlockSpec((1, tm, D), row),
                  pl.BlockSpec((1, tm, 2), row),
                  pl.BlockSpec((1, 1, D), lambda b, i: (b, 0, 0)),
                  pl.BlockSpec((1, D), lambda b, i: (0, 0))],
        out_specs=pl.BlockSpec((1, tm, D), row),
        out_shape=jax.ShapeDtypeStruct((G, R, D), F32),
        compiler_params=_cparams("arbitrary", "arbitrary"),
        name="combine_final" if final else "combine",
    )(xn, y0, y1, wt, g2, final_g)


def _rope_tables(T):
    lane = np.arange(LANE)
    half = ML_HEAD_DIM // 4
    inv = jnp.tile(ROPE_BASE ** (-jnp.arange(half, dtype=F32) / half), LANE // half)
    t = jnp.arange(T)
    pos = jnp.where(((lane // (2 * half)) % 2 == 0)[None, :], (t // GRID_W)[:, None], (t % GRID_W)[:, None])
    ang = pos.astype(F32) * inv[None, :]
    sign = np.where((lane // half) % 2 == 0, -1.0, 1.0).astype(np.float32)
    return jnp.cos(ang), jnp.sin(ang) * sign[None, :]


def _reorder_w_in(w_in):
    pad = jnp.zeros(w_in.shape[:2] + (LANE - 4 * ML_HEADS,), w_in.dtype)
    return jnp.concatenate([w_in[..., :OFF_GATES], w_in[..., OFF_NA:], w_in[..., OFF_GATES:OFF_NA], pad],
                           axis=-1).astype(BF16)


def _mixers(l, p, pc, prm, need_ctx):
    ygm, ml, gt, q, k, v = p
    ygmc, mlc, gtc, qc, kc, vc = pc
    B = ml.shape[0]
    conv = (prm["conv_w"][l], prm["conv_b"][l])
    nh = ML_HEADS
    tr = lambda g: jnp.transpose(
        jnp.concatenate([g[..., 0:nh], g[..., 2 * nh:3 * nh], g[..., nh:2 * nh], g[..., 3 * nh:4 * nh]], -1),
        (0, 2, 1))
    m_zero = jnp.zeros((B, 2 * nh, LANE), F32)
    s_zero = jnp.zeros((B, nh, ML_CHUNK, 2 * LANE), F32)
    qmc, kmc, grc = _mlstm_prep(mlc, prm["cos_c"], prm["sin_c"], *conv, tr(gtc), prm["gbias"][l])
    hfc, hbc, m_st, s_st = _mlstm(qmc, kmc, mlc, grc, m_zero, s_zero)
    qm, km, gr = _mlstm_prep(ml, prm["cos_l"], prm["sin_l"], *conv, tr(gt), prm["gbias"][l])
    hf, hb, _, _ = _mlstm(qm, km, ml, gr, m_st, s_st)
    y_na = _na_latent(q, k, v, kc, vc, prm["na_tab"][l])
    y = (ygm, hf, hb, ml, y_na)
    if not need_ctx:
        return y, None
    return y, (ygmc, hfc, hbc, mlc, _ctx_attn(qc, kc, vc))


def _rows(a, idx):
    return a.at[idx].get(mode="promise_in_bounds")


def _moe(l, h2_all, e_all, prm):
    dest, src, te, n_used = _dispatch(e_all)
    yb = _experts(l, te, n_used, src, h2_all, prm["wg"], prm["wu"], prm["wd"])
    return yb, dest


def kernel(x, c, ctx, c_ctx, ada_w, ada_b, norm1_g, norm2_g, w_in, w_out, gm_ws, gm_bs, gm_norm_g,
           ml_conv_w, ml_conv_b, ml_gate_b, ml_norm_g, na_rpb, router_w, router_b,
           moe_w_gate, moe_w_up, moe_w_down, final_g):
    B, T, D = x.shape
    Lc = ctx.shape[1]
    cos_l, sin_l = _rope_tables(T)
    prm = dict(
        gm_ws=gm_ws.astype(BF16),
        gm_bs_full=jnp.repeat(jnp.transpose(gm_bs, (0, 2, 1)), GM_HEAD_DIM, axis=-1),
        gm_g=gm_norm_g[:, None, :],
        conv_w=ml_conv_w, conv_b=ml_conv_b[:, None, :],
        gbias=ml_gate_b[:, jnp.array([0, 2, 1, 3])].reshape(DEPTH, 4 * ML_HEADS, 1),
        cos_l=cos_l, sin_l=sin_l,
        cos_c=jnp.ones((Lc, LANE), F32), sin_c=jnp.zeros((Lc, LANE), F32),
        na_tab=jnp.stack([_na_bias_table(na_rpb[l]) for l in range(DEPTH)]),
        wg=moe_w_gate, wu=moe_w_up, wd=moe_w_down,
    )
    w_in_r = _reorder_w_in(w_in)
    w_out_b = w_out.astype(BF16)
    rw_t = jnp.transpose(router_w)
    rw_hi = rw_t.astype(BF16)
    router_wt = jnp.stack([rw_hi, (rw_t - rw_hi.astype(F32)).astype(BF16)])
    router_bc = router_b[:, None]

    cs = jnp.concatenate([c, c_ctx[None, :], jnp.zeros((8 - B - 1, D), F32)], axis=0)
    mods = _ada_all(cs, ada_w, ada_b).reshape(DEPTH, 8, 6, D)

    xc = ctx
    for l in range(DEPTH):
        need_ctx = l < DEPTH - 1
        mod_l = mods[l, :B]
        mod_c = jnp.broadcast_to(mods[l, B:B + 1], (B, 6, D))
        n1 = norm1_g[l][None, :]
        gm_args = (prm["gm_ws"][l], prm["gm_bs_full"][l], prm["gm_g"][l])
        p = _in_proj(x, mod_l[:, 0:2], n1, w_in_r[l], *gm_args)
        pc = _in_proj(xc, mod_c[:, 0:2], n1, w_in_r[l], *gm_args)
        y, yc = _mixers(l, p, pc, prm, need_ctx)
        op_args = (w_out_b[l], ml_norm_g[l][None, :], norm2_g[l][None, :], router_wt, router_bc)
        xn, h2, re, rw = _out_proj(*y, x, mod_l[:, 2:5], *op_args)
        h2_all = h2.reshape(B * T, D)
        e_all = jnp.transpose(re, (0, 2, 1)).reshape(B * T, 2)
        if need_ctx:
            xnc, h2c, rec, rwc = _out_proj(*yc, xc, mod_c[:, 2:5], *op_args)
            h2_all = jnp.concatenate([h2_all, h2c.reshape(B * Lc, D)], axis=0)
            e_all = jnp.concatenate([e_all, jnp.transpose(rec, (0, 2, 1)).reshape(B * Lc, 2)], axis=0)
        yb, dest = _moe(l, h2_all, e_all, prm)
        final = l == DEPTH - 1
        fg = final_g[None, :]
        n_lat = B * T
        picked = lambda lo, hi, k, L: _rows(yb, dest[lo:hi, k]).reshape(B, L, D)
        x = _combine(xn, picked(0, n_lat, 0, T), picked(0, n_lat, 1, T),
                     jnp.transpose(rw, (0, 2, 1)), mod_l[:, 5:6], fg, final)
        if need_ctx:
            xc = _combine(xnc, picked(n_lat, None, 0, Lc), picked(n_lat, None, 1, Lc),
                          jnp.transpose(rwc, (0, 2, 1)), mod_c[:, 5:6], fg, False)
    return x
```

```python
import functools

import jax
import jax.numpy as jnp
import numpy as np
from jax import lax
from jax.experimental import pallas as pl
from jax.experimental.pallas import tpu as pltpu

F32 = jnp.float32
BF16 = jnp.bfloat16

D_MODEL = 1024
DEPTH = 4
GRID_W = 64
EPS = 1e-6

GM_HEADS = 4
GM_WIDTH = D_MODEL // 4
GM_HEAD_DIM = GM_WIDTH // GM_HEADS
GM_CHUNK = 128

ML_HEADS = 4
ML_WIDTH = D_MODEL // 4
ML_HEAD_DIM = ML_WIDTH // ML_HEADS
ML_CHUNK = 128

NA_HEADS = 8
NA_WIDTH = D_MODEL // 2
NA_HEAD_DIM = NA_WIDTH // NA_HEADS
NA_ROWS = 8
NA_COLS = 16

ROPE_BASE = 10000.0

OFF_ML = 2 * GM_WIDTH
OFF_GATES = OFF_ML + 4 * ML_WIDTH
OFF_NA = OFF_GATES + 4 * ML_HEADS
N_IN = OFF_NA + 3 * NA_WIDTH

N_EXPERTS = 16
N_GROUPS = 4
EXP_PER_GROUP = N_EXPERTS // N_GROUPS
D_EXPERT = D_MODEL

LANE = 128
SEG_GM = (0, 2 * GM_WIDTH)
SEG_ML = (SEG_GM[1], SEG_GM[1] + 4 * ML_WIDTH)
SEG_Q = (SEG_ML[1], SEG_ML[1] + NA_WIDTH)
SEG_K = (SEG_Q[1], SEG_Q[1] + NA_WIDTH)
SEG_V = (SEG_K[1], SEG_K[1] + NA_WIDTH)
SEG_GATES = (SEG_V[1], SEG_V[1] + LANE)
N_IN_PAD = SEG_GATES[1]

NA_QROWS = 8
NA_GROUP = 4
NA_UNROLL = 4
NA_BAND = 16
NA_KBLK_ROWS = 4
MOE_TILE = 512
EXPERT_CHUNK = 256
MASK_NEG = -1e30
N_GATE_ROWS = 6
VMEM_LIMIT = 56 * 2 ** 20


def _cparams(*sem):
    return pltpu.CompilerParams(dimension_semantics=sem, vmem_limit_bytes=VMEM_LIMIT)


def _nt_dot(a, b, precision=None):
    return lax.dot_general(a, b, (((1,), (1,)), ((), ())), precision=precision,
                           preferred_element_type=F32)


def _tn_dot(a, b):
    return lax.dot_general(a, b, (((0,), (0,)), ((), ())), preferred_element_type=F32)


def _ada_kernel(c_ref, w_ref, b_ref, o_ref):
    c = c_ref[...]
    sc = c * jax.nn.sigmoid(c)
    o_ref[0] = jnp.dot(sc, w_ref[0], preferred_element_type=F32) + b_ref[0]


def _ada_all(cs, ada_w, ada_b):
    n_out = ada_w.shape[-1] // D_MODEL
    return pl.pallas_call(
        _ada_kernel,
        grid=(DEPTH, n_out),
        in_specs=[pl.BlockSpec((8, D_MODEL), lambda l, j: (0, 0)),
                  pl.BlockSpec((1, D_MODEL, D_MODEL), lambda l, j: (l, 0, j)),
                  pl.BlockSpec((1, 1, D_MODEL), lambda l, j: (l, 0, j))],
        out_specs=pl.BlockSpec((1, 8, D_MODEL), lambda l, j: (l, 0, j)),
        out_shape=jax.ShapeDtypeStruct((DEPTH, 8, n_out * D_MODEL), F32),
        compiler_params=_cparams("arbitrary", "arbitrary"),
        name="adaln",
    )(cs, ada_w, ada_b.reshape(DEPTH, 1, -1))


def _gmlp_chunk(z_pre, ws_ref, bs, g):
    lane = lax.broadcasted_iota(jnp.int32, (1, GM_WIDTH), 1)
    z = jax.nn.gelu(z_pre)
    u = z[:, :GM_WIDTH]
    v = z[:, GM_WIDTH:]
    mu = jnp.mean(v, -1, keepdims=True)
    vc = v - mu
    var = jnp.mean(vc * vc, -1, keepdims=True)
    vn = (vc * lax.rsqrt(var + EPS) * g).astype(BF16)
    sv = jnp.zeros((GM_CHUNK, GM_WIDTH), F32)
    for h in range(GM_HEADS):
        full = jnp.dot(ws_ref[h], vn, preferred_element_type=F32)
        in_head = (lane >= h * GM_HEAD_DIM) & (lane < (h + 1) * GM_HEAD_DIM)
        sv = jnp.where(in_head, full, sv)
    return u * (sv + bs)


def _in_proj_kernel(x_ref, mod_ref, g_ref, w_ref, ws_ref, bs_ref, gmg_ref,
                    ygm_ref, ml_ref, gt_ref, q_ref, k_ref, v_ref):
    x = x_ref[0]
    h = x * lax.rsqrt(jnp.mean(x * x, -1, keepdims=True) + EPS) * g_ref[...]
    h = h * (1.0 + mod_ref[0, 1:2, :]) + mod_ref[0, 0:1, :]
    hb = h.astype(BF16)

    def seg(s):
        return jnp.dot(hb, w_ref[:, s[0]:s[1]], preferred_element_type=F32)

    gm = seg(SEG_GM)
    ml_ref[0] = seg(SEG_ML)
    for c in range(gm.shape[0] // GM_CHUNK):
        rows = slice(c * GM_CHUNK, (c + 1) * GM_CHUNK)
        ygm_ref[0, rows, :] = _gmlp_chunk(gm[rows], ws_ref, bs_ref[...], gmg_ref[...]).astype(BF16)
    gt_ref[0] = seg(SEG_GATES)
    q_ref[0] = (seg(SEG_Q) * (NA_HEAD_DIM ** -0.5)).astype(BF16)
    k_ref[0] = seg(SEG_K).astype(BF16)
    v_ref[0] = seg(SEG_V).astype(BF16)


def _in_proj(x, mod, g, w, gm_ws, gm_bs_full, gm_g):
    G, R, D = x.shape
    tm = min(R, 512)
    row = lambda b, i: (b, i, 0)
    whole = lambda shape: pl.BlockSpec(shape, lambda b, i: (0,) * len(shape))
    widths = (GM_WIDTH, SEG_ML[1] - SEG_ML[0], LANE, NA_WIDTH, NA_WIDTH, NA_WIDTH)
    dtypes = (BF16, F32, F32, BF16, BF16, BF16)
    return pl.pallas_call(
        _in_proj_kernel,
        grid=(G, R // tm),
        in_specs=[pl.BlockSpec((1, tm, D), row),
                  pl.BlockSpec((1, 2, D), lambda b, i: (b, 0, 0)),
                  whole((1, D)), whole((D, N_IN_PAD)),
                  whole((GM_HEADS, GM_CHUNK, GM_CHUNK)), whole((GM_CHUNK, GM_WIDTH)), whole((1, GM_WIDTH))],
        out_specs=[pl.BlockSpec((1, tm, n), row) for n in widths],
        out_shape=[jax.ShapeDtypeStruct((G, R, n), dt) for n, dt in zip(widths, dtypes)],
        compiler_params=_cparams("arbitrary", "arbitrary"),
        name="in_proj",
    )(x, mod, g, w, gm_ws, gm_bs_full, gm_g)


def _log_sigmoid(x):
    return jnp.minimum(x, 0.0) - jnp.log1p(jnp.exp(-jnp.abs(x)))


def _rope(x, cos, sin_signed):
    lane = lax.broadcasted_iota(jnp.int32, (1, LANE), 1)
    first_half = (lane & 16) == 0
    partner = jnp.where(first_half, pltpu.roll(x, LANE - 16, 1), pltpu.roll(x, 16, 1))
    return x * cos + partner * sin_signed


def _chunk_scan(x, op, fill, reverse):
    pos = lax.broadcasted_iota(jnp.int32, (1, x.shape[1]), 1) % ML_CHUNK
    n = x.shape[1]
    sh = 1
    while sh < ML_CHUNK:
        if reverse:
            shifted = jnp.where(pos < ML_CHUNK - sh, pltpu.roll(x, n - sh, 1), fill)
        else:
            shifted = jnp.where(pos >= sh, pltpu.roll(x, sh, 1), fill)
        x = op(x, shifted)
        sh *= 2
    return x


def _mlstm_prep_kernel(n_blocks, x_ref, hp_ref, hn_ref, cos_ref, sin_ref, cw_ref, cb_ref, g_ref, gbias_ref,
                       q_ref, k_ref, go_ref):
    nh2 = 2 * ML_HEADS
    g = g_ref[0] + gbias_ref[...]
    is_fwd = lax.broadcasted_iota(jnp.int32, (nh2, 1), 0) < ML_HEADS
    li = g[0:nh2]
    lf = _log_sigmoid(g[nh2:])
    ps = _chunk_scan(lf, jnp.add, 0.0, False)
    ss = _chunk_scan(lf, jnp.add, 0.0, True)
    bcum = jnp.where(is_fwd, ps, ss)
    r = li - bcum
    pm = _chunk_scan(r, jnp.maximum, MASK_NEG, False)
    sm = _chunk_scan(r, jnp.maximum, MASK_NEG, True)
    rmax = jnp.maximum(pm, sm)
    groups = (bcum, r, jnp.where(is_fwd, pm, sm), jnp.exp(r - rmax), rmax, ps + ss - lf)
    for j, val in enumerate(groups):
        go_ref[0, nh2 * j:nh2 * (j + 1), :] = val

    i = pl.program_id(1)
    W = ML_WIDTH
    x = x_ref[0]
    tr = x.shape[0]
    rid = lax.broadcasted_iota(jnp.int32, (tr, 1), 0)
    i_row = jnp.zeros((1, 2 * W), jnp.int32) + i
    prev = jnp.where(i_row == 0, 0.0, hp_ref[0, 7:8, :])
    nxt = jnp.where(i_row == n_blocks - 1, 0.0, hn_ref[0, 0:1, :])
    xm1 = jnp.where(rid == 0, prev, pltpu.roll(x, 1, 0))
    xp1 = jnp.where(rid == tr - 1, nxt, pltpu.roll(x, tr - 1, 0))
    y = xm1 * cw_ref[0:1, :] + x * cw_ref[1:2, :] + xp1 * cw_ref[2:3, :] + cb_ref[...]
    y = y * jax.nn.sigmoid(y)
    cos = cos_ref[...]
    sin = sin_ref[...]
    parts = [_rope(y[:, j * LANE:(j + 1) * LANE], cos, sin) for j in range(2 * W // LANE)]
    q_ref[0] = (jnp.concatenate(parts[:W // LANE], axis=1) * (ML_HEAD_DIM ** -0.5)).astype(BF16)
    for j in range(W // LANE):
        k_ref[0, j * LANE:(j + 1) * LANE, :] = parts[W // LANE + j].T.astype(BF16)


def _mlstm_prep(ml, cos, sin, conv_w, conv_b, gates_t, gbias):
    B, T, _ = ml.shape
    n_g = gates_t.shape[1]
    W = ML_WIDTH
    tr = min(T, 1024)
    nb = T // tr
    halo_per_blk = tr // 8
    n_hblk = T // 8
    whole = lambda shape: pl.BlockSpec(shape, lambda b, i: (0, 0))
    return pl.pallas_call(
        functools.partial(_mlstm_prep_kernel, nb),
        grid=(B, nb),
        in_specs=[pl.BlockSpec((1, tr, 2 * W), lambda b, i: (b, i, 0)),
                  pl.BlockSpec((1, 8, 2 * W), lambda b, i: (b, jnp.maximum(i * halo_per_blk - 1, 0), 0)),
                  pl.BlockSpec((1, 8, 2 * W),
                               lambda b, i: (b, jnp.minimum((i + 1) * halo_per_blk, n_hblk - 1), 0)),
                  pl.BlockSpec((tr, LANE), lambda b, i: (i, 0)),
                  pl.BlockSpec((tr, LANE), lambda b, i: (i, 0)),
                  whole((3, 2 * W)), whole((1, 2 * W)),
                  pl.BlockSpec((1, n_g, tr), lambda b, i: (b, 0, i)), whole((n_g, 1))],
        out_specs=[pl.BlockSpec((1, tr, W), lambda b, i: (b, i, 0)),
                   pl.BlockSpec((1, W, tr), lambda b, i: (b, 0, i)),
                   pl.BlockSpec((1, 2 * ML_HEADS * N_GATE_ROWS, tr), lambda b, i: (b, 0, i))],
        out_shape=[jax.ShapeDtypeStruct((B, T, W), BF16), jax.ShapeDtypeStruct((B, W, T), BF16),
                   jax.ShapeDtypeStruct((B, 2 * ML_HEADS * N_GATE_ROWS, T), F32)],
        compiler_params=_cparams("arbitrary", "arbitrary"),
        name="mlstm_prep",
    )(ml, ml, ml, cos, sin, conv_w, conv_b, gates_t, gbias)


def _mlstm_kernel(n_chunks, qf_ref, ktf_ref, vf_ref, gf_ref, qb_ref, ktb_ref, vb_ref, gb_ref,
                  m0_ref, s0_ref, hf_ref, hb_ref, mfin_ref, sfin_ref, m_scr, s_scr, bd_scr):
    c = pl.program_id(0)
    L = ML_CHUNK
    HD = ML_HEAD_DIM
    NH = ML_HEADS
    n_batch = m_scr.shape[0]

    @pl.when(c == 0)
    def _():
        m_scr[...] = m0_ref[...]
        s_scr[...] = s0_ref[...]
        bd_scr[...] = jnp.zeros_like(bd_scr)

    def row_bcast(x, j):
        return jnp.broadcast_to(x[j:j + 1, :], (L, LANE))

    row = lax.broadcasted_iota(jnp.int32, (L, L), 0)
    col = lax.broadcasted_iota(jnp.int32, (L, L), 1)
    low_half = lax.broadcasted_iota(jnp.int32, (1, LANE), 1) < HD
    own_rows = jnp.concatenate([row < HD, row >= HD], axis=1)
    is_fwd = lax.broadcasted_iota(jnp.int32, (2 * NH, 1), 0) < NH

    gate = []
    for b in range(n_batch):
        gq = lambda i: jnp.where(is_fwd, gf_ref[b, 2 * NH * i:2 * NH * (i + 1), :],
                                 gb_ref[b, 2 * NH * i:2 * NH * (i + 1), :])
        bcum, r, rcmax, ew, rmax, b_last = (gq(i) for i in range(N_GATE_ROWS))
        m0 = m_scr[b]
        mu = jnp.maximum(m0, rcmax)
        emt = jnp.exp(-(bcum + mu))
        m_last = jnp.maximum(m0, rmax)
        m_scr[b] = b_last + m_last
        stack = jnp.concatenate([mu, emt, jnp.zeros((L - 4 * NH, LANE), F32)], axis=0)
        gate.append(dict(r=r, ew=ew, m0=m0, a_st=jnp.exp(m0 - m_last), g_st=jnp.exp(rmax - m_last),
                         cols=stack.T))

    dirs = ((qf_ref, ktf_ref, vf_ref, hf_ref), (qb_ref, ktb_ref, vb_ref, hb_ref))
    items = [(b, d, pr) for b in range(n_batch) for d in range(2) for pr in range(NH // 2)]

    pre = []
    for b, d, pr in items:
        q_ref, kt_ref, v_ref, _ = dirs[d]
        ps = slice(pr * LANE, (pr + 1) * LANE)
        ja = NH * d + 2 * pr
        g = gate[b]
        qp = q_ref[b, :, ps]
        kt = kt_ref[b, ps, :]
        vp = v_ref[b, :, ps]
        st = s_scr[b, 2 * d + pr]
        ktf = kt.astype(F32)
        kt2 = jnp.concatenate([jnp.where(row < HD, ktf, 0.0), jnp.where(row < HD, 0.0, ktf)],
                              axis=1).astype(BF16)
        s2 = jnp.dot(qp, kt2, preferred_element_type=F32)
        inter = jnp.dot(qp, st.astype(BF16), preferred_element_type=F32)
        vab = jnp.concatenate([jnp.where(low_half, vp, 1.0), jnp.where(low_half, 1.0, vp)],
                              axis=1).astype(BF16)
        ktw = ktf * jnp.where(row < HD, row_bcast(g["ew"], ja), row_bcast(g["ew"], ja + 1))
        upd = jnp.dot(ktw.astype(BF16), vab, preferred_element_type=F32)
        pre.append((s2, inter, vab, upd, st))

    probs = []
    for (b, d, pr), (s2, _, _, _, _) in zip(items, pre):
        ja = NH * d + 2 * pr
        g = gate[b]
        valid = (col <= row) if d == 0 else (col >= row)
        mu_c = [jnp.broadcast_to(g["cols"][:, j:j + 1], (L, LANE)) for j in (ja, ja + 1)]
        e2 = jnp.concatenate([jnp.where(valid, jnp.exp(row_bcast(g["r"], j) - mc), 0.0)
                              for j, mc in zip((ja, ja + 1), mu_c)], axis=1)
        a2 = jnp.concatenate([jnp.exp(row_bcast(g["m0"], j) - mc) for j, mc in zip((ja, ja + 1), mu_c)],
                             axis=1)
        probs.append(((s2 * e2).astype(BF16), a2))

    for it, ((b, d, pr), (_, inter, vab, upd, st), (p2, a2)) in enumerate(zip(items, pre, probs)):
        out_ref = dirs[d][3]
        ps = slice(pr * LANE, (pr + 1) * LANE)
        ja = NH * d + 2 * pr
        g = gate[b]
        bd_scr[it, 0:L, 0:LANE] = vab[:, 0:LANE]
        bd_scr[it, L:2 * L, LANE:2 * LANE] = vab[:, LANE:2 * LANE]
        intra = jnp.dot(p2, bd_scr[it], preferred_element_type=F32)
        tot = intra + a2 * inter
        tot_a = tot[:, 0:LANE]
        tot_b = tot[:, LANE:2 * LANE]
        num = jnp.where(low_half, tot_a, tot_b)
        den = pltpu.roll(jnp.where(low_half, tot_b, tot_a), HD, 1)
        floor = jnp.where(low_half, jnp.broadcast_to(g["cols"][:, 8 + ja:8 + ja + 1], (L, LANE)),
                          jnp.broadcast_to(g["cols"][:, 8 + ja + 1:8 + ja + 2], (L, LANE)))
        out_ref[b, :, ps] = num / jnp.maximum(jnp.abs(den), floor)
        a_s = jnp.concatenate([row_bcast(g["a_st"], ja), row_bcast(g["a_st"], ja + 1)], axis=1)
        g_s = jnp.concatenate([row_bcast(g["g_st"], ja), row_bcast(g["g_st"], ja + 1)], axis=1)
        s_scr[b, 2 * d + pr] = a_s * st + g_s * jnp.where(own_rows, upd, 0.0)

    @pl.when(c == n_chunks - 1)
    def _():
        mfin_ref[...] = m_scr[...]
        sfin_ref[...] = s_scr[...]


def _mlstm(q, kt, ml, grows, m0, s0):
    B, T, W = q.shape
    L = ML_CHUNK
    nc = T // L

    def specs(ci):
        return [pl.BlockSpec((B, L, W), lambda c: (0, ci(c), 0)),
                pl.BlockSpec((B, W, L), lambda c: (0, 0, ci(c))),
                pl.BlockSpec((B, L, W), lambda c: (0, ci(c), 2)),
                pl.BlockSpec((B, grows.shape[1], L), lambda c: (0, 0, ci(c)))]

    m_spec = pl.BlockSpec(m0.shape, lambda c: (0, 0, 0))
    s_spec = pl.BlockSpec(s0.shape, lambda c: (0, 0, 0, 0))
    dir_args = [q, kt, ml, grows]
    return pl.pallas_call(
        functools.partial(_mlstm_kernel, nc),
        grid=(nc,),
        in_specs=specs(lambda c: c) + specs(lambda c: nc - 1 - c) + [m_spec, s_spec],
        out_specs=[pl.BlockSpec((B, L, W), lambda c: (0, c, 0)),
                   pl.BlockSpec((B, L, W), lambda c: (0, nc - 1 - c, 0)),
                   m_spec, s_spec],
        out_shape=[jax.ShapeDtypeStruct((B, T, W), F32),
                   jax.ShapeDtypeStruct((B, T, W), F32),
                   jax.ShapeDtypeStruct(m0.shape, F32),
                   jax.ShapeDtypeStruct(s0.shape, F32)],
        scratch_shapes=[pltpu.VMEM(m0.shape, F32), pltpu.VMEM(s0.shape, F32),
                        pltpu.VMEM((B * ML_HEADS, 2 * L, 2 * LANE), BF16)],
        compiler_params=_cparams("arbitrary"),
        name="mlstm",
    )(*dir_args, *dir_args, m0, s0)


def _softmax_pv(s_list, v_list):
    m = functools.reduce(jnp.maximum, [jnp.max(s, -1, keepdims=True) for s in s_list])
    ps = [jnp.exp(s - m) for s in s_list]
    l = functools.reduce(jnp.add, [jnp.sum(p, -1, keepdims=True) for p in ps])
    o = functools.reduce(jnp.add, [jnp.dot(p.astype(BF16), v, preferred_element_type=F32)
                                   for p, v in zip(ps, v_list)])
    return o / l


def _na_kernel(n_rows, q_ref, *refs):
    nb = NA_BAND // NA_KBLK_ROWS
    k_blks = refs[0:nb]
    v_blks = refs[nb:2 * nb]
    kc_ref, vc_ref, bt_ref, o_ref, kband, vband = refs[2 * nb:]
    i = pl.program_id(1)
    blk_tok = NA_KBLK_ROWS * GRID_W
    HD = NA_HEAD_DIM
    GW = NA_GROUP * HD
    head_of_lane = lax.broadcasted_iota(jnp.int32, (1, GW), 1) // HD
    for j in range(nb):
        kband[j * blk_tok:(j + 1) * blk_tok, :] = k_blks[j][0]
        vband[j * blk_tok:(j + 1) * blk_tok, :] = v_blks[j][0]
    band_row0 = jnp.clip(i * NA_QROWS - NA_ROWS // 2, 0, n_rows - NA_BAND)
    n_win = NA_ROWS * GRID_W
    zero = jnp.zeros((), BF16)

    def rows_body(t, carry):
        items = []
        for u in range(NA_UNROLL):
            a = t * NA_UNROLL + u
            r = i * NA_QROWS + a
            r0 = jnp.clip(r - NA_ROWS // 2, 0, n_rows - NA_ROWS)
            koff = pl.multiple_of((r0 - band_row0) * GRID_W, GRID_W)
            dr_first = r0 - r + (NA_ROWS - 1)
            qoff = pl.multiple_of(a * GRID_W, GRID_W)
            for gi in range(NA_HEADS // NA_GROUP):
                items.append((koff, dr_first, qoff, gi, slice(gi * GW, (gi + 1) * GW)))
        scores = []
        for koff, _, qoff, _, gs in items:
            qg = q_ref[0, pl.ds(qoff, GRID_W), gs]
            qm = jnp.concatenate([jnp.where(head_of_lane == g, qg, zero) for g in range(NA_GROUP)], axis=0)
            scores.append((_nt_dot(kband[pl.ds(koff, n_win), gs], qm),
                           _nt_dot(kc_ref[0, :, gs], qm)))
        probs = []
        for (_, dr_first, _, gi, _), (s_loc, s_ctx) in zip(items, scores):
            s_loc = s_loc + bt_ref[dr_first, gi]
            m = jnp.maximum(jnp.max(s_loc, 0, keepdims=True), jnp.max(s_ctx, 0, keepdims=True))
            e_loc = jnp.exp(s_loc - m)
            e_ctx = jnp.exp(s_ctx - m)
            inv = 1.0 / (jnp.sum(e_loc, 0, keepdims=True) + jnp.sum(e_ctx, 0, keepdims=True))
            probs.append(((e_loc * inv).astype(BF16), (e_ctx * inv).astype(BF16)))
        for (koff, _, qoff, _, gs), (p_loc, p_ctx) in zip(items, probs):
            res = (_tn_dot(p_loc, vband[pl.ds(koff, n_win), gs])
                   + _tn_dot(p_ctx, vc_ref[0, :, gs]))
            out = res[0:GRID_W]
            for g in range(1, NA_GROUP):
                out = jnp.where(head_of_lane == g, res[g * GRID_W:(g + 1) * GRID_W], out)
            o_ref[0, pl.ds(qoff, GRID_W), gs] = out.astype(BF16)
        return carry

    lax.fori_loop(0, NA_QROWS // NA_UNROLL, rows_body, 0)


def _na_latent(q, k, v, kc, vc, bias_tab):
    B, T, Wd = q.shape
    Lc = kc.shape[1]
    n_rows = T // GRID_W
    nb = NA_BAND // NA_KBLK_ROWS
    q_tok = NA_QROWS * GRID_W
    blk_tok = NA_KBLK_ROWS * GRID_W
    last_blk0 = (n_rows - NA_BAND) // NA_KBLK_ROWS

    def kv_spec(j):
        def idx(b, i):
            first = jnp.clip(i * (NA_QROWS // NA_KBLK_ROWS) - (NA_ROWS // 2) // NA_KBLK_ROWS, 0, last_blk0)
            return (b, first + j, 0)
        return pl.BlockSpec((1, blk_tok, Wd), idx)

    return pl.pallas_call(
        functools.partial(_na_kernel, n_rows),
        grid=(B, n_rows // NA_QROWS),
        in_specs=([pl.BlockSpec((1, q_tok, Wd), lambda b, i: (b, i, 0))]
                  + [kv_spec(j) for j in range(nb)] + [kv_spec(j) for j in range(nb)]
                  + [pl.BlockSpec((1, Lc, Wd), lambda b, i: (b, 0, 0)),
                     pl.BlockSpec((1, Lc, Wd), lambda b, i: (b, 0, 0)),
                     pl.BlockSpec(bias_tab.shape, lambda b, i: (0, 0, 0, 0))]),
        out_specs=pl.BlockSpec((1, q_tok, Wd), lambda b, i: (b, i, 0)),
        out_shape=jax.ShapeDtypeStruct((B, T, Wd), BF16),
        scratch_shapes=[pltpu.VMEM((NA_BAND * GRID_W, Wd), BF16),
                        pltpu.VMEM((NA_BAND * GRID_W, Wd), BF16)],
        compiler_params=_cparams("arbitrary", "arbitrary"),
        name="natten",
    )(q, *([k] * nb), *([v] * nb), kc, vc, bias_tab)


def _ctx_attn_kernel(q_ref, k_ref, v_ref, o_ref):
    HD = NA_HEAD_DIM
    for h in range(NA_HEADS):
        hs = slice(h * HD, (h + 1) * HD)
        s = _nt_dot(q_ref[0, :, hs], k_ref[0, :, hs])
        o_ref[0, :, hs] = _softmax_pv([s], [v_ref[0, :, hs]]).astype(BF16)


def _ctx_attn(q, k, v):
    B, Lc, Wd = q.shape
    spec = pl.BlockSpec((1, Lc, Wd), lambda b: (b, 0, 0))
    return pl.pallas_call(
        _ctx_attn_kernel, grid=(B,), in_specs=[spec, spec, spec], out_specs=spec,
        out_shape=jax.ShapeDtypeStruct((B, Lc, Wd), BF16),
        compiler_params=_cparams("arbitrary"), name="ctx_attn",
    )(q, k, v)


def _na_bias_table(rpb):
    qc = np.arange(GRID_W)[:, None]
    kcol = np.arange(GRID_W)[None, :]
    wstart = np.clip(qc - NA_COLS // 2, 0, GRID_W - NA_COLS)
    col_ok = (kcol >= wstart) & (kcol < wstart + NA_COLS)
    dc = np.clip(kcol - qc, -(NA_COLS - 1), NA_COLS - 1) + NA_COLS - 1
    per_dr = jnp.where(col_ok[None, None], rpb[:, :, dc], MASK_NEG)
    def tab(d):
        n_grp = NA_HEADS // NA_GROUP
        t = per_dr[:, d:d + NA_ROWS].reshape(n_grp, NA_GROUP, NA_ROWS, GRID_W, GRID_W)
        return jnp.transpose(t, (0, 2, 4, 1, 3)).reshape(n_grp, NA_ROWS * GRID_W, NA_GROUP * GRID_W)

    return jnp.stack([tab(d) for d in range(NA_ROWS)]).astype(F32)


def _route(sel, s):
    E = EXP_PER_GROUP
    scores = []
    for g in range(N_GROUPS):
        a, b, c, d = sel[E * g:E * (g + 1)]
        scores.append(functools.reduce(jnp.maximum, [a + b, a + c, a + d, b + c, b + d, c + d]))
    best = jnp.zeros_like(scores[0], dtype=jnp.int32)
    best_score = scores[0]
    for g in range(1, N_GROUPS):
        upd = scores[g] > best_score
        best = jnp.where(upd, g, best)
        best_score = jnp.where(upd, scores[g], best_score)

    def pick(rows, j):
        out = rows[j]
        for g in range(1, N_GROUPS):
            out = jnp.where(best == g, rows[E * g + j], out)
        return out

    v = [pick(sel, j) for j in range(E)]
    sv = [pick(s, j) for j in range(E)]
    i1 = jnp.zeros_like(best)
    m1 = v[0]
    for j in range(1, E):
        upd = v[j] > m1
        i1 = jnp.where(upd, j, i1)
        m1 = jnp.where(upd, v[j], m1)
    i2 = jnp.where(i1 == 0, 1, 0)
    m2 = jnp.where(i1 == 0, v[1], v[0])
    for j in range(1, E):
        upd = (i1 != j) & (v[j] > m2)
        i2 = jnp.where(upd, j, i2)
        m2 = jnp.where(upd, v[j], m2)

    def at(rows, idx):
        out = rows[0]
        for j in range(1, E):
            out = jnp.where(idx == j, rows[j], out)
        return out

    s1 = at(sv, i1)
    s2 = at(sv, i2)
    tot = s1 + s2
    return best * E + i1, best * E + i2, s1 / tot, s2 / tot


def _out_proj_kernel(ygm_ref, hf_ref, hb_ref, o_ref, yna_ref, x_ref, mod_ref, w_ref, mlg_ref, n2g_ref,
                     rw_ref, rb_ref, xn_ref, h2_ref, re_ref, rwt_ref):
    hs = hf_ref[0] + hb_ref[0]
    hsq = hs * hs
    lane = lax.broadcasted_iota(jnp.int32, (1, ML_WIDTH), 1)
    scale = jnp.zeros_like(hs)
    for h in range(ML_HEADS):
        in_head = (lane >= h * ML_HEAD_DIM) & (lane < (h + 1) * ML_HEAD_DIM)
        ms = jnp.sum(jnp.where(in_head, hsq, 0.0), -1, keepdims=True) * (1.0 / ML_HEAD_DIM)
        scale = jnp.where(in_head, lax.rsqrt(ms + EPS), scale)
    yml = hs * scale * mlg_ref[...] * jax.nn.sigmoid(o_ref[0])
    o1 = GM_WIDTH
    o2 = GM_WIDTH + ML_WIDTH
    acc = (jnp.dot(ygm_ref[0], w_ref[0:o1, :], preferred_element_type=F32)
           + jnp.dot(yml.astype(BF16), w_ref[o1:o2, :], preferred_element_type=F32)
           + jnp.dot(yna_ref[0], w_ref[o2:, :], preferred_element_type=F32))
    xn = x_ref[0] + mod_ref[0, 0:1, :] * acc
    xn_ref[0] = xn
    h2 = xn * lax.rsqrt(jnp.mean(xn * xn, -1, keepdims=True) + EPS) * n2g_ref[...]
    h2 = h2 * (1.0 + mod_ref[0, 2:3, :]) + mod_ref[0, 1:2, :]
    h2_ref[0] = h2
    h_hi = h2.astype(BF16)
    h_lo = (h2 - h_hi.astype(F32)).astype(BF16)
    rw_hi = rw_ref[0]
    logits = _nt_dot(rw_hi, h_hi) + _nt_dot(rw_hi, h_lo) + _nt_dot(rw_ref[1], h_hi)
    s = jax.nn.sigmoid(logits)
    sel = s + rb_ref[...]
    rows = lambda m: [m[e:e + 1, :] for e in range(N_EXPERTS)]
    e1, e2, w1, w2 = _route(rows(sel), rows(s))
    re_ref[0, 0:1, :] = e1
    re_ref[0, 1:2, :] = e2
    rwt_ref[0, 0:1, :] = w1
    rwt_ref[0, 1:2, :] = w2


def _out_proj(ygm, hf, hb, ml, yna, x, mod, w_out, ml_g, n2_g, router_wt, router_b):
    G, R, D = x.shape
    tm = min(R, 512)
    row = lambda b, i: (b, i, 0)
    whole2 = lambda shape: pl.BlockSpec(shape, lambda b, i: (0, 0))
    return pl.pallas_call(
        _out_proj_kernel,
        grid=(G, R // tm),
        in_specs=[pl.BlockSpec((1, tm, GM_WIDTH), row),
                  pl.BlockSpec((1, tm, ML_WIDTH), row),
                  pl.BlockSpec((1, tm, ML_WIDTH), row),
                  pl.BlockSpec((1, tm, ML_WIDTH), lambda b, i: (b, i, 3)),
                  pl.BlockSpec((1, tm, NA_WIDTH), row),
                  pl.BlockSpec((1, tm, D), row),
                  pl.BlockSpec((1, 3, D), lambda b, i: (b, 0, 0)),
                  whole2((D, D)), whole2((1, ML_WIDTH)), whole2((1, D)),
                  pl.BlockSpec((2, N_EXPERTS, D), lambda b, i: (0, 0, 0)), whole2((N_EXPERTS, 1))],
        out_specs=[pl.BlockSpec((1, tm, D), row), pl.BlockSpec((1, tm, D), row),
                   pl.BlockSpec((1, 2, tm), lambda b, i: (b, 0, i)),
                   pl.BlockSpec((1, 2, tm), lambda b, i: (b, 0, i))],
        out_shape=[jax.ShapeDtypeStruct((G, R, D), F32), jax.ShapeDtypeStruct((G, R, D), F32),
                   jax.ShapeDtypeStruct((G, 2, R), jnp.int32), jax.ShapeDtypeStruct((G, 2, R), F32)],
        compiler_params=_cparams("arbitrary", "arbitrary"),
        name="out_proj",
    )(ygm, hf, hb, ml, yna, x, mod, w_out, ml_g, n2_g, router_wt, router_b)


def _experts_kernel(te_ref, nu_ref, src_cur, src_nxt, h_hbm, wg_ref, wu_ref, wd_ref, o_ref,
                    acc_ref, wgb, wub, wdb, xbuf, sem):
    i = pl.program_id(0)
    n_used = nu_ref[0]
    used = i < n_used
    slot = lax.rem(i, 2)
    new_expert = (i == 0) | (te_ref[i] != te_ref[jnp.maximum(i - 1, 0)])

    def row_copy(src_ref, s, r):
        return pltpu.make_async_copy(h_hbm.at[pl.ds(src_ref[0, 0, r], 1), :], xbuf.at[s, pl.ds(r, 1), :],
                                     sem.at[s])

    def tile_wait(s):
        pltpu.make_async_copy(xbuf.at[s], xbuf.at[s], sem.at[s]).wait()

    @pl.when((i == 0) & used)
    def _():
        def body(r, carry):
            row_copy(src_cur, 0, r).start()
            return carry
        lax.fori_loop(0, MOE_TILE, body, 0, unroll=8)

    @pl.when(used & new_expert)
    def _():
        def cast_rows(c, carry):
            rs = pl.ds(pl.multiple_of(c * LANE, LANE), LANE)
            wgb[rs, :] = wg_ref[0, 0, rs, :].astype(BF16)
            wub[rs, :] = wu_ref[0, 0, rs, :].astype(BF16)
            wdb[rs, :] = wd_ref[0, 0, rs, :].astype(BF16)
            return carry
        lax.fori_loop(0, D_MODEL // LANE, cast_rows, 0)

    @pl.when(used)
    def _():
        tile_wait(slot)
        x = xbuf[slot].astype(BF16)
        nxt = 1 - slot
        n_chunks = D_EXPERT // EXPERT_CHUNK
        per = -(-MOE_TILE // (3 * n_chunks))
        issued = [0]

        def request_rows():
            for r in range(issued[0], min(issued[0] + per, MOE_TILE)):
                row_copy(src_nxt, nxt, r).start(priority=r % 2)
            issued[0] += per

        for j in range(n_chunks):
            cs = slice(j * EXPERT_CHUNK, (j + 1) * EXPERT_CHUNK)
            request_rows()
            g = jnp.dot(x, wgb[:, cs], preferred_element_type=F32)
            request_rows()
            u = jnp.dot(x, wub[:, cs], preferred_element_type=F32)
            a = (g * jax.nn.sigmoid(g) * u).astype(BF16)
            request_rows()
            y = jnp.dot(a, wdb[cs, :], preferred_element_type=F32)
            if j == 0:
                acc_ref[...] = y
            else:
                acc_ref[...] += y
        o_ref[...] = acc_ref[...].astype(BF16)

        @pl.when(i + 1 >= n_used)
        def _():
            tile_wait(nxt)

    @pl.when(jnp.logical_not(used))
    def _():
        o_ref[...] = jnp.zeros_like(o_ref)


def _experts(l, tile_expert, n_used, src, h_all, wg, wu, wd):
    D = h_all.shape[1]
    n_tiles = src.shape[0] // MOE_TILE
    src3 = src.reshape(n_tiles, 1, MOE_TILE)
    wspec = lambda r, c: pl.BlockSpec((1, 1, r, c), lambda i, te, nu: (l, te[i], 0, 0))
    src_spec = lambda ahead: pl.BlockSpec(
        (1, 1, MOE_TILE), lambda i, te, nu: (jnp.minimum(i + ahead, n_tiles - 1), 0, 0),
        memory_space=pltpu.SMEM)
    return pl.pallas_call(
        _experts_kernel,
        grid_spec=pltpu.PrefetchScalarGridSpec(
            num_scalar_prefetch=2,
            grid=(n_tiles,),
            in_specs=[src_spec(0), src_spec(1), pl.BlockSpec(memory_space=pl.ANY),
                      wspec(D, D_EXPERT), wspec(D, D_EXPERT), wspec(D_EXPERT, D)],
            out_specs=pl.BlockSpec((MOE_TILE, D), lambda i, te, nu: (i, 0)),
            scratch_shapes=[pltpu.VMEM((MOE_TILE, D), F32), pltpu.VMEM((D, D_EXPERT), BF16),
                            pltpu.VMEM((D, D_EXPERT), BF16), pltpu.VMEM((D_EXPERT, D), BF16),
                            pltpu.VMEM((2, MOE_TILE, D), F32), pltpu.SemaphoreType.DMA((2,))]),
        out_shape=jax.ShapeDtypeStruct((n_tiles * MOE_TILE, D), BF16),
        compiler_params=_cparams("arbitrary"),
        name="experts",
    )(tile_expert, n_used, src3, src3, h_all, wg, wu, wd)


def _dispatch(e_idx):
    n_tok = e_idx.shape[0]
    n_asg = 2 * n_tok
    n_tiles = -(-n_asg // MOE_TILE) + N_EXPERTS
    flat_e = e_idx.reshape(-1)
    blk = MOE_TILE
    assert n_asg % blk == 0
    onehot = (flat_e[:, None] == jnp.arange(N_EXPERTS, dtype=jnp.int32)[None, :]).astype(BF16)
    onehot = onehot.reshape(n_asg // blk, blk, N_EXPERTS)
    tri = (jnp.arange(blk)[:, None] >= jnp.arange(blk)[None, :]).astype(BF16)
    within = jnp.einsum("ij,bje->bie", tri, onehot, preferred_element_type=F32)
    blk_tot = within[:, -1, :]
    blk_off = jnp.cumsum(blk_tot, axis=0) - blk_tot
    csum = within + blk_off[:, None, :]
    rank = (jnp.sum(csum * onehot.astype(F32), -1) - 1.0).astype(jnp.int32).reshape(-1)
    counts = (blk_off[-1] + blk_tot[-1]).astype(jnp.int32)
    tiles_e = (counts + MOE_TILE - 1) // MOE_TILE
    tile_end = jnp.cumsum(tiles_e)
    tile_start = tile_end - tiles_e
    dest = tile_start[flat_e] * MOE_TILE + rank
    n_used = tile_end[-1]
    tid = jnp.arange(n_tiles, dtype=jnp.int32)
    te = jnp.minimum(jnp.searchsorted(tile_end, tid, side="right"), N_EXPERTS - 1).astype(jnp.int32)
    te = jnp.where(tid < n_used, te, te[jnp.maximum(n_used - 1, 0)])
    flat_t = jnp.arange(n_asg, dtype=jnp.int32) // 2
    src = jnp.zeros((n_tiles * MOE_TILE,), jnp.int32).at[dest].set(
        flat_t, unique_indices=True, mode="promise_in_bounds")
    return dest.reshape(n_tok, 2), src, te, n_used.reshape(1).astype(jnp.int32)


def _combine_kernel(final, xn_ref, y0_ref, y1_ref, w_ref, g2_ref, fg_ref, o_ref):
    w = w_ref[0]
    f = y0_ref[0].astype(F32) * w[:, 0:1] + y1_ref[0].astype(F32) * w[:, 1:2]
    x = xn_ref[0] + g2_ref[0] * f
    if final:
        x = x * lax.rsqrt(jnp.mean(x * x, -1, keepdims=True) + EPS) * fg_ref[...]
    o_ref[0] = x


def _combine(xn, y0, y1, wt, g2, final_g, final):
    G, R, D = xn.shape
    tm = min(R, 512)
    row = lambda b, i: (b, i, 0)
    return pl.pallas_call(
        functools.partial(_combine_kernel, final),
        grid=(G, R // tm),
        in_specs=[pl.BlockSpec((1, tm, D), row), pl.BlockSpec((1, tm, D), row), pl.BlockSpec((1, tm, D), row),
                  pl.BlockSpec((1, tm, 2), row),
                  pl.BlockSpec((1, 1, D), lambda b, i: (b, 0, 0)),
                  pl.BlockSpec((1, D), lambda b, i: (0, 0))],
        out_specs=pl.BlockSpec((1, tm, D), row),
        out_shape=jax.ShapeDtypeStruct((G, R, D), F32),
        compiler_params=_cparams("arbitrary", "arbitrary"),
        name="combine_final" if final else "combine",
    )(xn, y0, y1, wt, g2, final_g)


def _rope_tables(T):
    lane = np.arange(LANE)
    half = ML_HEAD_DIM // 4
    inv = jnp.tile(ROPE_BASE ** (-jnp.arange(half, dtype=F32) / half), LANE // half)
    t = jnp.arange(T)
    pos = jnp.where(((lane // (2 * half)) % 2 == 0)[None, :], (t // GRID_W)[:, None], (t % GRID_W)[:, None])
    ang = pos.astype(F32) * inv[None, :]
    sign = np.where((lane // half) % 2 == 0, -1.0, 1.0).astype(np.float32)
    return jnp.cos(ang), jnp.sin(ang) * sign[None, :]


def _reorder_w_in(w_in):
    pad = jnp.zeros(w_in.shape[:2] + (LANE - 4 * ML_HEADS,), w_in.dtype)
    return jnp.concatenate([w_in[..., :OFF_GATES], w_in[..., OFF_NA:], w_in[..., OFF_GATES:OFF_NA], pad],
                           axis=-1).astype(BF16)


def _mixers(l, p, pc, prm, need_ctx):
    ygm, ml, gt, q, k, v = p
    ygmc, mlc, gtc, qc, kc, vc = pc
    B = ml.shape[0]
    conv = (prm["conv_w"][l], prm["conv_b"][l])
    nh = ML_HEADS
    tr = lambda g: jnp.transpose(
        jnp.concatenate([g[..., 0:nh], g[..., 2 * nh:3 * nh], g[..., nh:2 * nh], g[..., 3 * nh:4 * nh]], -1),
        (0, 2, 1))
    m_zero = jnp.zeros((B, 2 * nh, LANE), F32)
    s_zero = jnp.zeros((B, nh, ML_CHUNK, 2 * LANE), F32)
    qmc, kmc, grc = _mlstm_prep(mlc, prm["cos_c"], prm["sin_c"], *conv, tr(gtc), prm["gbias"][l])
    hfc, hbc, m_st, s_st = _mlstm(qmc, kmc, mlc, grc, m_zero, s_zero)
    qm, km, gr = _mlstm_prep(ml, prm["cos_l"], prm["sin_l"], *conv, tr(gt), prm["gbias"][l])
    hf, hb, _, _ = _mlstm(qm, km, ml, gr, m_st, s_st)
    y_na = _na_latent(q, k, v, kc, vc, prm["na_tab"][l])
    y = (ygm, hf, hb, ml, y_na)
    if not need_ctx:
        return y, None
    return y, (ygmc, hfc, hbc, mlc, _ctx_attn(qc, kc, vc))


def _rows(a, idx):
    return a.at[idx].get(mode="promise_in_bounds")


def _moe(l, h2_all, e_all, prm):
    dest, src, te, n_used = _dispatch(e_all)
    yb = _experts(l, te, n_used, src, h2_all, prm["wg"], prm["wu"], prm["wd"])
    return yb, dest


def kernel(x, c, ctx, c_ctx, ada_w, ada_b, norm1_g, norm2_g, w_in, w_out, gm_ws, gm_bs, gm_norm_g,
           ml_conv_w, ml_conv_b, ml_gate_b, ml_norm_g, na_rpb, router_w, router_b,
           moe_w_gate, moe_w_up, moe_w_down, final_g):
    B, T, D = x.shape
    Lc = ctx.shape[1]
    cos_l, sin_l = _rope_tables(T)
    prm = dict(
        gm_ws=gm_ws.astype(BF16),
        gm_bs_full=jnp.repeat(jnp.transpose(gm_bs, (0, 2, 1)), GM_HEAD_DIM, axis=-1),
        gm_g=gm_norm_g[:, None, :],
        conv_w=ml_conv_w, conv_b=ml_conv_b[:, None, :],
        gbias=ml_gate_b[:, jnp.array([0, 2, 1, 3])].reshape(DEPTH, 4 * ML_HEADS, 1),
        cos_l=cos_l, sin_l=sin_l,
        cos_c=jnp.ones((Lc, LANE), F32), sin_c=jnp.zeros((Lc, LANE), F32),
        na_tab=jnp.stack([_na_bias_table(na_rpb[l]) for l in range(DEPTH)]),
        wg=moe_w_gate, wu=moe_w_up, wd=moe_w_down,
    )
    w_in_r = _reorder_w_in(w_in)
    w_out_b = w_out.astype(BF16)
    rw_t = jnp.transpose(router_w)
    rw_hi = rw_t.astype(BF16)
    router_wt = jnp.stack([rw_hi, (rw_t - rw_hi.astype(F32)).astype(BF16)])
    router_bc = router_b[:, None]

    cs = jnp.concatenate([c, c_ctx[None, :], jnp.zeros((8 - B - 1, D), F32)], axis=0)
    mods = _ada_all(cs, ada_w, ada_b).reshape(DEPTH, 8, 6, D)

    xc = ctx
    for l in range(DEPTH):
        need_ctx = l < DEPTH - 1
        mod_l = mods[l, :B]
        mod_c = jnp.broadcast_to(mods[l, B:B + 1], (B, 6, D))
        n1 = norm1_g[l][None, :]
        gm_args = (prm["gm_ws"][l], prm["gm_bs_full"][l], prm["gm_g"][l])
        p = _in_proj(x, mod_l[:, 0:2], n1, w_in_r[l], *gm_args)
        pc = _in_proj(xc, mod_c[:, 0:2], n1, w_in_r[l], *gm_args)
        y, yc = _mixers(l, p, pc, prm, need_ctx)
        op_args = (w_out_b[l], ml_norm_g[l][None, :], norm2_g[l][None, :], router_wt, router_bc)
        xn, h2, re, rw = _out_proj(*y, x, mod_l[:, 2:5], *op_args)
        h2_all = h2.reshape(B * T, D)
        e_all = jnp.transpose(re, (0, 2, 1)).reshape(B * T, 2)
        if need_ctx:
            xnc, h2c, rec, rwc = _out_proj(*yc, xc, mod_c[:, 2:5], *op_args)
            h2_all = jnp.concatenate([h2_all, h2c.reshape(B * Lc, D)], axis=0)
            e_all = jnp.concatenate([e_all, jnp.transpose(rec, (0, 2, 1)).reshape(B * Lc, 2)], axis=0)
        yb, dest = _moe(l, h2_all, e_all, prm)
        final = l == DEPTH - 1
        fg = final_g[None, :]
        n_lat = B * T
        picked = lambda lo, hi, k, L: _rows(yb, dest[lo:hi, k]).reshape(B, L, D)
        x = _combine(xn, picked(0, n_lat, 0, T), picked(0, n_lat, 1, T),
                     jnp.transpose(rw, (0, 2, 1)), mod_l[:, 5:6], fg, final)
        if need_ctx:
            xc = _combine(xnc, picked(n_lat, None, 0, Lc), picked(n_lat, None, 1, Lc),
                          jnp.transpose(rwc, (0, 2, 1)), mod_c[:, 5:6], fg, False)
    return x
```

```python
import functools

import jax
import jax.numpy as jnp
import numpy as np
from jax import lax
from jax.experimental import pallas as pl
from jax.experimental.pallas import tpu as pltpu

F32 = jnp.float32
BF16 = jnp.bfloat16

D_MODEL = 1024
DEPTH = 4
GRID_W = 64
EPS = 1e-6

GM_HEADS = 4
GM_WIDTH = D_MODEL // 4
GM_HEAD_DIM = GM_WIDTH // GM_HEADS
GM_CHUNK = 128

ML_HEADS = 4
ML_WIDTH = D_MODEL // 4
ML_HEAD_DIM = ML_WIDTH // ML_HEADS
ML_CHUNK = 128

NA_HEADS = 8
NA_WIDTH = D_MODEL // 2
NA_HEAD_DIM = NA_WIDTH // NA_HEADS
NA_ROWS = 8
NA_COLS = 16

ROPE_BASE = 10000.0

OFF_ML = 2 * GM_WIDTH
OFF_GATES = OFF_ML + 4 * ML_WIDTH
OFF_NA = OFF_GATES + 4 * ML_HEADS
N_IN = OFF_NA + 3 * NA_WIDTH

N_EXPERTS = 16
N_GROUPS = 4
EXP_PER_GROUP = N_EXPERTS // N_GROUPS
D_EXPERT = D_MODEL

LANE = 128
SEG_GM = (0, 2 * GM_WIDTH)
SEG_ML = (SEG_GM[1], SEG_GM[1] + 4 * ML_WIDTH)
SEG_Q = (SEG_ML[1], SEG_ML[1] + NA_WIDTH)
SEG_K = (SEG_Q[1], SEG_Q[1] + NA_WIDTH)
SEG_V = (SEG_K[1], SEG_K[1] + NA_WIDTH)
SEG_GATES = (SEG_V[1], SEG_V[1] + LANE)
N_IN_PAD = SEG_GATES[1]

NA_QROWS = 8
NA_GROUP = 4
NA_UNROLL = 4
NA_BAND = 16
NA_KBLK_ROWS = 4
MOE_TILE = 512
EXPERT_CHUNK = 256
MASK_NEG = -1e30
N_GATE_ROWS = 6
VMEM_LIMIT = 56 * 2 ** 20


def _cparams(*sem):
    return pltpu.CompilerParams(dimension_semantics=sem, vmem_limit_bytes=VMEM_LIMIT)


def _nt_dot(a, b, precision=None):
    return lax.dot_general(a, b, (((1,), (1,)), ((), ())), precision=precision,
                           preferred_element_type=F32)


def _tn_dot(a, b):
    return lax.dot_general(a, b, (((0,), (0,)), ((), ())), preferred_element_type=F32)


def _ada_kernel(c_ref, w_ref, b_ref, o_ref):
    c = c_ref[...]
    sc = c * jax.nn.sigmoid(c)
    o_ref[0] = jnp.dot(sc, w_ref[0], preferred_element_type=F32) + b_ref[0]


def _ada_all(cs, ada_w, ada_b):
    n_out = ada_w.shape[-1] // D_MODEL
    return pl.pallas_call(
        _ada_kernel,
        grid=(DEPTH, n_out),
        in_specs=[pl.BlockSpec((8, D_MODEL), lambda l, j: (0, 0)),
                  pl.BlockSpec((1, D_MODEL, D_MODEL), lambda l, j: (l, 0, j)),
                  pl.BlockSpec((1, 1, D_MODEL), lambda l, j: (l, 0, j))],
        out_specs=pl.BlockSpec((1, 8, D_MODEL), lambda l, j: (l, 0, j)),
        out_shape=jax.ShapeDtypeStruct((DEPTH, 8, n_out * D_MODEL), F32),
        compiler_params=_cparams("arbitrary", "arbitrary"),
        name="adaln",
    )(cs, ada_w, ada_b.reshape(DEPTH, 1, -1))


def _gmlp_chunk(z_pre, ws_ref, bs, g):
    lane = lax.broadcasted_iota(jnp.int32, (1, GM_WIDTH), 1)
    z = jax.nn.gelu(z_pre)
    u = z[:, :GM_WIDTH]
    v = z[:, GM_WIDTH:]
    mu = jnp.mean(v, -1, keepdims=True)
    vc = v - mu
    var = jnp.mean(vc * vc, -1, keepdims=True)
    vn = (vc * lax.rsqrt(var + EPS) * g).astype(BF16)
    sv = jnp.zeros((GM_CHUNK, GM_WIDTH), F32)
    for h in range(GM_HEADS):
        full = jnp.dot(ws_ref[h], vn, preferred_element_type=F32)
        in_head = (lane >= h * GM_HEAD_DIM) & (lane < (h + 1) * GM_HEAD_DIM)
        sv = jnp.where(in_head, full, sv)
    return u * (sv + bs)


def _in_proj_kernel(x_ref, mod_ref, g_ref, w_ref, ws_ref, bs_ref, gmg_ref,
                    ygm_ref, ml_ref, gt_ref, q_ref, k_ref, v_ref):
    x = x_ref[0]
    h = x * lax.rsqrt(jnp.mean(x * x, -1, keepdims=True) + EPS) * g_ref[...]
    h = h * (1.0 + mod_ref[0, 1:2, :]) + mod_ref[0, 0:1, :]
    hb = h.astype(BF16)

    def seg(s):
        return jnp.dot(hb, w_ref[:, s[0]:s[1]], preferred_element_type=F32)

    gm = seg(SEG_GM)
    ml_ref[0] = seg(SEG_ML)
    for c in range(gm.shape[0] // GM_CHUNK):
        rows = slice(c * GM_CHUNK, (c + 1) * GM_CHUNK)
        ygm_ref[0, rows, :] = _gmlp_chunk(gm[rows], ws_ref, bs_ref[...], gmg_ref[...]).astype(BF16)
    gt_ref[0] = seg(SEG_GATES)
    q_ref[0] = (seg(SEG_Q) * (NA_HEAD_DIM ** -0.5)).astype(BF16)
    k_ref[0] = seg(SEG_K).astype(BF16)
    v_ref[0] = seg(SEG_V).astype(BF16)


def _in_proj(x, mod, g, w, gm_ws, gm_bs_full, gm_g):
    G, R, D = x.shape
    tm = min(R, 512)
    row = lambda b, i: (b, i, 0)
    whole = lambda shape: pl.BlockSpec(shape, lambda b, i: (0,) * len(shape))
    widths = (GM_WIDTH, SEG_ML[1] - SEG_ML[0], LANE, NA_WIDTH, NA_WIDTH, NA_WIDTH)
    dtypes = (BF16, F32, F32, BF16, BF16, BF16)
    return pl.pallas_call(
        _in_proj_kernel,
        grid=(G, R // tm),
        in_specs=[pl.BlockSpec((1, tm, D), row),
                  pl.BlockSpec((1, 2, D), lambda b, i: (b, 0, 0)),
                  whole((1, D)), whole((D, N_IN_PAD)),
                  whole((GM_HEADS, GM_CHUNK, GM_CHUNK)), whole((GM_CHUNK, GM_WIDTH)), whole((1, GM_WIDTH))],
        out_specs=[pl.BlockSpec((1, tm, n), row) for n in widths],
        out_shape=[jax.ShapeDtypeStruct((G, R, n), dt) for n, dt in zip(widths, dtypes)],
        compiler_params=_cparams("arbitrary", "arbitrary"),
        name="in_proj",
    )(x, mod, g, w, gm_ws, gm_bs_full, gm_g)


def _log_sigmoid(x):
    return jnp.minimum(x, 0.0) - jnp.log1p(jnp.exp(-jnp.abs(x)))


def _rope(x, cos, sin_signed):
    lane = lax.broadcasted_iota(jnp.int32, (1, LANE), 1)
    first_half = (lane & 16) == 0
    partner = jnp.where(first_half, pltpu.roll(x, LANE - 16, 1), pltpu.roll(x, 16, 1))
    return x * cos + partner * sin_signed


def _chunk_scan(x, op, fill, reverse):
    pos = lax.broadcasted_iota(jnp.int32, (1, x.shape[1]), 1) % ML_CHUNK
    n = x.shape[1]
    sh = 1
    while sh < ML_CHUNK:
        if reverse:
            shifted = jnp.where(pos < ML_CHUNK - sh, pltpu.roll(x, n - sh, 1), fill)
        else:
            shifted = jnp.where(pos >= sh, pltpu.roll(x, sh, 1), fill)
        x = op(x, shifted)
        sh *= 2
    return x


def _mlstm_prep_kernel(n_blocks, x_ref, hp_ref, hn_ref, cos_ref, sin_ref, cw_ref, cb_ref, g_ref, gbias_ref,
                       q_ref, k_ref, go_ref):
    nh2 = 2 * ML_HEADS
    g = g_ref[0] + gbias_ref[...]
    is_fwd = lax.broadcasted_iota(jnp.int32, (nh2, 1), 0) < ML_HEADS
    li = g[0:nh2]
    lf = _log_sigmoid(g[nh2:])
    ps = _chunk_scan(lf, jnp.add, 0.0, False)
    ss = _chunk_scan(lf, jnp.add, 0.0, True)
    bcum = jnp.where(is_fwd, ps, ss)
    r = li - bcum
    pm = _chunk_scan(r, jnp.maximum, MASK_NEG, False)
    sm = _chunk_scan(r, jnp.maximum, MASK_NEG, True)
    rmax = jnp.maximum(pm, sm)
    groups = (bcum, r, jnp.where(is_fwd, pm, sm), jnp.exp(r - rmax), rmax, ps + ss - lf)
    for j, val in enumerate(groups):
        go_ref[0, nh2 * j:nh2 * (j + 1), :] = val

    i = pl.program_id(1)
    W = ML_WIDTH
    x = x_ref[0]
    tr = x.shape[0]
    rid = lax.broadcasted_iota(jnp.int32, (tr, 1), 0)
    i_row = jnp.zeros((1, 2 * W), jnp.int32) + i
    prev = jnp.where(i_row == 0, 0.0, hp_ref[0, 7:8, :])
    nxt = jnp.where(i_row == n_blocks - 1, 0.0, hn_ref[0, 0:1, :])
    xm1 = jnp.where(rid == 0, prev, pltpu.roll(x, 1, 0))
    xp1 = jnp.where(rid == tr - 1, nxt, pltpu.roll(x, tr - 1, 0))
    y = xm1 * cw_ref[0:1, :] + x * cw_ref[1:2, :] + xp1 * cw_ref[2:3, :] + cb_ref[...]
    y = y * jax.nn.sigmoid(y)
    cos = cos_ref[...]
    sin = sin_ref[...]
    parts = [_rope(y[:, j * LANE:(j + 1) * LANE], cos, sin) for j in range(2 * W // LANE)]
    q_ref[0] = (jnp.concatenate(parts[:W // LANE], axis=1) * (ML_HEAD_DIM ** -0.5)).astype(BF16)
    for j in range(W // LANE):
        k_ref[0, j * LANE:(j + 1) * LANE, :] = parts[W // LANE + j].T.astype(BF16)


def _mlstm_prep(ml, cos, sin, conv_w, conv_b, gates_t, gbias):
    B, T, _ = ml.shape
    n_g = gates_t.shape[1]
    W = ML_WIDTH
    tr = min(T, 1024)
    nb = T // tr
    halo_per_blk = tr // 8
    n_hblk = T // 8
    whole = lambda shape: pl.BlockSpec(shape, lambda b, i: (0, 0))
    return pl.pallas_call(
        functools.partial(_mlstm_prep_kernel, nb),
        grid=(B, nb),
        in_specs=[pl.BlockSpec((1, tr, 2 * W), lambda b, i: (b, i, 0)),
                  pl.BlockSpec((1, 8, 2 * W), lambda b, i: (b, jnp.maximum(i * halo_per_blk - 1, 0), 0)),
                  pl.BlockSpec((1, 8, 2 * W),
                               lambda b, i: (b, jnp.minimum((i + 1) * halo_per_blk, n_hblk - 1), 0)),
                  pl.BlockSpec((tr, LANE), lambda b, i: (i, 0)),
                  pl.BlockSpec((tr, LANE), lambda b, i: (i, 0)),
                  whole((3, 2 * W)), whole((1, 2 * W)),
                  pl.BlockSpec((1, n_g, tr), lambda b, i: (b, 0, i)), whole((n_g, 1))],
        out_specs=[pl.BlockSpec((1, tr, W), lambda b, i: (b, i, 0)),
                   pl.BlockSpec((1, W, tr), lambda b, i: (b, 0, i)),
                   pl.BlockSpec((1, 2 * ML_HEADS * N_GATE_ROWS, tr), lambda b, i: (b, 0, i))],
        out_shape=[jax.ShapeDtypeStruct((B, T, W), BF16), jax.ShapeDtypeStruct((B, W, T), BF16),
                   jax.ShapeDtypeStruct((B, 2 * ML_HEADS * N_GATE_ROWS, T), F32)],
        compiler_params=_cparams("arbitrary", "arbitrary"),
        name="mlstm_prep",
    )(ml, ml, ml, cos, sin, conv_w, conv_b, gates_t, gbias)


def _mlstm_kernel(n_chunks, qf_ref, ktf_ref, vf_ref, gf_ref, qb_ref, ktb_ref, vb_ref, gb_ref,
                  m0_ref, s0_ref, hf_ref, hb_ref, mfin_ref, sfin_ref, m_scr, s_scr, bd_scr):
    c = pl.program_id(0)
    L = ML_CHUNK
    HD = ML_HEAD_DIM
    NH = ML_HEADS
    n_batch = m_scr.shape[0]

    @pl.when(c == 0)
    def _():
        m_scr[...] = m0_ref[...]
        s_scr[...] = s0_ref[...]
        bd_scr[...] = jnp.zeros_like(bd_scr)

    def row_bcast(x, j):
        return jnp.broadcast_to(x[j:j + 1, :], (L, LANE))

    row = lax.broadcasted_iota(jnp.int32, (L, L), 0)
    col = lax.broadcasted_iota(jnp.int32, (L, L), 1)
    low_half = lax.broadcasted_iota(jnp.int32, (1, LANE), 1) < HD
    own_rows = jnp.concatenate([row < HD, row >= HD], axis=1)
    is_fwd = lax.broadcasted_iota(jnp.int32, (2 * NH, 1), 0) < NH

    gate = []
    for b in range(n_batch):
        gq = lambda i: jnp.where(is_fwd, gf_ref[b, 2 * NH * i:2 * NH * (i + 1), :],
                                 gb_ref[b, 2 * NH * i:2 * NH * (i + 1), :])
        bcum, r, rcmax, ew, rmax, b_last = (gq(i) for i in range(N_GATE_ROWS))
        m0 = m_scr[b]
        mu = jnp.maximum(m0, rcmax)
        emt = jnp.exp(-(bcum + mu))
        m_last = jnp.maximum(m0, rmax)
        m_scr[b] = b_last + m_last
        stack = jnp.concatenate([mu, emt, jnp.zeros((L - 4 * NH, LANE), F32)], axis=0)
        gate.append(dict(r=r, ew=ew, m0=m0, a_st=jnp.exp(m0 - m_last), g_st=jnp.exp(rmax - m_last),
                         cols=stack.T))

    dirs = ((qf_ref, ktf_ref, vf_ref, hf_ref), (qb_ref, ktb_ref, vb_ref, hb_ref))
    items = [(b, d, pr) for b in range(n_batch) for d in range(2) for pr in range(NH // 2)]

    pre = []
    for b, d, pr in items:
        q_ref, kt_ref, v_ref, _ = dirs[d]
        ps = slice(pr * LANE, (pr + 1) * LANE)
        ja = NH * d + 2 * pr
        g = gate[b]
        qp = q_ref[b, :, ps]
        kt = kt_ref[b, ps, :]
        vp = v_ref[b, :, ps]
        st = s_scr[b, 2 * d + pr]
        ktf = kt.astype(F32)
        kt2 = jnp.concatenate([jnp.where(row < HD, ktf, 0.0), jnp.where(row < HD, 0.0, ktf)],
                              axis=1).astype(BF16)
        s2 = jnp.dot(qp, kt2, preferred_element_type=F32)
        inter = jnp.dot(qp, st.astype(BF16), preferred_element_type=F32)
        vab = jnp.concatenate([jnp.where(low_half, vp, 1.0), jnp.where(low_half, 1.0, vp)],
                              axis=1).astype(BF16)
        ktw = ktf * jnp.where(row < HD, row_bcast(g["ew"], ja), row_bcast(g["ew"], ja + 1))
        upd = jnp.dot(ktw.astype(BF16), vab, preferred_element_type=F32)
        pre.append((s2, inter, vab, upd, st))

    probs = []
    for (b, d, pr), (s2, _, _, _, _) in zip(items, pre):
        ja = NH * d + 2 * pr
        g = gate[b]
        valid = (col <= row) if d == 0 else (col >= row)
        mu_c = [jnp.broadcast_to(g["cols"][:, j:j + 1], (L, LANE)) for j in (ja, ja + 1)]
        e2 = jnp.concatenate([jnp.where(valid, jnp.exp(row_bcast(g["r"], j) - mc), 0.0)
                              for j, mc in zip((ja, ja + 1), mu_c)], axis=1)
        a2 = jnp.concatenate([jnp.exp(row_bcast(g["m0"], j) - mc) for j, mc in zip((ja, ja + 1), mu_c)],
                             axis=1)
        probs.append(((s2 * e2).astype(BF16), a2))

    for it, ((b, d, pr), (_, inter, vab, upd, st), (p2, a2)) in enumerate(zip(items, pre, probs)):
        out_ref = dirs[d][3]
        ps = slice(pr * LANE, (pr + 1) * LANE)
        ja = NH * d + 2 * pr
        g = gate[b]
        bd_scr[it, 0:L, 0:LANE] = vab[:, 0:LANE]
        bd_scr[it, L:2 * L, LANE:2 * LANE] = vab[:, LANE:2 * LANE]
        intra = jnp.dot(p2, bd_scr[it], preferred_element_type=F32)
        tot = intra + a2 * inter
        tot_a = tot[:, 0:LANE]
        tot_b = tot[:, LANE:2 * LANE]
        num = jnp.where(low_half, tot_a, tot_b)
        den = pltpu.roll(jnp.where(low_half, tot_b, tot_a), HD, 1)
        floor = jnp.where(low_half, jnp.broadcast_to(g["cols"][:, 8 + ja:8 + ja + 1], (L, LANE)),
                          jnp.broadcast_to(g["cols"][:, 8 + ja + 1:8 + ja + 2], (L, LANE)))
        out_ref[b, :, ps] = num / jnp.maximum(jnp.abs(den), floor)
        a_s = jnp.concatenate([row_bcast(g["a_st"], ja), row_bcast(g["a_st"], ja + 1)], axis=1)
        g_s = jnp.concatenate([row_bcast(g["g_st"], ja), row_bcast(g["g_st"], ja + 1)], axis=1)
        s_scr[b, 2 * d + pr] = a_s * st + g_s * jnp.where(own_rows, upd, 0.0)

    @pl.when(c == n_chunks - 1)
    def _():
        mfin_ref[...] = m_scr[...]
        sfin_ref[...] = s_scr[...]


def _mlstm(q, kt, ml, grows, m0, s0):
    B, T, W = q.shape
    L = ML_CHUNK
    nc = T // L

    def specs(ci):
        return [pl.BlockSpec((B, L, W), lambda c: (0, ci(c), 0)),
                pl.BlockSpec((B, W, L), lambda c: (0, 0, ci(c))),
                pl.BlockSpec((B, L, W), lambda c: (0, ci(c), 2)),
                pl.BlockSpec((B, grows.shape[1], L), lambda c: (0, 0, ci(c)))]

    m_spec = pl.BlockSpec(m0.shape, lambda c: (0, 0, 0))
    s_spec = pl.BlockSpec(s0.shape, lambda c: (0, 0, 0, 0))
    dir_args = [q, kt, ml, grows]
    return pl.pallas_call(
        functools.partial(_mlstm_kernel, nc),
        grid=(nc,),
        in_specs=specs(lambda c: c) + specs(lambda c: nc - 1 - c) + [m_spec, s_spec],
        out_specs=[pl.BlockSpec((B, L, W), lambda c: (0, c, 0)),
                   pl.BlockSpec((B, L, W), lambda c: (0, nc - 1 - c, 0)),
                   m_spec, s_spec],
        out_shape=[jax.ShapeDtypeStruct((B, T, W), F32),
                   jax.ShapeDtypeStruct((B, T, W), F32),
                   jax.ShapeDtypeStruct(m0.shape, F32),
                   jax.ShapeDtypeStruct(s0.shape, F32)],
        scratch_shapes=[pltpu.VMEM(m0.shape, F32), pltpu.VMEM(s0.shape, F32),
                        pltpu.VMEM((B * ML_HEADS, 2 * L, 2 * LANE), BF16)],
        compiler_params=_cparams("arbitrary"),
        name="mlstm",
    )(*dir_args, *dir_args, m0, s0)


def _softmax_pv(s_list, v_list):
    m = functools.reduce(jnp.maximum, [jnp.max(s, -1, keepdims=True) for s in s_list])
    ps = [jnp.exp(s - m) for s in s_list]
    l = functools.reduce(jnp.add, [jnp.sum(p, -1, keepdims=True) for p in ps])
    o = functools.reduce(jnp.add, [jnp.dot(p.astype(BF16), v, preferred_element_type=F32)
                                   for p, v in zip(ps, v_list)])
    return o / l


def _na_kernel(n_rows, q_ref, *refs):
    nb = NA_BAND // NA_KBLK_ROWS
    k_blks = refs[0:nb]
    v_blks = refs[nb:2 * nb]
    kc_ref, vc_ref, bt_ref, o_ref, kband, vband = refs[2 * nb:]
    i = pl.program_id(1)
    blk_tok = NA_KBLK_ROWS * GRID_W
    HD = NA_HEAD_DIM
    GW = NA_GROUP * HD
    head_of_lane = lax.broadcasted_iota(jnp.int32, (1, GW), 1) // HD
    for j in range(nb):
        kband[j * blk_tok:(j + 1) * blk_tok, :] = k_blks[j][0]
        vband[j * blk_tok:(j + 1) * blk_tok, :] = v_blks[j][0]
    band_row0 = jnp.clip(i * NA_QROWS - NA_ROWS // 2, 0, n_rows - NA_BAND)
    n_win = NA_ROWS * GRID_W
    zero = jnp.zeros((), BF16)

    def rows_body(t, carry):
        items = []
        for u in range(NA_UNROLL):
            a = t * NA_UNROLL + u
            r = i * NA_QROWS + a
            r0 = jnp.clip(r - NA_ROWS // 2, 0, n_rows - NA_ROWS)
            koff = pl.multiple_of((r0 - band_row0) * GRID_W, GRID_W)
            dr_first = r0 - r + (NA_ROWS - 1)
            qoff = pl.multiple_of(a * GRID_W, GRID_W)
            for gi in range(NA_HEADS // NA_GROUP):
                items.append((koff, dr_first, qoff, gi, slice(gi * GW, (gi + 1) * GW)))
        scores = []
        for koff, _, qoff, _, gs in items:
            qg = q_ref[0, pl.ds(qoff, GRID_W), gs]
            qm = jnp.concatenate([jnp.where(head_of_lane == g, qg, zero) for g in range(NA_GROUP)], axis=0)
            scores.append((_nt_dot(kband[pl.ds(koff, n_win), gs], qm),
                           _nt_dot(kc_ref[0, :, gs], qm)))
        probs = []
        for (_, dr_first, _, gi, _), (s_loc, s_ctx) in zip(items, scores):
            s_loc = s_loc + bt_ref[dr_first, gi]
            m = jnp.maximum(jnp.max(s_loc, 0, keepdims=True), jnp.max(s_ctx, 0, keepdims=True))
            e_loc = jnp.exp(s_loc - m)
            e_ctx = jnp.exp(s_ctx - m)
            inv = 1.0 / (jnp.sum(e_loc, 0, keepdims=True) + jnp.sum(e_ctx, 0, keepdims=True))
            probs.append(((e_loc * inv).astype(BF16), (e_ctx * inv).astype(BF16)))
        for (koff, _, qoff, _, gs), (p_loc, p_ctx) in zip(items, probs):
            res = (_tn_dot(p_loc, vband[pl.ds(koff, n_win), gs])
                   + _tn_dot(p_ctx, vc_ref[0, :, gs]))
            out = res[0:GRID_W]
            for g in range(1, NA_GROUP):
                out = jnp.where(head_of_lane == g, res[g * GRID_W:(g + 1) * GRID_W], out)
            o_ref[0, pl.ds(qoff, GRID_W), gs] = out.astype(BF16)
        return carry

    lax.fori_loop(0, NA_QROWS // NA_UNROLL, rows_body, 0)


def _na_latent(q, k, v, kc, vc, bias_tab):
    B, T, Wd = q.shape
    Lc = kc.shape[1]
    n_rows = T // GRID_W
    nb = NA_BAND // NA_KBLK_ROWS
    q_tok = NA_QROWS * GRID_W
    blk_tok = NA_KBLK_ROWS * GRID_W
    last_blk0 = (n_rows - NA_BAND) // NA_KBLK_ROWS

    def kv_spec(j):
        def idx(b, i):
            first = jnp.clip(i * (NA_QROWS // NA_KBLK_ROWS) - (NA_ROWS // 2) // NA_KBLK_ROWS, 0, last_blk0)
            return (b, first + j, 0)
        return pl.BlockSpec((1, blk_tok, Wd), idx)

    return pl.pallas_call(
        functools.partial(_na_kernel, n_rows),
        grid=(B, n_rows // NA_QROWS),
        in_specs=([pl.BlockSpec((1, q_tok, Wd), lambda b, i: (b, i, 0))]
                  + [kv_spec(j) for j in range(nb)] + [kv_spec(j) for j in range(nb)]
                  + [pl.BlockSpec((1, Lc, Wd), lambda b, i: (b, 0, 0)),
                     pl.BlockSpec((1, Lc, Wd), lambda b, i: (b, 0, 0)),
                     pl.BlockSpec(bias_tab.shape, lambda b, i: (0, 0, 0, 0))]),
        out_specs=pl.BlockSpec((1, q_tok, Wd), lambda b, i: (b, i, 0)),
        out_shape=jax.ShapeDtypeStruct((B, T, Wd), BF16),
        scratch_shapes=[pltpu.VMEM((NA_BAND * GRID_W, Wd), BF16),
                        pltpu.VMEM((NA_BAND * GRID_W, Wd), BF16)],
        compiler_params=_cparams("arbitrary", "arbitrary"),
        name="natten",
    )(q, *([k] * nb), *([v] * nb), kc, vc, bias_tab)


def _ctx_attn_kernel(q_ref, k_ref, v_ref, o_ref):
    HD = NA_HEAD_DIM
    for h in range(NA_HEADS):
        hs = slice(h * HD, (h + 1) * HD)
        s = _nt_dot(q_ref[0, :, hs], k_ref[0, :, hs])
        o_ref[0, :, hs] = _softmax_pv([s], [v_ref[0, :, hs]]).astype(BF16)


def _ctx_attn(q, k, v):
    B, Lc, Wd = q.shape
    spec = pl.BlockSpec((1, Lc, Wd), lambda b: (b, 0, 0))
    return pl.pallas_call(
        _ctx_attn_kernel, grid=(B,), in_specs=[spec, spec, spec], out_specs=spec,
        out_shape=jax.ShapeDtypeStruct((B, Lc, Wd), BF16),
        compiler_params=_cparams("arbitrary"), name="ctx_attn",
    )(q, k, v)


def _na_bias_table(rpb):
    qc = np.arange(GRID_W)[:, None]
    kcol = np.arange(GRID_W)[None, :]
    wstart = np.clip(qc - NA_COLS // 2, 0, GRID_W - NA_COLS)
    col_ok = (kcol >= wstart) & (kcol < wstart + NA_COLS)
    dc = np.clip(kcol - qc, -(NA_COLS - 1), NA_COLS - 1) + NA_COLS - 1
    per_dr = jnp.where(col_ok[None, None], rpb[:, :, dc], MASK_NEG)
    def tab(d):
        n_grp = NA_HEADS // NA_GROUP
        t = per_dr[:, d:d + NA_ROWS].reshape(n_grp, NA_GROUP, NA_ROWS, GRID_W, GRID_W)
        return jnp.transpose(t, (0, 2, 4, 1, 3)).reshape(n_grp, NA_ROWS * GRID_W, NA_GROUP * GRID_W)

    return jnp.stack([tab(d) for d in range(NA_ROWS)]).astype(F32)


def _route(sel, s):
    E = EXP_PER_GROUP
    scores = []
    for g in range(N_GROUPS):
        a, b, c, d = sel[E * g:E * (g + 1)]
        scores.append(functools.reduce(jnp.maximum, [a + b, a + c, a + d, b + c, b + d, c + d]))
    best = jnp.zeros_like(scores[0], dtype=jnp.int32)
    best_score = scores[0]
    for g in range(1, N_GROUPS):
        upd = scores[g] > best_score
        best = jnp.where(upd, g, best)
        best_score = jnp.where(upd, scores[g], best_score)

    def pick(rows, j):
        out = rows[j]
        for g in range(1, N_GROUPS):
            out = jnp.where(best == g, rows[E * g + j], out)
        return out

    v = [pick(sel, j) for j in range(E)]
    sv = [pick(s, j) for j in range(E)]
    i1 = jnp.zeros_like(best)
    m1 = v[0]
    for j in range(1, E):
        upd = v[j] > m1
        i1 = jnp.where(upd, j, i1)
        m1 = jnp.where(upd, v[j], m1)
    i2 = jnp.where(i1 == 0, 1, 0)
    m2 = jnp.where(i1 == 0, v[1], v[0])
    for j in range(1, E):
        upd = (i1 != j) & (v[j] > m2)
        i2 = jnp.where(upd, j, i2)
        m2 = jnp.where(upd, v[j], m2)

    def at(rows, idx):
        out = rows[0]
        for j in range(1, E):
            out = jnp.where(idx == j, rows[j], out)
        return out

    s1 = at(sv, i1)
    s2 = at(sv, i2)
    tot = s1 + s2
    return best * E + i1, best * E + i2, s1 / tot, s2 / tot


def _out_proj_kernel(n_in, *refs):
    (ygm_ref, hf_ref, hb_ref, o_ref, yna_ref, x_ref, mod_ref, w_ref, mlg_ref, n2g_ref,
     rw_ref, rb_ref) = refs[:12]
    xn_ref, h2_ref, re_ref, rwt_ref = refs[n_in:]
    hs = hf_ref[0] + hb_ref[0]
    hsq = hs * hs
    lane = lax.broadcasted_iota(jnp.int32, (1, ML_WIDTH), 1)
    scale = jnp.zeros_like(hs)
    for h in range(ML_HEADS):
        in_head = (lane >= h * ML_HEAD_DIM) & (lane < (h + 1) * ML_HEAD_DIM)
        ms = jnp.sum(jnp.where(in_head, hsq, 0.0), -1, keepdims=True) * (1.0 / ML_HEAD_DIM)
        scale = jnp.where(in_head, lax.rsqrt(ms + EPS), scale)
    yml = hs * scale * mlg_ref[...] * jax.nn.sigmoid(o_ref[0])
    o1 = GM_WIDTH
    o2 = GM_WIDTH + ML_WIDTH
    acc = (jnp.dot(ygm_ref[0], w_ref[0:o1, :], preferred_element_type=F32)
           + jnp.dot(yml.astype(BF16), w_ref[o1:o2, :], preferred_element_type=F32)
           + jnp.dot(yna_ref[0], w_ref[o2:, :], preferred_element_type=F32))
    xn = x_ref[0] + mod_ref[0, 0:1, :] * acc
    xn_ref[0] = xn
    h2 = xn * lax.rsqrt(jnp.mean(xn * xn, -1, keepdims=True) + EPS) * n2g_ref[...]
    h2 = h2 * (1.0 + mod_ref[0, 2:3, :]) + mod_ref[0, 1:2, :]
    h2_ref[...] = h2
    h_hi = h2.astype(BF16)
    h_lo = (h2 - h_hi.astype(F32)).astype(BF16)
    rw_hi = rw_ref[0]
    logits = _nt_dot(rw_hi, h_hi) + _nt_dot(rw_hi, h_lo) + _nt_dot(rw_ref[1], h_hi)
    s = jax.nn.sigmoid(logits)
    sel = s + rb_ref[...]
    rows = lambda m: [m[e:e + 1, :] for e in range(N_EXPERTS)]
    e1, e2, w1, w2 = _route(rows(sel), rows(s))
    re_ref[0, 0:1, :] = e1
    re_ref[0, 1:2, :] = e2
    rwt_ref[0, 0:1, :] = w1
    rwt_ref[0, 1:2, :] = w2


def _out_proj(ygm, hf, hb, ml, yna, x, mod, w_out, ml_g, n2_g, router_wt, router_b, h2_rows, h2_row0,
              h2_buf=None):
    G, R, D = x.shape
    tm = min(R, 512)
    row = lambda b, i: (b, i, 0)
    whole2 = lambda shape: pl.BlockSpec(shape, lambda b, i: (0, 0))
    in_specs = [pl.BlockSpec((1, tm, GM_WIDTH), row),
                pl.BlockSpec((1, tm, ML_WIDTH), row),
                pl.BlockSpec((1, tm, ML_WIDTH), row),
                pl.BlockSpec((1, tm, ML_WIDTH), lambda b, i: (b, i, 3)),
                pl.BlockSpec((1, tm, NA_WIDTH), row),
                pl.BlockSpec((1, tm, D), row),
                pl.BlockSpec((1, 3, D), lambda b, i: (b, 0, 0)),
                whole2((D, D)), whole2((1, ML_WIDTH)), whole2((1, D)),
                pl.BlockSpec((2, N_EXPERTS, D), lambda b, i: (0, 0, 0)), whole2((N_EXPERTS, 1))]
    args = [ygm, hf, hb, ml, yna, x, mod, w_out, ml_g, n2_g, router_wt, router_b]
    aliases = {}
    if h2_buf is not None:
        aliases = {len(args): 1}
        in_specs.append(pl.BlockSpec(memory_space=pl.ANY))
        args.append(h2_buf)
    blk0 = h2_row0 // tm
    return pl.pallas_call(
        functools.partial(_out_proj_kernel, len(args)),
        grid=(G, R // tm),
        in_specs=in_specs,
        out_specs=[pl.BlockSpec((1, tm, D), row),
                   pl.BlockSpec((tm, D), lambda b, i: (blk0 + b * (R // tm) + i, 0)),
                   pl.BlockSpec((1, 2, tm), lambda b, i: (b, 0, i)),
                   pl.BlockSpec((1, 2, tm), lambda b, i: (b, 0, i))],
        out_shape=[jax.ShapeDtypeStruct((G, R, D), F32), jax.ShapeDtypeStruct((h2_rows, D), F32),
                   jax.ShapeDtypeStruct((G, 2, R), jnp.int32), jax.ShapeDtypeStruct((G, 2, R), F32)],
        input_output_aliases=aliases,
        compiler_params=_cparams("arbitrary", "arbitrary"),
        name="out_proj",
    )(*args)


def _experts_kernel(te_ref, nu_ref, src_cur, src_nxt, h_hbm, wg_ref, wu_ref, wd_ref, o_ref,
                    acc_ref, wgb, wub, wdb, xbuf, sem):
    i = pl.program_id(0)
    n_used = nu_ref[0]
    used = i < n_used
    slot = lax.rem(i, 2)
    new_expert = (i == 0) | (te_ref[i] != te_ref[jnp.maximum(i - 1, 0)])

    def row_copy(src_ref, s, r):
        return pltpu.make_async_copy(h_hbm.at[pl.ds(src_ref[0, 0, r], 1), :], xbuf.at[s, pl.ds(r, 1), :],
                                     sem.at[s])

    def tile_wait(s):
        pltpu.make_async_copy(xbuf.at[s], xbuf.at[s], sem.at[s]).wait()

    @pl.when((i == 0) & used)
    def _():
        def body(r, carry):
            row_copy(src_cur, 0, r).start()
            return carry
        lax.fori_loop(0, MOE_TILE, body, 0, unroll=8)

    @pl.when(used & new_expert)
    def _():
        def cast_rows(c, carry):
            rs = pl.ds(pl.multiple_of(c * LANE, LANE), LANE)
            wgb[rs, :] = wg_ref[0, 0, rs, :].astype(BF16)
            wub[rs, :] = wu_ref[0, 0, rs, :].astype(BF16)
            wdb[rs, :] = wd_ref[0, 0, rs, :].astype(BF16)
            return carry
        lax.fori_loop(0, D_MODEL // LANE, cast_rows, 0)

    @pl.when(used)
    def _():
        tile_wait(slot)
        x = xbuf[slot].astype(BF16)
        nxt = 1 - slot
        n_chunks = D_EXPERT // EXPERT_CHUNK
        per = -(-MOE_TILE // (3 * n_chunks))
        issued = [0]

        def request_rows():
            for r in range(issued[0], min(issued[0] + per, MOE_TILE)):
                row_copy(src_nxt, nxt, r).start(priority=r % 2)
            issued[0] += per

        for j in range(n_chunks):
            cs = slice(j * EXPERT_CHUNK, (j + 1) * EXPERT_CHUNK)
            request_rows()
            g = jnp.dot(x, wgb[:, cs], preferred_element_type=F32)
            request_rows()
            u = jnp.dot(x, wub[:, cs], preferred_element_type=F32)
            a = (g * jax.nn.sigmoid(g) * u).astype(BF16)
            request_rows()
            y = jnp.dot(a, wdb[cs, :], preferred_element_type=F32)
            if j == 0:
                acc_ref[...] = y
            else:
                acc_ref[...] += y
        o_ref[...] = acc_ref[...].astype(BF16)

        @pl.when(i + 1 >= n_used)
        def _():
            tile_wait(nxt)

    @pl.when(jnp.logical_not(used))
    def _():
        o_ref[...] = jnp.zeros_like(o_ref)


def _experts(l, tile_expert, n_used, src, h_all, wg, wu, wd):
    D = h_all.shape[1]
    n_tiles = src.shape[0] // MOE_TILE
    src3 = src.reshape(n_tiles, 1, MOE_TILE)
    wspec = lambda r, c: pl.BlockSpec((1, 1, r, c), lambda i, te, nu: (l, te[i], 0, 0))
    src_spec = lambda ahead: pl.BlockSpec(
        (1, 1, MOE_TILE), lambda i, te, nu: (jnp.minimum(i + ahead, n_tiles - 1), 0, 0),
        memory_space=pltpu.SMEM)
    return pl.pallas_call(
        _experts_kernel,
        grid_spec=pltpu.PrefetchScalarGridSpec(
            num_scalar_prefetch=2,
            grid=(n_tiles,),
            in_specs=[src_spec(0), src_spec(1), pl.BlockSpec(memory_space=pl.ANY),
                      wspec(D, D_EXPERT), wspec(D, D_EXPERT), wspec(D_EXPERT, D)],
            out_specs=pl.BlockSpec((MOE_TILE, D), lambda i, te, nu: (i, 0)),
            scratch_shapes=[pltpu.VMEM((MOE_TILE, D), F32), pltpu.VMEM((D, D_EXPERT), BF16),
                            pltpu.VMEM((D, D_EXPERT), BF16), pltpu.VMEM((D_EXPERT, D), BF16),
                            pltpu.VMEM((2, MOE_TILE, D), F32), pltpu.SemaphoreType.DMA((2,))]),
        out_shape=jax.ShapeDtypeStruct((n_tiles * MOE_TILE, D), BF16),
        compiler_params=_cparams("arbitrary"),
        name="experts",
    )(tile_expert, n_used, src3, src3, h_all, wg, wu, wd)


def _dispatch(e_idx):
    n_tok = e_idx.shape[0]
    n_asg = 2 * n_tok
    n_tiles = -(-n_asg // MOE_TILE) + N_EXPERTS
    flat_e = e_idx.reshape(-1)
    blk = MOE_TILE
    assert n_asg % blk == 0
    onehot = (flat_e[:, None] == jnp.arange(N_EXPERTS, dtype=jnp.int32)[None, :]).astype(BF16)
    onehot = onehot.reshape(n_asg // blk, blk, N_EXPERTS)
    tri = (jnp.arange(blk)[:, None] >= jnp.arange(blk)[None, :]).astype(BF16)
    within = jnp.einsum("ij,bje->bie", tri, onehot, preferred_element_type=F32)
    blk_tot = within[:, -1, :]
    blk_off = jnp.cumsum(blk_tot, axis=0) - blk_tot
    counts = (blk_off[-1] + blk_tot[-1]).astype(jnp.int32)
    tiles_e = (counts + MOE_TILE - 1) // MOE_TILE
    tile_end = jnp.cumsum(tiles_e)
    tile_start = tile_end - tiles_e
    row_base = (tile_start * MOE_TILE - 1).astype(F32)
    dest = jnp.sum((within + (blk_off + row_base)[:, None, :]) * onehot.astype(F32), -1)
    dest = dest.astype(jnp.int32).reshape(-1)
    n_used = tile_end[-1]
    tid = jnp.arange(n_tiles, dtype=jnp.int32)
    te = jnp.sum((tile_end[None, :] <= jnp.minimum(tid, n_used - 1)[:, None]).astype(jnp.int32), axis=1)
    te = jnp.minimum(te, N_EXPERTS - 1)
    flat_t = jnp.arange(n_asg, dtype=jnp.int32) // 2
    src = jnp.zeros((n_tiles * MOE_TILE,), jnp.int32).at[dest].set(
        flat_t, unique_indices=True, mode="promise_in_bounds")
    return dest.reshape(n_tok, 2), src, te, n_used.reshape(1).astype(jnp.int32)


def _combine_kernel(final, xn_ref, y0_ref, y1_ref, w_ref, g2_ref, fg_ref, o_ref):
    w = w_ref[0]
    f = y0_ref[0].astype(F32) * w[:, 0:1] + y1_ref[0].astype(F32) * w[:, 1:2]
    x = xn_ref[0] + g2_ref[0] * f
    if final:
        x = x * lax.rsqrt(jnp.mean(x * x, -1, keepdims=True) + EPS) * fg_ref[...]
    o_ref[0] = x


def _combine(xn, y0, y1, wt, g2, final_g, final):
    G, R, D = xn.shape
    tm = min(R, 512)
    row = lambda b, i: (b, i, 0)
    return pl.pallas_call(
        functools.partial(_combine_kernel, final),
        grid=(G, R // tm),
        in_specs=[pl.BlockSpec((1, tm, D), row), pl.BlockSpec((1, tm, D), row), pl.BlockSpec((1, tm, D), row),
                  pl.BlockSpec((1, tm, 2), row),
                  pl.BlockSpec((1, 1, D), lambda b, i: (b, 0, 0)),
                  pl.BlockSpec((1, D), lambda b, i: (0, 0))],
        out_specs=pl.BlockSpec((1, tm, D), row),
        out_shape=jax.ShapeDtypeStruct((G, R, D), F32),
        compiler_params=_cparams("arbitrary", "arbitrary"),
        name="combine_final" if final else "combine",
    )(xn, y0, y1, wt, g2, final_g)


def _rope_tables(T):
    lane = np.arange(LANE)
    half = ML_HEAD_DIM // 4
    inv = jnp.tile(ROPE_BASE ** (-jnp.arange(half, dtype=F32) / half), LANE // half)
    t = jnp.arange(T)
    pos = jnp.where(((lane // (2 * half)) % 2 == 0)[None, :], (t // GRID_W)[:, None], (t % GRID_W)[:, None])
    ang = pos.astype(F32) * inv[None, :]
    sign = np.where((lane // half) % 2 == 0, -1.0, 1.0).astype(np.float32)
    return jnp.cos(ang), jnp.sin(ang) * sign[None, :]


def _reorder_w_in(w_in):
    pad = jnp.zeros(w_in.shape[:2] + (LANE - 4 * ML_HEADS,), w_in.dtype)
    return jnp.concatenate([w_in[..., :OFF_GATES], w_in[..., OFF_NA:], w_in[..., OFF_GATES:OFF_NA], pad],
                           axis=-1).astype(BF16)


def _mixers(l, p, pc, prm, need_ctx):
    ygm, ml, gt, q, k, v = p
    ygmc, mlc, gtc, qc, kc, vc = pc
    B = ml.shape[0]
    conv = (prm["conv_w"][l], prm["conv_b"][l])
    nh = ML_HEADS
    tr = lambda g: jnp.transpose(
        jnp.concatenate([g[..., 0:nh], g[..., 2 * nh:3 * nh], g[..., nh:2 * nh], g[..., 3 * nh:4 * nh]], -1),
        (0, 2, 1))
    m_zero = jnp.zeros((B, 2 * nh, LANE), F32)
    s_zero = jnp.zeros((B, nh, ML_CHUNK, 2 * LANE), F32)
    qmc, kmc, grc = _mlstm_prep(mlc, prm["cos_c"], prm["sin_c"], *conv, tr(gtc), prm["gbias"][l])
    hfc, hbc, m_st, s_st = _mlstm(qmc, kmc, mlc, grc, m_zero, s_zero)
    qm, km, gr = _mlstm_prep(ml, prm["cos_l"], prm["sin_l"], *conv, tr(gt), prm["gbias"][l])
    hf, hb, _, _ = _mlstm(qm, km, ml, gr, m_st, s_st)
    y_na = _na_latent(q, k, v, kc, vc, prm["na_tab"][l])
    y = (ygm, hf, hb, ml, y_na)
    if not need_ctx:
        return y, None
    return y, (ygmc, hfc, hbc, mlc, _ctx_attn(qc, kc, vc))


def _rows(a, idx):
    return a.at[idx].get(mode="promise_in_bounds")


def _moe(l, h2_all, e_all, prm):
    dest, src, te, n_used = _dispatch(e_all)
    yb = _experts(l, te, n_used, src, h2_all, prm["wg"], prm["wu"], prm["wd"])
    return yb, dest


def kernel(x, c, ctx, c_ctx, ada_w, ada_b, norm1_g, norm2_g, w_in, w_out, gm_ws, gm_bs, gm_norm_g,
           ml_conv_w, ml_conv_b, ml_gate_b, ml_norm_g, na_rpb, router_w, router_b,
           moe_w_gate, moe_w_up, moe_w_down, final_g):
    B, T, D = x.shape
    Lc = ctx.shape[1]
    cos_l, sin_l = _rope_tables(T)
    prm = dict(
        gm_ws=gm_ws.astype(BF16),
        gm_bs_full=jnp.repeat(jnp.transpose(gm_bs, (0, 2, 1)), GM_HEAD_DIM, axis=-1),
        gm_g=gm_norm_g[:, None, :],
        conv_w=ml_conv_w, conv_b=ml_conv_b[:, None, :],
        gbias=ml_gate_b[:, jnp.array([0, 2, 1, 3])].reshape(DEPTH, 4 * ML_HEADS, 1),
        cos_l=cos_l, sin_l=sin_l,
        cos_c=jnp.ones((Lc, LANE), F32), sin_c=jnp.zeros((Lc, LANE), F32),
        na_tab=jnp.stack([_na_bias_table(na_rpb[l]) for l in range(DEPTH)]),
        wg=moe_w_gate, wu=moe_w_up, wd=moe_w_down,
    )
    w_in_r = _reorder_w_in(w_in)
    w_out_b = w_out.astype(BF16)
    rw_t = jnp.transpose(router_w)
    rw_hi = rw_t.astype(BF16)
    router_wt = jnp.stack([rw_hi, (rw_t - rw_hi.astype(F32)).astype(BF16)])
    router_bc = router_b[:, None]

    cs = jnp.concatenate([c, c_ctx[None, :], jnp.zeros((8 - B - 1, D), F32)], axis=0)
    mods = _ada_all(cs, ada_w, ada_b).reshape(DEPTH, 8, 6, D)

    xc = ctx
    for l in range(DEPTH):
        need_ctx = l < DEPTH - 1
        mod_l = mods[l, :B]
        mod_c = jnp.broadcast_to(mods[l, B:B + 1], (B, 6, D))
        n1 = norm1_g[l][None, :]
        gm_args = (prm["gm_ws"][l], prm["gm_bs_full"][l], prm["gm_g"][l])
        p = _in_proj(x, mod_l[:, 0:2], n1, w_in_r[l], *gm_args)
        pc = _in_proj(xc, mod_c[:, 0:2], n1, w_in_r[l], *gm_args)
        y, yc = _mixers(l, p, pc, prm, need_ctx)
        op_args = (w_out_b[l], ml_norm_g[l][None, :], norm2_g[l][None, :], router_wt, router_bc)
        n_all = B * T + (B * Lc if need_ctx else 0)
        h2_buf = (jnp.zeros((n_all, D), F32) if l == 0 else h2_all) if need_ctx else None
        xn, h2_all, re, rw = _out_proj(*y, x, mod_l[:, 2:5], *op_args, n_all, 0, h2_buf)
        e_all = jnp.transpose(re, (0, 2, 1)).reshape(B * T, 2)
        if need_ctx:
            xnc, h2_all, rec, rwc = _out_proj(*yc, xc, mod_c[:, 2:5], *op_args, n_all, B * T, h2_all)
            e_all = jnp.concatenate([e_all, jnp.transpose(rec, (0, 2, 1)).reshape(B * Lc, 2)], axis=0)
        yb, dest = _moe(l, h2_all, e_all, prm)
        final = l == DEPTH - 1
        fg = final_g[None, :]
        n_lat = B * T
        picked = lambda lo, hi, k, L: _rows(yb, dest[lo:hi, k]).reshape(B, L, D)
        x = _combine(xn, picked(0, n_lat, 0, T), picked(0, n_lat, 1, T),
                     jnp.transpose(rw, (0, 2, 1)), mod_l[:, 5:6], fg, final)
        if need_ctx:
            xc = _combine(xnc, picked(n_lat, None, 0, Lc), picked(n_lat, None, 1, Lc),
                          jnp.transpose(rwc, (0, 2, 1)), mod_c[:, 5:6], fg, False)
    return x
```

```python
import functools

import jax
import jax.numpy as jnp
import numpy as np
from jax import lax
from jax.experimental import pallas as pl
from jax.experimental.pallas import tpu as pltpu

F32 = jnp.float32
BF16 = jnp.bfloat16

D_MODEL = 1024
DEPTH = 4
GRID_W = 64
EPS = 1e-6

GM_HEADS = 4
GM_WIDTH = D_MODEL // 4
GM_HEAD_DIM = GM_WIDTH // GM_HEADS
GM_CHUNK = 128

ML_HEADS = 4
ML_WIDTH = D_MODEL // 4
ML_HEAD_DIM = ML_WIDTH // ML_HEADS
ML_CHUNK = 128

NA_HEADS = 8
NA_WIDTH = D_MODEL // 2
NA_HEAD_DIM = NA_WIDTH // NA_HEADS
NA_ROWS = 8
NA_COLS = 16

ROPE_BASE = 10000.0

OFF_ML = 2 * GM_WIDTH
OFF_GATES = OFF_ML + 4 * ML_WIDTH
OFF_NA = OFF_GATES + 4 * ML_HEADS
N_IN = OFF_NA + 3 * NA_WIDTH

N_EXPERTS = 16
N_GROUPS = 4
EXP_PER_GROUP = N_EXPERTS // N_GROUPS
D_EXPERT = D_MODEL

LANE = 128
SEG_GM = (0, 2 * GM_WIDTH)
SEG_ML = (SEG_GM[1], SEG_GM[1] + 4 * ML_WIDTH)
SEG_Q = (SEG_ML[1], SEG_ML[1] + NA_WIDTH)
SEG_K = (SEG_Q[1], SEG_Q[1] + NA_WIDTH)
SEG_V = (SEG_K[1], SEG_K[1] + NA_WIDTH)
SEG_GATES = (SEG_V[1], SEG_V[1] + LANE)
N_IN_PAD = SEG_GATES[1]

NA_QROWS = 8
NA_GROUP = 4
NA_UNROLL = 4
NA_BAND = 16
NA_KBLK_ROWS = 4
MOE_TILE = 512
EXPERT_CHUNK = 512
MASK_NEG = -1e30
N_GATE_ROWS = 6
VMEM_LIMIT = 56 * 2 ** 20


def _cparams(*sem):
    return pltpu.CompilerParams(dimension_semantics=sem, vmem_limit_bytes=VMEM_LIMIT)


def _nt_dot(a, b, precision=None):
    return lax.dot_general(a, b, (((1,), (1,)), ((), ())), precision=precision,
                           preferred_element_type=F32)


def _tn_dot(a, b):
    return lax.dot_general(a, b, (((0,), (0,)), ((), ())), preferred_element_type=F32)


def _ada_kernel(c_ref, w_ref, b_ref, o_ref):
    c = c_ref[...]
    sc = c * jax.nn.sigmoid(c)
    o_ref[0] = jnp.dot(sc, w_ref[0], preferred_element_type=F32) + b_ref[0]


def _ada_all(cs, ada_w, ada_b):
    n_out = ada_w.shape[-1] // D_MODEL
    return pl.pallas_call(
        _ada_kernel,
        grid=(DEPTH, n_out),
        in_specs=[pl.BlockSpec((8, D_MODEL), lambda l, j: (0, 0)),
                  pl.BlockSpec((1, D_MODEL, D_MODEL), lambda l, j: (l, 0, j)),
                  pl.BlockSpec((1, 1, D_MODEL), lambda l, j: (l, 0, j))],
        out_specs=pl.BlockSpec((1, 8, D_MODEL), lambda l, j: (l, 0, j)),
        out_shape=jax.ShapeDtypeStruct((DEPTH, 8, n_out * D_MODEL), F32),
        compiler_params=_cparams("arbitrary", "arbitrary"),
        name="adaln",
    )(cs, ada_w, ada_b.reshape(DEPTH, 1, -1))


def _gmlp_chunk(z_pre, ws_ref, bs, g):
    lane = lax.broadcasted_iota(jnp.int32, (1, GM_WIDTH), 1)
    z = jax.nn.gelu(z_pre)
    u = z[:, :GM_WIDTH]
    v = z[:, GM_WIDTH:]
    mu = jnp.mean(v, -1, keepdims=True)
    vc = v - mu
    var = jnp.mean(vc * vc, -1, keepdims=True)
    vn = (vc * lax.rsqrt(var + EPS) * g).astype(BF16)
    sv = jnp.zeros((GM_CHUNK, GM_WIDTH), F32)
    for h in range(GM_HEADS):
        full = jnp.dot(ws_ref[h], vn, preferred_element_type=F32)
        in_head = (lane >= h * GM_HEAD_DIM) & (lane < (h + 1) * GM_HEAD_DIM)
        sv = jnp.where(in_head, full, sv)
    return u * (sv + bs)


def _in_proj_kernel(x_ref, mod_ref, g_ref, w_ref, ws_ref, bs_ref, gmg_ref,
                    ygm_ref, ml_ref, gt_ref, q_ref, k_ref, v_ref):
    x = x_ref[0]
    h = x * lax.rsqrt(jnp.mean(x * x, -1, keepdims=True) + EPS) * g_ref[...]
    h = h * (1.0 + mod_ref[0, 1:2, :]) + mod_ref[0, 0:1, :]
    hb = h.astype(BF16)

    def seg(s):
        return jnp.dot(hb, w_ref[:, s[0]:s[1]], preferred_element_type=F32)

    gm = seg(SEG_GM)
    ml_ref[0] = seg(SEG_ML)
    for c in range(gm.shape[0] // GM_CHUNK):
        rows = slice(c * GM_CHUNK, (c + 1) * GM_CHUNK)
        ygm_ref[0, rows, :] = _gmlp_chunk(gm[rows], ws_ref, bs_ref[...], gmg_ref[...]).astype(BF16)
    gt_ref[0] = seg(SEG_GATES)
    q_ref[0] = (seg(SEG_Q) * (NA_HEAD_DIM ** -0.5)).astype(BF16)
    k_ref[0] = seg(SEG_K).astype(BF16)
    v_ref[0] = seg(SEG_V).astype(BF16)


def _in_proj(x, mod, g, w, gm_ws, gm_bs_full, gm_g):
    G, R, D = x.shape
    tm = min(R, 512)
    row = lambda b, i: (b, i, 0)
    whole = lambda shape: pl.BlockSpec(shape, lambda b, i: (0,) * len(shape))
    widths = (GM_WIDTH, SEG_ML[1] - SEG_ML[0], LANE, NA_WIDTH, NA_WIDTH, NA_WIDTH)
    dtypes = (BF16, F32, F32, BF16, BF16, BF16)
    return pl.pallas_call(
        _in_proj_kernel,
        grid=(G, R // tm),
        in_specs=[pl.BlockSpec((1, tm, D), row),
                  pl.BlockSpec((1, 2, D), lambda b, i: (b, 0, 0)),
                  whole((1, D)), whole((D, N_IN_PAD)),
                  whole((GM_HEADS, GM_CHUNK, GM_CHUNK)), whole((GM_CHUNK, GM_WIDTH)), whole((1, GM_WIDTH))],
        out_specs=[pl.BlockSpec((1, tm, n), row) for n in widths],
        out_shape=[jax.ShapeDtypeStruct((G, R, n), dt) for n, dt in zip(widths, dtypes)],
        compiler_params=_cparams("arbitrary", "arbitrary"),
        name="in_proj",
    )(x, mod, g, w, gm_ws, gm_bs_full, gm_g)


def _log_sigmoid(x):
    return jnp.minimum(x, 0.0) - jnp.log1p(jnp.exp(-jnp.abs(x)))


def _rope(x, cos, sin_signed):
    lane = lax.broadcasted_iota(jnp.int32, (1, LANE), 1)
    first_half = (lane & 16) == 0
    partner = jnp.where(first_half, pltpu.roll(x, LANE - 16, 1), pltpu.roll(x, 16, 1))
    return x * cos + partner * sin_signed


def _chunk_scan(x, op, fill, reverse):
    pos = lax.broadcasted_iota(jnp.int32, (1, x.shape[1]), 1) % ML_CHUNK
    n = x.shape[1]
    sh = 1
    while sh < ML_CHUNK:
        if reverse:
            shifted = jnp.where(pos < ML_CHUNK - sh, pltpu.roll(x, n - sh, 1), fill)
        else:
            shifted = jnp.where(pos >= sh, pltpu.roll(x, sh, 1), fill)
        x = op(x, shifted)
        sh *= 2
    return x


def _mlstm_prep_kernel(n_blocks, x_ref, hp_ref, hn_ref, cos_ref, sin_ref, cw_ref, cb_ref, g_ref, gbias_ref,
                       q_ref, k_ref, go_ref):
    nh2 = 2 * ML_HEADS
    g = g_ref[0] + gbias_ref[...]
    is_fwd = lax.broadcasted_iota(jnp.int32, (nh2, 1), 0) < ML_HEADS
    li = g[0:nh2]
    lf = _log_sigmoid(g[nh2:])
    ps = _chunk_scan(lf, jnp.add, 0.0, False)
    ss = _chunk_scan(lf, jnp.add, 0.0, True)
    bcum = jnp.where(is_fwd, ps, ss)
    r = li - bcum
    pm = _chunk_scan(r, jnp.maximum, MASK_NEG, False)
    sm = _chunk_scan(r, jnp.maximum, MASK_NEG, True)
    rmax = jnp.maximum(pm, sm)
    groups = (bcum, r, jnp.where(is_fwd, pm, sm), jnp.exp(r - rmax), rmax, ps + ss - lf)
    for j, val in enumerate(groups):
        go_ref[0, nh2 * j:nh2 * (j + 1), :] = val

    i = pl.program_id(1)
    W = ML_WIDTH
    x = x_ref[0]
    tr = x.shape[0]
    rid = lax.broadcasted_iota(jnp.int32, (tr, 1), 0)
    i_row = jnp.zeros((1, 2 * W), jnp.int32) + i
    prev = jnp.where(i_row == 0, 0.0, hp_ref[0, 7:8, :])
    nxt = jnp.where(i_row == n_blocks - 1, 0.0, hn_ref[0, 0:1, :])
    xm1 = jnp.where(rid == 0, prev, pltpu.roll(x, 1, 0))
    xp1 = jnp.where(rid == tr - 1, nxt, pltpu.roll(x, tr - 1, 0))
    y = xm1 * cw_ref[0:1, :] + x * cw_ref[1:2, :] + xp1 * cw_ref[2:3, :] + cb_ref[...]
    y = y * jax.nn.sigmoid(y)
    cos = cos_ref[...]
    sin = sin_ref[...]
    parts = [_rope(y[:, j * LANE:(j + 1) * LANE], cos, sin) for j in range(2 * W // LANE)]
    q_ref[0] = (jnp.concatenate(parts[:W // LANE], axis=1) * (ML_HEAD_DIM ** -0.5)).astype(BF16)
    for j in range(W // LANE):
        k_ref[0, j * LANE:(j + 1) * LANE, :] = parts[W // LANE + j].T.astype(BF16)


def _mlstm_prep(ml, cos, sin, conv_w, conv_b, gates_t, gbias):
    B, T, _ = ml.shape
    n_g = gates_t.shape[1]
    W = ML_WIDTH
    tr = min(T, 1024)
    nb = T // tr
    halo_per_blk = tr // 8
    n_hblk = T // 8
    whole = lambda shape: pl.BlockSpec(shape, lambda b, i: (0, 0))
    return pl.pallas_call(
        functools.partial(_mlstm_prep_kernel, nb),
        grid=(B, nb),
        in_specs=[pl.BlockSpec((1, tr, 2 * W), lambda b, i: (b, i, 0)),
                  pl.BlockSpec((1, 8, 2 * W), lambda b, i: (b, jnp.maximum(i * halo_per_blk - 1, 0), 0)),
                  pl.BlockSpec((1, 8, 2 * W),
                               lambda b, i: (b, jnp.minimum((i + 1) * halo_per_blk, n_hblk - 1), 0)),
                  pl.BlockSpec((tr, LANE), lambda b, i: (i, 0)),
                  pl.BlockSpec((tr, LANE), lambda b, i: (i, 0)),
                  whole((3, 2 * W)), whole((1, 2 * W)),
                  pl.BlockSpec((1, n_g, tr), lambda b, i: (b, 0, i)), whole((n_g, 1))],
        out_specs=[pl.BlockSpec((1, tr, W), lambda b, i: (b, i, 0)),
                   pl.BlockSpec((1, W, tr), lambda b, i: (b, 0, i)),
                   pl.BlockSpec((1, 2 * ML_HEADS * N_GATE_ROWS, tr), lambda b, i: (b, 0, i))],
        out_shape=[jax.ShapeDtypeStruct((B, T, W), BF16), jax.ShapeDtypeStruct((B, W, T), BF16),
                   jax.ShapeDtypeStruct((B, 2 * ML_HEADS * N_GATE_ROWS, T), F32)],
        compiler_params=_cparams("arbitrary", "arbitrary"),
        name="mlstm_prep",
    )(ml, ml, ml, cos, sin, conv_w, conv_b, gates_t, gbias)


def _mlstm_kernel(n_chunks, qf_ref, ktf_ref, vf_ref, gf_ref, qb_ref, ktb_ref, vb_ref, gb_ref,
                  m0_ref, s0_ref, hf_ref, hb_ref, mfin_ref, sfin_ref, m_scr, s_scr, bd_scr):
    c = pl.program_id(0)
    L = ML_CHUNK
    HD = ML_HEAD_DIM
    NH = ML_HEADS
    n_batch = m_scr.shape[0]

    @pl.when(c == 0)
    def _():
        m_scr[...] = m0_ref[...]
        s_scr[...] = s0_ref[...]
        bd_scr[...] = jnp.zeros_like(bd_scr)

    def row_bcast(x, j):
        return jnp.broadcast_to(x[j:j + 1, :], (L, LANE))

    row = lax.broadcasted_iota(jnp.int32, (L, L), 0)
    col = lax.broadcasted_iota(jnp.int32, (L, L), 1)
    low_half = lax.broadcasted_iota(jnp.int32, (1, LANE), 1) < HD
    own_rows = jnp.concatenate([row < HD, row >= HD], axis=1)
    is_fwd = lax.broadcasted_iota(jnp.int32, (2 * NH, 1), 0) < NH

    gate = []
    for b in range(n_batch):
        gq = lambda i: jnp.where(is_fwd, gf_ref[b, 2 * NH * i:2 * NH * (i + 1), :],
                                 gb_ref[b, 2 * NH * i:2 * NH * (i + 1), :])
        bcum, r, rcmax, ew, rmax, b_last = (gq(i) for i in range(N_GATE_ROWS))
        m0 = m_scr[b]
        mu = jnp.maximum(m0, rcmax)
        emt = jnp.exp(-(bcum + mu))
        m_last = jnp.maximum(m0, rmax)
        m_scr[b] = b_last + m_last
        stack = jnp.concatenate([mu, emt, jnp.zeros((L - 4 * NH, LANE), F32)], axis=0)
        gate.append(dict(r=r, ew=ew, m0=m0, a_st=jnp.exp(m0 - m_last), g_st=jnp.exp(rmax - m_last),
                         cols=stack.T))

    dirs = ((qf_ref, ktf_ref, vf_ref, hf_ref), (qb_ref, ktb_ref, vb_ref, hb_ref))
    items = [(b, d, pr) for b in range(n_batch) for d in range(2) for pr in range(NH // 2)]

    pre = []
    for b, d, pr in items:
        q_ref, kt_ref, v_ref, _ = dirs[d]
        ps = slice(pr * LANE, (pr + 1) * LANE)
        ja = NH * d + 2 * pr
        g = gate[b]
        qp = q_ref[b, :, ps]
        kt = kt_ref[b, ps, :]
        vp = v_ref[b, :, ps]
        st = s_scr[b, 2 * d + pr]
        ktf = kt.astype(F32)
        kt2 = jnp.concatenate([jnp.where(row < HD, ktf, 0.0), jnp.where(row < HD, 0.0, ktf)],
                              axis=1).astype(BF16)
        s2 = jnp.dot(qp, kt2, preferred_element_type=F32)
        inter = jnp.dot(qp, st.astype(BF16), preferred_element_type=F32)
        vab = jnp.concatenate([jnp.where(low_half, vp, 1.0), jnp.where(low_half, 1.0, vp)],
                              axis=1).astype(BF16)
        ktw = ktf * jnp.where(row < HD, row_bcast(g["ew"], ja), row_bcast(g["ew"], ja + 1))
        upd = jnp.dot(ktw.astype(BF16), vab, preferred_element_type=F32)
        pre.append((s2, inter, vab, upd, st))

    probs = []
    for (b, d, pr), (s2, _, _, _, _) in zip(items, pre):
        ja = NH * d + 2 * pr
        g = gate[b]
        valid = (col <= row) if d == 0 else (col >= row)
        mu_c = [jnp.broadcast_to(g["cols"][:, j:j + 1], (L, LANE)) for j in (ja, ja + 1)]
        e2 = jnp.concatenate([jnp.where(valid, jnp.exp(row_bcast(g["r"], j) - mc), 0.0)
                              for j, mc in zip((ja, ja + 1), mu_c)], axis=1)
        a2 = jnp.concatenate([jnp.exp(row_bcast(g["m0"], j) - mc) for j, mc in zip((ja, ja + 1), mu_c)],
                             axis=1)
        probs.append(((s2 * e2).astype(BF16), a2))

    for it, ((b, d, pr), (_, inter, vab, upd, st), (p2, a2)) in enumerate(zip(items, pre, probs)):
        out_ref = dirs[d][3]
        ps = slice(pr * LANE, (pr + 1) * LANE)
        ja = NH * d + 2 * pr
        g = gate[b]
        bd_scr[it, 0:L, 0:LANE] = vab[:, 0:LANE]
        bd_scr[it, L:2 * L, LANE:2 * LANE] = vab[:, LANE:2 * LANE]
        intra = jnp.dot(p2, bd_scr[it], preferred_element_type=F32)
        tot = intra + a2 * inter
        tot_a = tot[:, 0:LANE]
        tot_b = tot[:, LANE:2 * LANE]
        num = jnp.where(low_half, tot_a, tot_b)
        den = pltpu.roll(jnp.where(low_half, tot_b, tot_a), HD, 1)
        floor = jnp.where(low_half, jnp.broadcast_to(g["cols"][:, 8 + ja:8 + ja + 1], (L, LANE)),
                          jnp.broadcast_to(g["cols"][:, 8 + ja + 1:8 + ja + 2], (L, LANE)))
        out_ref[b, :, ps] = num / jnp.maximum(jnp.abs(den), floor)
        a_s = jnp.concatenate([row_bcast(g["a_st"], ja), row_bcast(g["a_st"], ja + 1)], axis=1)
        g_s = jnp.concatenate([row_bcast(g["g_st"], ja), row_bcast(g["g_st"], ja + 1)], axis=1)
        s_scr[b, 2 * d + pr] = a_s * st + g_s * jnp.where(own_rows, upd, 0.0)

    @pl.when(c == n_chunks - 1)
    def _():
        mfin_ref[...] = m_scr[...]
        sfin_ref[...] = s_scr[...]


def _mlstm(q, kt, ml, grows, m0, s0):
    B, T, W = q.shape
    L = ML_CHUNK
    nc = T // L

    def specs(ci):
        return [pl.BlockSpec((B, L, W), lambda c: (0, ci(c), 0)),
                pl.BlockSpec((B, W, L), lambda c: (0, 0, ci(c))),
                pl.BlockSpec((B, L, W), lambda c: (0, ci(c), 2)),
                pl.BlockSpec((B, grows.shape[1], L), lambda c: (0, 0, ci(c)))]

    m_spec = pl.BlockSpec(m0.shape, lambda c: (0, 0, 0))
    s_spec = pl.BlockSpec(s0.shape, lambda c: (0, 0, 0, 0))
    dir_args = [q, kt, ml, grows]
    return pl.pallas_call(
        functools.partial(_mlstm_kernel, nc),
        grid=(nc,),
        in_specs=specs(lambda c: c) + specs(lambda c: nc - 1 - c) + [m_spec, s_spec],
        out_specs=[pl.BlockSpec((B, L, W), lambda c: (0, c, 0)),
                   pl.BlockSpec((B, L, W), lambda c: (0, nc - 1 - c, 0)),
                   m_spec, s_spec],
        out_shape=[jax.ShapeDtypeStruct((B, T, W), F32),
                   jax.ShapeDtypeStruct((B, T, W), F32),
                   jax.ShapeDtypeStruct(m0.shape, F32),
                   jax.ShapeDtypeStruct(s0.shape, F32)],
        scratch_shapes=[pltpu.VMEM(m0.shape, F32), pltpu.VMEM(s0.shape, F32),
                        pltpu.VMEM((B * ML_HEADS, 2 * L, 2 * LANE), BF16)],
        compiler_params=_cparams("arbitrary"),
        name="mlstm",
    )(*dir_args, *dir_args, m0, s0)


def _softmax_pv(s_list, v_list):
    m = functools.reduce(jnp.maximum, [jnp.max(s, -1, keepdims=True) for s in s_list])
    ps = [jnp.exp(s - m) for s in s_list]
    l = functools.reduce(jnp.add, [jnp.sum(p, -1, keepdims=True) for p in ps])
    o = functools.reduce(jnp.add, [jnp.dot(p.astype(BF16), v, preferred_element_type=F32)
                                   for p, v in zip(ps, v_list)])
    return o / l


def _na_kernel(n_rows, q_ref, *refs):
    nb = NA_BAND // NA_KBLK_ROWS
    k_blks = refs[0:nb]
    v_blks = refs[nb:2 * nb]
    kc_ref, vc_ref, bt_ref, o_ref, kband, vband = refs[2 * nb:]
    i = pl.program_id(1)
    blk_tok = NA_KBLK_ROWS * GRID_W
    HD = NA_HEAD_DIM
    GW = NA_GROUP * HD
    head_of_lane = lax.broadcasted_iota(jnp.int32, (1, GW), 1) // HD
    for j in range(nb):
        kband[j * blk_tok:(j + 1) * blk_tok, :] = k_blks[j][0]
        vband[j * blk_tok:(j + 1) * blk_tok, :] = v_blks[j][0]
    band_row0 = jnp.clip(i * NA_QROWS - NA_ROWS // 2, 0, n_rows - NA_BAND)
    n_win = NA_ROWS * GRID_W
    zero = jnp.zeros((), BF16)

    def rows_body(t, carry):
        items = []
        for u in range(NA_UNROLL):
            a = t * NA_UNROLL + u
            r = i * NA_QROWS + a
            r0 = jnp.clip(r - NA_ROWS // 2, 0, n_rows - NA_ROWS)
            koff = pl.multiple_of((r0 - band_row0) * GRID_W, GRID_W)
            dr_first = r0 - r + (NA_ROWS - 1)
            qoff = pl.multiple_of(a * GRID_W, GRID_W)
            for gi in range(NA_HEADS // NA_GROUP):
                items.append((koff, dr_first, qoff, gi, slice(gi * GW, (gi + 1) * GW)))
        scores = []
        for koff, _, qoff, _, gs in items:
            qg = q_ref[0, pl.ds(qoff, GRID_W), gs]
            qm = jnp.concatenate([jnp.where(head_of_lane == g, qg, zero) for g in range(NA_GROUP)], axis=0)
            scores.append((_nt_dot(kband[pl.ds(koff, n_win), gs], qm),
                           _nt_dot(kc_ref[0, :, gs], qm)))
        probs = []
        for (_, dr_first, _, gi, _), (s_loc, s_ctx) in zip(items, scores):
            s_loc = s_loc + bt_ref[dr_first, gi]
            m = jnp.maximum(jnp.max(s_loc, 0, keepdims=True), jnp.max(s_ctx, 0, keepdims=True))
            e_loc = jnp.exp(s_loc - m)
            e_ctx = jnp.exp(s_ctx - m)
            inv = 1.0 / (jnp.sum(e_loc, 0, keepdims=True) + jnp.sum(e_ctx, 0, keepdims=True))
            probs.append(((e_loc * inv).astype(BF16), (e_ctx * inv).astype(BF16)))
        for (koff, _, qoff, _, gs), (p_loc, p_ctx) in zip(items, probs):
            res = (_tn_dot(p_loc, vband[pl.ds(koff, n_win), gs])
                   + _tn_dot(p_ctx, vc_ref[0, :, gs]))
            out = res[0:GRID_W]
            for g in range(1, NA_GROUP):
                out = jnp.where(head_of_lane == g, res[g * GRID_W:(g + 1) * GRID_W], out)
            o_ref[0, pl.ds(qoff, GRID_W), gs] = out.astype(BF16)
        return carry

    lax.fori_loop(0, NA_QROWS // NA_UNROLL, rows_body, 0)


def _na_latent(q, k, v, kc, vc, bias_tab):
    B, T, Wd = q.shape
    Lc = kc.shape[1]
    n_rows = T // GRID_W
    nb = NA_BAND // NA_KBLK_ROWS
    q_tok = NA_QROWS * GRID_W
    blk_tok = NA_KBLK_ROWS * GRID_W
    last_blk0 = (n_rows - NA_BAND) // NA_KBLK_ROWS

    def kv_spec(j):
        def idx(b, i):
            first = jnp.clip(i * (NA_QROWS // NA_KBLK_ROWS) - (NA_ROWS // 2) // NA_KBLK_ROWS, 0, last_blk0)
            return (b, first + j, 0)
        return pl.BlockSpec((1, blk_tok, Wd), idx)

    return pl.pallas_call(
        functools.partial(_na_kernel, n_rows),
        grid=(B, n_rows // NA_QROWS),
        in_specs=([pl.BlockSpec((1, q_tok, Wd), lambda b, i: (b, i, 0))]
                  + [kv_spec(j) for j in range(nb)] + [kv_spec(j) for j in range(nb)]
                  + [pl.BlockSpec((1, Lc, Wd), lambda b, i: (b, 0, 0)),
                     pl.BlockSpec((1, Lc, Wd), lambda b, i: (b, 0, 0)),
                     pl.BlockSpec(bias_tab.shape, lambda b, i: (0, 0, 0, 0))]),
        out_specs=pl.BlockSpec((1, q_tok, Wd), lambda b, i: (b, i, 0)),
        out_shape=jax.ShapeDtypeStruct((B, T, Wd), BF16),
        scratch_shapes=[pltpu.VMEM((NA_BAND * GRID_W, Wd), BF16),
                        pltpu.VMEM((NA_BAND * GRID_W, Wd), BF16)],
        compiler_params=_cparams("arbitrary", "arbitrary"),
        name="natten",
    )(q, *([k] * nb), *([v] * nb), kc, vc, bias_tab)


def _ctx_attn_kernel(q_ref, k_ref, v_ref, o_ref):
    HD = NA_HEAD_DIM
    for h in range(NA_HEADS):
        hs = slice(h * HD, (h + 1) * HD)
        s = _nt_dot(q_ref[0, :, hs], k_ref[0, :, hs])
        o_ref[0, :, hs] = _softmax_pv([s], [v_ref[0, :, hs]]).astype(BF16)


def _ctx_attn(q, k, v):
    B, Lc, Wd = q.shape
    spec = pl.BlockSpec((1, Lc, Wd), lambda b: (b, 0, 0))
    return pl.pallas_call(
        _ctx_attn_kernel, grid=(B,), in_specs=[spec, spec, spec], out_specs=spec,
        out_shape=jax.ShapeDtypeStruct((B, Lc, Wd), BF16),
        compiler_params=_cparams("arbitrary"), name="ctx_attn",
    )(q, k, v)


def _na_bias_table(rpb):
    qc = np.arange(GRID_W)[:, None]
    kcol = np.arange(GRID_W)[None, :]
    wstart = np.clip(qc - NA_COLS // 2, 0, GRID_W - NA_COLS)
    col_ok = (kcol >= wstart) & (kcol < wstart + NA_COLS)
    dc = np.clip(kcol - qc, -(NA_COLS - 1), NA_COLS - 1) + NA_COLS - 1
    per_dr = jnp.where(col_ok[None, None], rpb[:, :, dc], MASK_NEG)
    def tab(d):
        n_grp = NA_HEADS // NA_GROUP
        t = per_dr[:, d:d + NA_ROWS].reshape(n_grp, NA_GROUP, NA_ROWS, GRID_W, GRID_W)
        return jnp.transpose(t, (0, 2, 4, 1, 3)).reshape(n_grp, NA_ROWS * GRID_W, NA_GROUP * GRID_W)

    return jnp.stack([tab(d) for d in range(NA_ROWS)]).astype(F32)


def _route(sel, s):
    E = EXP_PER_GROUP
    scores = []
    for g in range(N_GROUPS):
        a, b, c, d = sel[E * g:E * (g + 1)]
        scores.append(functools.reduce(jnp.maximum, [a + b, a + c, a + d, b + c, b + d, c + d]))
    best = jnp.zeros_like(scores[0], dtype=jnp.int32)
    best_score = scores[0]
    for g in range(1, N_GROUPS):
        upd = scores[g] > best_score
        best = jnp.where(upd, g, best)
        best_score = jnp.where(upd, scores[g], best_score)

    def pick(rows, j):
        out = rows[j]
        for g in range(1, N_GROUPS):
            out = jnp.where(best == g, rows[E * g + j], out)
        return out

    v = [pick(sel, j) for j in range(E)]
    sv = [pick(s, j) for j in range(E)]
    i1 = jnp.zeros_like(best)
    m1 = v[0]
    for j in range(1, E):
        upd = v[j] > m1
        i1 = jnp.where(upd, j, i1)
        m1 = jnp.where(upd, v[j], m1)
    i2 = jnp.where(i1 == 0, 1, 0)
    m2 = jnp.where(i1 == 0, v[1], v[0])
    for j in range(1, E):
        upd = (i1 != j) & (v[j] > m2)
        i2 = jnp.where(upd, j, i2)
        m2 = jnp.where(upd, v[j], m2)

    def at(rows, idx):
        out = rows[0]
        for j in range(1, E):
            out = jnp.where(idx == j, rows[j], out)
        return out

    s1 = at(sv, i1)
    s2 = at(sv, i2)
    tot = s1 + s2
    return best * E + i1, best * E + i2, s1 / tot, s2 / tot


def _out_proj_kernel(n_in, *refs):
    (ygm_ref, hf_ref, hb_ref, o_ref, yna_ref, x_ref, mod_ref, w_ref, mlg_ref, n2g_ref,
     rw_ref, rb_ref) = refs[:12]
    xn_ref, h2_ref, re_ref, rwt_ref = refs[n_in:]
    hs = hf_ref[0] + hb_ref[0]
    hsq = hs * hs
    lane = lax.broadcasted_iota(jnp.int32, (1, ML_WIDTH), 1)
    scale = jnp.zeros_like(hs)
    for h in range(ML_HEADS):
        in_head = (lane >= h * ML_HEAD_DIM) & (lane < (h + 1) * ML_HEAD_DIM)
        ms = jnp.sum(jnp.where(in_head, hsq, 0.0), -1, keepdims=True) * (1.0 / ML_HEAD_DIM)
        scale = jnp.where(in_head, lax.rsqrt(ms + EPS), scale)
    yml = hs * scale * mlg_ref[...] * jax.nn.sigmoid(o_ref[0])
    o1 = GM_WIDTH
    o2 = GM_WIDTH + ML_WIDTH
    acc = (jnp.dot(ygm_ref[0], w_ref[0:o1, :], preferred_element_type=F32)
           + jnp.dot(yml.astype(BF16), w_ref[o1:o2, :], preferred_element_type=F32)
           + jnp.dot(yna_ref[0], w_ref[o2:, :], preferred_element_type=F32))
    xn = x_ref[0] + mod_ref[0, 0:1, :] * acc
    xn_ref[0] = xn
    h2 = xn * lax.rsqrt(jnp.mean(xn * xn, -1, keepdims=True) + EPS) * n2g_ref[...]
    h2 = h2 * (1.0 + mod_ref[0, 2:3, :]) + mod_ref[0, 1:2, :]
    h2_ref[...] = h2
    h_hi = h2.astype(BF16)
    h_lo = (h2 - h_hi.astype(F32)).astype(BF16)
    rw_hi = rw_ref[0]
    logits = _nt_dot(rw_hi, h_hi) + _nt_dot(rw_hi, h_lo) + _nt_dot(rw_ref[1], h_hi)
    s = jax.nn.sigmoid(logits)
    sel = s + rb_ref[...]
    rows = lambda m: [m[e:e + 1, :] for e in range(N_EXPERTS)]
    e1, e2, w1, w2 = _route(rows(sel), rows(s))
    re_ref[0, 0:1, :] = e1
    re_ref[0, 1:2, :] = e2
    rwt_ref[0, 0:1, :] = w1
    rwt_ref[0, 1:2, :] = w2


def _out_proj(ygm, hf, hb, ml, yna, x, mod, w_out, ml_g, n2_g, router_wt, router_b, h2_rows, h2_row0,
              h2_buf=None):
    G, R, D = x.shape
    tm = min(R, 512)
    row = lambda b, i: (b, i, 0)
    whole2 = lambda shape: pl.BlockSpec(shape, lambda b, i: (0, 0))
    in_specs = [pl.BlockSpec((1, tm, GM_WIDTH), row),
                pl.BlockSpec((1, tm, ML_WIDTH), row),
                pl.BlockSpec((1, tm, ML_WIDTH), row),
                pl.BlockSpec((1, tm, ML_WIDTH), lambda b, i: (b, i, 3)),
                pl.BlockSpec((1, tm, NA_WIDTH), row),
                pl.BlockSpec((1, tm, D), row),
                pl.BlockSpec((1, 3, D), lambda b, i: (b, 0, 0)),
                whole2((D, D)), whole2((1, ML_WIDTH)), whole2((1, D)),
                pl.BlockSpec((2, N_EXPERTS, D), lambda b, i: (0, 0, 0)), whole2((N_EXPERTS, 1))]
    args = [ygm, hf, hb, ml, yna, x, mod, w_out, ml_g, n2_g, router_wt, router_b]
    aliases = {}
    if h2_buf is not None:
        aliases = {len(args): 1}
        in_specs.append(pl.BlockSpec(memory_space=pl.ANY))
        args.append(h2_buf)
    blk0 = h2_row0 // tm
    return pl.pallas_call(
        functools.partial(_out_proj_kernel, len(args)),
        grid=(G, R // tm),
        in_specs=in_specs,
        out_specs=[pl.BlockSpec((1, tm, D), row),
                   pl.BlockSpec((tm, D), lambda b, i: (blk0 + b * (R // tm) + i, 0)),
                   pl.BlockSpec((1, 2, tm), lambda b, i: (b, 0, i)),
                   pl.BlockSpec((1, 2, tm), lambda b, i: (b, 0, i))],
        out_shape=[jax.ShapeDtypeStruct((G, R, D), F32), jax.ShapeDtypeStruct((h2_rows, D), F32),
                   jax.ShapeDtypeStruct((G, 2, R), jnp.int32), jax.ShapeDtypeStruct((G, 2, R), F32)],
        input_output_aliases=aliases,
        compiler_params=_cparams("arbitrary", "arbitrary"),
        name="out_proj",
    )(*args)


def _experts_kernel(te_ref, nu_ref, src_cur, src_nxt, h_hbm, wg_ref, wu_ref, wd_ref, o_ref,
                    acc_ref, wgb, wub, wdb, xbuf, sem):
    i = pl.program_id(0)
    n_used = nu_ref[0]
    used = i < n_used
    slot = lax.rem(i, 2)
    new_expert = (i == 0) | (te_ref[i] != te_ref[jnp.maximum(i - 1, 0)])

    def row_copy(src_ref, s, r):
        return pltpu.make_async_copy(h_hbm.at[pl.ds(src_ref[0, 0, r], 1), :], xbuf.at[s, pl.ds(r, 1), :],
                                     sem.at[s])

    def tile_wait(s):
        pltpu.make_async_copy(xbuf.at[s], xbuf.at[s], sem.at[s]).wait()

    @pl.when((i == 0) & used)
    def _():
        def body(r, carry):
            row_copy(src_cur, 0, r).start()
            return carry
        lax.fori_loop(0, MOE_TILE, body, 0, unroll=8)

    @pl.when(used & new_expert)
    def _():
        def cast_rows(c, carry):
            rs = pl.ds(pl.multiple_of(c * LANE, LANE), LANE)
            wgb[rs, :] = wg_ref[0, 0, rs, :].astype(BF16)
            wub[rs, :] = wu_ref[0, 0, rs, :].astype(BF16)
            wdb[rs, :] = wd_ref[0, 0, rs, :].astype(BF16)
            return carry
        lax.fori_loop(0, D_MODEL // LANE, cast_rows, 0)

    @pl.when(used)
    def _():
        tile_wait(slot)
        x = xbuf[slot].astype(BF16)
        nxt = 1 - slot
        n_chunks = D_EXPERT // EXPERT_CHUNK
        per = MOE_TILE // (2 * n_chunks)
        for j in range(n_chunks):
            cs = slice(j * EXPERT_CHUNK, (j + 1) * EXPERT_CHUNK)
            for r in range(2 * j * per, (2 * j + 1) * per):
                row_copy(src_nxt, nxt, r).start()
            g = jnp.dot(x, wgb[:, cs], preferred_element_type=F32)
            u = jnp.dot(x, wub[:, cs], preferred_element_type=F32)
            a = (g * jax.nn.sigmoid(g) * u).astype(BF16)
            for r in range((2 * j + 1) * per, (2 * j + 2) * per):
                row_copy(src_nxt, nxt, r).start()
            y = jnp.dot(a, wdb[cs, :], preferred_element_type=F32)
            if j == 0:
                acc_ref[...] = y
            else:
                acc_ref[...] += y
        o_ref[...] = acc_ref[...].astype(BF16)

        @pl.when(i + 1 >= n_used)
        def _():
            tile_wait(nxt)

    @pl.when(jnp.logical_not(used))
    def _():
        o_ref[...] = jnp.zeros_like(o_ref)


def _experts(l, tile_expert, n_used, src, h_all, wg, wu, wd):
    D = h_all.shape[1]
    n_tiles = src.shape[0] // MOE_TILE
    src3 = src.reshape(n_tiles, 1, MOE_TILE)
    wspec = lambda r, c: pl.BlockSpec((1, 1, r, c), lambda i, te, nu: (l, te[i], 0, 0))
    src_spec = lambda ahead: pl.BlockSpec(
        (1, 1, MOE_TILE), lambda i, te, nu: (jnp.minimum(i + ahead, n_tiles - 1), 0, 0),
        memory_space=pltpu.SMEM)
    return pl.pallas_call(
        _experts_kernel,
        grid_spec=pltpu.PrefetchScalarGridSpec(
            num_scalar_prefetch=2,
            grid=(n_tiles,),
            in_specs=[src_spec(0), src_spec(1), pl.BlockSpec(memory_space=pl.ANY),
                      wspec(D, D_EXPERT), wspec(D, D_EXPERT), wspec(D_EXPERT, D)],
            out_specs=pl.BlockSpec((MOE_TILE, D), lambda i, te, nu: (i, 0)),
            scratch_shapes=[pltpu.VMEM((MOE_TILE, D), F32), pltpu.VMEM((D, D_EXPERT), BF16),
                            pltpu.VMEM((D, D_EXPERT), BF16), pltpu.VMEM((D_EXPERT, D), BF16),
                            pltpu.VMEM((2, MOE_TILE, D), F32), pltpu.SemaphoreType.DMA((2,))]),
        out_shape=jax.ShapeDtypeStruct((n_tiles * MOE_TILE, D), BF16),
        compiler_params=_cparams("arbitrary"),
        name="experts",
    )(tile_expert, n_used, src3, src3, h_all, wg, wu, wd)


def _dispatch(e_idx):
    n_tok = e_idx.shape[0]
    n_asg = 2 * n_tok
    n_tiles = -(-n_asg // MOE_TILE) + N_EXPERTS
    flat_e = e_idx.reshape(-1)
    blk = MOE_TILE
    assert n_asg % blk == 0
    onehot = (flat_e[:, None] == jnp.arange(N_EXPERTS, dtype=jnp.int32)[None, :]).astype(BF16)
    onehot = onehot.reshape(n_asg // blk, blk, N_EXPERTS)
    tri = (jnp.arange(blk)[:, None] >= jnp.arange(blk)[None, :]).astype(BF16)
    within = jnp.einsum("ij,bje->bie", tri, onehot, preferred_element_type=F32)
    blk_tot = within[:, -1, :]
    blk_off = jnp.cumsum(blk_tot, axis=0) - blk_tot
    counts = (blk_off[-1] + blk_tot[-1]).astype(jnp.int32)
    tiles_e = (counts + MOE_TILE - 1) // MOE_TILE
    tile_end = jnp.cumsum(tiles_e)
    tile_start = tile_end - tiles_e
    row_base = (tile_start * MOE_TILE - 1).astype(F32)
    dest = jnp.sum((within + (blk_off + row_base)[:, None, :]) * onehot.astype(F32), -1)
    dest = dest.astype(jnp.int32).reshape(-1)
    n_used = tile_end[-1]
    tid = jnp.arange(n_tiles, dtype=jnp.int32)
    te = jnp.sum((tile_end[None, :] <= jnp.minimum(tid, n_used - 1)[:, None]).astype(jnp.int32), axis=1)
    te = jnp.minimum(te, N_EXPERTS - 1)
    flat_t = jnp.arange(n_asg, dtype=jnp.int32) // 2
    src = jnp.zeros((n_tiles * MOE_TILE,), jnp.int32).at[dest].set(
        flat_t, unique_indices=True, mode="promise_in_bounds")
    return dest.reshape(n_tok, 2), src, te, n_used.reshape(1).astype(jnp.int32)


def _combine_kernel(final, xn_ref, y0_ref, y1_ref, w_ref, g2_ref, fg_ref, o_ref):
    w = w_ref[0]
    f = y0_ref[0].astype(F32) * w[:, 0:1] + y1_ref[0].astype(F32) * w[:, 1:2]
    x = xn_ref[0] + g2_ref[0] * f
    if final:
        x = x * lax.rsqrt(jnp.mean(x * x, -1, keepdims=True) + EPS) * fg_ref[...]
    o_ref[0] = x


def _combine(xn, y0, y1, wt, g2, final_g, final):
    G, R, D = xn.shape
    tm = min(R, 512)
    row = lambda b, i: (b, i, 0)
    return pl.pallas_call(
        functools.partial(_combine_kernel, final),
        grid=(G, R // tm),
        in_specs=[pl.BlockSpec((1, tm, D), row), pl.BlockSpec((1, tm, D), row), pl.BlockSpec((1, tm, D), row),
                  pl.BlockSpec((1, tm, 2), row),
                  pl.BlockSpec((1, 1, D), lambda b, i: (b, 0, 0)),
                  pl.BlockSpec((1, D), lambda b, i: (0, 0))],
        out_specs=pl.BlockSpec((1, tm, D), row),
        out_shape=jax.ShapeDtypeStruct((G, R, D), F32),
        compiler_params=_cparams("arbitrary", "arbitrary"),
        name="combine_final" if final else "combine",
    )(xn, y0, y1, wt, g2, final_g)


def _rope_tables(T):
    lane = np.arange(LANE)
    half = ML_HEAD_DIM // 4
    inv = jnp.tile(ROPE_BASE ** (-jnp.arange(half, dtype=F32) / half), LANE // half)
    t = jnp.arange(T)
    pos = jnp.where(((lane // (2 * half)) % 2 == 0)[None, :], (t // GRID_W)[:, None], (t % GRID_W)[:, None])
    ang = pos.astype(F32) * inv[None, :]
    sign = np.where((lane // half) % 2 == 0, -1.0, 1.0).astype(np.float32)
    return jnp.cos(ang), jnp.sin(ang) * sign[None, :]


def _reorder_w_in(w_in):
    pad = jnp.zeros(w_in.shape[:2] + (LANE - 4 * ML_HEADS,), w_in.dtype)
    return jnp.concatenate([w_in[..., :OFF_GATES], w_in[..., OFF_NA:], w_in[..., OFF_GATES:OFF_NA], pad],
                           axis=-1).astype(BF16)


def _mixers(l, p, pc, prm, need_ctx):
    ygm, ml, gt, q, k, v = p
    ygmc, mlc, gtc, qc, kc, vc = pc
    B = ml.shape[0]
    conv = (prm["conv_w"][l], prm["conv_b"][l])
    nh = ML_HEADS
    tr = lambda g: jnp.transpose(
        jnp.concatenate([g[..., 0:nh], g[..., 2 * nh:3 * nh], g[..., nh:2 * nh], g[..., 3 * nh:4 * nh]], -1),
        (0, 2, 1))
    m_zero = jnp.zeros((B, 2 * nh, LANE), F32)
    s_zero = jnp.zeros((B, nh, ML_CHUNK, 2 * LANE), F32)
    qmc, kmc, grc = _mlstm_prep(mlc, prm["cos_c"], prm["sin_c"], *conv, tr(gtc), prm["gbias"][l])
    hfc, hbc, m_st, s_st = _mlstm(qmc, kmc, mlc, grc, m_zero, s_zero)
    qm, km, gr = _mlstm_prep(ml, prm["cos_l"], prm["sin_l"], *conv, tr(gt), prm["gbias"][l])
    hf, hb, _, _ = _mlstm(qm, km, ml, gr, m_st, s_st)
    y_na = _na_latent(q, k, v, kc, vc, prm["na_tab"][l])
    y = (ygm, hf, hb, ml, y_na)
    if not need_ctx:
        return y, None
    return y, (ygmc, hfc, hbc, mlc, _ctx_attn(qc, kc, vc))


def _rows(a, idx):
    return a.at[idx].get(mode="promise_in_bounds")


def _moe(l, h2_all, e_all, prm):
    dest, src, te, n_used = _dispatch(e_all)
    yb = _experts(l, te, n_used, src, h2_all, prm["wg"], prm["wu"], prm["wd"])
    return yb, dest


def kernel(x, c, ctx, c_ctx, ada_w, ada_b, norm1_g, norm2_g, w_in, w_out, gm_ws, gm_bs, gm_norm_g,
           ml_conv_w, ml_conv_b, ml_gate_b, ml_norm_g, na_rpb, router_w, router_b,
           moe_w_gate, moe_w_up, moe_w_down, final_g):
    B, T, D = x.shape
    Lc = ctx.shape[1]
    cos_l, sin_l = _rope_tables(T)
    prm = dict(
        gm_ws=gm_ws.astype(BF16),
        gm_bs_full=jnp.repeat(jnp.transpose(gm_bs, (0, 2, 1)), GM_HEAD_DIM, axis=-1),
        gm_g=gm_norm_g[:, None, :],
        conv_w=ml_conv_w, conv_b=ml_conv_b[:, None, :],
        gbias=ml_gate_b[:, jnp.array([0, 2, 1, 3])].reshape(DEPTH, 4 * ML_HEADS, 1),
        cos_l=cos_l, sin_l=sin_l,
        cos_c=jnp.ones((Lc, LANE), F32), sin_c=jnp.zeros((Lc, LANE), F32),
        na_tab=jax.vmap(_na_bias_table)(na_rpb),
        wg=moe_w_gate, wu=moe_w_up, wd=moe_w_down,
    )
    w_in_r = _reorder_w_in(w_in)
    w_out_b = w_out.astype(BF16)
    rw_t = jnp.transpose(router_w)
    rw_hi = rw_t.astype(BF16)
    router_wt = jnp.stack([rw_hi, (rw_t - rw_hi.astype(F32)).astype(BF16)])
    router_bc = router_b[:, None]

    cs = jnp.concatenate([c, c_ctx[None, :], jnp.zeros((8 - B - 1, D), F32)], axis=0)
    mods = _ada_all(cs, ada_w, ada_b).reshape(DEPTH, 8, 6, D)

    xc = ctx
    for l in range(DEPTH):
        need_ctx = l < DEPTH - 1
        mod_l = mods[l, :B]
        mod_c = jnp.broadcast_to(mods[l, B:B + 1], (B, 6, D))
        n1 = norm1_g[l][None, :]
        gm_args = (prm["gm_ws"][l], prm["gm_bs_full"][l], prm["gm_g"][l])
        p = _in_proj(x, mod_l[:, 0:2], n1, w_in_r[l], *gm_args)
        pc = _in_proj(xc, mod_c[:, 0:2], n1, w_in_r[l], *gm_args)
        y, yc = _mixers(l, p, pc, prm, need_ctx)
        op_args = (w_out_b[l], ml_norm_g[l][None, :], norm2_g[l][None, :], router_wt, router_bc)
        n_all = B * T + (B * Lc if need_ctx else 0)
        h2_buf = (jnp.zeros((n_all, D), F32) if l == 0 else h2_all) if need_ctx else None
        xn, h2_all, re, rw = _out_proj(*y, x, mod_l[:, 2:5], *op_args, n_all, 0, h2_buf)
        e_all = jnp.transpose(re, (0, 2, 1)).reshape(B * T, 2)
        if need_ctx:
            xnc, h2_all, rec, rwc = _out_proj(*yc, xc, mod_c[:, 2:5], *op_args, n_all, B * T, h2_all)
            e_all = jnp.concatenate([e_all, jnp.transpose(rec, (0, 2, 1)).reshape(B * Lc, 2)], axis=0)
        yb, dest = _moe(l, h2_all, e_all, prm)
        final = l == DEPTH - 1
        fg = final_g[None, :]
        n_lat = B * T
        picked = lambda lo, hi, k, L: _rows(yb, dest[lo:hi, k]).reshape(B, L, D)
        x = _combine(xn, picked(0, n_lat, 0, T), picked(0, n_lat, 1, T),
                     jnp.transpose(rw, (0, 2, 1)), mod_l[:, 5:6], fg, final)
        if need_ctx:
            xc = _combine(xnc, picked(n_lat, None, 0, Lc), picked(n_lat, None, 1, Lc),
                          jnp.transpose(rwc, (0, 2, 1)), mod_c[:, 5:6], fg, False)
    return x
```

```python
import functools

import jax
import jax.numpy as jnp
import numpy as np
from jax import lax
from jax.experimental import pallas as pl
from jax.experimental.pallas import tpu as pltpu

F32 = jnp.float32
BF16 = jnp.bfloat16

D_MODEL = 1024
DEPTH = 4
GRID_W = 64
EPS = 1e-6

GM_HEADS = 4
GM_WIDTH = D_MODEL // 4
GM_HEAD_DIM = GM_WIDTH // GM_HEADS
GM_CHUNK = 128

ML_HEADS = 4
ML_WIDTH = D_MODEL // 4
ML_HEAD_DIM = ML_WIDTH // ML_HEADS
ML_CHUNK = 128

NA_HEADS = 8
NA_WIDTH = D_MODEL // 2
NA_HEAD_DIM = NA_WIDTH // NA_HEADS
NA_ROWS = 8
NA_COLS = 16

ROPE_BASE = 10000.0

OFF_ML = 2 * GM_WIDTH
OFF_GATES = OFF_ML + 4 * ML_WIDTH
OFF_NA = OFF_GATES + 4 * ML_HEADS
N_IN = OFF_NA + 3 * NA_WIDTH

N_EXPERTS = 16
N_GROUPS = 4
EXP_PER_GROUP = N_EXPERTS // N_GROUPS
D_EXPERT = D_MODEL

LANE = 128
SEG_GM = (0, 2 * GM_WIDTH)
SEG_ML = (SEG_GM[1], SEG_GM[1] + 4 * ML_WIDTH)
SEG_Q = (SEG_ML[1], SEG_ML[1] + NA_WIDTH)
SEG_K = (SEG_Q[1], SEG_Q[1] + NA_WIDTH)
SEG_V = (SEG_K[1], SEG_K[1] + NA_WIDTH)
SEG_GATES = (SEG_V[1], SEG_V[1] + LANE)
N_IN_PAD = SEG_GATES[1]

NA_QROWS = 8
NA_GROUP = 4
NA_UNROLL = 4
NA_BAND = 16
NA_KBLK_ROWS = 4
MOE_TILE = 512
EXPERT_CHUNK = 512
MASK_NEG = -1e30
N_GATE_ROWS = 6
VMEM_LIMIT = 56 * 2 ** 20


def _cparams(*sem):
    return pltpu.CompilerParams(dimension_semantics=sem, vmem_limit_bytes=VMEM_LIMIT)


def _nt_dot(a, b, precision=None):
    return lax.dot_general(a, b, (((1,), (1,)), ((), ())), precision=precision,
                           preferred_element_type=F32)


def _tn_dot(a, b):
    return lax.dot_general(a, b, (((0,), (0,)), ((), ())), preferred_element_type=F32)


def _ada_kernel(c_ref, w_ref, b_ref, o_ref):
    c = c_ref[...]
    sc = c * jax.nn.sigmoid(c)
    o_ref[0] = jnp.dot(sc, w_ref[0], preferred_element_type=F32) + b_ref[0]


def _ada_all(cs, ada_w, ada_b):
    n_out = ada_w.shape[-1] // D_MODEL
    return pl.pallas_call(
        _ada_kernel,
        grid=(DEPTH, n_out),
        in_specs=[pl.BlockSpec((8, D_MODEL), lambda l, j: (0, 0)),
                  pl.BlockSpec((1, D_MODEL, D_MODEL), lambda l, j: (l, 0, j)),
                  pl.BlockSpec((1, 1, D_MODEL), lambda l, j: (l, 0, j))],
        out_specs=pl.BlockSpec((1, 8, D_MODEL), lambda l, j: (l, 0, j)),
        out_shape=jax.ShapeDtypeStruct((DEPTH, 8, n_out * D_MODEL), F32),
        compiler_params=_cparams("arbitrary", "arbitrary"),
        name="adaln",
    )(cs, ada_w, ada_b.reshape(DEPTH, 1, -1))


def _gmlp_chunk(z_pre, ws_ref, bs, g):
    lane = lax.broadcasted_iota(jnp.int32, (1, GM_WIDTH), 1)
    z = jax.nn.gelu(z_pre)
    u = z[:, :GM_WIDTH]
    v = z[:, GM_WIDTH:]
    mu = jnp.mean(v, -1, keepdims=True)
    vc = v - mu
    var = jnp.mean(vc * vc, -1, keepdims=True)
    vn = (vc * lax.rsqrt(var + EPS) * g).astype(BF16)
    sv = jnp.zeros((GM_CHUNK, GM_WIDTH), F32)
    for h in range(GM_HEADS):
        full = jnp.dot(ws_ref[h], vn, preferred_element_type=F32)
        in_head = (lane >= h * GM_HEAD_DIM) & (lane < (h + 1) * GM_HEAD_DIM)
        sv = jnp.where(in_head, full, sv)
    return u * (sv + bs)


def _in_proj_kernel(x_ref, mod_ref, g_ref, w_ref, ws_ref, bs_ref, gmg_ref,
                    ygm_ref, ml_ref, gt_ref, q_ref, k_ref, v_ref):
    x = x_ref[0]
    h = x * lax.rsqrt(jnp.mean(x * x, -1, keepdims=True) + EPS) * g_ref[...]
    h = h * (1.0 + mod_ref[0, 1:2, :]) + mod_ref[0, 0:1, :]
    hb = h.astype(BF16)

    def seg(s):
        return jnp.dot(hb, w_ref[:, s[0]:s[1]], preferred_element_type=F32)

    gm = seg(SEG_GM)
    ml_ref[0] = seg(SEG_ML)
    for c in range(gm.shape[0] // GM_CHUNK):
        rows = slice(c * GM_CHUNK, (c + 1) * GM_CHUNK)
        ygm_ref[0, rows, :] = _gmlp_chunk(gm[rows], ws_ref, bs_ref[...], gmg_ref[...]).astype(BF16)
    gt_ref[0] = seg(SEG_GATES)
    q_ref[0] = (seg(SEG_Q) * (NA_HEAD_DIM ** -0.5)).astype(BF16)
    k_ref[0] = seg(SEG_K).astype(BF16)
    v_ref[0] = seg(SEG_V).astype(BF16)


def _in_proj(x, mod, g, w, gm_ws, gm_bs_full, gm_g):
    G, R, D = x.shape
    tm = min(R, 512)
    row = lambda b, i: (b, i, 0)
    whole = lambda shape: pl.BlockSpec(shape, lambda b, i: (0,) * len(shape))
    widths = (GM_WIDTH, SEG_ML[1] - SEG_ML[0], LANE, NA_WIDTH, NA_WIDTH, NA_WIDTH)
    dtypes = (BF16, F32, F32, BF16, BF16, BF16)
    return pl.pallas_call(
        _in_proj_kernel,
        grid=(G, R // tm),
        in_specs=[pl.BlockSpec((1, tm, D), row),
                  pl.BlockSpec((1, 2, D), lambda b, i: (b, 0, 0)),
                  whole((1, D)), whole((D, N_IN_PAD)),
                  whole((GM_HEADS, GM_CHUNK, GM_CHUNK)), whole((GM_CHUNK, GM_WIDTH)), whole((1, GM_WIDTH))],
        out_specs=[pl.BlockSpec((1, tm, n), row) for n in widths],
        out_shape=[jax.ShapeDtypeStruct((G, R, n), dt) for n, dt in zip(widths, dtypes)],
        compiler_params=_cparams("arbitrary", "arbitrary"),
        name="in_proj",
    )(x, mod, g, w, gm_ws, gm_bs_full, gm_g)


def _log_sigmoid(x):
    return jnp.minimum(x, 0.0) - jnp.log1p(jnp.exp(-jnp.abs(x)))


def _rope(x, cos, sin_signed):
    lane = lax.broadcasted_iota(jnp.int32, (1, LANE), 1)
    first_half = (lane & 16) == 0
    partner = jnp.where(first_half, pltpu.roll(x, LANE - 16, 1), pltpu.roll(x, 16, 1))
    return x * cos + partner * sin_signed


def _chunk_scan(x, op, fill, reverse):
    pos = lax.broadcasted_iota(jnp.int32, (1, x.shape[1]), 1) % ML_CHUNK
    n = x.shape[1]
    sh = 1
    while sh < ML_CHUNK:
        if reverse:
            shifted = jnp.where(pos < ML_CHUNK - sh, pltpu.roll(x, n - sh, 1), fill)
        else:
            shifted = jnp.where(pos >= sh, pltpu.roll(x, sh, 1), fill)
        x = op(x, shifted)
        sh *= 2
    return x


def _mlstm_prep_kernel(n_blocks, x_ref, hp_ref, hn_ref, cos_ref, sin_ref, cw_ref, cb_ref, g_ref, gbias_ref,
                       q_ref, k_ref, go_ref):
    nh2 = 2 * ML_HEADS
    g = g_ref[0] + gbias_ref[...]
    is_fwd = lax.broadcasted_iota(jnp.int32, (nh2, 1), 0) < ML_HEADS
    li = g[0:nh2]
    lf = _log_sigmoid(g[nh2:])
    ps = _chunk_scan(lf, jnp.add, 0.0, False)
    ss = _chunk_scan(lf, jnp.add, 0.0, True)
    bcum = jnp.where(is_fwd, ps, ss)
    r = li - bcum
    pm = _chunk_scan(r, jnp.maximum, MASK_NEG, False)
    sm = _chunk_scan(r, jnp.maximum, MASK_NEG, True)
    rmax = jnp.maximum(pm, sm)
    groups = (bcum, r, jnp.where(is_fwd, pm, sm), jnp.exp(r - rmax), rmax, ps + ss - lf)
    for j, val in enumerate(groups):
        go_ref[0, nh2 * j:nh2 * (j + 1), :] = val

    i = pl.program_id(1)
    W = ML_WIDTH
    x = x_ref[0]
    tr = x.shape[0]
    rid = lax.broadcasted_iota(jnp.int32, (tr, 1), 0)
    i_row = jnp.zeros((1, 2 * W), jnp.int32) + i
    prev = jnp.where(i_row == 0, 0.0, hp_ref[0, 7:8, :])
    nxt = jnp.where(i_row == n_blocks - 1, 0.0, hn_ref[0, 0:1, :])
    xm1 = jnp.where(rid == 0, prev, pltpu.roll(x, 1, 0))
    xp1 = jnp.where(rid == tr - 1, nxt, pltpu.roll(x, tr - 1, 0))
    y = xm1 * cw_ref[0:1, :] + x * cw_ref[1:2, :] + xp1 * cw_ref[2:3, :] + cb_ref[...]
    y = y * jax.nn.sigmoid(y)
    cos = cos_ref[...]
    sin = sin_ref[...]
    parts = [_rope(y[:, j * LANE:(j + 1) * LANE], cos, sin) for j in range(2 * W // LANE)]
    q_ref[0] = (jnp.concatenate(parts[:W // LANE], axis=1) * (ML_HEAD_DIM ** -0.5)).astype(BF16)
    for j in range(W // LANE):
        k_ref[0, j * LANE:(j + 1) * LANE, :] = parts[W // LANE + j].T.astype(BF16)


def _mlstm_prep(ml, cos, sin, conv_w, conv_b, gates_t, gbias):
    B, T, _ = ml.shape
    n_g = gates_t.shape[1]
    W = ML_WIDTH
    tr = min(T, 1024)
    nb = T // tr
    halo_per_blk = tr // 8
    n_hblk = T // 8
    whole = lambda shape: pl.BlockSpec(shape, lambda b, i: (0, 0))
    return pl.pallas_call(
        functools.partial(_mlstm_prep_kernel, nb),
        grid=(B, nb),
        in_specs=[pl.BlockSpec((1, tr, 2 * W), lambda b, i: (b, i, 0)),
                  pl.BlockSpec((1, 8, 2 * W), lambda b, i: (b, jnp.maximum(i * halo_per_blk - 1, 0), 0)),
                  pl.BlockSpec((1, 8, 2 * W),
                               lambda b, i: (b, jnp.minimum((i + 1) * halo_per_blk, n_hblk - 1), 0)),
                  pl.BlockSpec((tr, LANE), lambda b, i: (i, 0)),
                  pl.BlockSpec((tr, LANE), lambda b, i: (i, 0)),
                  whole((3, 2 * W)), whole((1, 2 * W)),
                  pl.BlockSpec((1, n_g, tr), lambda b, i: (b, 0, i)), whole((n_g, 1))],
        out_specs=[pl.BlockSpec((1, tr, W), lambda b, i: (b, i, 0)),
                   pl.BlockSpec((1, W, tr), lambda b, i: (b, 0, i)),
                   pl.BlockSpec((1, 2 * ML_HEADS * N_GATE_ROWS, tr), lambda b, i: (b, 0, i))],
        out_shape=[jax.ShapeDtypeStruct((B, T, W), BF16), jax.ShapeDtypeStruct((B, W, T), BF16),
                   jax.ShapeDtypeStruct((B, 2 * ML_HEADS * N_GATE_ROWS, T), F32)],
        compiler_params=_cparams("arbitrary", "arbitrary"),
        name="mlstm_prep",
    )(ml, ml, ml, cos, sin, conv_w, conv_b, gates_t, gbias)


def _mlstm_kernel(n_chunks, qf_ref, ktf_ref, vf_ref, gf_ref, qb_ref, ktb_ref, vb_ref, gb_ref,
                  m0_ref, s0_ref, hf_ref, hb_ref, mfin_ref, sfin_ref, m_scr, s_scr, bd_scr):
    c = pl.program_id(0)
    L = ML_CHUNK
    HD = ML_HEAD_DIM
    NH = ML_HEADS
    n_batch = m_scr.shape[0]

    @pl.when(c == 0)
    def _():
        m_scr[...] = m0_ref[...]
        s_scr[...] = s0_ref[...]
        bd_scr[...] = jnp.zeros_like(bd_scr)

    def row_bcast(x, j):
        return jnp.broadcast_to(x[j:j + 1, :], (L, LANE))

    row = lax.broadcasted_iota(jnp.int32, (L, L), 0)
    col = lax.broadcasted_iota(jnp.int32, (L, L), 1)
    low_half = lax.broadcasted_iota(jnp.int32, (1, LANE), 1) < HD
    own_rows = jnp.concatenate([row < HD, row >= HD], axis=1)
    is_fwd = lax.broadcasted_iota(jnp.int32, (2 * NH, 1), 0) < NH

    gate = []
    for b in range(n_batch):
        gq = lambda i: jnp.where(is_fwd, gf_ref[b, 2 * NH * i:2 * NH * (i + 1), :],
                                 gb_ref[b, 2 * NH * i:2 * NH * (i + 1), :])
        bcum, r, rcmax, ew, rmax, b_last = (gq(i) for i in range(N_GATE_ROWS))
        m0 = m_scr[b]
        mu = jnp.maximum(m0, rcmax)
        emt = jnp.exp(-(bcum + mu))
        m_last = jnp.maximum(m0, rmax)
        m_scr[b] = b_last + m_last
        stack = jnp.concatenate([mu, emt, jnp.zeros((L - 4 * NH, LANE), F32)], axis=0)
        gate.append(dict(r=r, ew=ew, m0=m0, a_st=jnp.exp(m0 - m_last), g_st=jnp.exp(rmax - m_last),
                         cols=stack.T))

    dirs = ((qf_ref, ktf_ref, vf_ref, hf_ref), (qb_ref, ktb_ref, vb_ref, hb_ref))
    items = [(b, d, pr) for b in range(n_batch) for d in range(2) for pr in range(NH // 2)]

    pre = []
    for b, d, pr in items:
        q_ref, kt_ref, v_ref, _ = dirs[d]
        ps = slice(pr * LANE, (pr + 1) * LANE)
        ja = NH * d + 2 * pr
        g = gate[b]
        qp = q_ref[b, :, ps]
        kt = kt_ref[b, ps, :]
        vp = v_ref[b, :, ps]
        st = s_scr[b, 2 * d + pr]
        ktf = kt.astype(F32)
        kt2 = jnp.concatenate([jnp.where(row < HD, ktf, 0.0), jnp.where(row < HD, 0.0, ktf)],
                              axis=1).astype(BF16)
        s2 = jnp.dot(qp, kt2, preferred_element_type=F32)
        inter = jnp.dot(qp, st.astype(BF16), preferred_element_type=F32)
        vab = jnp.concatenate([jnp.where(low_half, vp, 1.0), jnp.where(low_half, 1.0, vp)],
                              axis=1).astype(BF16)
        ktw = ktf * jnp.where(row < HD, row_bcast(g["ew"], ja), row_bcast(g["ew"], ja + 1))
        upd = jnp.dot(ktw.astype(BF16), vab, preferred_element_type=F32)
        pre.append((s2, inter, vab, upd, st))

    probs = []
    for (b, d, pr), (s2, _, _, _, _) in zip(items, pre):
        ja = NH * d + 2 * pr
        g = gate[b]
        valid = (col <= row) if d == 0 else (col >= row)
        mu_c = [jnp.broadcast_to(g["cols"][:, j:j + 1], (L, LANE)) for j in (ja, ja + 1)]
        e2 = jnp.concatenate([jnp.where(valid, jnp.exp(row_bcast(g["r"], j) - mc), 0.0)
                              for j, mc in zip((ja, ja + 1), mu_c)], axis=1)
        a2 = jnp.concatenate([jnp.exp(row_bcast(g["m0"], j) - mc) for j, mc in zip((ja, ja + 1), mu_c)],
                             axis=1)
        probs.append(((s2 * e2).astype(BF16), a2))

    for it, ((b, d, pr), (_, inter, vab, upd, st), (p2, a2)) in enumerate(zip(items, pre, probs)):
        out_ref = dirs[d][3]
        ps = slice(pr * LANE, (pr + 1) * LANE)
        ja = NH * d + 2 * pr
        g = gate[b]
        bd_scr[it, 0:L, 0:LANE] = vab[:, 0:LANE]
        bd_scr[it, L:2 * L, LANE:2 * LANE] = vab[:, LANE:2 * LANE]
        intra = jnp.dot(p2, bd_scr[it], preferred_element_type=F32)
        tot = intra + a2 * inter
        tot_a = tot[:, 0:LANE]
        tot_b = tot[:, LANE:2 * LANE]
        num = jnp.where(low_half, tot_a, tot_b)
        den = pltpu.roll(jnp.where(low_half, tot_b, tot_a), HD, 1)
        floor = jnp.where(low_half, jnp.broadcast_to(g["cols"][:, 8 + ja:8 + ja + 1], (L, LANE)),
                          jnp.broadcast_to(g["cols"][:, 8 + ja + 1:8 + ja + 2], (L, LANE)))
        out_ref[b, :, ps] = num / jnp.maximum(jnp.abs(den), floor)
        a_s = jnp.concatenate([row_bcast(g["a_st"], ja), row_bcast(g["a_st"], ja + 1)], axis=1)
        g_s = jnp.concatenate([row_bcast(g["g_st"], ja), row_bcast(g["g_st"], ja + 1)], axis=1)
        s_scr[b, 2 * d + pr] = a_s * st + g_s * jnp.where(own_rows, upd, 0.0)

    @pl.when(c == n_chunks - 1)
    def _():
        mfin_ref[...] = m_scr[...]
        sfin_ref[...] = s_scr[...]


def _mlstm(q, kt, ml, grows, m0, s0):
    B, T, W = q.shape
    L = ML_CHUNK
    nc = T // L

    def specs(ci):
        return [pl.BlockSpec((B, L, W), lambda c: (0, ci(c), 0)),
                pl.BlockSpec((B, W, L), lambda c: (0, 0, ci(c))),
                pl.BlockSpec((B, L, W), lambda c: (0, ci(c), 2)),
                pl.BlockSpec((B, grows.shape[1], L), lambda c: (0, 0, ci(c)))]

    m_spec = pl.BlockSpec(m0.shape, lambda c: (0, 0, 0))
    s_spec = pl.BlockSpec(s0.shape, lambda c: (0, 0, 0, 0))
    dir_args = [q, kt, ml, grows]
    return pl.pallas_call(
        functools.partial(_mlstm_kernel, nc),
        grid=(nc,),
        in_specs=specs(lambda c: c) + specs(lambda c: nc - 1 - c) + [m_spec, s_spec],
        out_specs=[pl.BlockSpec((B, L, W), lambda c: (0, c, 0)),
                   pl.BlockSpec((B, L, W), lambda c: (0, nc - 1 - c, 0)),
                   m_spec, s_spec],
        out_shape=[jax.ShapeDtypeStruct((B, T, W), F32),
                   jax.ShapeDtypeStruct((B, T, W), F32),
                   jax.ShapeDtypeStruct(m0.shape, F32),
                   jax.ShapeDtypeStruct(s0.shape, F32)],
        scratch_shapes=[pltpu.VMEM(m0.shape, F32), pltpu.VMEM(s0.shape, F32),
                        pltpu.VMEM((B * ML_HEADS, 2 * L, 2 * LANE), BF16)],
        compiler_params=_cparams("arbitrary"),
        name="mlstm",
    )(*dir_args, *dir_args, m0, s0)


def _softmax_pv(s_list, v_list):
    m = functools.reduce(jnp.maximum, [jnp.max(s, -1, keepdims=True) for s in s_list])
    ps = [jnp.exp(s - m) for s in s_list]
    l = functools.reduce(jnp.add, [jnp.sum(p, -1, keepdims=True) for p in ps])
    o = functools.reduce(jnp.add, [jnp.dot(p.astype(BF16), v, preferred_element_type=F32)
                                   for p, v in zip(ps, v_list)])
    return o / l


def _na_kernel(n_rows, q_ref, *refs):
    nb = NA_BAND // NA_KBLK_ROWS
    k_blks = refs[0:nb]
    v_blks = refs[nb:2 * nb]
    kc_ref, vc_ref, bt_ref, o_ref, kband, vband = refs[2 * nb:]
    i = pl.program_id(1)
    blk_tok = NA_KBLK_ROWS * GRID_W
    HD = NA_HEAD_DIM
    GW = NA_GROUP * HD
    head_of_lane = lax.broadcasted_iota(jnp.int32, (1, GW), 1) // HD
    for j in range(nb):
        kband[j * blk_tok:(j + 1) * blk_tok, :] = k_blks[j][0]
        vband[j * blk_tok:(j + 1) * blk_tok, :] = v_blks[j][0]
    band_row0 = jnp.clip(i * NA_QROWS - NA_ROWS // 2, 0, n_rows - NA_BAND)
    n_win = NA_ROWS * GRID_W
    zero = jnp.zeros((), BF16)

    def rows_body(t, carry):
        items = []
        for u in range(NA_UNROLL):
            a = t * NA_UNROLL + u
            r = i * NA_QROWS + a
            r0 = jnp.clip(r - NA_ROWS // 2, 0, n_rows - NA_ROWS)
            koff = pl.multiple_of((r0 - band_row0) * GRID_W, GRID_W)
            dr_first = r0 - r + (NA_ROWS - 1)
            qoff = pl.multiple_of(a * GRID_W, GRID_W)
            for gi in range(NA_HEADS // NA_GROUP):
                items.append((koff, dr_first, qoff, gi, slice(gi * GW, (gi + 1) * GW)))
        scores = []
        for koff, _, qoff, _, gs in items:
            qg = q_ref[0, pl.ds(qoff, GRID_W), gs]
            qm = jnp.concatenate([jnp.where(head_of_lane == g, qg, zero) for g in range(NA_GROUP)], axis=0)
            scores.append((_nt_dot(kband[pl.ds(koff, n_win), gs], qm),
                           _nt_dot(kc_ref[0, :, gs], qm)))
        probs = []
        for (_, dr_first, _, gi, _), (s_loc, s_ctx) in zip(items, scores):
            s_loc = s_loc + bt_ref[dr_first, gi]
            m = jnp.maximum(jnp.max(s_loc, 0, keepdims=True), jnp.max(s_ctx, 0, keepdims=True))
            e_loc = jnp.exp(s_loc - m)
            e_ctx = jnp.exp(s_ctx - m)
            inv = 1.0 / (jnp.sum(e_loc, 0, keepdims=True) + jnp.sum(e_ctx, 0, keepdims=True))
            probs.append(((e_loc * inv).astype(BF16), (e_ctx * inv).astype(BF16)))
        for (koff, _, qoff, _, gs), (p_loc, p_ctx) in zip(items, probs):
            res = (_tn_dot(p_loc, vband[pl.ds(koff, n_win), gs])
                   + _tn_dot(p_ctx, vc_ref[0, :, gs]))
            out = res[0:GRID_W]
            for g in range(1, NA_GROUP):
                out = jnp.where(head_of_lane == g, res[g * GRID_W:(g + 1) * GRID_W], out)
            o_ref[0, pl.ds(qoff, GRID_W), gs] = out.astype(BF16)
        return carry

    lax.fori_loop(0, NA_QROWS // NA_UNROLL, rows_body, 0)


def _na_latent(q, k, v, kc, vc, bias_tab):
    B, T, Wd = q.shape
    Lc = kc.shape[1]
    n_rows = T // GRID_W
    nb = NA_BAND // NA_KBLK_ROWS
    q_tok = NA_QROWS * GRID_W
    blk_tok = NA_KBLK_ROWS * GRID_W
    last_blk0 = (n_rows - NA_BAND) // NA_KBLK_ROWS

    def kv_spec(j):
        def idx(b, i):
            first = jnp.clip(i * (NA_QROWS // NA_KBLK_ROWS) - (NA_ROWS // 2) // NA_KBLK_ROWS, 0, last_blk0)
            return (b, first + j, 0)
        return pl.BlockSpec((1, blk_tok, Wd), idx)

    return pl.pallas_call(
        functools.partial(_na_kernel, n_rows),
        grid=(B, n_rows // NA_QROWS),
        in_specs=([pl.BlockSpec((1, q_tok, Wd), lambda b, i: (b, i, 0))]
                  + [kv_spec(j) for j in range(nb)] + [kv_spec(j) for j in range(nb)]
                  + [pl.BlockSpec((1, Lc, Wd), lambda b, i: (b, 0, 0)),
                     pl.BlockSpec((1, Lc, Wd), lambda b, i: (b, 0, 0)),
                     pl.BlockSpec(bias_tab.shape, lambda b, i: (0, 0, 0, 0))]),
        out_specs=pl.BlockSpec((1, q_tok, Wd), lambda b, i: (b, i, 0)),
        out_shape=jax.ShapeDtypeStruct((B, T, Wd), BF16),
        scratch_shapes=[pltpu.VMEM((NA_BAND * GRID_W, Wd), BF16),
                        pltpu.VMEM((NA_BAND * GRID_W, Wd), BF16)],
        compiler_params=_cparams("arbitrary", "arbitrary"),
        name="natten",
    )(q, *([k] * nb), *([v] * nb), kc, vc, bias_tab)


def _ctx_attn_kernel(q_ref, k_ref, v_ref, o_ref):
    HD = NA_HEAD_DIM
    for h in range(NA_HEADS):
        hs = slice(h * HD, (h + 1) * HD)
        s = _nt_dot(q_ref[0, :, hs], k_ref[0, :, hs])
        o_ref[0, :, hs] = _softmax_pv([s], [v_ref[0, :, hs]]).astype(BF16)


def _ctx_attn(q, k, v):
    B, Lc, Wd = q.shape
    spec = pl.BlockSpec((1, Lc, Wd), lambda b: (b, 0, 0))
    return pl.pallas_call(
        _ctx_attn_kernel, grid=(B,), in_specs=[spec, spec, spec], out_specs=spec,
        out_shape=jax.ShapeDtypeStruct((B, Lc, Wd), BF16),
        compiler_params=_cparams("arbitrary"), name="ctx_attn",
    )(q, k, v)


def _na_bias_table(rpb):
    qc = np.arange(GRID_W)[:, None]
    kcol = np.arange(GRID_W)[None, :]
    wstart = np.clip(qc - NA_COLS // 2, 0, GRID_W - NA_COLS)
    col_ok = (kcol >= wstart) & (kcol < wstart + NA_COLS)
    dc = np.clip(kcol - qc, -(NA_COLS - 1), NA_COLS - 1) + NA_COLS - 1
    per_dr = jnp.where(col_ok[None, None], rpb[:, :, dc], MASK_NEG)
    def tab(d):
        n_grp = NA_HEADS // NA_GROUP
        t = per_dr[:, d:d + NA_ROWS].reshape(n_grp, NA_GROUP, NA_ROWS, GRID_W, GRID_W)
        return jnp.transpose(t, (0, 2, 4, 1, 3)).reshape(n_grp, NA_ROWS * GRID_W, NA_GROUP * GRID_W)

    return jnp.stack([tab(d) for d in range(NA_ROWS)]).astype(F32)


def _route(sel, s):
    E = EXP_PER_GROUP
    scores = []
    for g in range(N_GROUPS):
        a, b, c, d = sel[E * g:E * (g + 1)]
        scores.append(functools.reduce(jnp.maximum, [a + b, a + c, a + d, b + c, b + d, c + d]))
    best = jnp.zeros_like(scores[0], dtype=jnp.int32)
    best_score = scores[0]
    for g in range(1, N_GROUPS):
        upd = scores[g] > best_score
        best = jnp.where(upd, g, best)
        best_score = jnp.where(upd, scores[g], best_score)

    def pick(rows, j):
        out = rows[j]
        for g in range(1, N_GROUPS):
            out = jnp.where(best == g, rows[E * g + j], out)
        return out

    v = [pick(sel, j) for j in range(E)]
    sv = [pick(s, j) for j in range(E)]
    i1 = jnp.zeros_like(best)
    m1 = v[0]
    for j in range(1, E):
        upd = v[j] > m1
        i1 = jnp.where(upd, j, i1)
        m1 = jnp.where(upd, v[j], m1)
    i2 = jnp.where(i1 == 0, 1, 0)
    m2 = jnp.where(i1 == 0, v[1], v[0])
    for j in range(1, E):
        upd = (i1 != j) & (v[j] > m2)
        i2 = jnp.where(upd, j, i2)
        m2 = jnp.where(upd, v[j], m2)

    def at(rows, idx):
        out = rows[0]
        for j in range(1, E):
            out = jnp.where(idx == j, rows[j], out)
        return out

    s1 = at(sv, i1)
    s2 = at(sv, i2)
    tot = s1 + s2
    return best * E + i1, best * E + i2, s1 / tot, s2 / tot


def _out_proj_kernel(n_in, *refs):
    (ygm_ref, hf_ref, hb_ref, o_ref, yna_ref, x_ref, mod_ref, w_ref, mlg_ref, n2g_ref,
     rw_ref, rb_ref) = refs[:12]
    xn_ref, h2_ref, re_ref, rwt_ref = refs[n_in:]
    hs = hf_ref[0] + hb_ref[0]
    hsq = hs * hs
    lane = lax.broadcasted_iota(jnp.int32, (1, ML_WIDTH), 1)
    scale = jnp.zeros_like(hs)
    for h in range(ML_HEADS):
        in_head = (lane >= h * ML_HEAD_DIM) & (lane < (h + 1) * ML_HEAD_DIM)
        ms = jnp.sum(jnp.where(in_head, hsq, 0.0), -1, keepdims=True) * (1.0 / ML_HEAD_DIM)
        scale = jnp.where(in_head, lax.rsqrt(ms + EPS), scale)
    yml = hs * scale * mlg_ref[...] * jax.nn.sigmoid(o_ref[0])
    o1 = GM_WIDTH
    o2 = GM_WIDTH + ML_WIDTH
    acc = (jnp.dot(ygm_ref[0], w_ref[0:o1, :], preferred_element_type=F32)
           + jnp.dot(yml.astype(BF16), w_ref[o1:o2, :], preferred_element_type=F32)
           + jnp.dot(yna_ref[0], w_ref[o2:, :], preferred_element_type=F32))
    xn = x_ref[0] + mod_ref[0, 0:1, :] * acc
    xn_ref[0] = xn
    h2 = xn * lax.rsqrt(jnp.mean(xn * xn, -1, keepdims=True) + EPS) * n2g_ref[...]
    h2 = h2 * (1.0 + mod_ref[0, 2:3, :]) + mod_ref[0, 1:2, :]
    h2_ref[...] = h2
    h_hi = h2.astype(BF16)
    h_lo = (h2 - h_hi.astype(F32)).astype(BF16)
    rw_hi = rw_ref[0]
    logits = _nt_dot(rw_hi, h_hi) + _nt_dot(rw_hi, h_lo) + _nt_dot(rw_ref[1], h_hi)
    s = jax.nn.sigmoid(logits)
    sel = s + rb_ref[...]
    rows = lambda m: [m[e:e + 1, :] for e in range(N_EXPERTS)]
    e1, e2, w1, w2 = _route(rows(sel), rows(s))
    re_ref[0, 0:1, :] = e1
    re_ref[0, 1:2, :] = e2
    rwt_ref[0, 0:1, :] = w1
    rwt_ref[0, 1:2, :] = w2


def _out_proj(ygm, hf, hb, ml, yna, x, mod, w_out, ml_g, n2_g, router_wt, router_b, h2_rows, h2_row0,
              h2_buf=None):
    G, R, D = x.shape
    tm = min(R, 512)
    row = lambda b, i: (b, i, 0)
    whole2 = lambda shape: pl.BlockSpec(shape, lambda b, i: (0, 0))
    in_specs = [pl.BlockSpec((1, tm, GM_WIDTH), row),
                pl.BlockSpec((1, tm, ML_WIDTH), row),
                pl.BlockSpec((1, tm, ML_WIDTH), row),
                pl.BlockSpec((1, tm, ML_WIDTH), lambda b, i: (b, i, 3)),
                pl.BlockSpec((1, tm, NA_WIDTH), row),
                pl.BlockSpec((1, tm, D), row),
                pl.BlockSpec((1, 3, D), lambda b, i: (b, 0, 0)),
                whole2((D, D)), whole2((1, ML_WIDTH)), whole2((1, D)),
                pl.BlockSpec((2, N_EXPERTS, D), lambda b, i: (0, 0, 0)), whole2((N_EXPERTS, 1))]
    args = [ygm, hf, hb, ml, yna, x, mod, w_out, ml_g, n2_g, router_wt, router_b]
    aliases = {}
    if h2_buf is not None:
        aliases = {len(args): 1}
        in_specs.append(pl.BlockSpec(memory_space=pl.ANY))
        args.append(h2_buf)
    blk0 = h2_row0 // tm
    return pl.pallas_call(
        functools.partial(_out_proj_kernel, len(args)),
        grid=(G, R // tm),
        in_specs=in_specs,
        out_specs=[pl.BlockSpec((1, tm, D), row),
                   pl.BlockSpec((tm, D), lambda b, i: (blk0 + b * (R // tm) + i, 0)),
                   pl.BlockSpec((1, 2, tm), lambda b, i: (b, 0, i)),
                   pl.BlockSpec((1, 2, tm), lambda b, i: (b, 0, i))],
        out_shape=[jax.ShapeDtypeStruct((G, R, D), F32), jax.ShapeDtypeStruct((h2_rows, D), F32),
                   jax.ShapeDtypeStruct((G, 2, R), jnp.int32), jax.ShapeDtypeStruct((G, 2, R), F32)],
        input_output_aliases=aliases,
        compiler_params=_cparams("arbitrary", "arbitrary"),
        name="out_proj",
    )(*args)


def _experts_kernel(te_ref, nu_ref, src_cur, src_nxt, h_hbm, wg_ref, wu_ref, wd_ref, o_ref,
                    acc_ref, wgb, wub, wdb, xbuf, sem):
    i = pl.program_id(0)
    n_used = nu_ref[0]
    used = i < n_used
    slot = lax.rem(i, 2)
    new_expert = (i == 0) | (te_ref[i] != te_ref[jnp.maximum(i - 1, 0)])

    def row_copy(src_ref, s, r):
        return pltpu.make_async_copy(h_hbm.at[pl.ds(src_ref[0, 0, r], 1), :], xbuf.at[s, pl.ds(r, 1), :],
                                     sem.at[s])

    def tile_wait(s):
        pltpu.make_async_copy(xbuf.at[s], xbuf.at[s], sem.at[s]).wait()

    @pl.when((i == 0) & used)
    def _():
        def body(r, carry):
            row_copy(src_cur, 0, r).start()
            return carry
        lax.fori_loop(0, MOE_TILE, body, 0, unroll=8)

    @pl.when(used & new_expert)
    def _():
        def cast_rows(c, carry):
            rs = pl.ds(pl.multiple_of(c * LANE, LANE), LANE)
            wgb[rs, :] = wg_ref[0, 0, rs, :].astype(BF16)
            wub[rs, :] = wu_ref[0, 0, rs, :].astype(BF16)
            wdb[rs, :] = wd_ref[0, 0, rs, :].astype(BF16)
            return carry
        lax.fori_loop(0, D_MODEL // LANE, cast_rows, 0)

    @pl.when(used)
    def _():
        tile_wait(slot)
        x = xbuf[slot].astype(BF16)
        nxt = 1 - slot
        n_chunks = D_EXPERT // EXPERT_CHUNK
        per = MOE_TILE // (2 * n_chunks)
        for j in range(n_chunks):
            cs = slice(j * EXPERT_CHUNK, (j + 1) * EXPERT_CHUNK)
            for r in range(2 * j * per, (2 * j + 1) * per):
                row_copy(src_nxt, nxt, r).start()
            g = jnp.dot(x, wgb[:, cs], preferred_element_type=F32)
            u = jnp.dot(x, wub[:, cs], preferred_element_type=F32)
            a = (g * jax.nn.sigmoid(g) * u).astype(BF16)
            for r in range((2 * j + 1) * per, (2 * j + 2) * per):
                row_copy(src_nxt, nxt, r).start()
            y = jnp.dot(a, wdb[cs, :], preferred_element_type=F32)
            if j == 0:
                acc_ref[...] = y
            else:
                acc_ref[...] += y
        o_ref[...] = acc_ref[...].astype(BF16)

        @pl.when(i + 1 >= n_used)
        def _():
            tile_wait(nxt)

    @pl.when(jnp.logical_not(used))
    def _():
        o_ref[...] = jnp.zeros_like(o_ref)


def _experts(l, tile_expert, n_used, src, h_all, wg, wu, wd):
    D = h_all.shape[1]
    n_tiles = src.shape[0] // MOE_TILE
    src3 = src.reshape(n_tiles, 1, MOE_TILE)
    wspec = lambda r, c: pl.BlockSpec((1, 1, r, c), lambda i, te, nu: (l, te[i], 0, 0))
    src_spec = lambda ahead: pl.BlockSpec(
        (1, 1, MOE_TILE), lambda i, te, nu: (jnp.minimum(i + ahead, n_tiles - 1), 0, 0),
        memory_space=pltpu.SMEM)
    return pl.pallas_call(
        _experts_kernel,
        grid_spec=pltpu.PrefetchScalarGridSpec(
            num_scalar_prefetch=2,
            grid=(n_tiles,),
            in_specs=[src_spec(0), src_spec(1), pl.BlockSpec(memory_space=pl.ANY),
                      wspec(D, D_EXPERT), wspec(D, D_EXPERT), wspec(D_EXPERT, D)],
            out_specs=pl.BlockSpec((MOE_TILE, D), lambda i, te, nu: (i, 0)),
            scratch_shapes=[pltpu.VMEM((MOE_TILE, D), F32), pltpu.VMEM((D, D_EXPERT), BF16),
                            pltpu.VMEM((D, D_EXPERT), BF16), pltpu.VMEM((D_EXPERT, D), BF16),
                            pltpu.VMEM((2, MOE_TILE, D), F32), pltpu.SemaphoreType.DMA((2,))]),
        out_shape=jax.ShapeDtypeStruct((n_tiles * MOE_TILE, D), BF16),
        compiler_params=_cparams("arbitrary"),
        name="experts",
    )(tile_expert, n_used, src3, src3, h_all, wg, wu, wd)


def _dispatch(e_idx):
    n_tok = e_idx.shape[0]
    n_asg = 2 * n_tok
    n_tiles = -(-n_asg // MOE_TILE) + N_EXPERTS
    flat_e = e_idx.reshape(-1)
    blk = MOE_TILE
    assert n_asg % blk == 0
    onehot = (flat_e[:, None] == jnp.arange(N_EXPERTS, dtype=jnp.int32)[None, :]).astype(BF16)
    onehot = onehot.reshape(n_asg // blk, blk, N_EXPERTS)
    tri = (jnp.arange(blk)[:, None] >= jnp.arange(blk)[None, :]).astype(BF16)
    within = jnp.einsum("ij,bje->bie", tri, onehot, preferred_element_type=F32)
    blk_tot = within[:, -1, :]
    blk_off = jnp.cumsum(blk_tot, axis=0) - blk_tot
    counts = (blk_off[-1] + blk_tot[-1]).astype(jnp.int32)
    tiles_e = (counts + MOE_TILE - 1) // MOE_TILE
    tile_end = jnp.cumsum(tiles_e)
    tile_start = tile_end - tiles_e
    row_base = (tile_start * MOE_TILE - 1).astype(F32)
    dest = jnp.sum((within + (blk_off + row_base)[:, None, :]) * onehot.astype(F32), -1)
    dest = dest.astype(jnp.int32).reshape(-1)
    n_used = tile_end[-1]
    tid = jnp.arange(n_tiles, dtype=jnp.int32)
    te = jnp.sum((tile_end[None, :] <= jnp.minimum(tid, n_used - 1)[:, None]).astype(jnp.int32), axis=1)
    te = jnp.minimum(te, N_EXPERTS - 1)
    src = _invert_rows(dest, n_tiles * MOE_TILE)
    return dest.reshape(n_tok, 2), src, te, n_used.reshape(1).astype(jnp.int32)


SRC_CHUNK = 1024


def _invert_rows_kernel(n_rows, dest_ref, src_ref):
    i = pl.program_id(0)

    @pl.when(i == 0)
    def _():
        def clear(r, carry):
            src_ref[r] = 0
            return carry
        lax.fori_loop(0, n_rows, clear, 0, unroll=8)

    def place(a, carry):
        src_ref[dest_ref[0, 0, a]] = lax.shift_right_logical(i * SRC_CHUNK + a, 1)
        return carry
    lax.fori_loop(0, SRC_CHUNK, place, 0, unroll=8)


def _invert_rows(dest, n_rows):
    n_asg = dest.shape[0]
    assert n_asg % SRC_CHUNK == 0
    return pl.pallas_call(
        functools.partial(_invert_rows_kernel, n_rows),
        grid=(n_asg // SRC_CHUNK,),
        in_specs=[pl.BlockSpec((1, 1, SRC_CHUNK), lambda i: (i, 0, 0), memory_space=pltpu.SMEM)],
        out_specs=pl.BlockSpec(memory_space=pltpu.SMEM),
        out_shape=jax.ShapeDtypeStruct((n_rows,), jnp.int32),
        compiler_params=_cparams("arbitrary"),
        name="invert_rows",
    )(dest.reshape(n_asg // SRC_CHUNK, 1, SRC_CHUNK))


def _combine_kernel(final, xn_ref, y0_ref, y1_ref, w_ref, g2_ref, fg_ref, o_ref):
    w = w_ref[0]
    f = y0_ref[0].astype(F32) * w[:, 0:1] + y1_ref[0].astype(F32) * w[:, 1:2]
    x = xn_ref[0] + g2_ref[0] * f
    if final:
        x = x * lax.rsqrt(jnp.mean(x * x, -1, keepdims=True) + EPS) * fg_ref[...]
    o_ref[0] = x


def _combine(xn, y0, y1, wt, g2, final_g, final):
    G, R, D = xn.shape
    tm = min(R, 512)
    row = lambda b, i: (b, i, 0)
    return pl.pallas_call(
        functools.partial(_combine_kernel, final),
        grid=(G, R // tm),
        in_specs=[pl.BlockSpec((1, tm, D), row), pl.BlockSpec((1, tm, D), row), pl.BlockSpec((1, tm, D), row),
                  pl.BlockSpec((1, tm, 2), row),
                  pl.BlockSpec((1, 1, D), lambda b, i: (b, 0, 0)),
                  pl.BlockSpec((1, D), lambda b, i: (0, 0))],
        out_specs=pl.BlockSpec((1, tm, D), row),
        out_shape=jax.ShapeDtypeStruct((G, R, D), F32),
        compiler_params=_cparams("arbitrary", "arbitrary"),
        name="combine_final" if final else "combine",
    )(xn, y0, y1, wt, g2, final_g)


def _rope_tables(T):
    lane = np.arange(LANE)
    half = ML_HEAD_DIM // 4
    inv = jnp.tile(ROPE_BASE ** (-jnp.arange(half, dtype=F32) / half), LANE // half)
    t = jnp.arange(T)
    pos = jnp.where(((lane // (2 * half)) % 2 == 0)[None, :], (t // GRID_W)[:, None], (t % GRID_W)[:, None])
    ang = pos.astype(F32) * inv[None, :]
    sign = np.where((lane // half) % 2 == 0, -1.0, 1.0).astype(np.float32)
    return jnp.cos(ang), jnp.sin(ang) * sign[None, :]


def _reorder_w_in(w_in):
    pad = jnp.zeros(w_in.shape[:2] + (LANE - 4 * ML_HEADS,), w_in.dtype)
    return jnp.concatenate([w_in[..., :OFF_GATES], w_in[..., OFF_NA:], w_in[..., OFF_GATES:OFF_NA], pad],
                           axis=-1).astype(BF16)


def _mixers(l, p, pc, prm, need_ctx):
    ygm, ml, gt, q, k, v = p
    ygmc, mlc, gtc, qc, kc, vc = pc
    B = ml.shape[0]
    conv = (prm["conv_w"][l], prm["conv_b"][l])
    nh = ML_HEADS
    tr = lambda g: jnp.transpose(
        jnp.concatenate([g[..., 0:nh], g[..., 2 * nh:3 * nh], g[..., nh:2 * nh], g[..., 3 * nh:4 * nh]], -1),
        (0, 2, 1))
    m_zero = jnp.zeros((B, 2 * nh, LANE), F32)
    s_zero = jnp.zeros((B, nh, ML_CHUNK, 2 * LANE), F32)
    qmc, kmc, grc = _mlstm_prep(mlc, prm["cos_c"], prm["sin_c"], *conv, tr(gtc), prm["gbias"][l])
    hfc, hbc, m_st, s_st = _mlstm(qmc, kmc, mlc, grc, m_zero, s_zero)
    qm, km, gr = _mlstm_prep(ml, prm["cos_l"], prm["sin_l"], *conv, tr(gt), prm["gbias"][l])
    hf, hb, _, _ = _mlstm(qm, km, ml, gr, m_st, s_st)
    y_na = _na_latent(q, k, v, kc, vc, prm["na_tab"][l])
    y = (ygm, hf, hb, ml, y_na)
    if not need_ctx:
        return y, None
    return y, (ygmc, hfc, hbc, mlc, _ctx_attn(qc, kc, vc))


def _rows(a, idx):
    return a.at[idx].get(mode="promise_in_bounds")


def _moe(l, h2_all, e_all, prm):
    dest, src, te, n_used = _dispatch(e_all)
    yb = _experts(l, te, n_used, src, h2_all, prm["wg"], prm["wu"], prm["wd"])
    return yb, dest


def kernel(x, c, ctx, c_ctx, ada_w, ada_b, norm1_g, norm2_g, w_in, w_out, gm_ws, gm_bs, gm_norm_g,
           ml_conv_w, ml_conv_b, ml_gate_b, ml_norm_g, na_rpb, router_w, router_b,
           moe_w_gate, moe_w_up, moe_w_down, final_g):
    B, T, D = x.shape
    Lc = ctx.shape[1]
    cos_l, sin_l = _rope_tables(T)
    prm = dict(
        gm_ws=gm_ws.astype(BF16),
        gm_bs_full=jnp.repeat(jnp.transpose(gm_bs, (0, 2, 1)), GM_HEAD_DIM, axis=-1),
        gm_g=gm_norm_g[:, None, :],
        conv_w=ml_conv_w, conv_b=ml_conv_b[:, None, :],
        gbias=ml_gate_b[:, jnp.array([0, 2, 1, 3])].reshape(DEPTH, 4 * ML_HEADS, 1),
        cos_l=cos_l, sin_l=sin_l,
        cos_c=jnp.ones((Lc, LANE), F32), sin_c=jnp.zeros((Lc, LANE), F32),
        na_tab=jax.vmap(_na_bias_table)(na_rpb),
        wg=moe_w_gate, wu=moe_w_up, wd=moe_w_down,
    )
    w_in_r = _reorder_w_in(w_in)
    w_out_b = w_out.astype(BF16)
    rw_t = jnp.transpose(router_w)
    rw_hi = rw_t.astype(BF16)
    router_wt = jnp.stack([rw_hi, (rw_t - rw_hi.astype(F32)).astype(BF16)])
    router_bc = router_b[:, None]

    cs = jnp.concatenate([c, c_ctx[None, :], jnp.zeros((8 - B - 1, D), F32)], axis=0)
    mods = _ada_all(cs, ada_w, ada_b).reshape(DEPTH, 8, 6, D)

    xc = ctx
    for l in range(DEPTH):
        need_ctx = l < DEPTH - 1
        mod_l = mods[l, :B]
        mod_c = jnp.broadcast_to(mods[l, B:B + 1], (B, 6, D))
        n1 = norm1_g[l][None, :]
        gm_args = (prm["gm_ws"][l], prm["gm_bs_full"][l], prm["gm_g"][l])
        p = _in_proj(x, mod_l[:, 0:2], n1, w_in_r[l], *gm_args)
        pc = _in_proj(xc, mod_c[:, 0:2], n1, w_in_r[l], *gm_args)
        y, yc = _mixers(l, p, pc, prm, need_ctx)
        op_args = (w_out_b[l], ml_norm_g[l][None, :], norm2_g[l][None, :], router_wt, router_bc)
        n_all = B * T + (B * Lc if need_ctx else 0)
        h2_buf = (jnp.zeros((n_all, D), F32) if l == 0 else h2_all) if need_ctx else None
        xn, h2_all, re, rw = _out_proj(*y, x, mod_l[:, 2:5], *op_args, n_all, 0, h2_buf)
        e_all = jnp.transpose(re, (0, 2, 1)).reshape(B * T, 2)
        if need_ctx:
            xnc, h2_all, rec, rwc = _out_proj(*yc, xc, mod_c[:, 2:5], *op_args, n_all, B * T, h2_all)
            e_all = jnp.concatenate([e_all, jnp.transpose(rec, (0, 2, 1)).reshape(B * Lc, 2)], axis=0)
        yb, dest = _moe(l, h2_all, e_all, prm)
        final = l == DEPTH - 1
        fg = final_g[None, :]
        n_lat = B * T
        picked = lambda lo, hi, k, L: _rows(yb, dest[lo:hi, k]).reshape(B, L, D)
        x = _combine(xn, picked(0, n_lat, 0, T), picked(0, n_lat, 1, T),
                     jnp.transpose(rw, (0, 2, 1)), mod_l[:, 5:6], fg, final)
        if need_ctx:
            xc = _combine(xnc, picked(n_lat, None, 0, Lc), picked(n_lat, None, 1, Lc),
                          jnp.transpose(rwc, (0, 2, 1)), mod_c[:, 5:6], fg, False)
    return x
```
